```python
import math
import jax, jax.numpy as jnp
from jax import lax
import numpy as np

D_MODEL = 2048
BATCH = 16
SEQ = 256
DEPTH = 4
DEC_BATCH = 8
DEC_SEQ = 1024
PAST_LEN = 256

GRID_W = 64
HEAD_DIM = 128
NA_HEADS = 4
GQA_Q_HEADS = 8
GQA_KV_HEADS = 2
GQA_GROUP = GQA_Q_HEADS // GQA_KV_HEADS
DIFF_HEADS = 4
DIFF_QK_DIM = HEAD_DIM // 2
NA_KH = 8
NA_KW = 16
NA_QC = NA_KW
NA_CB = 2 * NA_KW
Q_BLOCK = 128
D_FF = 4 * D_MODEL
ROPE_THETA = 10000.0
EPS = 1e-6
N_MOD = 6
NEG_BIG = -1e30

NA_W = NA_HEADS * HEAD_DIM
GQA_QW = GQA_Q_HEADS * HEAD_DIM
GQA_KVW = GQA_KV_HEADS * HEAD_DIM
DIFF_W = DIFF_HEADS * HEAD_DIM
D_IN = 3 * NA_W + GQA_QW + 2 * GQA_KVW + 3 * DIFF_W
D_MIX = NA_W + GQA_QW + DIFF_W
IN_SPLITS = (NA_W, 2 * NA_W, 3 * NA_W,
             3 * NA_W + GQA_QW,
             3 * NA_W + GQA_QW + GQA_KVW,
             3 * NA_W + GQA_QW + 2 * GQA_KVW,
             3 * NA_W + GQA_QW + 2 * GQA_KVW + DIFF_W,
             3 * NA_W + GQA_QW + 2 * GQA_KVW + 2 * DIFF_W)

kernel_name = "hybrid_diffusion_prefix_trunk_step"


def rms_norm(x, g):
    xf = x.astype(jnp.float32)
    y = xf * lax.rsqrt(jnp.mean(xf * xf, axis=-1, keepdims=True) + EPS)
    return (y * g.astype(jnp.float32)).astype(x.dtype)


def rope_1d(x, pos):
    T = x.shape[1]
    half = x.shape[-1] // 2
    inv = ROPE_THETA ** (-jnp.arange(half, dtype=jnp.float32) / half)
    ang = pos.astype(jnp.float32)[:, None] * inv[None, :]
    bshape = (T,) + (1,) * (x.ndim - 3) + (half,)
    cos = jnp.cos(ang).reshape(bshape)
    sin = jnp.sin(ang).reshape(bshape)
    x1 = x[..., :half].astype(jnp.float32)
    x2 = x[..., half:].astype(jnp.float32)
    return jnp.concatenate([x1 * cos - x2 * sin, x2 * cos + x1 * sin], axis=-1).astype(x.dtype)


def axial_rope(x):
    T = x.shape[1]
    t = jnp.arange(T)
    d = x.shape[-1]
    return jnp.concatenate([rope_1d(x[..., : d // 2], t // GRID_W),
                            rope_1d(x[..., d // 2:], t % GRID_W)], axis=-1)


def sweep_query_blocks(fn, q):
    B, T = q.shape[:2]
    nb = T // Q_BLOCK
    qb = q.reshape((B, nb, Q_BLOCK) + q.shape[2:]).swapaxes(0, 1)
    out = lax.map(fn, qb)
    return out.swapaxes(0, 1).reshape((B, T) + out.shape[3:])


def dense_attention(q, k, v):
    scale = q.shape[-1] ** -0.5
    def block(qb):
        s = jnp.einsum('bqkgd,bskd->bkgqs', qb, k).astype(jnp.float32) * scale
        p = jax.nn.softmax(s, axis=-1).astype(v.dtype)
        return jnp.einsum('bkgqs,bskd->bqkgd', p, v)
    return sweep_query_blocks(block, q)


def diff_attention(q, k, v, lam):
    scale = q.shape[-1] ** -0.5
    def block(qb):
        s = jnp.einsum('bqhmd,bshmd->bhmqs', qb, k).astype(jnp.float32) * scale
        p = jax.nn.softmax(s, axis=-1)
        a = (p[:, :, 0] - lam * p[:, :, 1]).astype(v.dtype)
        return jnp.einsum('bhqs,bshd->bqhd', a, v)
    return sweep_query_blocks(block, q)


def diff_lambda(lam_vec, lambda_init):
    lf = lam_vec.astype(jnp.float32)
    return jnp.exp(jnp.sum(lf[0] * lf[1])) - jnp.exp(jnp.sum(lf[2] * lf[3])) + lambda_init


def neighbourhood_attention(q, k, v, k_ctx, v_ctx, rpb):
    B, T, H, d = q.shape
    rows = T // GRID_W
    kh = min(NA_KH, rows)
    ncb = GRID_W // NA_QC
    scale = d ** -0.5
    r = jnp.arange(rows)
    r0 = jnp.clip(r - kh // 2, 0, rows - kh)
    row_idx = r0[:, None] + jnp.arange(kh)[None, :]
    j = jnp.arange(ncb)
    c0 = jnp.clip(j * NA_QC - NA_KW // 2, 0, GRID_W - NA_CB)
    col_idx = c0[:, None] + jnp.arange(NA_CB)[None, :]
    qc = j[:, None] * NA_QC + jnp.arange(NA_QC)[None, :]
    ws = jnp.clip(qc - NA_KW // 2, 0, GRID_W - NA_KW)
    ri = row_idx[:, None, :, None]
    ci = col_idx[None, :, None, :]
    n_loc = kh * NA_CB
    kg = k.reshape(B, rows, GRID_W, H, d)[:, ri, ci].reshape(B, rows, ncb, n_loc, H, d)
    vg = v.reshape(B, rows, GRID_W, H, d)[:, ri, ci].reshape(B, rows, ncb, n_loc, H, d)
    cand = col_idx[:, None, :]
    valid = (cand >= ws[:, :, None]) & (cand < ws[:, :, None] + NA_KW)
    drow = (row_idx - r[:, None] + NA_KH - 1).reshape(rows, 1, 1, kh, 1)
    dcol = jnp.clip(cand - qc[:, :, None] + NA_KW - 1, 0, 2 * NA_KW - 2).reshape(1, ncb, NA_QC, 1, NA_CB)
    bias = rpb.astype(jnp.float32)[:, drow, dcol]
    bias = jnp.where(valid[None, None, :, :, None, :], bias, NEG_BIG)
    bias = bias.reshape(H, rows, ncb, NA_QC, n_loc).transpose(1, 2, 0, 3, 4)
    qg = q.reshape(B, rows, ncb, NA_QC, H, d)
    s_loc = jnp.einsum('brjqhd,brjnhd->brjhqn', qg, kg).astype(jnp.float32) * scale + bias
    s_ctx = jnp.einsum('brjqhd,blhd->brjhql', qg, k_ctx).astype(jnp.float32) * scale
    p = jax.nn.softmax(jnp.concatenate([s_loc, s_ctx], axis=-1), axis=-1).astype(v.dtype)
    o = (jnp.einsum('brjhqn,brjnhd->brjqhd', p[..., :n_loc], vg)
         + jnp.einsum('brjhql,blhd->brjqhd', p[..., n_loc:], v_ctx))
    return o.reshape(B, T, H, d)


def modulation(cvec, w_ada_l, b_ada_l):
    m = jax.nn.silu(cvec) @ w_ada_l + b_ada_l
    return jnp.split(m[:, None, :], N_MOD, axis=-1)


def project(h, w_in_l):
    B, T, _ = h.shape
    parts = jnp.split(h @ w_in_l, IN_SPLITS, axis=-1)
    return [p.reshape(B, T, -1, HEAD_DIM) for p in parts]


def split_maps(a):
    return a.reshape(a.shape[:-1] + (2, DIFF_QK_DIM))


def merge_heads(o_na, o_gqa, o_diff, diff_g_l, lambda_init, w_out_l):
    B, T = o_na.shape[:2]
    o_diff = rms_norm(o_diff, diff_g_l) * (1.0 - lambda_init)
    o = jnp.concatenate([o_na.reshape(B, T, NA_W), o_gqa.reshape(B, T, GQA_QW),
                         o_diff.reshape(B, T, DIFF_W)], axis=-1)
    return o @ w_out_l


def mix_context(h, w_in_l, w_out_l, q_g, k_g, lam_vec, diff_g_l, lambda_init):
    B, T, _ = h.shape
    na_q, na_k, na_v, g_q, g_k, g_v, d_q, d_k, d_v = project(h, w_in_l)
    g_q = rms_norm(g_q, q_g)
    g_k = rms_norm(g_k, k_g)
    o_na = dense_attention(na_q[:, :, :, None], na_k, na_v)
    o_gqa = dense_attention(g_q.reshape(B, T, GQA_KV_HEADS, GQA_GROUP, HEAD_DIM), g_k, g_v)
    lam = diff_lambda(lam_vec, lambda_init)
    o_diff = diff_attention(split_maps(d_q), split_maps(d_k), d_v, lam)
    o = merge_heads(o_na, o_gqa, o_diff, diff_g_l, lambda_init, w_out_l)
    return o, (na_k, na_v, g_k, g_v, d_k, d_v)


def mix_latent(h, ctx, w_in_l, w_out_l, rpb_l, q_g, k_g, lam_vec, diff_g_l, lambda_init):
    B, T, _ = h.shape
    na_kc, na_vc, g_kc, g_vc, d_kc, d_vc = ctx
    na_q, na_k, na_v, g_q, g_k, g_v, d_q, d_k, d_v = project(h, w_in_l)
    o_na = neighbourhood_attention(na_q, na_k, na_v, na_kc, na_vc, rpb_l)
    g_q = axial_rope(rms_norm(g_q, q_g))
    g_k = axial_rope(rms_norm(g_k, k_g))
    o_gqa = dense_attention(g_q.reshape(B, T, GQA_KV_HEADS, GQA_GROUP, HEAD_DIM),
                            jnp.concatenate([g_k, g_kc], axis=1),
                            jnp.concatenate([g_v, g_vc], axis=1))
    lam = diff_lambda(lam_vec, lambda_init)
    o_diff = diff_attention(axial_rope(split_maps(d_q)),
                            jnp.concatenate([axial_rope(split_maps(d_k)), split_maps(d_kc)], axis=1),
                            jnp.concatenate([d_v, d_vc], axis=1), lam)
    return merge_heads(o_na, o_gqa, o_diff, diff_g_l, lambda_init, w_out_l)


def squared_relu_mlp(h, w_up_l, w_down_l):
    return jnp.square(jax.nn.relu(h @ w_up_l)) @ w_down_l


def setup_inputs(seed: int = 0) -> dict:
    key = jax.random.key(seed)
    ks = jax.random.split(key, 22)
    f32 = jnp.float32
    def nrm(k, shape, s):
        return s * jax.random.normal(k, shape, f32)
    return {
        "x_prompt": nrm(ks[0], (BATCH, SEQ, D_MODEL), 1.0),
        "x_sample": nrm(ks[1], (DEC_BATCH, DEC_SEQ, D_MODEL), 1.0),
        "c": nrm(ks[2], (DEC_BATCH, D_MODEL), 1.0),
        "cache_na_k": nrm(ks[3], (DEC_BATCH, DEPTH, PAST_LEN, NA_HEADS, HEAD_DIM), 1.0),
        "cache_na_v": nrm(ks[4], (DEC_BATCH, DEPTH, PAST_LEN, NA_HEADS, HEAD_DIM), 1.0),
        "cache_gqa_k": nrm(ks[5], (DEC_BATCH, DEPTH, PAST_LEN, GQA_KV_HEADS, HEAD_DIM), 1.0),
        "cache_gqa_v": nrm(ks[6], (DEC_BATCH, DEPTH, PAST_LEN, GQA_KV_HEADS, HEAD_DIM), 1.0),
        "cache_diff_k": nrm(ks[7], (DEC_BATCH, DEPTH, PAST_LEN, DIFF_HEADS, HEAD_DIM), 1.0),
        "cache_diff_v": nrm(ks[8], (DEC_BATCH, DEPTH, PAST_LEN, DIFF_HEADS, HEAD_DIM), 1.0),
        "c_ctx": nrm(ks[9], (D_MODEL,), 1.0),
        "w_ada": nrm(ks[10], (DEPTH, D_MODEL, N_MOD * D_MODEL), D_MODEL ** -0.5),
        "b_ada": nrm(ks[11], (DEPTH, N_MOD * D_MODEL), 0.02),
        "norm_g": 1.0 + nrm(ks[12], (DEPTH, 4, D_MODEL), 0.02),
        "w_in": nrm(ks[13], (DEPTH, D_MODEL, D_IN), D_MODEL ** -0.5),
        "w_out": nrm(ks[14], (DEPTH, D_MIX, D_MODEL), D_MIX ** -0.5),
        "na_rpb": nrm(ks[15], (DEPTH, NA_HEADS, 2 * NA_KH - 1, 2 * NA_KW - 1), 0.1),
        "gqa_q_g": 1.0 + nrm(ks[16], (DEPTH, HEAD_DIM), 0.02),
        "gqa_k_g": 1.0 + nrm(ks[17], (DEPTH, HEAD_DIM), 0.02),
        "diff_lam": nrm(ks[18], (DEPTH, 4, DIFF_QK_DIM), 0.1),
        "diff_g": 1.0 + nrm(ks[19], (DEPTH, HEAD_DIM), 0.02),
        "w_up": nrm(ks[20], (DEPTH, D_MODEL, D_FF), D_MODEL ** -0.5),
        "w_down": nrm(ks[21], (DEPTH, D_FF, D_MODEL), D_FF ** -0.5),
    }


def reference(x_prompt, x_sample, c, cache_na_k, cache_na_v, cache_gqa_k, cache_gqa_v,
              cache_diff_k, cache_diff_v, c_ctx, w_ada, b_ada, norm_g, w_in, w_out,
              na_rpb, gqa_q_g, gqa_k_g, diff_lam, diff_g, w_up, w_down):
    xp = x_prompt
    xs = x_sample
    new_kv = [[] for _ in range(6)]
    for l in range(DEPTH):
        lambda_init = 0.8 - 0.6 * math.exp(-0.3 * l)
        g_pre_a, g_post_a, g_pre_m, g_post_m = norm_g[l, 0], norm_g[l, 1], norm_g[l, 2], norm_g[l, 3]
        sh_a, sc_a, gt_a, sh_m, sc_m, gt_m = modulation(c_ctx[None, :], w_ada[l], b_ada[l])
        h = rms_norm(xp, g_pre_a) * (1.0 + sc_a) + sh_a
        o, ctx_kv = mix_context(h, w_in[l], w_out[l], gqa_q_g[l], gqa_k_g[l],
                                diff_lam[l], diff_g[l], lambda_init)
        for i in range(6):
            new_kv[i].append(ctx_kv[i])
        xp = xp + gt_a * rms_norm(o, g_post_a)
        h = rms_norm(xp, g_pre_m) * (1.0 + sc_m) + sh_m
        xp = xp + gt_m * rms_norm(squared_relu_mlp(h, w_up[l], w_down[l]), g_post_m)
        sh_a, sc_a, gt_a, sh_m, sc_m, gt_m = modulation(c, w_ada[l], b_ada[l])
        h = rms_norm(xs, g_pre_a) * (1.0 + sc_a) + sh_a
        ctx = (cache_na_k[:, l], cache_na_v[:, l], cache_gqa_k[:, l], cache_gqa_v[:, l],
               cache_diff_k[:, l], cache_diff_v[:, l])
        o = mix_latent(h, ctx, w_in[l], w_out[l], na_rpb[l], gqa_q_g[l], gqa_k_g[l],
                       diff_lam[l], diff_g[l], lambda_init)
        xs = xs + gt_a * rms_norm(o, g_post_a)
        h = rms_norm(xs, g_pre_m) * (1.0 + sc_m) + sh_m
        xs = xs + gt_m * rms_norm(squared_relu_mlp(h, w_up[l], w_down[l]), g_post_m)
    new_na_k = jnp.stack(new_kv[0], axis=1)
    new_na_v = jnp.stack(new_kv[1], axis=1)
    new_gqa_k = jnp.stack(new_kv[2], axis=1)
    new_gqa_v = jnp.stack(new_kv[3], axis=1)
    new_diff_k = jnp.stack(new_kv[4], axis=1)
    new_diff_v = jnp.stack(new_kv[5], axis=1)
    return (xp, xs, new_na_k, new_na_v, new_gqa_k, new_gqa_v, new_diff_k, new_diff_v)
```

```python
import functools
import math

import jax
import jax.numpy as jnp
from jax import lax
from jax.experimental import pallas as pl
from jax.experimental.pallas import tpu as pltpu

D_MODEL = 2048
BATCH = 16
SEQ = 256
DEPTH = 4
DEC_BATCH = 8
DEC_SEQ = 1024
PAST_LEN = 256
GRID_W = 64
GRID_ROWS = DEC_SEQ // GRID_W
HEAD_DIM = 128
NA_HEADS = 4
GQA_Q_HEADS = 8
GQA_KV_HEADS = 2
GQA_GROUP = GQA_Q_HEADS // GQA_KV_HEADS
DIFF_HEADS = 4
DIFF_QK_DIM = HEAD_DIM // 2
NA_KH = 8
NA_KW = 16
D_FF = 4 * D_MODEL
ROPE_THETA = 10000.0
EPS = 1e-6
N_MOD = 6
NEG_BIG = -1e30

NA_W = NA_HEADS * HEAD_DIM
GQA_QW = GQA_Q_HEADS * HEAD_DIM
GQA_KVW = GQA_KV_HEADS * HEAD_DIM
DIFF_W = DIFF_HEADS * HEAD_DIM
D_IN = 3 * NA_W + GQA_QW + 2 * GQA_KVW + 3 * DIFF_W
D_MIX = NA_W + GQA_QW + DIFF_W
C_NA_Q, C_NA_K, C_NA_V = 0, NA_W, 2 * NA_W
C_G_Q = 3 * NA_W
C_G_K = C_G_Q + GQA_QW
C_G_V = C_G_K + GQA_KVW
C_D_Q = C_G_V + GQA_KVW
C_D_K = C_D_Q + DIFF_W
C_D_V = C_D_K + DIFF_W

N_KEYS = DEC_SEQ + PAST_LEN
NA_SLAB = NA_KH * GRID_W
MOD_ROWS = 16

F32 = jnp.float32
BF16 = jnp.bfloat16

VMEM_LIMIT = 52 * 1024 * 1024

TM_IN = 1024
TN_IN = 512
TM_OUT = 512
TM_MLP = 512
TF_MLP = 512
TN_ADA = 1024
TQ = 256


def _params(sem):
    return pltpu.CompilerParams(dimension_semantics=sem, vmem_limit_bytes=VMEM_LIMIT)


def _rms(x, g):
    ms = jnp.mean(x * x, axis=-1, keepdims=True)
    return x * lax.rsqrt(ms + EPS) * g


def _nt(a, b):
    return lax.dot_general(a, b, (((1,), (1,)), ((), ())), preferred_element_type=F32)


def _mm(a, b):
    return jnp.dot(a, b, preferred_element_type=F32)


def _softmax(s):
    m = jnp.max(s, axis=-1, keepdims=True)
    e = jnp.exp(s - m)
    return e / jnp.sum(e, axis=-1, keepdims=True)


def _rope(x, cos, sin_signed, half):
    n = x.shape[-1]
    lane = lax.broadcasted_iota(jnp.int32, x.shape, 1)
    first = (lane % (2 * half)) < half
    rot = jnp.where(first, pltpu.roll(x, n - half, 1), pltpu.roll(x, half, 1))
    return x * cos + rot * sin_signed


def _diff_lambda(lam_ref, lambda_init):
    lf = lam_ref[...]
    a = jnp.sum(lf[0:1] * lf[1:2], axis=-1, keepdims=True)
    b = jnp.sum(lf[2:3] * lf[3:4], axis=-1, keepdims=True)
    return jnp.exp(a) - jnp.exp(b) + lambda_init


def _mod_kernel(cv_ref, w_ref, b_ref, o_ref):
    cv = cv_ref[...]
    s = cv / (1.0 + jnp.exp(-cv))
    o_ref[...] = _mm(s.astype(BF16), w_ref[...].astype(BF16)) + b_ref[...]


def _modulation(cv, w_ada, b_ada):
    n = N_MOD * D_MODEL
    return pl.pallas_call(
        _mod_kernel,
        grid=(DEPTH, n // TN_ADA),
        in_specs=[
            pl.BlockSpec((MOD_ROWS, D_MODEL), lambda l, j: (0, 0)),
            pl.BlockSpec((None, D_MODEL, TN_ADA), lambda l, j: (l, 0, j)),
            pl.BlockSpec((None, 1, TN_ADA), lambda l, j: (l, 0, j)),
        ],
        out_specs=pl.BlockSpec((None, MOD_ROWS, TN_ADA), lambda l, j: (l, 0, j)),
        out_shape=jax.ShapeDtypeStruct((DEPTH, MOD_ROWS, n), F32),
        compiler_params=_params(("parallel", "parallel")),
        name="modulation",
    )(cv, w_ada, b_ada.reshape(DEPTH, 1, n))


def _mod_spec(chunk, row_fn):
    return pl.BlockSpec((None, None, 1, D_MODEL), lambda i, *_: (row_fn(i), chunk, 0, 0))


def _row_fn(tm, rows_per_batch, first_row):
    if rows_per_batch is None:
        return lambda i: first_row
    return lambda i: first_row + (i * tm) // rows_per_batch


def _inproj_kernel(x_ref, g_ref, sc_ref, sh_ref, w_ref, o_ref, h_ref):
    @pl.when(pl.program_id(1) == 0)
    def _():
        h = _rms(x_ref[...], g_ref[...]) * (1.0 + sc_ref[...]) + sh_ref[...]
        h_ref[...] = h.astype(BF16)

    o_ref[...] = _mm(h_ref[...], w_ref[...])


def _inproj(x, g_pre, mod_l, w_in, layer, rows_per_batch, first_row):
    m = x.shape[0]
    row = _row_fn(TM_IN, rows_per_batch, first_row)
    return pl.pallas_call(
        _inproj_kernel,
        grid=(m // TM_IN, D_IN // TN_IN),
        in_specs=[
            pl.BlockSpec((TM_IN, D_MODEL), lambda i, j: (i, 0)),
            pl.BlockSpec((1, D_MODEL), lambda i, j: (0, 0)),
            _mod_spec(1, row),
            _mod_spec(0, row),
            pl.BlockSpec((None, D_MODEL, TN_IN), lambda i, j: (layer, 0, j)),
        ],
        out_specs=pl.BlockSpec((TM_IN, TN_IN), lambda i, j: (i, j)),
        out_shape=jax.ShapeDtypeStruct((m, D_IN), F32),
        scratch_shapes=[pltpu.VMEM((TM_IN, D_MODEL), BF16)],
        compiler_params=_params(("parallel", "arbitrary")),
        name="inproj",
    )(x, g_pre, mod_l, mod_l, w_in)


def _ctx_attn_kernel(qkv_ref, qg_ref, kg_ref, lam_ref, dg_ref,
                     o_ref, nak_ref, nav_ref, gk_ref, gv_ref, dk_ref, dv_ref, *, lambda_init):
    scale = HEAD_DIM ** -0.5
    dscale = DIFF_QK_DIM ** -0.5
    hd = HEAD_DIM

    def cols(c0, h):
        return qkv_ref[:, c0 + h * hd:c0 + (h + 1) * hd]

    nak_ref[...] = qkv_ref[:, C_NA_K:C_NA_K + NA_W]
    nav_ref[...] = qkv_ref[:, C_NA_V:C_NA_V + NA_W]
    gv_ref[...] = qkv_ref[:, C_G_V:C_G_V + GQA_KVW]
    dk_ref[...] = qkv_ref[:, C_D_K:C_D_K + DIFF_W]
    dv_ref[...] = qkv_ref[:, C_D_V:C_D_V + DIFF_W]

    for h in range(NA_HEADS):
        q = cols(C_NA_Q, h).astype(BF16)
        k = cols(C_NA_K, h).astype(BF16)
        v = cols(C_NA_V, h).astype(BF16)
        p = _softmax(_nt(q, k) * scale).astype(BF16)
        o_ref[:, h * hd:(h + 1) * hd] = _mm(p, v).astype(BF16)

    for kv in range(GQA_KV_HEADS):
        kf = _rms(cols(C_G_K, kv), kg_ref[...])
        gk_ref[:, kv * hd:(kv + 1) * hd] = kf
        k = kf.astype(BF16)
        v = cols(C_G_V, kv).astype(BF16)
        for g in range(GQA_GROUP):
            hq = kv * GQA_GROUP + g
            q = _rms(cols(C_G_Q, hq), qg_ref[...]).astype(BF16)
            p = _softmax(_nt(q, k) * scale).astype(BF16)
            o_ref[:, NA_W + hq * hd:NA_W + (hq + 1) * hd] = _mm(p, v).astype(BF16)

    lam = _diff_lambda(lam_ref, lambda_init)
    lane = lax.broadcasted_iota(jnp.int32, (SEQ, hd), 1)
    for h in range(DIFF_HEADS):
        qf = cols(C_D_Q, h)
        k = cols(C_D_K, h).astype(BF16)
        v = cols(C_D_V, h).astype(BF16)
        q0 = jnp.where(lane < DIFF_QK_DIM, qf, 0.0).astype(BF16)
        q1 = jnp.where(lane >= DIFF_QK_DIM, qf, 0.0).astype(BF16)
        p0 = _softmax(_nt(q0, k) * dscale)
        p1 = _softmax(_nt(q1, k) * dscale)
        a = (p0 - lam * p1).astype(BF16)
        od = _rms(_mm(a, v), dg_ref[...]) * (1.0 - lambda_init)
        c0 = NA_W + GQA_QW + h * hd
        o_ref[:, c0:c0 + hd] = od.astype(BF16)


def _ctx_attn(qkv, q_g, k_g, lam_vec, diff_g, lambda_init):
    n = qkv.shape[0]
    vec = pl.BlockSpec((1, HEAD_DIM), lambda b: (0, 0))

    def rows(w):
        return pl.BlockSpec((SEQ, w), lambda b: (b, 0))

    def sds(w, dt):
        return jax.ShapeDtypeStruct((n, w), dt)

    return pl.pallas_call(
        functools.partial(_ctx_attn_kernel, lambda_init=lambda_init),
        grid=(n // SEQ,),
        in_specs=[rows(D_IN), vec, vec,
                  pl.BlockSpec((4, DIFF_QK_DIM), lambda b: (0, 0)), vec],
        out_specs=[rows(D_MIX), rows(NA_W), rows(NA_W), rows(GQA_KVW), rows(GQA_KVW),
                   rows(DIFF_W), rows(DIFF_W)],
        out_shape=[sds(D_MIX, BF16), sds(NA_W, F32), sds(NA_W, F32), sds(GQA_KVW, F32),
                   sds(GQA_KVW, F32), sds(DIFF_W, F32), sds(DIFF_W, F32)],
        compiler_params=_params(("parallel",)),
        name="ctx_attn",
    )(qkv, q_g, k_g, lam_vec, diff_g)


def _lat_na_kernel(q_ref, k_ref, v_ref, kc_ref, vc_ref, bias_ref, o_ref):
    scale = HEAD_DIM ** -0.5
    k = k_ref[...].astype(BF16)
    v = v_ref[...].astype(BF16)
    kc = kc_ref[...].astype(BF16)
    vc = vc_ref[...].astype(BF16)
    for r in range(GRID_ROWS):
        r0 = min(max(r - NA_KH // 2, 0), GRID_ROWS - NA_KH)
        q = q_ref[r * GRID_W:(r + 1) * GRID_W, :].astype(BF16)
        ks = k[r0 * GRID_W:r0 * GRID_W + NA_SLAB]
        vs = v[r0 * GRID_W:r0 * GRID_W + NA_SLAB]
        s_loc = _nt(q, ks) * scale + bias_ref[r]
        s_ctx = _nt(q, kc) * scale
        m = jnp.maximum(jnp.max(s_loc, axis=-1, keepdims=True),
                        jnp.max(s_ctx, axis=-1, keepdims=True))
        e_loc = jnp.exp(s_loc - m)
        e_ctx = jnp.exp(s_ctx - m)
        den = jnp.sum(e_loc, axis=-1, keepdims=True) + jnp.sum(e_ctx, axis=-1, keepdims=True)
        o = _mm((e_loc / den).astype(BF16), vs) + _mm((e_ctx / den).astype(BF16), vc)
        o_ref[r * GRID_W:(r + 1) * GRID_W, :] = o.astype(BF16)


def _lat_na(qkv, cache_k, cache_v, bias, layer):
    hd = HEAD_DIM
    cq, ck, cv = C_NA_Q // hd, C_NA_K // hd, C_NA_V // hd
    cache = pl.BlockSpec((None, None, PAST_LEN, hd), lambda b, h: (b, layer, 0, h))
    return pl.pallas_call(
        _lat_na_kernel,
        grid=(DEC_BATCH, NA_HEADS),
        in_specs=[
            pl.BlockSpec((DEC_SEQ, hd), lambda b, h: (b, cq + h)),
            pl.BlockSpec((DEC_SEQ, hd), lambda b, h: (b, ck + h)),
            pl.BlockSpec((DEC_SEQ, hd), lambda b, h: (b, cv + h)),
            cache, cache,
            pl.BlockSpec((None, GRID_ROWS, GRID_W, NA_SLAB), lambda b, h: (h, 0, 0, 0)),
        ],
        out_specs=pl.BlockSpec((DEC_SEQ, hd), lambda b, h: (b, h)),
        out_shape=jax.ShapeDtypeStruct((DEC_BATCH * DEC_SEQ, NA_W), BF16),
        compiler_params=_params(("parallel", "parallel")),
        name="lat_na",
    )(qkv, qkv, qkv, cache_k, cache_v, bias)


def _lat_gqa_kernel(q_ref, k_ref, v_ref, kc_ref, vc_ref, cos_ref, sin_ref, qg_ref, kg_ref,
                    o_ref, kall_ref, vall_ref):
    scale = HEAD_DIM ** -0.5
    hd = HEAD_DIM
    half = HEAD_DIM // 4
    qb = pl.program_id(2)

    @pl.when(qb == 0)
    def _():
        kf = _rope(_rms(k_ref[...], kg_ref[...]), cos_ref[...], sin_ref[...], half)
        kall_ref[0:DEC_SEQ, :] = kf.astype(BF16)
        kall_ref[DEC_SEQ:N_KEYS, :] = kc_ref[...].astype(BF16)
        vall_ref[0:DEC_SEQ, :] = v_ref[...].astype(BF16)
        vall_ref[DEC_SEQ:N_KEYS, :] = vc_ref[...].astype(BF16)

    row0 = pl.multiple_of(qb * TQ, TQ)
    cos = cos_ref[pl.ds(row0, TQ), :]
    sin = sin_ref[pl.ds(row0, TQ), :]
    for g in range(GQA_GROUP):
        qf = _rms(q_ref[:, g * hd:(g + 1) * hd], qg_ref[...])
        q = _rope(qf, cos, sin, half).astype(BF16)
        p = _softmax(_nt(q, kall_ref[...]) * scale).astype(BF16)
        o_ref[:, g * hd:(g + 1) * hd] = _mm(p, vall_ref[...]).astype(BF16)


def _lat_gqa(qkv, cache_k, cache_v, cos, sin, q_g, k_g, layer):
    hd = HEAD_DIM
    gw = GQA_GROUP * hd
    nqb = DEC_SEQ // TQ
    cq, ck, cv = C_G_Q // gw, C_G_K // hd, C_G_V // hd
    cache = pl.BlockSpec((None, None, PAST_LEN, hd), lambda b, kv, qb: (b, layer, 0, kv))
    table = pl.BlockSpec((DEC_SEQ, hd), lambda b, kv, qb: (0, 0))
    vec = pl.BlockSpec((1, hd), lambda b, kv, qb: (0, 0))
    return pl.pallas_call(
        _lat_gqa_kernel,
        grid=(DEC_BATCH, GQA_KV_HEADS, nqb),
        in_specs=[
            pl.BlockSpec((TQ, gw), lambda b, kv, qb: (b * nqb + qb, cq + kv)),
            pl.BlockSpec((DEC_SEQ, hd), lambda b, kv, qb: (b, ck + kv)),
            pl.BlockSpec((DEC_SEQ, hd), lambda b, kv, qb: (b, cv + kv)),
            cache, cache, table, table, vec, vec,
        ],
        out_specs=pl.BlockSpec((TQ, gw), lambda b, kv, qb: (b * nqb + qb, kv)),
        out_shape=jax.ShapeDtypeStruct((DEC_BATCH * DEC_SEQ, GQA_QW), BF16),
        scratch_shapes=[pltpu.VMEM((N_KEYS, hd), BF16), pltpu.VMEM((N_KEYS, hd), BF16)],
        compiler_params=_params(("parallel", "parallel", "arbitrary")),
        name="lat_gqa",
    )(qkv, qkv, qkv, cache_k, cache_v, cos, sin, q_g, k_g)


def _lat_diff_kernel(q_ref, k_ref, v_ref, kc_ref, vc_ref, cos_ref, sin_ref, lam_ref, dg_ref,
                     o_ref, kall_ref, vall_ref, *, lambda_init):
    dscale = DIFF_QK_DIM ** -0.5
    half = DIFF_QK_DIM // 4
    qb = pl.program_id(2)

    @pl.when(qb == 0)
    def _():
        kf = _rope(k_ref[...], cos_ref[...], sin_ref[...], half)
        kall_ref[0:DEC_SEQ, :] = kf.astype(BF16)
        kall_ref[DEC_SEQ:N_KEYS, :] = kc_ref[...].astype(BF16)
        vall_ref[0:DEC_SEQ, :] = v_ref[...].astype(BF16)
        vall_ref[DEC_SEQ:N_KEYS, :] = vc_ref[...].astype(BF16)

    row0 = pl.multiple_of(qb * TQ, TQ)
    cos = cos_ref[pl.ds(row0, TQ), :]
    sin = sin_ref[pl.ds(row0, TQ), :]
    qf = _rope(q_ref[...], cos, sin, half)
    lane = lax.broadcasted_iota(jnp.int32, qf.shape, 1)
    q0 = jnp.where(lane < DIFF_QK_DIM, qf, 0.0).astype(BF16)
    q1 = jnp.where(lane >= DIFF_QK_DIM, qf, 0.0).astype(BF16)
    kall = kall_ref[...]
    p0 = _softmax(_nt(q0, kall) * dscale)
    p1 = _softmax(_nt(q1, kall) * dscale)
    lam = _diff_lambda(lam_ref, lambda_init)
    a = (p0 - lam * p1).astype(BF16)
    od = _rms(_mm(a, vall_ref[...]), dg_ref[...]) * (1.0 - lambda_init)
    o_ref[...] = od.astype(BF16)


def _lat_diff(qkv, cache_k, cache_v, cos, sin, lam_vec, diff_g, layer, lambda_init):
    hd = HEAD_DIM
    nqb = DEC_SEQ // TQ
    cq, ck, cv = C_D_Q // hd, C_D_K // hd, C_D_V // hd
    cache = pl.BlockSpec((None, None, PAST_LEN, hd), lambda b, h, qb: (b, layer, 0, h))
    table = pl.BlockSpec((DEC_SEQ, hd), lambda b, h, qb: (0, 0))
    return pl.pallas_call(
        functools.partial(_lat_diff_kernel, lambda_init=lambda_init),
        grid=(DEC_BATCH, DIFF_HEADS, nqb),
        in_specs=[
            pl.BlockSpec((TQ, hd), lambda b, h, qb: (b * nqb + qb, cq + h)),
            pl.BlockSpec((DEC_SEQ, hd), lambda b, h, qb: (b, ck + h)),
            pl.BlockSpec((DEC_SEQ, hd), lambda b, h, qb: (b, cv + h)),
            cache, cache, table, table,
            pl.BlockSpec((4, DIFF_QK_DIM), lambda b, h, qb: (0, 0)),
            pl.BlockSpec((1, hd), lambda b, h, qb: (0, 0)),
        ],
        out_specs=pl.BlockSpec((TQ, hd), lambda b, h, qb: (b * nqb + qb, h)),
        out_shape=jax.ShapeDtypeStruct((DEC_BATCH * DEC_SEQ, DIFF_W), BF16),
        scratch_shapes=[pltpu.VMEM((N_KEYS, hd), BF16), pltpu.VMEM((N_KEYS, hd), BF16)],
        compiler_params=_params(("parallel", "parallel", "arbitrary")),
        name="lat_diff",
    )(qkv, qkv, qkv, cache_k, cache_v, cos, sin, lam_vec, diff_g)


def _outproj_kernel(*refs, n_o):
    o_refs = refs[:n_o]
    w_ref, x_ref, g_ref, gt_ref, out_ref = refs[n_o:]
    y = None
    off = 0
    for o_ref in o_refs:
        wd = o_ref.shape[1]
        part = _mm(o_ref[...], w_ref[off:off + wd, :])
        y = part if y is None else y + part
        off += wd
    out_ref[...] = x_ref[...] + gt_ref[...] * _rms(y, g_ref[...])


def _outproj(o_parts, w_out, layer, x, g_post, mod_l, rows_per_batch, first_row):
    m = x.shape[0]
    row = _row_fn(TM_OUT, rows_per_batch, first_row)
    full = pl.BlockSpec((TM_OUT, D_MODEL), lambda i: (i, 0))
    return pl.pallas_call(
        functools.partial(_outproj_kernel, n_o=len(o_parts)),
        grid=(m // TM_OUT,),
        in_specs=[pl.BlockSpec((TM_OUT, o.shape[1]), lambda i: (i, 0)) for o in o_parts] + [
            pl.BlockSpec((None, D_MIX, D_MODEL), lambda i: (layer, 0, 0)),
            full,
            pl.BlockSpec((1, D_MODEL), lambda i: (0, 0)),
            _mod_spec(2, row),
        ],
        out_specs=full,
        out_shape=jax.ShapeDtypeStruct((m, D_MODEL), F32),
        compiler_params=_params(("parallel",)),
        name="outproj",
    )(*o_parts, w_out, x, g_post, mod_l)


def _mlp_kernel(x_ref, gpre_ref, sc_ref, sh_ref, wup_ref, wdn_ref, gpost_ref, gt_ref,
                out_ref, h_ref, acc_ref):
    k = pl.program_id(1)

    @pl.when(k == 0)
    def _():
        h = _rms(x_ref[...], gpre_ref[...]) * (1.0 + sc_ref[...]) + sh_ref[...]
        h_ref[...] = h.astype(BF16)
        acc_ref[...] = jnp.zeros_like(acc_ref)

    u = _mm(h_ref[...], wup_ref[...])
    a = jnp.square(jnp.maximum(u, 0.0)).astype(BF16)
    acc_ref[...] += _mm(a, wdn_ref[...])

    @pl.when(k == pl.num_programs(1) - 1)
    def _():
        out_ref[...] = x_ref[...] + gt_ref[...] * _rms(acc_ref[...], gpost_ref[...])


def _mlp(x, g_pre, g_post, mod_l, w_up, w_down, layer, rows_per_batch, first_row):
    m = x.shape[0]
    row = _row_fn(TM_MLP, rows_per_batch, first_row)
    full = pl.BlockSpec((TM_MLP, D_MODEL), lambda i, k: (i, 0))
    vec = pl.BlockSpec((1, D_MODEL), lambda i, k: (0, 0))
    return pl.pallas_call(
        _mlp_kernel,
        grid=(m // TM_MLP, D_FF // TF_MLP),
        in_specs=[
            full, vec, _mod_spec(4, row), _mod_spec(3, row),
            pl.BlockSpec((None, D_MODEL, TF_MLP), lambda i, k: (layer, 0, k)),
            pl.BlockSpec((None, TF_MLP, D_MODEL), lambda i, k: (layer, k, 0)),
            vec, _mod_spec(5, row),
        ],
        out_specs=full,
        out_shape=jax.ShapeDtypeStruct((m, D_MODEL), F32),
        scratch_shapes=[pltpu.VMEM((TM_MLP, D_MODEL), BF16), pltpu.VMEM((TM_MLP, D_MODEL), F32)],
        compiler_params=_params(("parallel", "arbitrary")),
        name="mlp",
    )(x, g_pre, mod_l, mod_l, w_up, w_down, g_post, mod_l)


def _rope_tables(half, n_rep):
    t = jnp.arange(DEC_SEQ)
    inv = ROPE_THETA ** (-jnp.arange(half, dtype=F32) / half)

    def cs(pos):
        ang = pos.astype(F32)[:, None] * inv[None, :]
        c, s = jnp.cos(ang), jnp.sin(ang)
        return jnp.concatenate([c, c], axis=-1), jnp.concatenate([-s, s], axis=-1)

    cr, sr = cs(t // GRID_W)
    cc, sc = cs(t % GRID_W)
    return (jnp.concatenate([cr, cc] * n_rep, axis=-1),
            jnp.concatenate([sr, sc] * n_rep, axis=-1))


def _na_bias(rpb_l):
    r = jnp.arange(GRID_ROWS)
    r0 = jnp.clip(r - NA_KH // 2, 0, GRID_ROWS - NA_KH)
    drow = r0[:, None] + jnp.arange(NA_KH)[None, :] - r[:, None] + NA_KH - 1
    qc = jnp.arange(GRID_W)[:, None]
    kc = jnp.arange(GRID_W)[None, :]
    ws = jnp.clip(qc - NA_KW // 2, 0, GRID_W - NA_KW)
    valid = (kc >= ws) & (kc < ws + NA_KW)
    dcol = jnp.clip(kc - qc + NA_KW - 1, 0, 2 * NA_KW - 2)
    b = rpb_l.astype(F32)[:, drow[:, None, :, None], dcol[None, :, None, :]]
    b = jnp.where(valid[None, None, :, None, :], b, NEG_BIG)
    return b.reshape(NA_HEADS, GRID_ROWS, GRID_W, NA_SLAB)


def kernel(x_prompt, x_sample, c, cache_na_k, cache_na_v, cache_gqa_k, cache_gqa_v,
           cache_diff_k, cache_diff_v, c_ctx, w_ada, b_ada, norm_g, w_in, w_out, na_rpb,
           gqa_q_g, gqa_k_g, diff_lam, diff_g, w_up, w_down):
    np_rows = BATCH * SEQ
    ns_rows = DEC_BATCH * DEC_SEQ
    xp = x_prompt.reshape(np_rows, D_MODEL)
    xs = x_sample.reshape(ns_rows, D_MODEL)

    cv = jnp.concatenate(
        [c_ctx[None, :], c, jnp.zeros((MOD_ROWS - 1 - DEC_BATCH, D_MODEL), F32)], axis=0)
    mod = _modulation(cv, w_ada, b_ada).reshape(DEPTH, MOD_ROWS, N_MOD, 1, D_MODEL)

    w_in_b = w_in.astype(BF16)
    w_out_b = w_out.astype(BF16)
    w_up_b = w_up.astype(BF16)
    w_down_b = w_down.astype(BF16)

    def flat_cache(a):
        return a.reshape(DEC_BATCH, DEPTH, PAST_LEN, a.shape[3] * HEAD_DIM)

    c_na_k, c_na_v = flat_cache(cache_na_k), flat_cache(cache_na_v)
    c_g_k, c_g_v = flat_cache(cache_gqa_k), flat_cache(cache_gqa_v)
    c_d_k, c_d_v = flat_cache(cache_diff_k), flat_cache(cache_diff_v)

    cos_g, sin_g = _rope_tables(HEAD_DIM // 4, 1)
    cos_d, sin_d = _rope_tables(DIFF_QK_DIM // 4, 2)

    new_kv = [[] for _ in range(6)]
    for l in range(DEPTH):
        lambda_init = 0.8 - 0.6 * math.exp(-0.3 * l)
        mod_l = mod[l]
        g = norm_g[l].reshape(4, 1, D_MODEL)
        q_g = gqa_q_g[l].reshape(1, HEAD_DIM)
        k_g = gqa_k_g[l].reshape(1, HEAD_DIM)
        d_g = diff_g[l].reshape(1, HEAD_DIM)
        lam_vec = diff_lam[l]

        qkv_p = _inproj(xp, g[0], mod_l, w_in_b, l, None, 0)
        o_p, *kv = _ctx_attn(qkv_p, q_g, k_g, lam_vec, d_g, lambda_init)
        for i in range(6):
            new_kv[i].append(kv[i])
        xp = _outproj([o_p], w_out_b, l, xp, g[1], mod_l, None, 0)
        xp = _mlp(xp, g[2], g[3], mod_l, w_up_b, w_down_b, l, None, 0)

        qkv_s = _inproj(xs, g[0], mod_l, w_in_b, l, DEC_SEQ, 1)
        o_na = _lat_na(qkv_s, c_na_k, c_na_v, _na_bias(na_rpb[l]), l)
        o_gqa = _lat_gqa(qkv_s, c_g_k, c_g_v, cos_g, sin_g, q_g, k_g, l)
        o_diff = _lat_diff(qkv_s, c_d_k, c_d_v, cos_d, sin_d, lam_vec, d_g, l, lambda_init)
        xs = _outproj([o_na, o_gqa, o_diff], w_out_b, l, xs, g[1], mod_l, DEC_SEQ, 1)
        xs = _mlp(xs, g[2], g[3], mod_l, w_up_b, w_down_b, l, DEC_SEQ, 1)

    def stack(parts, heads):
        return jnp.stack([p.reshape(BATCH, SEQ, heads, HEAD_DIM) for p in parts], axis=1)

    return (xp.reshape(BATCH, SEQ, D_MODEL), xs.reshape(DEC_BATCH, DEC_SEQ, D_MODEL),
            stack(new_kv[0], NA_HEADS), stack(new_kv[1], NA_HEADS),
            stack(new_kv[2], GQA_KV_HEADS), stack(new_kv[3], GQA_KV_HEADS),
            stack(new_kv[4], DIFF_HEADS), stack(new_kv[5], DIFF_HEADS))
```

```python
import functools
import math

import jax
import jax.numpy as jnp
import numpy as np
from jax import lax
from jax.experimental import pallas as pl
from jax.experimental.pallas import tpu as pltpu

D_MODEL = 2048
BATCH = 16
SEQ = 256
DEPTH = 4
DEC_BATCH = 8
DEC_SEQ = 1024
PAST_LEN = 256
GRID_W = 64
GRID_ROWS = DEC_SEQ // GRID_W
HEAD_DIM = 128
NA_HEADS = 4
GQA_Q_HEADS = 8
GQA_KV_HEADS = 2
GQA_GROUP = GQA_Q_HEADS // GQA_KV_HEADS
DIFF_HEADS = 4
DIFF_QK_DIM = HEAD_DIM // 2
NA_KH = 8
NA_KW = 16
D_FF = 4 * D_MODEL
ROPE_THETA = 10000.0
EPS = 1e-6
N_MOD = 6
NEG_BIG = -1e30

NA_W = NA_HEADS * HEAD_DIM
GQA_QW = GQA_Q_HEADS * HEAD_DIM
GQA_KVW = GQA_KV_HEADS * HEAD_DIM
DIFF_W = DIFF_HEADS * HEAD_DIM
D_IN = 3 * NA_W + GQA_QW + 2 * GQA_KVW + 3 * DIFF_W
D_MIX = NA_W + GQA_QW + DIFF_W
C_NA_Q, C_NA_K, C_NA_V = 0, NA_W, 2 * NA_W
C_G_Q = 3 * NA_W
C_G_K = C_G_Q + GQA_QW
C_G_V = C_G_K + GQA_KVW
C_D_Q = C_G_V + GQA_KVW
C_D_K = C_D_Q + DIFF_W
C_D_V = C_D_K + DIFF_W

N_KEYS = DEC_SEQ + PAST_LEN
NA_SLAB = NA_KH * GRID_W
MOD_ROWS = 16

F32 = jnp.float32
BF16 = jnp.bfloat16

VMEM_LIMIT = 52 * 1024 * 1024

TM_IN = 1024
TN_IN = 512
TM_OUT = 512
TM_MLP = 512
TF_MLP = 512
TN_ADA = 1024
TQ = 256


def _params(sem):
    return pltpu.CompilerParams(dimension_semantics=sem, vmem_limit_bytes=VMEM_LIMIT)


def _rms(x, g):
    ms = jnp.mean(x * x, axis=-1, keepdims=True)
    return x * lax.rsqrt(ms + EPS) * g


def _nt(a, b):
    return lax.dot_general(a, b, (((1,), (1,)), ((), ())), preferred_element_type=F32)


def _mm(a, b):
    return jnp.dot(a, b, preferred_element_type=F32)


def _softmax(s):
    m = jnp.max(s, axis=-1, keepdims=True)
    e = jnp.exp(s - m)
    return e / jnp.sum(e, axis=-1, keepdims=True)


def _rope(x, cos, sin_signed, half):
    n = x.shape[-1]
    lane = lax.broadcasted_iota(jnp.int32, x.shape, 1)
    first = (lane % (2 * half)) < half
    rot = jnp.where(first, pltpu.roll(x, n - half, 1), pltpu.roll(x, half, 1))
    return x * cos + rot * sin_signed


def _diff_lambda(lam_ref, lambda_init):
    lf = lam_ref[...]
    a = jnp.sum(lf[0:1] * lf[1:2], axis=-1, keepdims=True)
    b = jnp.sum(lf[2:3] * lf[3:4], axis=-1, keepdims=True)
    return jnp.exp(a) - jnp.exp(b) + lambda_init


def _mod_kernel(cv_ref, w_ref, b_ref, o_ref):
    cv = cv_ref[...]
    s = cv / (1.0 + jnp.exp(-cv))
    o_ref[...] = _mm(s.astype(BF16), w_ref[...].astype(BF16)) + b_ref[...]


def _modulation(cv, w_ada, b_ada):
    n = N_MOD * D_MODEL
    return pl.pallas_call(
        _mod_kernel,
        grid=(DEPTH, n // TN_ADA),
        in_specs=[
            pl.BlockSpec((MOD_ROWS, D_MODEL), lambda l, j: (0, 0)),
            pl.BlockSpec((None, D_MODEL, TN_ADA), lambda l, j: (l, 0, j)),
            pl.BlockSpec((None, 1, TN_ADA), lambda l, j: (l, 0, j)),
        ],
        out_specs=pl.BlockSpec((None, MOD_ROWS, TN_ADA), lambda l, j: (l, 0, j)),
        out_shape=jax.ShapeDtypeStruct((DEPTH, MOD_ROWS, n), F32),
        compiler_params=_params(("parallel", "parallel")),
        name="modulation",
    )(cv, w_ada, b_ada.reshape(DEPTH, 1, n))


def _mod_spec(chunk, row_fn):
    return pl.BlockSpec((None, None, 1, D_MODEL), lambda i, *_: (row_fn(i), chunk, 0, 0))


def _row_fn(tm, rows_per_batch, first_row):
    if rows_per_batch is None:
        return lambda i: first_row
    return lambda i: first_row + (i * tm) // rows_per_batch


def _inproj_kernel(x_ref, g_ref, sc_ref, sh_ref, w_ref, o_ref, h_ref):
    @pl.when(pl.program_id(1) == 0)
    def _():
        h = _rms(x_ref[...], g_ref[...]) * (1.0 + sc_ref[...]) + sh_ref[...]
        h_ref[...] = h.astype(BF16)

    o_ref[...] = _mm(h_ref[...], w_ref[...])


def _inproj(x, g_pre, mod_l, w_in, layer, rows_per_batch, first_row):
    m = x.shape[0]
    row = _row_fn(TM_IN, rows_per_batch, first_row)
    return pl.pallas_call(
        _inproj_kernel,
        grid=(m // TM_IN, D_IN // TN_IN),
        in_specs=[
            pl.BlockSpec((TM_IN, D_MODEL), lambda i, j: (i, 0)),
            pl.BlockSpec((1, D_MODEL), lambda i, j: (0, 0)),
            _mod_spec(1, row),
            _mod_spec(0, row),
            pl.BlockSpec((None, D_MODEL, TN_IN), lambda i, j: (layer, 0, j)),
        ],
        out_specs=pl.BlockSpec((TM_IN, TN_IN), lambda i, j: (i, j)),
        out_shape=jax.ShapeDtypeStruct((m, D_IN), F32),
        scratch_shapes=[pltpu.VMEM((TM_IN, D_MODEL), BF16)],
        compiler_params=_params(("parallel", "arbitrary")),
        name="inproj",
    )(x, g_pre, mod_l, mod_l, w_in)


def _ctx_attn_kernel(qkv_ref, qg_ref, kg_ref, lam_ref, dg_ref,
                     o_ref, nak_ref, nav_ref, gk_ref, gv_ref, dk_ref, dv_ref, *, lambda_init):
    scale = HEAD_DIM ** -0.5
    dscale = DIFF_QK_DIM ** -0.5
    hd = HEAD_DIM

    def cols(c0, h):
        return qkv_ref[:, c0 + h * hd:c0 + (h + 1) * hd]

    nak_ref[...] = qkv_ref[:, C_NA_K:C_NA_K + NA_W]
    nav_ref[...] = qkv_ref[:, C_NA_V:C_NA_V + NA_W]
    gv_ref[...] = qkv_ref[:, C_G_V:C_G_V + GQA_KVW]
    dk_ref[...] = qkv_ref[:, C_D_K:C_D_K + DIFF_W]
    dv_ref[...] = qkv_ref[:, C_D_V:C_D_V + DIFF_W]

    for h in range(NA_HEADS):
        q = cols(C_NA_Q, h).astype(BF16)
        k = cols(C_NA_K, h).astype(BF16)
        v = cols(C_NA_V, h).astype(BF16)
        p = _softmax(_nt(q, k) * scale).astype(BF16)
        o_ref[:, h * hd:(h + 1) * hd] = _mm(p, v).astype(BF16)

    for kv in range(GQA_KV_HEADS):
        kf = _rms(cols(C_G_K, kv), kg_ref[...])
        gk_ref[:, kv * hd:(kv + 1) * hd] = kf
        k = kf.astype(BF16)
        v = cols(C_G_V, kv).astype(BF16)
        for g in range(GQA_GROUP):
            hq = kv * GQA_GROUP + g
            q = _rms(cols(C_G_Q, hq), qg_ref[...]).astype(BF16)
            p = _softmax(_nt(q, k) * scale).astype(BF16)
            o_ref[:, NA_W + hq * hd:NA_W + (hq + 1) * hd] = _mm(p, v).astype(BF16)

    lam = _diff_lambda(lam_ref, lambda_init)
    lane = lax.broadcasted_iota(jnp.int32, (SEQ, hd), 1)
    for h in range(DIFF_HEADS):
        qf = cols(C_D_Q, h)
        k = cols(C_D_K, h).astype(BF16)
        v = cols(C_D_V, h).astype(BF16)
        q0 = jnp.where(lane < DIFF_QK_DIM, qf, 0.0).astype(BF16)
        q1 = jnp.where(lane >= DIFF_QK_DIM, qf, 0.0).astype(BF16)
        p0 = _softmax(_nt(q0, k) * dscale)
        p1 = _softmax(_nt(q1, k) * dscale)
        a = (p0 - lam * p1).astype(BF16)
        od = _rms(_mm(a, v), dg_ref[...]) * (1.0 - lambda_init)
        c0 = NA_W + GQA_QW + h * hd
        o_ref[:, c0:c0 + hd] = od.astype(BF16)


def _ctx_attn(qkv, q_g, k_g, lam_vec, diff_g, lambda_init):
    n = qkv.shape[0]
    vec = pl.BlockSpec((1, HEAD_DIM), lambda b: (0, 0))

    def rows(w):
        return pl.BlockSpec((SEQ, w), lambda b: (b, 0))

    def sds(w, dt):
        return jax.ShapeDtypeStruct((n, w), dt)

    return pl.pallas_call(
        functools.partial(_ctx_attn_kernel, lambda_init=lambda_init),
        grid=(n // SEQ,),
        in_specs=[rows(D_IN), vec, vec,
                  pl.BlockSpec((4, DIFF_QK_DIM), lambda b: (0, 0)), vec],
        out_specs=[rows(D_MIX), rows(NA_W), rows(NA_W), rows(GQA_KVW), rows(GQA_KVW),
                   rows(DIFF_W), rows(DIFF_W)],
        out_shape=[sds(D_MIX, BF16), sds(NA_W, F32), sds(NA_W, F32), sds(GQA_KVW, F32),
                   sds(GQA_KVW, F32), sds(DIFF_W, F32), sds(DIFF_W, F32)],
        compiler_params=_params(("parallel",)),
        name="ctx_attn",
    )(qkv, q_g, k_g, lam_vec, diff_g)


def _lat_na_kernel(q_ref, k_ref, v_ref, kc_ref, vc_ref, bias_ref, o_ref):
    scale = HEAD_DIM ** -0.5
    k = k_ref[...].astype(BF16)
    v = v_ref[...].astype(BF16)
    kc = kc_ref[...].astype(BF16)
    vc = vc_ref[...].astype(BF16)
    for r in range(GRID_ROWS):
        r0, d0 = _na_row_offset(r)
        q = q_ref[r * GRID_W:(r + 1) * GRID_W, :].astype(BF16)
        ks = k[r0 * GRID_W:r0 * GRID_W + NA_SLAB]
        vs = v[r0 * GRID_W:r0 * GRID_W + NA_SLAB]
        s_loc = _nt(q, ks) * scale + bias_ref[d0]
        s_ctx = _nt(q, kc) * scale
        m = jnp.maximum(jnp.max(s_loc, axis=-1, keepdims=True),
                        jnp.max(s_ctx, axis=-1, keepdims=True))
        e_loc = jnp.exp(s_loc - m)
        e_ctx = jnp.exp(s_ctx - m)
        den = jnp.sum(e_loc, axis=-1, keepdims=True) + jnp.sum(e_ctx, axis=-1, keepdims=True)
        o = _mm((e_loc / den).astype(BF16), vs) + _mm((e_ctx / den).astype(BF16), vc)
        o_ref[r * GRID_W:(r + 1) * GRID_W, :] = o.astype(BF16)


def _lat_na(qkv, cache_k, cache_v, bias, layer):
    hd = HEAD_DIM
    cq, ck, cv = C_NA_Q // hd, C_NA_K // hd, C_NA_V // hd
    cache = pl.BlockSpec((None, None, PAST_LEN, hd), lambda b, h: (b, layer, 0, h))
    return pl.pallas_call(
        _lat_na_kernel,
        grid=(DEC_BATCH, NA_HEADS),
        in_specs=[
            pl.BlockSpec((DEC_SEQ, hd), lambda b, h: (b, cq + h)),
            pl.BlockSpec((DEC_SEQ, hd), lambda b, h: (b, ck + h)),
            pl.BlockSpec((DEC_SEQ, hd), lambda b, h: (b, cv + h)),
            cache, cache,
            pl.BlockSpec((None, None, NA_KH, GRID_W, NA_SLAB), lambda b, h: (layer, h, 0, 0, 0)),
        ],
        out_specs=pl.BlockSpec((DEC_SEQ, hd), lambda b, h: (b, h)),
        out_shape=jax.ShapeDtypeStruct((DEC_BATCH * DEC_SEQ, NA_W), BF16),
        compiler_params=_params(("parallel", "parallel")),
        name="lat_na",
    )(qkv, qkv, qkv, cache_k, cache_v, bias)


def _lat_gqa_kernel(q_ref, k_ref, v_ref, kc_ref, vc_ref, cos_ref, sin_ref, qg_ref, kg_ref,
                    o_ref, kall_ref, vall_ref):
    scale = HEAD_DIM ** -0.5
    hd = HEAD_DIM
    half = HEAD_DIM // 4
    qb = pl.program_id(2)

    @pl.when(qb == 0)
    def _():
        kf = _rope(_rms(k_ref[...], kg_ref[...]), cos_ref[...], sin_ref[...], half)
        kall_ref[0:DEC_SEQ, :] = kf.astype(BF16)
        kall_ref[DEC_SEQ:N_KEYS, :] = kc_ref[...].astype(BF16)
        vall_ref[0:DEC_SEQ, :] = v_ref[...].astype(BF16)
        vall_ref[DEC_SEQ:N_KEYS, :] = vc_ref[...].astype(BF16)

    row0 = pl.multiple_of(qb * TQ, TQ)
    cos = cos_ref[pl.ds(row0, TQ), :]
    sin = sin_ref[pl.ds(row0, TQ), :]
    for g in range(GQA_GROUP):
        qf = _rms(q_ref[:, g * hd:(g + 1) * hd], qg_ref[...])
        q = _rope(qf, cos, sin, half).astype(BF16)
        p = _softmax(_nt(q, kall_ref[...]) * scale).astype(BF16)
        o_ref[:, g * hd:(g + 1) * hd] = _mm(p, vall_ref[...]).astype(BF16)


def _lat_gqa(qkv, cache_k, cache_v, cos, sin, q_g, k_g, layer):
    hd = HEAD_DIM
    gw = GQA_GROUP * hd
    nqb = DEC_SEQ // TQ
    cq, ck, cv = C_G_Q // gw, C_G_K // hd, C_G_V // hd
    cache = pl.BlockSpec((None, None, PAST_LEN, hd), lambda b, kv, qb: (b, layer, 0, kv))
    table = pl.BlockSpec((DEC_SEQ, hd), lambda b, kv, qb: (0, 0))
    vec = pl.BlockSpec((1, hd), lambda b, kv, qb: (0, 0))
    return pl.pallas_call(
        _lat_gqa_kernel,
        grid=(DEC_BATCH, GQA_KV_HEADS, nqb),
        in_specs=[
            pl.BlockSpec((TQ, gw), lambda b, kv, qb: (b * nqb + qb, cq + kv)),
            pl.BlockSpec((DEC_SEQ, hd), lambda b, kv, qb: (b, ck + kv)),
            pl.BlockSpec((DEC_SEQ, hd), lambda b, kv, qb: (b, cv + kv)),
            cache, cache, table, table, vec, vec,
        ],
        out_specs=pl.BlockSpec((TQ, gw), lambda b, kv, qb: (b * nqb + qb, kv)),
        out_shape=jax.ShapeDtypeStruct((DEC_BATCH * DEC_SEQ, GQA_QW), BF16),
        scratch_shapes=[pltpu.VMEM((N_KEYS, hd), BF16), pltpu.VMEM((N_KEYS, hd), BF16)],
        compiler_params=_params(("parallel", "parallel", "arbitrary")),
        name="lat_gqa",
    )(qkv, qkv, qkv, cache_k, cache_v, cos, sin, q_g, k_g)


def _lat_diff_kernel(q_ref, k_ref, v_ref, kc_ref, vc_ref, cos_ref, sin_ref, lam_ref, dg_ref,
                     o_ref, kall_ref, vall_ref, *, lambda_init):
    dscale = DIFF_QK_DIM ** -0.5
    half = DIFF_QK_DIM // 4
    qb = pl.program_id(2)

    @pl.when(qb == 0)
    def _():
        kf = _rope(k_ref[...], cos_ref[...], sin_ref[...], half)
        kall_ref[0:DEC_SEQ, :] = kf.astype(BF16)
        kall_ref[DEC_SEQ:N_KEYS, :] = kc_ref[...].astype(BF16)
        vall_ref[0:DEC_SEQ, :] = v_ref[...].astype(BF16)
        vall_ref[DEC_SEQ:N_KEYS, :] = vc_ref[...].astype(BF16)

    row0 = pl.multiple_of(qb * TQ, TQ)
    cos = cos_ref[pl.ds(row0, TQ), :]
    sin = sin_ref[pl.ds(row0, TQ), :]
    qf = _rope(q_ref[...], cos, sin, half)
    lane = lax.broadcasted_iota(jnp.int32, qf.shape, 1)
    q0 = jnp.where(lane < DIFF_QK_DIM, qf, 0.0).astype(BF16)
    q1 = jnp.where(lane >= DIFF_QK_DIM, qf, 0.0).astype(BF16)
    kall = kall_ref[...]
    p0 = _softmax(_nt(q0, kall) * dscale)
    p1 = _softmax(_nt(q1, kall) * dscale)
    lam = _diff_lambda(lam_ref, lambda_init)
    a = (p0 - lam * p1).astype(BF16)
    od = _rms(_mm(a, vall_ref[...]), dg_ref[...]) * (1.0 - lambda_init)
    o_ref[...] = od.astype(BF16)


def _lat_diff(qkv, cache_k, cache_v, cos, sin, lam_vec, diff_g, layer, lambda_init):
    hd = HEAD_DIM
    nqb = DEC_SEQ // TQ
    cq, ck, cv = C_D_Q // hd, C_D_K // hd, C_D_V // hd
    cache = pl.BlockSpec((None, None, PAST_LEN, hd), lambda b, h, qb: (b, layer, 0, h))
    table = pl.BlockSpec((DEC_SEQ, hd), lambda b, h, qb: (0, 0))
    return pl.pallas_call(
        functools.partial(_lat_diff_kernel, lambda_init=lambda_init),
        grid=(DEC_BATCH, DIFF_HEADS, nqb),
        in_specs=[
            pl.BlockSpec((TQ, hd), lambda b, h, qb: (b * nqb + qb, cq + h)),
            pl.BlockSpec((DEC_SEQ, hd), lambda b, h, qb: (b, ck + h)),
            pl.BlockSpec((DEC_SEQ, hd), lambda b, h, qb: (b, cv + h)),
            cache, cache, table, table,
            pl.BlockSpec((4, DIFF_QK_DIM), lambda b, h, qb: (0, 0)),
            pl.BlockSpec((1, hd), lambda b, h, qb: (0, 0)),
        ],
        out_specs=pl.BlockSpec((TQ, hd), lambda b, h, qb: (b * nqb + qb, h)),
        out_shape=jax.ShapeDtypeStruct((DEC_BATCH * DEC_SEQ, DIFF_W), BF16),
        scratch_shapes=[pltpu.VMEM((N_KEYS, hd), BF16), pltpu.VMEM((N_KEYS, hd), BF16)],
        compiler_params=_params(("parallel", "parallel", "arbitrary")),
        name="lat_diff",
    )(qkv, qkv, qkv, cache_k, cache_v, cos, sin, lam_vec, diff_g)


def _outproj_kernel(*refs, n_o):
    o_refs = refs[:n_o]
    w_ref, x_ref, g_ref, gt_ref, out_ref = refs[n_o:]
    y = None
    off = 0
    for o_ref in o_refs:
        wd = o_ref.shape[1]
        part = _mm(o_ref[...], w_ref[off:off + wd, :])
        y = part if y is None else y + part
        off += wd
    out_ref[...] = x_ref[...] + gt_ref[...] * _rms(y, g_ref[...])


def _outproj(o_parts, w_out, layer, x, g_post, mod_l, rows_per_batch, first_row):
    m = x.shape[0]
    row = _row_fn(TM_OUT, rows_per_batch, first_row)
    full = pl.BlockSpec((TM_OUT, D_MODEL), lambda i: (i, 0))
    return pl.pallas_call(
        functools.partial(_outproj_kernel, n_o=len(o_parts)),
        grid=(m // TM_OUT,),
        in_specs=[pl.BlockSpec((TM_OUT, o.shape[1]), lambda i: (i, 0)) for o in o_parts] + [
            pl.BlockSpec((None, D_MIX, D_MODEL), lambda i: (layer, 0, 0)),
            full,
            pl.BlockSpec((1, D_MODEL), lambda i: (0, 0)),
            _mod_spec(2, row),
        ],
        out_specs=full,
        out_shape=jax.ShapeDtypeStruct((m, D_MODEL), F32),
        compiler_params=_params(("parallel",)),
        name="outproj",
    )(*o_parts, w_out, x, g_post, mod_l)


def _mlp_kernel(x_ref, gpre_ref, sc_ref, sh_ref, wup_ref, wdn_ref, gpost_ref, gt_ref,
                out_ref, h_ref, acc_ref):
    k = pl.program_id(1)

    @pl.when(k == 0)
    def _():
        h = _rms(x_ref[...], gpre_ref[...]) * (1.0 + sc_ref[...]) + sh_ref[...]
        h_ref[...] = h.astype(BF16)
        acc_ref[...] = jnp.zeros_like(acc_ref)

    u = _mm(h_ref[...], wup_ref[...])
    a = jnp.square(jnp.maximum(u, 0.0)).astype(BF16)
    acc_ref[...] += _mm(a, wdn_ref[...])

    @pl.when(k == pl.num_programs(1) - 1)
    def _():
        out_ref[...] = x_ref[...] + gt_ref[...] * _rms(acc_ref[...], gpost_ref[...])


def _mlp(x, g_pre, g_post, mod_l, w_up, w_down, layer, rows_per_batch, first_row):
    m = x.shape[0]
    row = _row_fn(TM_MLP, rows_per_batch, first_row)
    full = pl.BlockSpec((TM_MLP, D_MODEL), lambda i, k: (i, 0))
    vec = pl.BlockSpec((1, D_MODEL), lambda i, k: (0, 0))
    return pl.pallas_call(
        _mlp_kernel,
        grid=(m // TM_MLP, D_FF // TF_MLP),
        in_specs=[
            full, vec, _mod_spec(4, row), _mod_spec(3, row),
            pl.BlockSpec((None, D_MODEL, TF_MLP), lambda i, k: (layer, 0, k)),
            pl.BlockSpec((None, TF_MLP, D_MODEL), lambda i, k: (layer, k, 0)),
            vec, _mod_spec(5, row),
        ],
        out_specs=full,
        out_shape=jax.ShapeDtypeStruct((m, D_MODEL), F32),
        scratch_shapes=[pltpu.VMEM((TM_MLP, D_MODEL), BF16), pltpu.VMEM((TM_MLP, D_MODEL), F32)],
        compiler_params=_params(("parallel", "arbitrary")),
        name="mlp",
    )(x, g_pre, mod_l, mod_l, w_up, w_down, g_post, mod_l)


def _rope_tables(half, n_rep):
    t = jnp.arange(DEC_SEQ)
    inv = ROPE_THETA ** (-jnp.arange(half, dtype=F32) / half)

    def cs(pos):
        ang = pos.astype(F32)[:, None] * inv[None, :]
        c, s = jnp.cos(ang), jnp.sin(ang)
        return jnp.concatenate([c, c], axis=-1), jnp.concatenate([-s, s], axis=-1)

    cr, sr = cs(t // GRID_W)
    cc, sc = cs(t % GRID_W)
    return (jnp.concatenate([cr, cc] * n_rep, axis=-1),
            jnp.concatenate([sr, sc] * n_rep, axis=-1))


def _na_row_offset(r):
    r0 = min(max(r - NA_KH // 2, 0), GRID_ROWS - NA_KH)
    return r0, r0 - r + NA_KH - 1


def _na_bias(rpb):
    qc = np.arange(GRID_W)[:, None]
    kc = np.arange(GRID_W)[None, :]
    ws = np.clip(qc - NA_KW // 2, 0, GRID_W - NA_KW)
    valid = (kc >= ws) & (kc < ws + NA_KW)
    dcol = np.clip(kc - qc + NA_KW - 1, 0, 2 * NA_KW - 2)
    n_dcol = 2 * NA_KW - 1
    onehot = (dcol.reshape(-1)[None, :] == np.arange(n_dcol)[:, None]).astype(np.float32)
    t = jnp.einsum("lhdc,cq->lhdq", rpb.astype(F32), jnp.asarray(onehot),
                   precision=lax.Precision.HIGHEST)
    t = t.reshape(DEPTH, NA_HEADS, 2 * NA_KH - 1, GRID_W, GRID_W)
    t = jnp.where(jnp.asarray(valid)[None, None, None], t, NEG_BIG)
    slabs = [t[:, :, d0:d0 + NA_KH].transpose(0, 1, 3, 2, 4).reshape(
        DEPTH, NA_HEADS, GRID_W, NA_SLAB) for d0 in range(NA_KH)]
    return jnp.stack(slabs, axis=2)


def kernel(x_prompt, x_sample, c, cache_na_k, cache_na_v, cache_gqa_k, cache_gqa_v,
           cache_diff_k, cache_diff_v, c_ctx, w_ada, b_ada, norm_g, w_in, w_out, na_rpb,
           gqa_q_g, gqa_k_g, diff_lam, diff_g, w_up, w_down):
    np_rows = BATCH * SEQ
    ns_rows = DEC_BATCH * DEC_SEQ
    xp = x_prompt.reshape(np_rows, D_MODEL)
    xs = x_sample.reshape(ns_rows, D_MODEL)

    cv = jnp.concatenate(
        [c_ctx[None, :], c, jnp.zeros((MOD_ROWS - 1 - DEC_BATCH, D_MODEL), F32)], axis=0)
    mod = _modulation(cv, w_ada, b_ada).reshape(DEPTH, MOD_ROWS, N_MOD, 1, D_MODEL)

    w_in_b = w_in.astype(BF16)
    w_out_b = w_out.astype(BF16)
    w_up_b = w_up.astype(BF16)
    w_down_b = w_down.astype(BF16)

    def flat_cache(a):
        return a.reshape(DEC_BATCH, DEPTH, PAST_LEN, a.shape[3] * HEAD_DIM)

    c_na_k, c_na_v = flat_cache(cache_na_k), flat_cache(cache_na_v)
    c_g_k, c_g_v = flat_cache(cache_gqa_k), flat_cache(cache_gqa_v)
    c_d_k, c_d_v = flat_cache(cache_diff_k), flat_cache(cache_diff_v)

    cos_g, sin_g = _rope_tables(HEAD_DIM // 4, 1)
    cos_d, sin_d = _rope_tables(DIFF_QK_DIM // 4, 2)
    na_bias = _na_bias(na_rpb)

    new_kv = [[] for _ in range(6)]
    for l in range(DEPTH):
        lambda_init = 0.8 - 0.6 * math.exp(-0.3 * l)
        mod_l = mod[l]
        g = norm_g[l].reshape(4, 1, D_MODEL)
        q_g = gqa_q_g[l].reshape(1, HEAD_DIM)
        k_g = gqa_k_g[l].reshape(1, HEAD_DIM)
        d_g = diff_g[l].reshape(1, HEAD_DIM)
        lam_vec = diff_lam[l]

        qkv_p = _inproj(xp, g[0], mod_l, w_in_b, l, None, 0)
        o_p, *kv = _ctx_attn(qkv_p, q_g, k_g, lam_vec, d_g, lambda_init)
        for i in range(6):
            new_kv[i].append(kv[i])
        xp = _outproj([o_p], w_out_b, l, xp, g[1], mod_l, None, 0)
        xp = _mlp(xp, g[2], g[3], mod_l, w_up_b, w_down_b, l, None, 0)

        qkv_s = _inproj(xs, g[0], mod_l, w_in_b, l, DEC_SEQ, 1)
        o_na = _lat_na(qkv_s, c_na_k, c_na_v, na_bias, l)
        o_gqa = _lat_gqa(qkv_s, c_g_k, c_g_v, cos_g, sin_g, q_g, k_g, l)
        o_diff = _lat_diff(qkv_s, c_d_k, c_d_v, cos_d, sin_d, lam_vec, d_g, l, lambda_init)
        xs = _outproj([o_na, o_gqa, o_diff], w_out_b, l, xs, g[1], mod_l, DEC_SEQ, 1)
        xs = _mlp(xs, g[2], g[3], mod_l, w_up_b, w_down_b, l, DEC_SEQ, 1)

    def stack(parts, heads):
        return jnp.stack([p.reshape(BATCH, SEQ, heads, HEAD_DIM) for p in parts], axis=1)

    return (xp.reshape(BATCH, SEQ, D_MODEL), xs.reshape(DEC_BATCH, DEC_SEQ, D_MODEL),
            stack(new_kv[0], NA_HEADS), stack(new_kv[1], NA_HEADS),
            stack(new_kv[2], GQA_KV_HEADS), stack(new_kv[3], GQA_KV_HEADS),
            stack(new_kv[4], DIFF_HEADS), stack(new_kv[5], DIFF_HEADS))
```

```python
import functools
import math

import jax
import jax.numpy as jnp
import numpy as np
from jax import lax
from jax.experimental import pallas as pl
from jax.experimental.pallas import tpu as pltpu

D_MODEL = 2048
BATCH = 16
SEQ = 256
DEPTH = 4
DEC_BATCH = 8
DEC_SEQ = 1024
PAST_LEN = 256
GRID_W = 64
GRID_ROWS = DEC_SEQ // GRID_W
HEAD_DIM = 128
NA_HEADS = 4
GQA_Q_HEADS = 8
GQA_KV_HEADS = 2
GQA_GROUP = GQA_Q_HEADS // GQA_KV_HEADS
DIFF_HEADS = 4
DIFF_QK_DIM = HEAD_DIM // 2
NA_KH = 8
NA_KW = 16
D_FF = 4 * D_MODEL
ROPE_THETA = 10000.0
EPS = 1e-6
N_MOD = 6
NEG_BIG = -1e30

NA_W = NA_HEADS * HEAD_DIM
GQA_QW = GQA_Q_HEADS * HEAD_DIM
GQA_KVW = GQA_KV_HEADS * HEAD_DIM
DIFF_W = DIFF_HEADS * HEAD_DIM
D_IN = 3 * NA_W + GQA_QW + 2 * GQA_KVW + 3 * DIFF_W
D_MIX = NA_W + GQA_QW + DIFF_W
C_NA_Q, C_NA_K, C_NA_V = 0, NA_W, 2 * NA_W
C_G_Q = 3 * NA_W
C_G_K = C_G_Q + GQA_QW
C_G_V = C_G_K + GQA_KVW
C_D_Q = C_G_V + GQA_KVW
C_D_K = C_D_Q + DIFF_W
C_D_V = C_D_K + DIFF_W

N_KEYS = DEC_SEQ + PAST_LEN
LOG2E = 1.4426950408889634
QSCALE = HEAD_DIM ** -0.5 * LOG2E
DIFF_QSCALE = DIFF_QK_DIM ** -0.5 * LOG2E

NA_QROWS = 4
NA_QTOK = NA_QROWS * GRID_W
NA_GROUPS = GRID_ROWS // NA_QROWS
NA_SLAB_ROWS = 12
NA_SLAB = NA_SLAB_ROWS * GRID_W
NA_NKEY = NA_SLAB + PAST_LEN
NA_SLAB_START = (0, 0, 4, 4)
MOD_ROWS = 16

F32 = jnp.float32
BF16 = jnp.bfloat16

VMEM_LIMIT = 52 * 1024 * 1024

TM_IN = 1024
TN_IN = 512
TM_OUT = 512
TM_MLP = 512
TF_MLP = 512
TN_ADA = 1024
TQ = 256


def _params(sem):
    return pltpu.CompilerParams(dimension_semantics=sem, vmem_limit_bytes=VMEM_LIMIT)


def _rms(x, g):
    ms = jnp.mean(x * x, axis=-1, keepdims=True)
    return x * lax.rsqrt(ms + EPS) * g


def _nt(a, b):
    return lax.dot_general(a, b, (((1,), (1,)), ((), ())), preferred_element_type=F32)


def _mm(a, b):
    return jnp.dot(a, b, preferred_element_type=F32)


def _scores_exp(q, k, bias=None):
    s = _nt(q, k)
    if bias is not None:
        s = s + bias
    return jnp.exp2(s - jnp.max(s, axis=-1, keepdims=True)).astype(BF16)


def _weighted(e, v_ones):
    oa = _mm(e, v_ones)
    d = v_ones.shape[1] // 2
    return oa[:, :d] / oa[:, d:]


def _with_ones(v):
    return jnp.concatenate([v, jnp.ones_like(v)], axis=1)


def _rope(x, cos, sin_signed, half):
    n = x.shape[-1]
    lane = lax.broadcasted_iota(jnp.int32, x.shape, 1)
    first = (lane % (2 * half)) < half
    rot = jnp.where(first, pltpu.roll(x, n - half, 1), pltpu.roll(x, half, 1))
    return x * cos + rot * sin_signed


def _diff_lambda(lam_ref, lambda_init):
    lf = lam_ref[...]
    a = jnp.sum(lf[0:1] * lf[1:2], axis=-1, keepdims=True)
    b = jnp.sum(lf[2:3] * lf[3:4], axis=-1, keepdims=True)
    return jnp.exp(a) - jnp.exp(b) + lambda_init


def _mod_kernel(cv_ref, w_ref, b_ref, o_ref):
    cv = cv_ref[...]
    s = cv / (1.0 + jnp.exp(-cv))
    o_ref[...] = _mm(s.astype(BF16), w_ref[...].astype(BF16)) + b_ref[...]


def _modulation(cv, w_ada, b_ada):
    n = N_MOD * D_MODEL
    return pl.pallas_call(
        _mod_kernel,
        grid=(DEPTH, n // TN_ADA),
        in_specs=[
            pl.BlockSpec((MOD_ROWS, D_MODEL), lambda l, j: (0, 0)),
            pl.BlockSpec((None, D_MODEL, TN_ADA), lambda l, j: (l, 0, j)),
            pl.BlockSpec((None, 1, TN_ADA), lambda l, j: (l, 0, j)),
        ],
        out_specs=pl.BlockSpec((None, MOD_ROWS, TN_ADA), lambda l, j: (l, 0, j)),
        out_shape=jax.ShapeDtypeStruct((DEPTH, MOD_ROWS, n), F32),
        compiler_params=_params(("parallel", "parallel")),
        name="modulation",
    )(cv, w_ada, b_ada.reshape(DEPTH, 1, n))


def _mod_spec(chunk, row_fn):
    return pl.BlockSpec((None, None, 1, D_MODEL), lambda i, *_: (row_fn(i), chunk, 0, 0))


def _row_fn(tm, rows_per_batch, first_row):
    if rows_per_batch is None:
        return lambda i: first_row
    return lambda i: first_row + (i * tm) // rows_per_batch


def _inproj_kernel(x_ref, g_ref, sc_ref, sh_ref, w_ref, o_ref, h_ref):
    @pl.when(pl.program_id(1) == 0)
    def _():
        h = _rms(x_ref[...], g_ref[...]) * (1.0 + sc_ref[...]) + sh_ref[...]
        h_ref[...] = h.astype(BF16)

    o_ref[...] = _mm(h_ref[...], w_ref[...])


def _inproj(x, g_pre, mod_l, w_in, layer, rows_per_batch, first_row):
    m = x.shape[0]
    row = _row_fn(TM_IN, rows_per_batch, first_row)
    return pl.pallas_call(
        _inproj_kernel,
        grid=(m // TM_IN, D_IN // TN_IN),
        in_specs=[
            pl.BlockSpec((TM_IN, D_MODEL), lambda i, j: (i, 0)),
            pl.BlockSpec((1, D_MODEL), lambda i, j: (0, 0)),
            _mod_spec(1, row),
            _mod_spec(0, row),
            pl.BlockSpec((None, D_MODEL, TN_IN), lambda i, j: (layer, 0, j)),
        ],
        out_specs=pl.BlockSpec((TM_IN, TN_IN), lambda i, j: (i, j)),
        out_shape=jax.ShapeDtypeStruct((m, D_IN), F32),
        scratch_shapes=[pltpu.VMEM((TM_IN, D_MODEL), BF16)],
        compiler_params=_params(("parallel", "arbitrary")),
        name="inproj",
    )(x, g_pre, mod_l, mod_l, w_in)


def _ctx_attn_kernel(qkv_ref, qg_ref, kg_ref, lam_ref, dg_ref,
                     o_ref, nak_ref, nav_ref, gk_ref, gv_ref, dk_ref, dv_ref, *, lambda_init):
    hd = HEAD_DIM

    def cols(c0, h):
        return qkv_ref[:, c0 + h * hd:c0 + (h + 1) * hd]

    nak_ref[...] = qkv_ref[:, C_NA_K:C_NA_K + NA_W]
    nav_ref[...] = qkv_ref[:, C_NA_V:C_NA_V + NA_W]
    gv_ref[...] = qkv_ref[:, C_G_V:C_G_V + GQA_KVW]
    dk_ref[...] = qkv_ref[:, C_D_K:C_D_K + DIFF_W]
    dv_ref[...] = qkv_ref[:, C_D_V:C_D_V + DIFF_W]

    for h in range(NA_HEADS):
        q = (cols(C_NA_Q, h) * QSCALE).astype(BF16)
        k = cols(C_NA_K, h).astype(BF16)
        v1 = _with_ones(cols(C_NA_V, h).astype(BF16))
        o_ref[:, h * hd:(h + 1) * hd] = _weighted(_scores_exp(q, k), v1).astype(BF16)

    for kv in range(GQA_KV_HEADS):
        kf = _rms(cols(C_G_K, kv), kg_ref[...])
        gk_ref[:, kv * hd:(kv + 1) * hd] = kf
        k = kf.astype(BF16)
        v1 = _with_ones(cols(C_G_V, kv).astype(BF16))
        for g in range(GQA_GROUP):
            hq = kv * GQA_GROUP + g
            q = (_rms(cols(C_G_Q, hq), qg_ref[...]) * QSCALE).astype(BF16)
            o = _weighted(_scores_exp(q, k), v1)
            o_ref[:, NA_W + hq * hd:NA_W + (hq + 1) * hd] = o.astype(BF16)

    lam = _diff_lambda(lam_ref, lambda_init)
    lane = lax.broadcasted_iota(jnp.int32, (SEQ, hd), 1)
    for h in range(DIFF_HEADS):
        qf = cols(C_D_Q, h) * DIFF_QSCALE
        k = cols(C_D_K, h).astype(BF16)
        v1 = _with_ones(cols(C_D_V, h).astype(BF16))
        q0 = jnp.where(lane < DIFF_QK_DIM, qf, 0.0).astype(BF16)
        q1 = jnp.where(lane >= DIFF_QK_DIM, qf, 0.0).astype(BF16)
        od = _weighted(_scores_exp(q0, k), v1) - lam * _weighted(_scores_exp(q1, k), v1)
        od = _rms(od, dg_ref[...]) * (1.0 - lambda_init)
        c0 = NA_W + GQA_QW + h * hd
        o_ref[:, c0:c0 + hd] = od.astype(BF16)


def _ctx_attn(qkv, q_g, k_g, lam_vec, diff_g, lambda_init):
    n = qkv.shape[0]
    vec = pl.BlockSpec((1, HEAD_DIM), lambda b: (0, 0))

    def rows(w):
        return pl.BlockSpec((SEQ, w), lambda b: (b, 0))

    def sds(w, dt):
        return jax.ShapeDtypeStruct((n, w), dt)

    return pl.pallas_call(
        functools.partial(_ctx_attn_kernel, lambda_init=lambda_init),
        grid=(n // SEQ,),
        in_specs=[rows(D_IN), vec, vec,
                  pl.BlockSpec((4, DIFF_QK_DIM), lambda b: (0, 0)), vec],
        out_specs=[rows(D_MIX), rows(NA_W), rows(NA_W), rows(GQA_KVW), rows(GQA_KVW),
                   rows(DIFF_W), rows(DIFF_W)],
        out_shape=[sds(D_MIX, BF16), sds(NA_W, F32), sds(NA_W, F32), sds(GQA_KVW, F32),
                   sds(GQA_KVW, F32), sds(DIFF_W, F32), sds(DIFF_W, F32)],
        compiler_params=_params(("parallel",)),
        name="ctx_attn",
    )(qkv, q_g, k_g, lam_vec, diff_g)


def _lat_na_kernel(q_ref, k_ref, v_ref, kc_ref, vc_ref, bias_ref, o_ref, kbig_ref, vbig_ref):
    lat0, lat1 = PAST_LEN, PAST_LEN + DEC_SEQ
    kc = kc_ref[...].astype(BF16)
    vc = _with_ones(vc_ref[...].astype(BF16))
    kbig_ref[0:lat0, :] = kc
    kbig_ref[lat0:lat1, :] = k_ref[...].astype(BF16)
    kbig_ref[lat1:, :] = kc
    vbig_ref[0:lat0, :] = vc
    vbig_ref[lat0:lat1, :] = _with_ones(v_ref[...].astype(BF16))
    vbig_ref[lat1:, :] = vc
    for j in range(NA_GROUPS):
        s0 = NA_SLAB_START[j]
        w0 = 0 if s0 == 0 else PAST_LEN + s0 * GRID_W
        q = (q_ref[j * NA_QTOK:(j + 1) * NA_QTOK, :] * QSCALE).astype(BF16)
        e = _scores_exp(q, kbig_ref[w0:w0 + NA_NKEY, :], bias_ref[j])
        o = _weighted(e, vbig_ref[w0:w0 + NA_NKEY, :])
        o_ref[j * NA_QTOK:(j + 1) * NA_QTOK, :] = o.astype(BF16)


def _lat_na(qkv, cache_k, cache_v, bias, layer):
    hd = HEAD_DIM
    cq, ck, cv = C_NA_Q // hd, C_NA_K // hd, C_NA_V // hd
    cache = pl.BlockSpec((None, None, PAST_LEN, hd), lambda h, b: (b, layer, 0, h))
    nbig = DEC_SEQ + 2 * PAST_LEN
    return pl.pallas_call(
        _lat_na_kernel,
        grid=(NA_HEADS, DEC_BATCH),
        in_specs=[
            pl.BlockSpec((DEC_SEQ, hd), lambda h, b: (b, cq + h)),
            pl.BlockSpec((DEC_SEQ, hd), lambda h, b: (b, ck + h)),
            pl.BlockSpec((DEC_SEQ, hd), lambda h, b: (b, cv + h)),
            cache, cache,
            pl.BlockSpec((None, None, NA_GROUPS, NA_QTOK, NA_NKEY),
                         lambda h, b: (layer, h, 0, 0, 0)),
        ],
        out_specs=pl.BlockSpec((DEC_SEQ, hd), lambda h, b: (b, h)),
        out_shape=jax.ShapeDtypeStruct((DEC_BATCH * DEC_SEQ, NA_W), BF16),
        scratch_shapes=[pltpu.VMEM((nbig, hd), BF16), pltpu.VMEM((nbig, 2 * hd), BF16)],
        compiler_params=_params(("parallel", "parallel")),
        name="lat_na",
    )(qkv, qkv, qkv, cache_k, cache_v, bias)


def _fill_keys(kall_ref, vall_ref, k_lat, v_ref, kc_ref, vc_ref):
    kall_ref[0:DEC_SEQ, :] = k_lat.astype(BF16)
    kall_ref[DEC_SEQ:N_KEYS, :] = kc_ref[...].astype(BF16)
    vall_ref[0:DEC_SEQ, :] = _with_ones(v_ref[...].astype(BF16))
    vall_ref[DEC_SEQ:N_KEYS, :] = _with_ones(vc_ref[...].astype(BF16))


def _lat_gqa_kernel(q_ref, k_ref, v_ref, kc_ref, vc_ref, cos_ref, sin_ref, qg_ref, kg_ref,
                    o_ref, kall_ref, vall_ref):
    hd = HEAD_DIM
    half = HEAD_DIM // 4
    qb = pl.program_id(2)

    @pl.when(qb == 0)
    def _():
        kf = _rope(_rms(k_ref[...], kg_ref[...]), cos_ref[...], sin_ref[...], half)
        _fill_keys(kall_ref, vall_ref, kf, v_ref, kc_ref, vc_ref)

    row0 = pl.multiple_of(qb * TQ, TQ)
    cos = cos_ref[pl.ds(row0, TQ), :]
    sin = sin_ref[pl.ds(row0, TQ), :]
    for g in range(GQA_GROUP):
        qf = _rms(q_ref[:, g * hd:(g + 1) * hd], qg_ref[...])
        q = (_rope(qf, cos, sin, half) * QSCALE).astype(BF16)
        o = _weighted(_scores_exp(q, kall_ref[...]), vall_ref[...])
        o_ref[:, g * hd:(g + 1) * hd] = o.astype(BF16)


def _lat_gqa(qkv, cache_k, cache_v, cos, sin, q_g, k_g, layer):
    hd = HEAD_DIM
    gw = GQA_GROUP * hd
    nqb = DEC_SEQ // TQ
    cq, ck, cv = C_G_Q // gw, C_G_K // hd, C_G_V // hd
    cache = pl.BlockSpec((None, None, PAST_LEN, hd), lambda b, kv, qb: (b, layer, 0, kv))
    table = pl.BlockSpec((DEC_SEQ, hd), lambda b, kv, qb: (0, 0))
    vec = pl.BlockSpec((1, hd), lambda b, kv, qb: (0, 0))
    return pl.pallas_call(
        _lat_gqa_kernel,
        grid=(DEC_BATCH, GQA_KV_HEADS, nqb),
        in_specs=[
            pl.BlockSpec((TQ, gw), lambda b, kv, qb: (b * nqb + qb, cq + kv)),
            pl.BlockSpec((DEC_SEQ, hd), lambda b, kv, qb: (b, ck + kv)),
            pl.BlockSpec((DEC_SEQ, hd), lambda b, kv, qb: (b, cv + kv)),
            cache, cache, table, table, vec, vec,
        ],
        out_specs=pl.BlockSpec((TQ, gw), lambda b, kv, qb: (b * nqb + qb, kv)),
        out_shape=jax.ShapeDtypeStruct((DEC_BATCH * DEC_SEQ, GQA_QW), BF16),
        scratch_shapes=[pltpu.VMEM((N_KEYS, hd), BF16), pltpu.VMEM((N_KEYS, 2 * hd), BF16)],
        compiler_params=_params(("parallel", "parallel", "arbitrary")),
        name="lat_gqa",
    )(qkv, qkv, qkv, cache_k, cache_v, cos, sin, q_g, k_g)


def _lat_diff_kernel(q_ref, k_ref, v_ref, kc_ref, vc_ref, cos_ref, sin_ref, lam_ref, dg_ref,
                     o_ref, kall_ref, vall_ref, *, lambda_init):
    half = DIFF_QK_DIM // 4
    qb = pl.program_id(2)

    @pl.when(qb == 0)
    def _():
        kf = _rope(k_ref[...], cos_ref[...], sin_ref[...], half)
        _fill_keys(kall_ref, vall_ref, kf, v_ref, kc_ref, vc_ref)

    row0 = pl.multiple_of(qb * TQ, TQ)
    cos = cos_ref[pl.ds(row0, TQ), :]
    sin = sin_ref[pl.ds(row0, TQ), :]
    qf = _rope(q_ref[...], cos, sin, half) * DIFF_QSCALE
    lane = lax.broadcasted_iota(jnp.int32, qf.shape, 1)
    q0 = jnp.where(lane < DIFF_QK_DIM, qf, 0.0).astype(BF16)
    q1 = jnp.where(lane >= DIFF_QK_DIM, qf, 0.0).astype(BF16)
    lam = _diff_lambda(lam_ref, lambda_init)
    od = (_weighted(_scores_exp(q0, kall_ref[...]), vall_ref[...])
          - lam * _weighted(_scores_exp(q1, kall_ref[...]), vall_ref[...]))
    od = _rms(od, dg_ref[...]) * (1.0 - lambda_init)
    o_ref[...] = od.astype(BF16)


def _lat_diff(qkv, cache_k, cache_v, cos, sin, lam_vec, diff_g, layer, lambda_init):
    hd = HEAD_DIM
    nqb = DEC_SEQ // TQ
    cq, ck, cv = C_D_Q // hd, C_D_K // hd, C_D_V // hd
    cache = pl.BlockSpec((None, None, PAST_LEN, hd), lambda b, h, qb: (b, layer, 0, h))
    table = pl.BlockSpec((DEC_SEQ, hd), lambda b, h, qb: (0, 0))
    return pl.pallas_call(
        functools.partial(_lat_diff_kernel, lambda_init=lambda_init),
        grid=(DEC_BATCH, DIFF_HEADS, nqb),
        in_specs=[
            pl.BlockSpec((TQ, hd), lambda b, h, qb: (b * nqb + qb, cq + h)),
            pl.BlockSpec((DEC_SEQ, hd), lambda b, h, qb: (b, ck + h)),
            pl.BlockSpec((DEC_SEQ, hd), lambda b, h, qb: (b, cv + h)),
            cache, cache, table, table,
            pl.BlockSpec((4, DIFF_QK_DIM), lambda b, h, qb: (0, 0)),
            pl.BlockSpec((1, hd), lambda b, h, qb: (0, 0)),
        ],
        out_specs=pl.BlockSpec((TQ, hd), lambda b, h, qb: (b * nqb + qb, h)),
        out_shape=jax.ShapeDtypeStruct((DEC_BATCH * DEC_SEQ, DIFF_W), BF16),
        scratch_shapes=[pltpu.VMEM((N_KEYS, hd), BF16), pltpu.VMEM((N_KEYS, 2 * hd), BF16)],
        compiler_params=_params(("parallel", "parallel", "arbitrary")),
        name="lat_diff",
    )(qkv, qkv, qkv, cache_k, cache_v, cos, sin, lam_vec, diff_g)


def _outproj_kernel(*refs, n_o):
    o_refs = refs[:n_o]
    w_ref, x_ref, g_ref, gt_ref, out_ref = refs[n_o:]
    y = None
    off = 0
    for o_ref in o_refs:
        wd = o_ref.shape[1]
        part = _mm(o_ref[...], w_ref[off:off + wd, :])
        y = part if y is None else y + part
        off += wd
    out_ref[...] = x_ref[...] + gt_ref[...] * _rms(y, g_ref[...])


def _outproj(o_parts, w_out, layer, x, g_post, mod_l, rows_per_batch, first_row):
    m = x.shape[0]
    row = _row_fn(TM_OUT, rows_per_batch, first_row)
    full = pl.BlockSpec((TM_OUT, D_MODEL), lambda i: (i, 0))
    return pl.pallas_call(
        functools.partial(_outproj_kernel, n_o=len(o_parts)),
        grid=(m // TM_OUT,),
        in_specs=[pl.BlockSpec((TM_OUT, o.shape[1]), lambda i: (i, 0)) for o in o_parts] + [
            pl.BlockSpec((None, D_MIX, D_MODEL), lambda i: (layer, 0, 0)),
            full,
            pl.BlockSpec((1, D_MODEL), lambda i: (0, 0)),
            _mod_spec(2, row),
        ],
        out_specs=full,
        out_shape=jax.ShapeDtypeStruct((m, D_MODEL), F32),
        compiler_params=_params(("parallel",)),
        name="outproj",
    )(*o_parts, w_out, x, g_post, mod_l)


def _mlp_kernel(x_ref, gpre_ref, sc_ref, sh_ref, wup_ref, wdn_ref, gpost_ref, gt_ref,
                out_ref, h_ref, acc_ref):
    k = pl.program_id(1)

    @pl.when(k == 0)
    def _():
        h = _rms(x_ref[...], gpre_ref[...]) * (1.0 + sc_ref[...]) + sh_ref[...]
        h_ref[...] = h.astype(BF16)
        acc_ref[...] = jnp.zeros_like(acc_ref)

    u = _mm(h_ref[...], wup_ref[...])
    a = jnp.square(jnp.maximum(u, 0.0)).astype(BF16)
    acc_ref[...] += _mm(a, wdn_ref[...])

    @pl.when(k == pl.num_programs(1) - 1)
    def _():
        out_ref[...] = x_ref[...] + gt_ref[...] * _rms(acc_ref[...], gpost_ref[...])


def _mlp(x, g_pre, g_post, mod_l, w_up, w_down, layer, rows_per_batch, first_row):
    m = x.shape[0]
    row = _row_fn(TM_MLP, rows_per_batch, first_row)
    full = pl.BlockSpec((TM_MLP, D_MODEL), lambda i, k: (i, 0))
    vec = pl.BlockSpec((1, D_MODEL), lambda i, k: (0, 0))
    return pl.pallas_call(
        _mlp_kernel,
        grid=(m // TM_MLP, D_FF // TF_MLP),
        in_specs=[
            full, vec, _mod_spec(4, row), _mod_spec(3, row),
            pl.BlockSpec((None, D_MODEL, TF_MLP), lambda i, k: (layer, 0, k)),
            pl.BlockSpec((None, TF_MLP, D_MODEL), lambda i, k: (layer, k, 0)),
            vec, _mod_spec(5, row),
        ],
        out_specs=full,
        out_shape=jax.ShapeDtypeStruct((m, D_MODEL), F32),
        scratch_shapes=[pltpu.VMEM((TM_MLP, D_MODEL), BF16), pltpu.VMEM((TM_MLP, D_MODEL), F32)],
        compiler_params=_params(("parallel", "arbitrary")),
        name="mlp",
    )(x, g_pre, mod_l, mod_l, w_up, w_down, g_post, mod_l)


def _rope_tables(half, n_rep):
    t = jnp.arange(DEC_SEQ)
    inv = ROPE_THETA ** (-jnp.arange(half, dtype=F32) / half)

    def cs(pos):
        ang = pos.astype(F32)[:, None] * inv[None, :]
        c, s = jnp.cos(ang), jnp.sin(ang)
        return jnp.concatenate([c, c], axis=-1), jnp.concatenate([-s, s], axis=-1)

    cr, sr = cs(t // GRID_W)
    cc, sc = cs(t % GRID_W)
    return (jnp.concatenate([cr, cc] * n_rep, axis=-1),
            jnp.concatenate([sr, sc] * n_rep, axis=-1))


def _na_bias(rpb):
    qc = np.arange(GRID_W)[:, None]
    kc = np.arange(GRID_W)[None, :]
    ws = np.clip(qc - NA_KW // 2, 0, GRID_W - NA_KW)
    valid = (kc >= ws) & (kc < ws + NA_KW)
    dcol = np.clip(kc - qc + NA_KW - 1, 0, 2 * NA_KW - 2)
    n_dcol = 2 * NA_KW - 1
    onehot = (dcol.reshape(-1)[None, :] == np.arange(n_dcol)[:, None]).astype(np.float32)
    t = jnp.einsum("lhdc,cq->lhdq", rpb.astype(F32), jnp.asarray(onehot),
                   precision=lax.Precision.HIGHEST)
    t = t.reshape(DEPTH, NA_HEADS, 2 * NA_KH - 1, GRID_W, GRID_W) * LOG2E
    t = jnp.where(jnp.asarray(valid)[None, None, None], t, NEG_BIG)
    masked = jnp.full((DEPTH, NA_HEADS, GRID_W, GRID_W), NEG_BIG, F32)
    ctx = jnp.zeros((DEPTH, NA_HEADS, NA_QTOK, PAST_LEN), F32)
    groups = []
    for j in range(NA_GROUPS):
        s0 = NA_SLAB_START[j]
        q_rows = []
        for r in range(j * NA_QROWS, (j + 1) * NA_QROWS):
            r0 = min(max(r - NA_KH // 2, 0), GRID_ROWS - NA_KH)
            blocks = []
            for key_row in range(s0, s0 + NA_SLAB_ROWS):
                in_window = r0 <= key_row < r0 + NA_KH
                blocks.append(t[:, :, key_row - r + NA_KH - 1] if in_window else masked)
            q_rows.append(jnp.concatenate(blocks, axis=-1))
        slab = jnp.concatenate(q_rows, axis=-2)
        groups.append(jnp.concatenate([ctx, slab] if s0 == 0 else [slab, ctx], axis=-1))
    return jnp.stack(groups, axis=2)


def kernel(x_prompt, x_sample, c, cache_na_k, cache_na_v, cache_gqa_k, cache_gqa_v,
           cache_diff_k, cache_diff_v, c_ctx, w_ada, b_ada, norm_g, w_in, w_out, na_rpb,
           gqa_q_g, gqa_k_g, diff_lam, diff_g, w_up, w_down):
    np_rows = BATCH * SEQ
    ns_rows = DEC_BATCH * DEC_SEQ
    xp = x_prompt.reshape(np_rows, D_MODEL)
    xs = x_sample.reshape(ns_rows, D_MODEL)

    cv = jnp.concatenate(
        [c_ctx[None, :], c, jnp.zeros((MOD_ROWS - 1 - DEC_BATCH, D_MODEL), F32)], axis=0)
    mod = _modulation(cv, w_ada, b_ada).reshape(DEPTH, MOD_ROWS, N_MOD, 1, D_MODEL)

    w_in_b = w_in.astype(BF16)
    w_out_b = w_out.astype(BF16)
    w_up_b = w_up.astype(BF16)
    w_down_b = w_down.astype(BF16)

    def flat_cache(a):
        return a.reshape(DEC_BATCH, DEPTH, PAST_LEN, a.shape[3] * HEAD_DIM)

    c_na_k, c_na_v = flat_cache(cache_na_k), flat_cache(cache_na_v)
    c_g_k, c_g_v = flat_cache(cache_gqa_k), flat_cache(cache_gqa_v)
    c_d_k, c_d_v = flat_cache(cache_diff_k), flat_cache(cache_diff_v)

    cos_g, sin_g = _rope_tables(HEAD_DIM // 4, 1)
    cos_d, sin_d = _rope_tables(DIFF_QK_DIM // 4, 2)
    na_bias = _na_bias(na_rpb)

    new_kv = [[] for _ in range(6)]
    for l in range(DEPTH):
        lambda_init = 0.8 - 0.6 * math.exp(-0.3 * l)
        mod_l = mod[l]
        g = norm_g[l].reshape(4, 1, D_MODEL)
        q_g = gqa_q_g[l].reshape(1, HEAD_DIM)
        k_g = gqa_k_g[l].reshape(1, HEAD_DIM)
        d_g = diff_g[l].reshape(1, HEAD_DIM)
        lam_vec = diff_lam[l]

        qkv_p = _inproj(xp, g[0], mod_l, w_in_b, l, None, 0)
        o_p, *kv = _ctx_attn(qkv_p, q_g, k_g, lam_vec, d_g, lambda_init)
        for i in range(6):
            new_kv[i].append(kv[i])
        xp = _outproj([o_p], w_out_b, l, xp, g[1], mod_l, None, 0)
        xp = _mlp(xp, g[2], g[3], mod_l, w_up_b, w_down_b, l, None, 0)

        qkv_s = _inproj(xs, g[0], mod_l, w_in_b, l, DEC_SEQ, 1)
        o_na = _lat_na(qkv_s, c_na_k, c_na_v, na_bias, l)
        o_gqa = _lat_gqa(qkv_s, c_g_k, c_g_v, cos_g, sin_g, q_g, k_g, l)
        o_diff = _lat_diff(qkv_s, c_d_k, c_d_v, cos_d, sin_d, lam_vec, d_g, l, lambda_init)
        xs = _outproj([o_na, o_gqa, o_diff], w_out_b, l, xs, g[1], mod_l, DEC_SEQ, 1)
        xs = _mlp(xs, g[2], g[3], mod_l, w_up_b, w_down_b, l, DEC_SEQ, 1)

    def stack(parts, heads):
        return jnp.stack([p.reshape(BATCH, SEQ, heads, HEAD_DIM) for p in parts], axis=1)

    return (xp.reshape(BATCH, SEQ, D_MODEL), xs.reshape(DEC_BATCH, DEC_SEQ, D_MODEL),
            stack(new_kv[0], NA_HEADS), stack(new_kv[1], NA_HEADS),
            stack(new_kv[2], GQA_KV_HEADS), stack(new_kv[3], GQA_KV_HEADS),
            stack(new_kv[4], DIFF_HEADS), stack(new_kv[5], DIFF_HEADS))
```

```python
import functools
import math

import jax
import jax.numpy as jnp
import numpy as np
from jax import lax
from jax.experimental import pallas as pl
from jax.experimental.pallas import tpu as pltpu

D_MODEL = 2048
BATCH = 16
SEQ = 256
DEPTH = 4
DEC_BATCH = 8
DEC_SEQ = 1024
PAST_LEN = 256
GRID_W = 64
GRID_ROWS = DEC_SEQ // GRID_W
HEAD_DIM = 128
NA_HEADS = 4
GQA_Q_HEADS = 8
GQA_KV_HEADS = 2
GQA_GROUP = GQA_Q_HEADS // GQA_KV_HEADS
DIFF_HEADS = 4
DIFF_QK_DIM = HEAD_DIM // 2
NA_KH = 8
NA_KW = 16
D_FF = 4 * D_MODEL
ROPE_THETA = 10000.0
EPS = 1e-6
N_MOD = 6
NEG_BIG = -1e30

NA_W = NA_HEADS * HEAD_DIM
GQA_QW = GQA_Q_HEADS * HEAD_DIM
GQA_KVW = GQA_KV_HEADS * HEAD_DIM
DIFF_W = DIFF_HEADS * HEAD_DIM
D_IN = 3 * NA_W + GQA_QW + 2 * GQA_KVW + 3 * DIFF_W
D_MIX = NA_W + GQA_QW + DIFF_W
C_NA_Q, C_NA_K, C_NA_V = 0, NA_W, 2 * NA_W
C_G_Q = 3 * NA_W
C_G_K = C_G_Q + GQA_QW
C_G_V = C_G_K + GQA_KVW
C_D_Q = C_G_V + GQA_KVW
C_D_K = C_D_Q + DIFF_W
C_D_V = C_D_K + DIFF_W

N_KEYS = DEC_SEQ + PAST_LEN
LOG2E = 1.4426950408889634
QSCALE = HEAD_DIM ** -0.5 * LOG2E
DIFF_QSCALE = DIFF_QK_DIM ** -0.5 * LOG2E

NA_QROWS = 4
NA_QTOK = NA_QROWS * GRID_W
NA_GROUPS = GRID_ROWS // NA_QROWS
NA_SLAB_ROWS = 12
NA_SLAB = NA_SLAB_ROWS * GRID_W
NA_NKEY = NA_SLAB + PAST_LEN
NA_SLAB_START = (0, 0, 4, 4)
MOD_ROWS = 16

F32 = jnp.float32
BF16 = jnp.bfloat16

VMEM_LIMIT = 52 * 1024 * 1024

TM_IN = 1024
TN_IN = 512
TM_OUT = 512
TM_MLP = 512
TF_MLP = 1024
TN_ADA = 1024
TQ = 256
NORM_ROWS = 128


def _params(sem):
    return pltpu.CompilerParams(dimension_semantics=sem, vmem_limit_bytes=VMEM_LIMIT)


def _rms(x, g):
    ms = jnp.mean(x * x, axis=-1, keepdims=True)
    return x * lax.rsqrt(ms + EPS) * g


def _nt(a, b):
    return lax.dot_general(a, b, (((1,), (1,)), ((), ())), preferred_element_type=F32)


def _mm(a, b):
    return jnp.dot(a, b, preferred_element_type=F32)


def _scores_exp(q, k, bias=None):
    s = _nt(q, k)
    if bias is not None:
        s = s + bias
    return jnp.exp2(s - jnp.max(s, axis=-1, keepdims=True)).astype(BF16)


def _weighted(e, v_ones):
    oa = _mm(e, v_ones)
    d = v_ones.shape[1] // 2
    return oa[:, :d] / oa[:, d:]


def _with_ones(v):
    return jnp.concatenate([v, jnp.ones_like(v)], axis=1)


def _rope(x, cos, sin_signed, half):
    n = x.shape[-1]
    lane = lax.broadcasted_iota(jnp.int32, x.shape, 1)
    first = (lane % (2 * half)) < half
    rot = jnp.where(first, pltpu.roll(x, n - half, 1), pltpu.roll(x, half, 1))
    return x * cos + rot * sin_signed


def _diff_lambda(lam_ref, lambda_init):
    lf = lam_ref[...]
    a = jnp.sum(lf[0:1] * lf[1:2], axis=-1, keepdims=True)
    b = jnp.sum(lf[2:3] * lf[3:4], axis=-1, keepdims=True)
    return jnp.exp(a) - jnp.exp(b) + lambda_init


def _mod_kernel(cv_ref, w_ref, b_ref, o_ref):
    cv = cv_ref[...]
    s = cv / (1.0 + jnp.exp(-cv))
    o_ref[...] = _mm(s.astype(BF16), w_ref[...].astype(BF16)) + b_ref[...]


def _modulation(cv, w_ada, b_ada):
    n = N_MOD * D_MODEL
    return pl.pallas_call(
        _mod_kernel,
        grid=(DEPTH, n // TN_ADA),
        in_specs=[
            pl.BlockSpec((MOD_ROWS, D_MODEL), lambda l, j: (0, 0)),
            pl.BlockSpec((None, D_MODEL, TN_ADA), lambda l, j: (l, 0, j)),
            pl.BlockSpec((None, 1, TN_ADA), lambda l, j: (l, 0, j)),
        ],
        out_specs=pl.BlockSpec((None, MOD_ROWS, TN_ADA), lambda l, j: (l, 0, j)),
        out_shape=jax.ShapeDtypeStruct((DEPTH, MOD_ROWS, n), F32),
        compiler_params=_params(("parallel", "parallel")),
        name="modulation",
    )(cv, w_ada, b_ada.reshape(DEPTH, 1, n))


def _mod_spec(chunk, row_fn):
    return pl.BlockSpec((None, None, 1, D_MODEL), lambda i, *_: (row_fn(i), chunk, 0, 0))


def _row_fn(tm, rows_per_batch, first_row):
    if rows_per_batch is None:
        return lambda i: first_row
    return lambda i: first_row + (i * tm) // rows_per_batch


def _row_chunks(n_rows, body):
    def step(c, carry):
        body(pl.ds(pl.multiple_of(c * NORM_ROWS, NORM_ROWS), NORM_ROWS))
        return carry

    lax.fori_loop(0, n_rows // NORM_ROWS, step, 0)


def _modulated_norm(h_ref, x_ref, g_ref, sc_ref, sh_ref, gm_ref):
    gm_ref[...] = g_ref[...] * (1.0 + sc_ref[...])

    def body(rows):
        x = x_ref[rows, :]
        r = lax.rsqrt(jnp.mean(x * x, axis=-1, keepdims=True) + EPS)
        h_ref[rows, :] = (x * r * gm_ref[...] + sh_ref[...]).astype(BF16)

    _row_chunks(x_ref.shape[0], body)


def _gated_norm_residual(out_ref, x_ref, y_ref, g_ref, gt_ref, gm_ref):
    gm_ref[...] = gt_ref[...] * g_ref[...]

    def body(rows):
        y = y_ref[rows, :]
        r = lax.rsqrt(jnp.mean(y * y, axis=-1, keepdims=True) + EPS)
        out_ref[rows, :] = x_ref[rows, :] + y * r * gm_ref[...]

    _row_chunks(x_ref.shape[0], body)


def _inproj_kernel(x_ref, g_ref, sc_ref, sh_ref, w_ref, o_ref, h_ref, gm_ref):
    @pl.when(pl.program_id(1) == 0)
    def _():
        _modulated_norm(h_ref, x_ref, g_ref, sc_ref, sh_ref, gm_ref)

    o_ref[...] = _mm(h_ref[...], w_ref[...])


def _inproj(x, g_pre, mod_l, w_in, layer, rows_per_batch, first_row):
    m = x.shape[0]
    row = _row_fn(TM_IN, rows_per_batch, first_row)
    return pl.pallas_call(
        _inproj_kernel,
        grid=(m // TM_IN, D_IN // TN_IN),
        in_specs=[
            pl.BlockSpec((TM_IN, D_MODEL), lambda i, j: (i, 0)),
            pl.BlockSpec((1, D_MODEL), lambda i, j: (0, 0)),
            _mod_spec(1, row),
            _mod_spec(0, row),
            pl.BlockSpec((None, D_MODEL, TN_IN), lambda i, j: (layer, 0, j)),
        ],
        out_specs=pl.BlockSpec((TM_IN, TN_IN), lambda i, j: (i, j)),
        out_shape=jax.ShapeDtypeStruct((m, D_IN), F32),
        scratch_shapes=[pltpu.VMEM((TM_IN, D_MODEL), BF16), pltpu.VMEM((1, D_MODEL), F32)],
        compiler_params=_params(("parallel", "arbitrary")),
        name="inproj",
    )(x, g_pre, mod_l, mod_l, w_in)


def _ctx_attn_kernel(qkv_ref, qg_ref, kg_ref, lam_ref, dg_ref,
                     o_ref, nak_ref, nav_ref, gk_ref, gv_ref, dk_ref, dv_ref, *, lambda_init):
    hd = HEAD_DIM

    def cols(c0, h):
        return qkv_ref[:, c0 + h * hd:c0 + (h + 1) * hd]

    nak_ref[...] = qkv_ref[:, C_NA_K:C_NA_K + NA_W]
    nav_ref[...] = qkv_ref[:, C_NA_V:C_NA_V + NA_W]
    gv_ref[...] = qkv_ref[:, C_G_V:C_G_V + GQA_KVW]
    dk_ref[...] = qkv_ref[:, C_D_K:C_D_K + DIFF_W]
    dv_ref[...] = qkv_ref[:, C_D_V:C_D_V + DIFF_W]

    for h in range(NA_HEADS):
        q = (cols(C_NA_Q, h) * QSCALE).astype(BF16)
        k = cols(C_NA_K, h).astype(BF16)
        v1 = _with_ones(cols(C_NA_V, h).astype(BF16))
        o_ref[:, h * hd:(h + 1) * hd] = _weighted(_scores_exp(q, k), v1).astype(BF16)

    for kv in range(GQA_KV_HEADS):
        kf = _rms(cols(C_G_K, kv), kg_ref[...])
        gk_ref[:, kv * hd:(kv + 1) * hd] = kf
        k = kf.astype(BF16)
        v1 = _with_ones(cols(C_G_V, kv).astype(BF16))
        for g in range(GQA_GROUP):
            hq = kv * GQA_GROUP + g
            q = (_rms(cols(C_G_Q, hq), qg_ref[...]) * QSCALE).astype(BF16)
            o = _weighted(_scores_exp(q, k), v1)
            o_ref[:, NA_W + hq * hd:NA_W + (hq + 1) * hd] = o.astype(BF16)

    lam = _diff_lambda(lam_ref, lambda_init)
    lane = lax.broadcasted_iota(jnp.int32, (SEQ, hd), 1)
    for h in range(DIFF_HEADS):
        qf = cols(C_D_Q, h) * DIFF_QSCALE
        k = cols(C_D_K, h).astype(BF16)
        v1 = _with_ones(cols(C_D_V, h).astype(BF16))
        q0 = jnp.where(lane < DIFF_QK_DIM, qf, 0.0).astype(BF16)
        q1 = jnp.where(lane >= DIFF_QK_DIM, qf, 0.0).astype(BF16)
        od = _weighted(_scores_exp(q0, k), v1) - lam * _weighted(_scores_exp(q1, k), v1)
        od = _rms(od, dg_ref[...]) * (1.0 - lambda_init)
        c0 = NA_W + GQA_QW + h * hd
        o_ref[:, c0:c0 + hd] = od.astype(BF16)


def _ctx_attn(qkv, q_g, k_g, lam_vec, diff_g, lambda_init):
    n = qkv.shape[0]
    vec = pl.BlockSpec((1, HEAD_DIM), lambda b: (0, 0))

    def rows(w):
        return pl.BlockSpec((SEQ, w), lambda b: (b, 0))

    def sds(w, dt):
        return jax.ShapeDtypeStruct((n, w), dt)

    return pl.pallas_call(
        functools.partial(_ctx_attn_kernel, lambda_init=lambda_init),
        grid=(n // SEQ,),
        in_specs=[rows(D_IN), vec, vec,
                  pl.BlockSpec((4, DIFF_QK_DIM), lambda b: (0, 0)), vec],
        out_specs=[rows(D_MIX), rows(NA_W), rows(NA_W), rows(GQA_KVW), rows(GQA_KVW),
                   rows(DIFF_W), rows(DIFF_W)],
        out_shape=[sds(D_MIX, BF16), sds(NA_W, F32), sds(NA_W, F32), sds(GQA_KVW, F32),
                   sds(GQA_KVW, F32), sds(DIFF_W, F32), sds(DIFF_W, F32)],
        compiler_params=_params(("parallel",)),
        name="ctx_attn",
    )(qkv, q_g, k_g, lam_vec, diff_g)


def _lat_na_kernel(q_ref, k_ref, v_ref, kc_ref, vc_ref, bias_ref, o_ref, kbig_ref, vbig_ref):
    lat0, lat1 = PAST_LEN, PAST_LEN + DEC_SEQ
    kc = kc_ref[...].astype(BF16)
    vc = _with_ones(vc_ref[...].astype(BF16))
    kbig_ref[0:lat0, :] = kc
    kbig_ref[lat0:lat1, :] = k_ref[...].astype(BF16)
    kbig_ref[lat1:, :] = kc
    vbig_ref[0:lat0, :] = vc
    vbig_ref[lat0:lat1, :] = _with_ones(v_ref[...].astype(BF16))
    vbig_ref[lat1:, :] = vc
    for j in range(NA_GROUPS):
        s0 = NA_SLAB_START[j]
        w0 = 0 if s0 == 0 else PAST_LEN + s0 * GRID_W
        q = (q_ref[j * NA_QTOK:(j + 1) * NA_QTOK, :] * QSCALE).astype(BF16)
        e = _scores_exp(q, kbig_ref[w0:w0 + NA_NKEY, :], bias_ref[j])
        o = _weighted(e, vbig_ref[w0:w0 + NA_NKEY, :])
        o_ref[j * NA_QTOK:(j + 1) * NA_QTOK, :] = o.astype(BF16)


def _lat_na(qkv, cache_k, cache_v, bias, layer):
    hd = HEAD_DIM
    cq, ck, cv = C_NA_Q // hd, C_NA_K // hd, C_NA_V // hd
    cache = pl.BlockSpec((None, None, PAST_LEN, hd), lambda h, b: (b, layer, 0, h))
    nbig = DEC_SEQ + 2 * PAST_LEN
    return pl.pallas_call(
        _lat_na_kernel,
        grid=(NA_HEADS, DEC_BATCH),
        in_specs=[
            pl.BlockSpec((DEC_SEQ, hd), lambda h, b: (b, cq + h)),
            pl.BlockSpec((DEC_SEQ, hd), lambda h, b: (b, ck + h)),
            pl.BlockSpec((DEC_SEQ, hd), lambda h, b: (b, cv + h)),
            cache, cache,
            pl.BlockSpec((None, None, NA_GROUPS, NA_QTOK, NA_NKEY),
                         lambda h, b: (layer, h, 0, 0, 0)),
        ],
        out_specs=pl.BlockSpec((DEC_SEQ, hd), lambda h, b: (b, h)),
        out_shape=jax.ShapeDtypeStruct((DEC_BATCH * DEC_SEQ, NA_W), BF16),
        scratch_shapes=[pltpu.VMEM((nbig, hd), BF16), pltpu.VMEM((nbig, 2 * hd), BF16)],
        compiler_params=_params(("parallel", "parallel")),
        name="lat_na",
    )(qkv, qkv, qkv, cache_k, cache_v, bias)


def _fill_keys(kall_ref, vall_ref, k_lat, v_ref, kc_ref, vc_ref):
    kall_ref[0:DEC_SEQ, :] = k_lat.astype(BF16)
    kall_ref[DEC_SEQ:N_KEYS, :] = kc_ref[...].astype(BF16)
    vall_ref[0:DEC_SEQ, :] = _with_ones(v_ref[...].astype(BF16))
    vall_ref[DEC_SEQ:N_KEYS, :] = _with_ones(vc_ref[...].astype(BF16))


def _lat_gqa_kernel(q_ref, k_ref, v_ref, kc_ref, vc_ref, cos_ref, sin_ref, qg_ref, kg_ref,
                    o_ref, kall_ref, vall_ref):
    hd = HEAD_DIM
    half = HEAD_DIM // 4
    qb = pl.program_id(2)

    @pl.when(qb == 0)
    def _():
        kf = _rope(_rms(k_ref[...], kg_ref[...]), cos_ref[...], sin_ref[...], half)
        _fill_keys(kall_ref, vall_ref, kf, v_ref, kc_ref, vc_ref)

    row0 = pl.multiple_of(qb * TQ, TQ)
    cos = cos_ref[pl.ds(row0, TQ), :]
    sin = sin_ref[pl.ds(row0, TQ), :]
    for g in range(GQA_GROUP):
        qf = _rms(q_ref[:, g * hd:(g + 1) * hd], qg_ref[...])
        q = (_rope(qf, cos, sin, half) * QSCALE).astype(BF16)
        o = _weighted(_scores_exp(q, kall_ref[...]), vall_ref[...])
        o_ref[:, g * hd:(g + 1) * hd] = o.astype(BF16)


def _lat_gqa(qkv, cache_k, cache_v, cos, sin, q_g, k_g, layer):
    hd = HEAD_DIM
    gw = GQA_GROUP * hd
    nqb = DEC_SEQ // TQ
    cq, ck, cv = C_G_Q // gw, C_G_K // hd, C_G_V // hd
    cache = pl.BlockSpec((None, None, PAST_LEN, hd), lambda b, kv, qb: (b, layer, 0, kv))
    table = pl.BlockSpec((DEC_SEQ, hd), lambda b, kv, qb: (0, 0))
    vec = pl.BlockSpec((1, hd), lambda b, kv, qb: (0, 0))
    return pl.pallas_call(
        _lat_gqa_kernel,
        grid=(DEC_BATCH, GQA_KV_HEADS, nqb),
        in_specs=[
            pl.BlockSpec((TQ, gw), lambda b, kv, qb: (b * nqb + qb, cq + kv)),
            pl.BlockSpec((DEC_SEQ, hd), lambda b, kv, qb: (b, ck + kv)),
            pl.BlockSpec((DEC_SEQ, hd), lambda b, kv, qb: (b, cv + kv)),
            cache, cache, table, table, vec, vec,
        ],
        out_specs=pl.BlockSpec((TQ, gw), lambda b, kv, qb: (b * nqb + qb, kv)),
        out_shape=jax.ShapeDtypeStruct((DEC_BATCH * DEC_SEQ, GQA_QW), BF16),
        scratch_shapes=[pltpu.VMEM((N_KEYS, hd), BF16), pltpu.VMEM((N_KEYS, 2 * hd), BF16)],
        compiler_params=_params(("parallel", "parallel", "arbitrary")),
        name="lat_gqa",
    )(qkv, qkv, qkv, cache_k, cache_v, cos, sin, q_g, k_g)


def _lat_diff_kernel(q_ref, k_ref, v_ref, kc_ref, vc_ref, cos_ref, sin_ref, lam_ref, dg_ref,
                     o_ref, kall_ref, vall_ref, *, lambda_init):
    half = DIFF_QK_DIM // 4
    qb = pl.program_id(2)

    @pl.when(qb == 0)
    def _():
        kf = _rope(k_ref[...], cos_ref[...], sin_ref[...], half)
        _fill_keys(kall_ref, vall_ref, kf, v_ref, kc_ref, vc_ref)

    row0 = pl.multiple_of(qb * TQ, TQ)
    cos = cos_ref[pl.ds(row0, TQ), :]
    sin = sin_ref[pl.ds(row0, TQ), :]
    qf = _rope(q_ref[...], cos, sin, half) * DIFF_QSCALE
    lane = lax.broadcasted_iota(jnp.int32, qf.shape, 1)
    q0 = jnp.where(lane < DIFF_QK_DIM, qf, 0.0).astype(BF16)
    q1 = jnp.where(lane >= DIFF_QK_DIM, qf, 0.0).astype(BF16)
    lam = _diff_lambda(lam_ref, lambda_init)
    od = (_weighted(_scores_exp(q0, kall_ref[...]), vall_ref[...])
          - lam * _weighted(_scores_exp(q1, kall_ref[...]), vall_ref[...]))
    od = _rms(od, dg_ref[...]) * (1.0 - lambda_init)
    o_ref[...] = od.astype(BF16)


def _lat_diff(qkv, cache_k, cache_v, cos, sin, lam_vec, diff_g, layer, lambda_init):
    hd = HEAD_DIM
    nqb = DEC_SEQ // TQ
    cq, ck, cv = C_D_Q // hd, C_D_K // hd, C_D_V // hd
    cache = pl.BlockSpec((None, None, PAST_LEN, hd), lambda b, h, qb: (b, layer, 0, h))
    table = pl.BlockSpec((DEC_SEQ, hd), lambda b, h, qb: (0, 0))
    return pl.pallas_call(
        functools.partial(_lat_diff_kernel, lambda_init=lambda_init),
        grid=(DEC_BATCH, DIFF_HEADS, nqb),
        in_specs=[
            pl.BlockSpec((TQ, hd), lambda b, h, qb: (b * nqb + qb, cq + h)),
            pl.BlockSpec((DEC_SEQ, hd), lambda b, h, qb: (b, ck + h)),
            pl.BlockSpec((DEC_SEQ, hd), lambda b, h, qb: (b, cv + h)),
            cache, cache, table, table,
            pl.BlockSpec((4, DIFF_QK_DIM), lambda b, h, qb: (0, 0)),
            pl.BlockSpec((1, hd), lambda b, h, qb: (0, 0)),
        ],
        out_specs=pl.BlockSpec((TQ, hd), lambda b, h, qb: (b * nqb + qb, h)),
        out_shape=jax.ShapeDtypeStruct((DEC_BATCH * DEC_SEQ, DIFF_W), BF16),
        scratch_shapes=[pltpu.VMEM((N_KEYS, hd), BF16), pltpu.VMEM((N_KEYS, 2 * hd), BF16)],
        compiler_params=_params(("parallel", "parallel", "arbitrary")),
        name="lat_diff",
    )(qkv, qkv, qkv, cache_k, cache_v, cos, sin, lam_vec, diff_g)


def _outproj_kernel(*refs, n_o):
    o_refs = refs[:n_o]
    w_ref, x_ref, g_ref, gt_ref, out_ref = refs[n_o:]
    y = None
    off = 0
    for o_ref in o_refs:
        wd = o_ref.shape[1]
        part = _mm(o_ref[...], w_ref[off:off + wd, :])
        y = part if y is None else y + part
        off += wd
    out_ref[...] = x_ref[...] + gt_ref[...] * _rms(y, g_ref[...])


def _outproj(o_parts, w_out, layer, x, g_post, mod_l, rows_per_batch, first_row):
    m = x.shape[0]
    row = _row_fn(TM_OUT, rows_per_batch, first_row)
    full = pl.BlockSpec((TM_OUT, D_MODEL), lambda i: (i, 0))
    return pl.pallas_call(
        functools.partial(_outproj_kernel, n_o=len(o_parts)),
        grid=(m // TM_OUT,),
        in_specs=[pl.BlockSpec((TM_OUT, o.shape[1]), lambda i: (i, 0)) for o in o_parts] + [
            pl.BlockSpec((None, D_MIX, D_MODEL), lambda i: (layer, 0, 0)),
            full,
            pl.BlockSpec((1, D_MODEL), lambda i: (0, 0)),
            _mod_spec(2, row),
        ],
        out_specs=full,
        out_shape=jax.ShapeDtypeStruct((m, D_MODEL), F32),
        compiler_params=_params(("parallel",)),
        name="outproj",
    )(*o_parts, w_out, x, g_post, mod_l)


def _mlp_kernel(x_ref, gpre_ref, sc_ref, sh_ref, wup_ref, wdn_ref, gpost_ref, gt_ref,
                out_ref, h_ref, acc_ref, gm_ref):
    k = pl.program_id(1)

    @pl.when(k == 0)
    def _():
        _modulated_norm(h_ref, x_ref, gpre_ref, sc_ref, sh_ref, gm_ref)
        acc_ref[...] = jnp.zeros_like(acc_ref)

    u = _mm(h_ref[...], wup_ref[...])
    a = jnp.square(jnp.maximum(u, 0.0)).astype(BF16)
    acc_ref[...] += _mm(a, wdn_ref[...])

    @pl.when(k == pl.num_programs(1) - 1)
    def _():
        _gated_norm_residual(out_ref, x_ref, acc_ref, gpost_ref, gt_ref, gm_ref)


def _mlp(x, g_pre, g_post, mod_l, w_up, w_down, layer, rows_per_batch, first_row):
    m = x.shape[0]
    row = _row_fn(TM_MLP, rows_per_batch, first_row)
    full = pl.BlockSpec((TM_MLP, D_MODEL), lambda i, k: (i, 0))
    vec = pl.BlockSpec((1, D_MODEL), lambda i, k: (0, 0))
    return pl.pallas_call(
        _mlp_kernel,
        grid=(m // TM_MLP, D_FF // TF_MLP),
        in_specs=[
            full, vec, _mod_spec(4, row), _mod_spec(3, row),
            pl.BlockSpec((None, D_MODEL, TF_MLP), lambda i, k: (layer, 0, k)),
            pl.BlockSpec((None, TF_MLP, D_MODEL), lambda i, k: (layer, k, 0)),
            vec, _mod_spec(5, row),
        ],
        out_specs=full,
        out_shape=jax.ShapeDtypeStruct((m, D_MODEL), F32),
        scratch_shapes=[pltpu.VMEM((TM_MLP, D_MODEL), BF16), pltpu.VMEM((TM_MLP, D_MODEL), F32),
                        pltpu.VMEM((1, D_MODEL), F32)],
        compiler_params=_params(("parallel", "arbitrary")),
        name="mlp",
    )(x, g_pre, mod_l, mod_l, w_up, w_down, g_post, mod_l)


def _rope_tables(half, n_rep):
    t = jnp.arange(DEC_SEQ)
    inv = ROPE_THETA ** (-jnp.arange(half, dtype=F32) / half)

    def cs(pos):
        ang = pos.astype(F32)[:, None] * inv[None, :]
        c, s = jnp.cos(ang), jnp.sin(ang)
        return jnp.concatenate([c, c], axis=-1), jnp.concatenate([-s, s], axis=-1)

    cr, sr = cs(t // GRID_W)
    cc, sc = cs(t % GRID_W)
    return (jnp.concatenate([cr, cc] * n_rep, axis=-1),
            jnp.concatenate([sr, sc] * n_rep, axis=-1))


def _na_bias(rpb):
    qc = np.arange(GRID_W)[:, None]
    kc = np.arange(GRID_W)[None, :]
    ws = np.clip(qc - NA_KW // 2, 0, GRID_W - NA_KW)
    valid = (kc >= ws) & (kc < ws + NA_KW)
    dcol = np.clip(kc - qc + NA_KW - 1, 0, 2 * NA_KW - 2)
    n_dcol = 2 * NA_KW - 1
    onehot = (dcol.reshape(-1)[None, :] == np.arange(n_dcol)[:, None]).astype(np.float32)
    t = jnp.einsum("lhdc,cq->lhdq", rpb.astype(F32), jnp.asarray(onehot),
                   precision=lax.Precision.HIGHEST)
    t = t.reshape(DEPTH, NA_HEADS, 2 * NA_KH - 1, GRID_W, GRID_W) * LOG2E
    t = jnp.where(jnp.asarray(valid)[None, None, None], t, NEG_BIG)
    masked = jnp.full((DEPTH, NA_HEADS, GRID_W, GRID_W), NEG_BIG, F32)
    ctx = jnp.zeros((DEPTH, NA_HEADS, NA_QTOK, PAST_LEN), F32)
    groups = []
    for j in range(NA_GROUPS):
        s0 = NA_SLAB_START[j]
        q_rows = []
        for r in range(j * NA_QROWS, (j + 1) * NA_QROWS):
            r0 = min(max(r - NA_KH // 2, 0), GRID_ROWS - NA_KH)
            blocks = []
            for key_row in range(s0, s0 + NA_SLAB_ROWS):
                in_window = r0 <= key_row < r0 + NA_KH
                blocks.append(t[:, :, key_row - r + NA_KH - 1] if in_window else masked)
            q_rows.append(jnp.concatenate(blocks, axis=-1))
        slab = jnp.concatenate(q_rows, axis=-2)
        groups.append(jnp.concatenate([ctx, slab] if s0 == 0 else [slab, ctx], axis=-1))
    return jnp.stack(groups, axis=2)


def kernel(x_prompt, x_sample, c, cache_na_k, cache_na_v, cache_gqa_k, cache_gqa_v,
           cache_diff_k, cache_diff_v, c_ctx, w_ada, b_ada, norm_g, w_in, w_out, na_rpb,
           gqa_q_g, gqa_k_g, diff_lam, diff_g, w_up, w_down):
    np_rows = BATCH * SEQ
    ns_rows = DEC_BATCH * DEC_SEQ
    xp = x_prompt.reshape(np_rows, D_MODEL)
    xs = x_sample.reshape(ns_rows, D_MODEL)

    cv = jnp.concatenate(
        [c_ctx[None, :], c, jnp.zeros((MOD_ROWS - 1 - DEC_BATCH, D_MODEL), F32)], axis=0)
    mod = _modulation(cv, w_ada, b_ada).reshape(DEPTH, MOD_ROWS, N_MOD, 1, D_MODEL)

    w_in_b = w_in.astype(BF16)
    w_out_b = w_out.astype(BF16)
    w_up_b = w_up.astype(BF16)
    w_down_b = w_down.astype(BF16)

    def flat_cache(a):
        return a.reshape(DEC_BATCH, DEPTH, PAST_LEN, a.shape[3] * HEAD_DIM)

    c_na_k, c_na_v = flat_cache(cache_na_k), flat_cache(cache_na_v)
    c_g_k, c_g_v = flat_cache(cache_gqa_k), flat_cache(cache_gqa_v)
    c_d_k, c_d_v = flat_cache(cache_diff_k), flat_cache(cache_diff_v)

    cos_g, sin_g = _rope_tables(HEAD_DIM // 4, 1)
    cos_d, sin_d = _rope_tables(DIFF_QK_DIM // 4, 2)
    na_bias = _na_bias(na_rpb)

    new_kv = [[] for _ in range(6)]
    for l in range(DEPTH):
        lambda_init = 0.8 - 0.6 * math.exp(-0.3 * l)
        mod_l = mod[l]
        g = norm_g[l].reshape(4, 1, D_MODEL)
        q_g = gqa_q_g[l].reshape(1, HEAD_DIM)
        k_g = gqa_k_g[l].reshape(1, HEAD_DIM)
        d_g = diff_g[l].reshape(1, HEAD_DIM)
        lam_vec = diff_lam[l]

        qkv_p = _inproj(xp, g[0], mod_l, w_in_b, l, None, 0)
        o_p, *kv = _ctx_attn(qkv_p, q_g, k_g, lam_vec, d_g, lambda_init)
        for i in range(6):
            new_kv[i].append(kv[i])
        xp = _outproj([o_p], w_out_b, l, xp, g[1], mod_l, None, 0)
        xp = _mlp(xp, g[2], g[3], mod_l, w_up_b, w_down_b, l, None, 0)

        qkv_s = _inproj(xs, g[0], mod_l, w_in_b, l, DEC_SEQ, 1)
        o_na = _lat_na(qkv_s, c_na_k, c_na_v, na_bias, l)
        o_gqa = _lat_gqa(qkv_s, c_g_k, c_g_v, cos_g, sin_g, q_g, k_g, l)
        o_diff = _lat_diff(qkv_s, c_d_k, c_d_v, cos_d, sin_d, lam_vec, d_g, l, lambda_init)
        xs = _outproj([o_na, o_gqa, o_diff], w_out_b, l, xs, g[1], mod_l, DEC_SEQ, 1)
        xs = _mlp(xs, g[2], g[3], mod_l, w_up_b, w_down_b, l, DEC_SEQ, 1)

    def stack(parts, heads):
        return jnp.stack([p.reshape(BATCH, SEQ, heads, HEAD_DIM) for p in parts], axis=1)

    return (xp.reshape(BATCH, SEQ, D_MODEL), xs.reshape(DEC_BATCH, DEC_SEQ, D_MODEL),
            stack(new_kv[0], NA_HEADS), stack(new_kv[1], NA_HEADS),
            stack(new_kv[2], GQA_KV_HEADS), stack(new_kv[3], GQA_KV_HEADS),
            stack(new_kv[4], DIFF_HEADS), stack(new_kv[5], DIFF_HEADS))
```

```python
import functools
import math

import jax
import jax.numpy as jnp
import numpy as np
from jax import lax
from jax.experimental import pallas as pl
from jax.experimental.pallas import tpu as pltpu

D_MODEL = 2048
BATCH = 16
SEQ = 256
DEPTH = 4
DEC_BATCH = 8
DEC_SEQ = 1024
PAST_LEN = 256
GRID_W = 64
GRID_ROWS = DEC_SEQ // GRID_W
HEAD_DIM = 128
NA_HEADS = 4
GQA_Q_HEADS = 8
GQA_KV_HEADS = 2
GQA_GROUP = GQA_Q_HEADS // GQA_KV_HEADS
DIFF_HEADS = 4
DIFF_QK_DIM = HEAD_DIM // 2
NA_KH = 8
NA_KW = 16
D_FF = 4 * D_MODEL
ROPE_THETA = 10000.0
EPS = 1e-6
N_MOD = 6
NEG_BIG = -1e30

NA_W = NA_HEADS * HEAD_DIM
GQA_QW = GQA_Q_HEADS * HEAD_DIM
GQA_KVW = GQA_KV_HEADS * HEAD_DIM
DIFF_W = DIFF_HEADS * HEAD_DIM
D_IN = 3 * NA_W + GQA_QW + 2 * GQA_KVW + 3 * DIFF_W
D_MIX = NA_W + GQA_QW + DIFF_W
C_NA_Q, C_NA_K, C_NA_V = 0, NA_W, 2 * NA_W
C_G_Q = 3 * NA_W
C_G_K = C_G_Q + GQA_QW
C_G_V = C_G_K + GQA_KVW
C_D_Q = C_G_V + GQA_KVW
C_D_K = C_D_Q + DIFF_W
C_D_V = C_D_K + DIFF_W

N_KEYS = DEC_SEQ + PAST_LEN
LOG2E = 1.4426950408889634
QSCALE = HEAD_DIM ** -0.5 * LOG2E
DIFF_QSCALE = DIFF_QK_DIM ** -0.5 * LOG2E

NA_QROWS = 4
NA_QTOK = NA_QROWS * GRID_W
NA_GROUPS = GRID_ROWS // NA_QROWS
NA_SLAB_ROWS = 12
NA_SLAB = NA_SLAB_ROWS * GRID_W
NA_NKEY = NA_SLAB + PAST_LEN
NA_SLAB_START = (0, 0, 4, 4)
MOD_ROWS = 16

F32 = jnp.float32
BF16 = jnp.bfloat16

VMEM_LIMIT = 52 * 1024 * 1024

TM_IN = 1024
TN_IN = 512
TM_OUT = 512
TM_MLP = 512
TF_MLP = 1024
TN_ADA = 1024
TQ = 512
ATT_ROWS = 256
NORM_ROWS = 128


def _params(sem):
    return pltpu.CompilerParams(dimension_semantics=sem, vmem_limit_bytes=VMEM_LIMIT)


def _rms(x, g):
    ms = jnp.mean(x * x, axis=-1, keepdims=True)
    return x * lax.rsqrt(ms + EPS) * g


def _nt(a, b):
    return lax.dot_general(a, b, (((1,), (1,)), ((), ())), preferred_element_type=F32)


def _mm(a, b):
    return jnp.dot(a, b, preferred_element_type=F32)


def _scores_exp(q, k, bias=None):
    s = _nt(q, k)
    if bias is not None:
        s = s + bias
    return jnp.exp2(s - jnp.max(s, axis=-1, keepdims=True)).astype(BF16)


def _weighted(e, v_ones):
    oa = _mm(e, v_ones)
    d = v_ones.shape[1] // 2
    return oa[:, :d] / oa[:, d:]


def _with_ones(v):
    return jnp.concatenate([v, jnp.ones_like(v)], axis=1)


def _rope(x, cos, sin_signed, half):
    n = x.shape[-1]
    lane = lax.broadcasted_iota(jnp.int32, x.shape, 1)
    first = (lane % (2 * half)) < half
    rot = jnp.where(first, pltpu.roll(x, n - half, 1), pltpu.roll(x, half, 1))
    return x * cos + rot * sin_signed


def _diff_lambda(lam_ref, lambda_init):
    lf = lam_ref[...]
    a = jnp.sum(lf[0:1] * lf[1:2], axis=-1, keepdims=True)
    b = jnp.sum(lf[2:3] * lf[3:4], axis=-1, keepdims=True)
    return jnp.exp(a) - jnp.exp(b) + lambda_init


def _mod_kernel(cv_ref, w_ref, b_ref, o_ref):
    cv = cv_ref[...]
    s = cv / (1.0 + jnp.exp(-cv))
    o_ref[...] = _mm(s.astype(BF16), w_ref[...].astype(BF16)) + b_ref[...]


def _modulation(cv, w_ada, b_ada):
    n = N_MOD * D_MODEL
    return pl.pallas_call(
        _mod_kernel,
        grid=(DEPTH, n // TN_ADA),
        in_specs=[
            pl.BlockSpec((MOD_ROWS, D_MODEL), lambda l, j: (0, 0)),
            pl.BlockSpec((None, D_MODEL, TN_ADA), lambda l, j: (l, 0, j)),
            pl.BlockSpec((None, 1, TN_ADA), lambda l, j: (l, 0, j)),
        ],
        out_specs=pl.BlockSpec((None, MOD_ROWS, TN_ADA), lambda l, j: (l, 0, j)),
        out_shape=jax.ShapeDtypeStruct((DEPTH, MOD_ROWS, n), F32),
        compiler_params=_params(("parallel", "parallel")),
        name="modulation",
    )(cv, w_ada, b_ada.reshape(DEPTH, 1, n))


def _mod_spec(chunk, row_fn):
    return pl.BlockSpec((None, None, 1, D_MODEL), lambda i, *_: (row_fn(i), chunk, 0, 0))


def _row_fn(tm, rows_per_batch, first_row):
    if rows_per_batch is None:
        return lambda i: first_row
    return lambda i: first_row + (i * tm) // rows_per_batch


def _row_chunks(n_rows, body):
    def step(c, carry):
        body(pl.ds(pl.multiple_of(c * NORM_ROWS, NORM_ROWS), NORM_ROWS))
        return carry

    lax.fori_loop(0, n_rows // NORM_ROWS, step, 0)


def _modulated_norm(h_ref, x_ref, g_ref, sc_ref, sh_ref, gm_ref):
    gm_ref[...] = g_ref[...] * (1.0 + sc_ref[...])

    def body(rows):
        x = x_ref[rows, :]
        r = lax.rsqrt(jnp.mean(x * x, axis=-1, keepdims=True) + EPS)
        h_ref[rows, :] = (x * r * gm_ref[...] + sh_ref[...]).astype(BF16)

    _row_chunks(x_ref.shape[0], body)


def _gated_norm_residual(out_ref, x_ref, y_ref, g_ref, gt_ref, gm_ref):
    gm_ref[...] = gt_ref[...] * g_ref[...]

    def body(rows):
        y = y_ref[rows, :]
        r = lax.rsqrt(jnp.mean(y * y, axis=-1, keepdims=True) + EPS)
        out_ref[rows, :] = x_ref[rows, :] + y * r * gm_ref[...]

    _row_chunks(x_ref.shape[0], body)


def _inproj_kernel(x_ref, g_ref, sc_ref, sh_ref, w_ref, o_ref, h_ref, gm_ref):
    @pl.when(pl.program_id(1) == 0)
    def _():
        _modulated_norm(h_ref, x_ref, g_ref, sc_ref, sh_ref, gm_ref)

    o_ref[...] = _mm(h_ref[...], w_ref[...])


def _inproj(x, g_pre, mod_l, w_in, layer, rows_per_batch, first_row):
    m = x.shape[0]
    row = _row_fn(TM_IN, rows_per_batch, first_row)
    return pl.pallas_call(
        _inproj_kernel,
        grid=(m // TM_IN, D_IN // TN_IN),
        in_specs=[
            pl.BlockSpec((TM_IN, D_MODEL), lambda i, j: (i, 0)),
            pl.BlockSpec((1, D_MODEL), lambda i, j: (0, 0)),
            _mod_spec(1, row),
            _mod_spec(0, row),
            pl.BlockSpec((None, D_MODEL, TN_IN), lambda i, j: (layer, 0, j)),
        ],
        out_specs=pl.BlockSpec((TM_IN, TN_IN), lambda i, j: (i, j)),
        out_shape=jax.ShapeDtypeStruct((m, D_IN), F32),
        scratch_shapes=[pltpu.VMEM((TM_IN, D_MODEL), BF16), pltpu.VMEM((1, D_MODEL), F32)],
        compiler_params=_params(("parallel", "arbitrary")),
        name="inproj",
    )(x, g_pre, mod_l, mod_l, w_in)


def _ctx_attn_kernel(qkv_ref, qg_ref, kg_ref, lam_ref, dg_ref,
                     o_ref, nak_ref, nav_ref, gk_ref, gv_ref, dk_ref, dv_ref, *, lambda_init):
    hd = HEAD_DIM

    def cols(c0, h):
        return qkv_ref[:, c0 + h * hd:c0 + (h + 1) * hd]

    nak_ref[...] = qkv_ref[:, C_NA_K:C_NA_K + NA_W]
    nav_ref[...] = qkv_ref[:, C_NA_V:C_NA_V + NA_W]
    gv_ref[...] = qkv_ref[:, C_G_V:C_G_V + GQA_KVW]
    dk_ref[...] = qkv_ref[:, C_D_K:C_D_K + DIFF_W]
    dv_ref[...] = qkv_ref[:, C_D_V:C_D_V + DIFF_W]

    for h in range(NA_HEADS):
        q = (cols(C_NA_Q, h) * QSCALE).astype(BF16)
        k = cols(C_NA_K, h).astype(BF16)
        v1 = _with_ones(cols(C_NA_V, h).astype(BF16))
        o_ref[:, h * hd:(h + 1) * hd] = _weighted(_scores_exp(q, k), v1).astype(BF16)

    for kv in range(GQA_KV_HEADS):
        kf = _rms(cols(C_G_K, kv), kg_ref[...])
        gk_ref[:, kv * hd:(kv + 1) * hd] = kf
        k = kf.astype(BF16)
        v1 = _with_ones(cols(C_G_V, kv).astype(BF16))
        for g in range(GQA_GROUP):
            hq = kv * GQA_GROUP + g
            q = (_rms(cols(C_G_Q, hq), qg_ref[...]) * QSCALE).astype(BF16)
            o = _weighted(_scores_exp(q, k), v1)
            o_ref[:, NA_W + hq * hd:NA_W + (hq + 1) * hd] = o.astype(BF16)

    lam = _diff_lambda(lam_ref, lambda_init)
    lane = lax.broadcasted_iota(jnp.int32, (SEQ, hd), 1)
    for h in range(DIFF_HEADS):
        qf = cols(C_D_Q, h) * DIFF_QSCALE
        k = cols(C_D_K, h).astype(BF16)
        v1 = _with_ones(cols(C_D_V, h).astype(BF16))
        q0 = jnp.where(lane < DIFF_QK_DIM, qf, 0.0).astype(BF16)
        q1 = jnp.where(lane >= DIFF_QK_DIM, qf, 0.0).astype(BF16)
        od = _weighted(_scores_exp(q0, k), v1) - lam * _weighted(_scores_exp(q1, k), v1)
        od = _rms(od, dg_ref[...]) * (1.0 - lambda_init)
        c0 = NA_W + GQA_QW + h * hd
        o_ref[:, c0:c0 + hd] = od.astype(BF16)


def _ctx_attn(qkv, q_g, k_g, lam_vec, diff_g, lambda_init):
    n = qkv.shape[0]
    vec = pl.BlockSpec((1, HEAD_DIM), lambda b: (0, 0))

    def rows(w):
        return pl.BlockSpec((SEQ, w), lambda b: (b, 0))

    def sds(w, dt):
        return jax.ShapeDtypeStruct((n, w), dt)

    return pl.pallas_call(
        functools.partial(_ctx_attn_kernel, lambda_init=lambda_init),
        grid=(n // SEQ,),
        in_specs=[rows(D_IN), vec, vec,
                  pl.BlockSpec((4, DIFF_QK_DIM), lambda b: (0, 0)), vec],
        out_specs=[rows(D_MIX), rows(NA_W), rows(NA_W), rows(GQA_KVW), rows(GQA_KVW),
                   rows(DIFF_W), rows(DIFF_W)],
        out_shape=[sds(D_MIX, BF16), sds(NA_W, F32), sds(NA_W, F32), sds(GQA_KVW, F32),
                   sds(GQA_KVW, F32), sds(DIFF_W, F32), sds(DIFF_W, F32)],
        compiler_params=_params(("parallel",)),
        name="ctx_attn",
    )(qkv, q_g, k_g, lam_vec, diff_g)


def _lat_na_kernel(q_ref, k_ref, v_ref, kc_ref, vc_ref, bias_ref, o_ref, kbig_ref, vbig_ref):
    lat0, lat1 = PAST_LEN, PAST_LEN + DEC_SEQ
    kc = kc_ref[...].astype(BF16)
    vc = _with_ones(vc_ref[...].astype(BF16))
    kbig_ref[0:lat0, :] = kc
    kbig_ref[lat0:lat1, :] = k_ref[...].astype(BF16)
    kbig_ref[lat1:, :] = kc
    vbig_ref[0:lat0, :] = vc
    vbig_ref[lat0:lat1, :] = _with_ones(v_ref[...].astype(BF16))
    vbig_ref[lat1:, :] = vc
    for j in range(NA_GROUPS):
        s0 = NA_SLAB_START[j]
        w0 = 0 if s0 == 0 else PAST_LEN + s0 * GRID_W
        q = (q_ref[j * NA_QTOK:(j + 1) * NA_QTOK, :] * QSCALE).astype(BF16)
        e = _scores_exp(q, kbig_ref[w0:w0 + NA_NKEY, :], bias_ref[j])
        o = _weighted(e, vbig_ref[w0:w0 + NA_NKEY, :])
        o_ref[j * NA_QTOK:(j + 1) * NA_QTOK, :] = o.astype(BF16)


def _lat_na(qkv, cache_k, cache_v, bias, layer):
    hd = HEAD_DIM
    cq, ck, cv = C_NA_Q // hd, C_NA_K // hd, C_NA_V // hd
    cache = pl.BlockSpec((None, None, PAST_LEN, hd), lambda h, b: (b, layer, 0, h))
    nbig = DEC_SEQ + 2 * PAST_LEN
    return pl.pallas_call(
        _lat_na_kernel,
        grid=(NA_HEADS, DEC_BATCH),
        in_specs=[
            pl.BlockSpec((DEC_SEQ, hd), lambda h, b: (b, cq + h)),
            pl.BlockSpec((DEC_SEQ, hd), lambda h, b: (b, ck + h)),
            pl.BlockSpec((DEC_SEQ, hd), lambda h, b: (b, cv + h)),
            cache, cache,
            pl.BlockSpec((None, None, NA_GROUPS, NA_QTOK, NA_NKEY),
                         lambda h, b: (layer, h, 0, 0, 0)),
        ],
        out_specs=pl.BlockSpec((DEC_SEQ, hd), lambda h, b: (b, h)),
        out_shape=jax.ShapeDtypeStruct((DEC_BATCH * DEC_SEQ, NA_W), BF16),
        scratch_shapes=[pltpu.VMEM((nbig, hd), BF16), pltpu.VMEM((nbig, 2 * hd), BF16)],
        compiler_params=_params(("parallel", "parallel")),
        name="lat_na",
    )(qkv, qkv, qkv, cache_k, cache_v, bias)


def _fill_keys(kall_ref, vall_ref, k_lat, v_ref, kc_ref, vc_ref):
    kall_ref[0:DEC_SEQ, :] = k_lat.astype(BF16)
    kall_ref[DEC_SEQ:N_KEYS, :] = kc_ref[...].astype(BF16)
    vall_ref[0:DEC_SEQ, :] = _with_ones(v_ref[...].astype(BF16))
    vall_ref[DEC_SEQ:N_KEYS, :] = _with_ones(vc_ref[...].astype(BF16))


def _lat_gqa_kernel(q_ref, k_ref, v_ref, kc_ref, vc_ref, cos_ref, sin_ref, qg_ref, kg_ref,
                    o_ref, kall_ref, vall_ref):
    hd = HEAD_DIM
    half = HEAD_DIM // 4
    qb = pl.program_id(2)

    @pl.when(qb == 0)
    def _():
        kf = _rope(_rms(k_ref[...], kg_ref[...]), cos_ref[...], sin_ref[...], half)
        _fill_keys(kall_ref, vall_ref, kf, v_ref, kc_ref, vc_ref)

    for s in range(TQ // ATT_ROWS):
        rows = slice(s * ATT_ROWS, (s + 1) * ATT_ROWS)
        row0 = pl.multiple_of(qb * TQ + s * ATT_ROWS, ATT_ROWS)
        cos = cos_ref[pl.ds(row0, ATT_ROWS), :]
        sin = sin_ref[pl.ds(row0, ATT_ROWS), :]
        for g in range(GQA_GROUP):
            qf = _rms(q_ref[rows, g * hd:(g + 1) * hd], qg_ref[...])
            q = (_rope(qf, cos, sin, half) * QSCALE).astype(BF16)
            o = _weighted(_scores_exp(q, kall_ref[...]), vall_ref[...])
            o_ref[rows, g * hd:(g + 1) * hd] = o.astype(BF16)


def _lat_gqa(qkv, cache_k, cache_v, cos, sin, q_g, k_g, layer):
    hd = HEAD_DIM
    gw = GQA_GROUP * hd
    nqb = DEC_SEQ // TQ
    cq, ck, cv = C_G_Q // gw, C_G_K // hd, C_G_V // hd
    cache = pl.BlockSpec((None, None, PAST_LEN, hd), lambda b, kv, qb: (b, layer, 0, kv))
    table = pl.BlockSpec((DEC_SEQ, hd), lambda b, kv, qb: (0, 0))
    vec = pl.BlockSpec((1, hd), lambda b, kv, qb: (0, 0))
    return pl.pallas_call(
        _lat_gqa_kernel,
        grid=(DEC_BATCH, GQA_KV_HEADS, nqb),
        in_specs=[
            pl.BlockSpec((TQ, gw), lambda b, kv, qb: (b * nqb + qb, cq + kv)),
            pl.BlockSpec((DEC_SEQ, hd), lambda b, kv, qb: (b, ck + kv)),
            pl.BlockSpec((DEC_SEQ, hd), lambda b, kv, qb: (b, cv + kv)),
            cache, cache, table, table, vec, vec,
        ],
        out_specs=pl.BlockSpec((TQ, gw), lambda b, kv, qb: (b * nqb + qb, kv)),
        out_shape=jax.ShapeDtypeStruct((DEC_BATCH * DEC_SEQ, GQA_QW), BF16),
        scratch_shapes=[pltpu.VMEM((N_KEYS, hd), BF16), pltpu.VMEM((N_KEYS, 2 * hd), BF16)],
        compiler_params=_params(("parallel", "parallel", "arbitrary")),
        name="lat_gqa",
    )(qkv, qkv, qkv, cache_k, cache_v, cos, sin, q_g, k_g)


def _lat_diff_kernel(q_ref, k_ref, v_ref, kc_ref, vc_ref, cos_ref, sin_ref, lam_ref, dg_ref,
                     o_ref, kall_ref, vall_ref, *, lambda_init):
    half = DIFF_QK_DIM // 4
    qb = pl.program_id(2)

    @pl.when(qb == 0)
    def _():
        kf = _rope(k_ref[...], cos_ref[...], sin_ref[...], half)
        _fill_keys(kall_ref, vall_ref, kf, v_ref, kc_ref, vc_ref)

    lam = _diff_lambda(lam_ref, lambda_init)
    lane = lax.broadcasted_iota(jnp.int32, (ATT_ROWS, HEAD_DIM), 1)
    for s in range(TQ // ATT_ROWS):
        rows = slice(s * ATT_ROWS, (s + 1) * ATT_ROWS)
        row0 = pl.multiple_of(qb * TQ + s * ATT_ROWS, ATT_ROWS)
        cos = cos_ref[pl.ds(row0, ATT_ROWS), :]
        sin = sin_ref[pl.ds(row0, ATT_ROWS), :]
        qf = _rope(q_ref[rows, :], cos, sin, half) * DIFF_QSCALE
        q0 = jnp.where(lane < DIFF_QK_DIM, qf, 0.0).astype(BF16)
        q1 = jnp.where(lane >= DIFF_QK_DIM, qf, 0.0).astype(BF16)
        od = (_weighted(_scores_exp(q0, kall_ref[...]), vall_ref[...])
              - lam * _weighted(_scores_exp(q1, kall_ref[...]), vall_ref[...]))
        od = _rms(od, dg_ref[...]) * (1.0 - lambda_init)
        o_ref[rows, :] = od.astype(BF16)


def _lat_diff(qkv, cache_k, cache_v, cos, sin, lam_vec, diff_g, layer, lambda_init):
    hd = HEAD_DIM
    nqb = DEC_SEQ // TQ
    cq, ck, cv = C_D_Q // hd, C_D_K // hd, C_D_V // hd
    cache = pl.BlockSpec((None, None, PAST_LEN, hd), lambda b, h, qb: (b, layer, 0, h))
    table = pl.BlockSpec((DEC_SEQ, hd), lambda b, h, qb: (0, 0))
    return pl.pallas_call(
        functools.partial(_lat_diff_kernel, lambda_init=lambda_init),
        grid=(DEC_BATCH, DIFF_HEADS, nqb),
        in_specs=[
            pl.BlockSpec((TQ, hd), lambda b, h, qb: (b * nqb + qb, cq + h)),
            pl.BlockSpec((DEC_SEQ, hd), lambda b, h, qb: (b, ck + h)),
            pl.BlockSpec((DEC_SEQ, hd), lambda b, h, qb: (b, cv + h)),
            cache, cache, table, table,
            pl.BlockSpec((4, DIFF_QK_DIM), lambda b, h, qb: (0, 0)),
            pl.BlockSpec((1, hd), lambda b, h, qb: (0, 0)),
        ],
        out_specs=pl.BlockSpec((TQ, hd), lambda b, h, qb: (b * nqb + qb, h)),
        out_shape=jax.ShapeDtypeStruct((DEC_BATCH * DEC_SEQ, DIFF_W), BF16),
        scratch_shapes=[pltpu.VMEM((N_KEYS, hd), BF16), pltpu.VMEM((N_KEYS, 2 * hd), BF16)],
        compiler_params=_params(("parallel", "parallel", "arbitrary")),
        name="lat_diff",
    )(qkv, qkv, qkv, cache_k, cache_v, cos, sin, lam_vec, diff_g)


def _outproj_kernel(*refs, n_o):
    o_refs = refs[:n_o]
    w_ref, x_ref, g_ref, gt_ref, out_ref = refs[n_o:]
    y = None
    off = 0
    for o_ref in o_refs:
        wd = o_ref.shape[1]
        part = _mm(o_ref[...], w_ref[off:off + wd, :])
        y = part if y is None else y + part
        off += wd
    out_ref[...] = x_ref[...] + gt_ref[...] * _rms(y, g_ref[...])


def _outproj(o_parts, w_out, layer, x, g_post, mod_l, rows_per_batch, first_row):
    m = x.shape[0]
    row = _row_fn(TM_OUT, rows_per_batch, first_row)
    full = pl.BlockSpec((TM_OUT, D_MODEL), lambda i: (i, 0))
    return pl.pallas_call(
        functools.partial(_outproj_kernel, n_o=len(o_parts)),
        grid=(m // TM_OUT,),
        in_specs=[pl.BlockSpec((TM_OUT, o.shape[1]), lambda i: (i, 0)) for o in o_parts] + [
            pl.BlockSpec((None, D_MIX, D_MODEL), lambda i: (layer, 0, 0)),
            full,
            pl.BlockSpec((1, D_MODEL), lambda i: (0, 0)),
            _mod_spec(2, row),
        ],
        out_specs=full,
        out_shape=jax.ShapeDtypeStruct((m, D_MODEL), F32),
        compiler_params=_params(("parallel",)),
        name="outproj",
    )(*o_parts, w_out, x, g_post, mod_l)


def _mlp_kernel(x_ref, gpre_ref, sc_ref, sh_ref, wup_ref, wdn_ref, gpost_ref, gt_ref,
                out_ref, h_ref, acc_ref, gm_ref):
    k = pl.program_id(1)

    @pl.when(k == 0)
    def _():
        _modulated_norm(h_ref, x_ref, gpre_ref, sc_ref, sh_ref, gm_ref)
        acc_ref[...] = jnp.zeros_like(acc_ref)

    u = _mm(h_ref[...], wup_ref[...])
    a = jnp.square(jnp.maximum(u, 0.0)).astype(BF16)
    acc_ref[...] += _mm(a, wdn_ref[...])

    @pl.when(k == pl.num_programs(1) - 1)
    def _():
        _gated_norm_residual(out_ref, x_ref, acc_ref, gpost_ref, gt_ref, gm_ref)


def _mlp(x, g_pre, g_post, mod_l, w_up, w_down, layer, rows_per_batch, first_row):
    m = x.shape[0]
    row = _row_fn(TM_MLP, rows_per_batch, first_row)
    full = pl.BlockSpec((TM_MLP, D_MODEL), lambda i, k: (i, 0))
    vec = pl.BlockSpec((1, D_MODEL), lambda i, k: (0, 0))
    return pl.pallas_call(
        _mlp_kernel,
        grid=(m // TM_MLP, D_FF // TF_MLP),
        in_specs=[
            full, vec, _mod_spec(4, row), _mod_spec(3, row),
            pl.BlockSpec((None, D_MODEL, TF_MLP), lambda i, k: (layer, 0, k)),
            pl.BlockSpec((None, TF_MLP, D_MODEL), lambda i, k: (layer, k, 0)),
            vec, _mod_spec(5, row),
        ],
        out_specs=full,
        out_shape=jax.ShapeDtypeStruct((m, D_MODEL), F32),
        scratch_shapes=[pltpu.VMEM((TM_MLP, D_MODEL), BF16), pltpu.VMEM((TM_MLP, D_MODEL), F32),
                        pltpu.VMEM((1, D_MODEL), F32)],
        compiler_params=_params(("parallel", "arbitrary")),
        name="mlp",
    )(x, g_pre, mod_l, mod_l, w_up, w_down, g_post, mod_l)


def _rope_tables(half, n_rep):
    t = jnp.arange(DEC_SEQ)
    inv = ROPE_THETA ** (-jnp.arange(half, dtype=F32) / half)

    def cs(pos):
        ang = pos.astype(F32)[:, None] * inv[None, :]
        c, s = jnp.cos(ang), jnp.sin(ang)
        return jnp.concatenate([c, c], axis=-1), jnp.concatenate([-s, s], axis=-1)

    cr, sr = cs(t // GRID_W)
    cc, sc = cs(t % GRID_W)
    return (jnp.concatenate([cr, cc] * n_rep, axis=-1),
            jnp.concatenate([sr, sc] * n_rep, axis=-1))


def _na_bias(rpb):
    qc = np.arange(GRID_W)[:, None]
    kc = np.arange(GRID_W)[None, :]
    ws = np.clip(qc - NA_KW // 2, 0, GRID_W - NA_KW)
    valid = (kc >= ws) & (kc < ws + NA_KW)
    dcol = np.clip(kc - qc + NA_KW - 1, 0, 2 * NA_KW - 2)
    n_dcol = 2 * NA_KW - 1
    onehot = (dcol.reshape(-1)[None, :] == np.arange(n_dcol)[:, None]).astype(np.float32)
    t = jnp.einsum("lhdc,cq->lhdq", rpb.astype(F32), jnp.asarray(onehot),
                   precision=lax.Precision.HIGHEST)
    t = t.reshape(DEPTH, NA_HEADS, 2 * NA_KH - 1, GRID_W, GRID_W) * LOG2E
    t = jnp.where(jnp.asarray(valid)[None, None, None], t, NEG_BIG)
    masked = jnp.full((DEPTH, NA_HEADS, GRID_W, GRID_W), NEG_BIG, F32)
    ctx = jnp.zeros((DEPTH, NA_HEADS, NA_QTOK, PAST_LEN), F32)
    groups = []
    for j in range(NA_GROUPS):
        s0 = NA_SLAB_START[j]
        q_rows = []
        for r in range(j * NA_QROWS, (j + 1) * NA_QROWS):
            r0 = min(max(r - NA_KH // 2, 0), GRID_ROWS - NA_KH)
            blocks = []
            for key_row in range(s0, s0 + NA_SLAB_ROWS):
                in_window = r0 <= key_row < r0 + NA_KH
                blocks.append(t[:, :, key_row - r + NA_KH - 1] if in_window else masked)
            q_rows.append(jnp.concatenate(blocks, axis=-1))
        slab = jnp.concatenate(q_rows, axis=-2)
        groups.append(jnp.concatenate([ctx, slab] if s0 == 0 else [slab, ctx], axis=-1))
    return jnp.stack(groups, axis=2)


def kernel(x_prompt, x_sample, c, cache_na_k, cache_na_v, cache_gqa_k, cache_gqa_v,
           cache_diff_k, cache_diff_v, c_ctx, w_ada, b_ada, norm_g, w_in, w_out, na_rpb,
           gqa_q_g, gqa_k_g, diff_lam, diff_g, w_up, w_down):
    np_rows = BATCH * SEQ
    ns_rows = DEC_BATCH * DEC_SEQ
    xp = x_prompt.reshape(np_rows, D_MODEL)
    xs = x_sample.reshape(ns_rows, D_MODEL)

    cv = jnp.concatenate(
        [c_ctx[None, :], c, jnp.zeros((MOD_ROWS - 1 - DEC_BATCH, D_MODEL), F32)], axis=0)
    mod = _modulation(cv, w_ada, b_ada).reshape(DEPTH, MOD_ROWS, N_MOD, 1, D_MODEL)

    w_in_b = w_in.astype(BF16)
    w_out_b = w_out.astype(BF16)
    w_up_b = w_up.astype(BF16)
    w_down_b = w_down.astype(BF16)

    def flat_cache(a):
        return a.reshape(DEC_BATCH, DEPTH, PAST_LEN, a.shape[3] * HEAD_DIM)

    c_na_k, c_na_v = flat_cache(cache_na_k), flat_cache(cache_na_v)
    c_g_k, c_g_v = flat_cache(cache_gqa_k), flat_cache(cache_gqa_v)
    c_d_k, c_d_v = flat_cache(cache_diff_k), flat_cache(cache_diff_v)

    cos_g, sin_g = _rope_tables(HEAD_DIM // 4, 1)
    cos_d, sin_d = _rope_tables(DIFF_QK_DIM // 4, 2)
    na_bias = _na_bias(na_rpb)

    new_kv = [[] for _ in range(6)]
    for l in range(DEPTH):
        lambda_init = 0.8 - 0.6 * math.exp(-0.3 * l)
        mod_l = mod[l]
        g = norm_g[l].reshape(4, 1, D_MODEL)
        q_g = gqa_q_g[l].reshape(1, HEAD_DIM)
        k_g = gqa_k_g[l].reshape(1, HEAD_DIM)
        d_g = diff_g[l].reshape(1, HEAD_DIM)
        lam_vec = diff_lam[l]

        qkv_p = _inproj(xp, g[0], mod_l, w_in_b, l, None, 0)
        o_p, *kv = _ctx_attn(qkv_p, q_g, k_g, lam_vec, d_g, lambda_init)
        for i in range(6):
            new_kv[i].append(kv[i])
        xp = _outproj([o_p], w_out_b, l, xp, g[1], mod_l, None, 0)
        xp = _mlp(xp, g[2], g[3], mod_l, w_up_b, w_down_b, l, None, 0)

        qkv_s = _inproj(xs, g[0], mod_l, w_in_b, l, DEC_SEQ, 1)
        o_na = _lat_na(qkv_s, c_na_k, c_na_v, na_bias, l)
        o_gqa = _lat_gqa(qkv_s, c_g_k, c_g_v, cos_g, sin_g, q_g, k_g, l)
        o_diff = _lat_diff(qkv_s, c_d_k, c_d_v, cos_d, sin_d, lam_vec, d_g, l, lambda_init)
        xs = _outproj([o_na, o_gqa, o_diff], w_out_b, l, xs, g[1], mod_l, DEC_SEQ, 1)
        xs = _mlp(xs, g[2], g[3], mod_l, w_up_b, w_down_b, l, DEC_SEQ, 1)

    def stack(parts, heads):
        return jnp.stack([p.reshape(BATCH, SEQ, heads, HEAD_DIM) for p in parts], axis=1)

    return (xp.reshape(BATCH, SEQ, D_MODEL), xs.reshape(DEC_BATCH, DEC_SEQ, D_MODEL),
            stack(new_kv[0], NA_HEADS), stack(new_kv[1], NA_HEADS),
            stack(new_kv[2], GQA_KV_HEADS), stack(new_kv[3], GQA_KV_HEADS),
            stack(new_kv[4], DIFF_HEADS), stack(new_kv[5], DIFF_HEADS))
```

```python
import functools
import math

import jax
import jax.numpy as jnp
import numpy as np
from jax import lax
from jax.experimental import pallas as pl
from jax.experimental.pallas import tpu as pltpu

D_MODEL = 2048
BATCH = 16
SEQ = 256
DEPTH = 4
DEC_BATCH = 8
DEC_SEQ = 1024
PAST_LEN = 256
GRID_W = 64
GRID_ROWS = DEC_SEQ // GRID_W
HEAD_DIM = 128
NA_HEADS = 4
GQA_Q_HEADS = 8
GQA_KV_HEADS = 2
GQA_GROUP = GQA_Q_HEADS // GQA_KV_HEADS
DIFF_HEADS = 4
DIFF_QK_DIM = HEAD_DIM // 2
NA_KH = 8
NA_KW = 16
D_FF = 4 * D_MODEL
ROPE_THETA = 10000.0
EPS = 1e-6
N_MOD = 6
NEG_BIG = -1e30

NA_W = NA_HEADS * HEAD_DIM
GQA_QW = GQA_Q_HEADS * HEAD_DIM
GQA_KVW = GQA_KV_HEADS * HEAD_DIM
DIFF_W = DIFF_HEADS * HEAD_DIM
D_IN = 3 * NA_W + GQA_QW + 2 * GQA_KVW + 3 * DIFF_W
D_MIX = NA_W + GQA_QW + DIFF_W
C_NA_Q, C_NA_K, C_NA_V = 0, NA_W, 2 * NA_W
C_G_Q = 3 * NA_W
C_G_K = C_G_Q + GQA_QW
C_G_V = C_G_K + GQA_KVW
C_D_Q = C_G_V + GQA_KVW
C_D_K = C_D_Q + DIFF_W
C_D_V = C_D_K + DIFF_W

N_KEYS = DEC_SEQ + PAST_LEN
LOG2E = 1.4426950408889634
QSCALE = HEAD_DIM ** -0.5 * LOG2E
DIFF_QSCALE = DIFF_QK_DIM ** -0.5 * LOG2E

NA_QROWS = 4
NA_QTOK = NA_QROWS * GRID_W
NA_GROUPS = GRID_ROWS // NA_QROWS
NA_SLAB_ROWS = 12
NA_SLAB = NA_SLAB_ROWS * GRID_W
NA_NKEY = NA_SLAB + PAST_LEN
NA_SLAB_START = (0, 0, 4, 4)
MOD_ROWS = 16

F32 = jnp.float32
BF16 = jnp.bfloat16

VMEM_LIMIT = 52 * 1024 * 1024

TM_IN = 1024
TN_IN = 512
TM_OUT = 512
TM_MLP = 512
TF_MLP = 1024
TN_ADA = 1024
TQ = 512
ATT_ROWS = 256
NORM_ROWS = 128


def _params(sem):
    return pltpu.CompilerParams(dimension_semantics=sem, vmem_limit_bytes=VMEM_LIMIT)


def _rms(x, g):
    ms = jnp.mean(x * x, axis=-1, keepdims=True)
    return x * lax.rsqrt(ms + EPS) * g


def _nt(a, b):
    return lax.dot_general(a, b, (((1,), (1,)), ((), ())), preferred_element_type=F32)


def _mm(a, b):
    return jnp.dot(a, b, preferred_element_type=F32)


def _scores_exp(q, k, bias=None):
    s = _nt(q, k)
    if bias is not None:
        s = s + bias
    return jnp.exp2(s - jnp.max(s, axis=-1, keepdims=True)).astype(BF16)


def _weighted(e, v_ones):
    oa = _mm(e, v_ones)
    d = v_ones.shape[1] // 2
    return oa[:, :d] / oa[:, d:]


def _with_ones(v):
    return jnp.concatenate([v, jnp.ones_like(v)], axis=1)


def _rope(x, cos, sin_signed, half):
    n = x.shape[-1]
    lane = lax.broadcasted_iota(jnp.int32, x.shape, 1)
    first = (lane % (2 * half)) < half
    rot = jnp.where(first, pltpu.roll(x, n - half, 1), pltpu.roll(x, half, 1))
    return x * cos + rot * sin_signed


def _diff_lambda(lam_ref, lambda_init):
    lf = lam_ref[...]
    a = jnp.sum(lf[0:1] * lf[1:2], axis=-1, keepdims=True)
    b = jnp.sum(lf[2:3] * lf[3:4], axis=-1, keepdims=True)
    return jnp.exp(a) - jnp.exp(b) + lambda_init


def _mod_kernel(cv_ref, w_ref, b_ref, o_ref):
    cv = cv_ref[...]
    s = cv / (1.0 + jnp.exp(-cv))
    o_ref[...] = _mm(s.astype(BF16), w_ref[...].astype(BF16)) + b_ref[...]


def _modulation(cv, w_ada, b_ada):
    n = N_MOD * D_MODEL
    return pl.pallas_call(
        _mod_kernel,
        grid=(DEPTH, n // TN_ADA),
        in_specs=[
            pl.BlockSpec((MOD_ROWS, D_MODEL), lambda l, j: (0, 0)),
            pl.BlockSpec((None, D_MODEL, TN_ADA), lambda l, j: (l, 0, j)),
            pl.BlockSpec((None, 1, TN_ADA), lambda l, j: (l, 0, j)),
        ],
        out_specs=pl.BlockSpec((None, MOD_ROWS, TN_ADA), lambda l, j: (l, 0, j)),
        out_shape=jax.ShapeDtypeStruct((DEPTH, MOD_ROWS, n), F32),
        compiler_params=_params(("parallel", "parallel")),
        name="modulation",
    )(cv, w_ada, b_ada.reshape(DEPTH, 1, n))


def _mod_spec(chunk, row_fn):
    return pl.BlockSpec((None, None, 1, D_MODEL), lambda i, *_: (row_fn(i), chunk, 0, 0))


def _row_fn(tm, rows_per_batch, first_row):
    if rows_per_batch is None:
        return lambda i: first_row
    return lambda i: first_row + (i * tm) // rows_per_batch


def _row_chunks(n_rows, body):
    def step(c, carry):
        body(pl.ds(pl.multiple_of(c * NORM_ROWS, NORM_ROWS), NORM_ROWS))
        return carry

    lax.fori_loop(0, n_rows // NORM_ROWS, step, 0)


def _modulated_norm(h_ref, x_ref, g_ref, sc_ref, sh_ref, gm_ref):
    gm_ref[...] = g_ref[...] * (1.0 + sc_ref[...])

    def body(rows):
        x = x_ref[rows, :]
        r = lax.rsqrt(jnp.mean(x * x, axis=-1, keepdims=True) + EPS)
        h_ref[rows, :] = (x * r * gm_ref[...] + sh_ref[...]).astype(BF16)

    _row_chunks(x_ref.shape[0], body)


def _gated_norm_residual(out_ref, x_ref, y_ref, g_ref, gt_ref, gm_ref):
    gm_ref[...] = gt_ref[...] * g_ref[...]

    def body(rows):
        y = y_ref[rows, :]
        r = lax.rsqrt(jnp.mean(y * y, axis=-1, keepdims=True) + EPS)
        out_ref[rows, :] = x_ref[rows, :] + y * r * gm_ref[...]

    _row_chunks(x_ref.shape[0], body)


def _inproj_kernel(x_ref, g_ref, sc_ref, sh_ref, w_ref, o_ref, h_ref, gm_ref):
    @pl.when(pl.program_id(1) == 0)
    def _():
        _modulated_norm(h_ref, x_ref, g_ref, sc_ref, sh_ref, gm_ref)

    o_ref[...] = _mm(h_ref[...], w_ref[...])


def _inproj(x, g_pre, mod_l, w_in, layer, rows_per_batch, first_row):
    m = x.shape[0]
    row = _row_fn(TM_IN, rows_per_batch, first_row)
    return pl.pallas_call(
        _inproj_kernel,
        grid=(m // TM_IN, D_IN // TN_IN),
        in_specs=[
            pl.BlockSpec((TM_IN, D_MODEL), lambda i, j: (i, 0)),
            pl.BlockSpec((1, D_MODEL), lambda i, j: (0, 0)),
            _mod_spec(1, row),
            _mod_spec(0, row),
            pl.BlockSpec((None, D_MODEL, TN_IN), lambda i, j: (layer, 0, j)),
        ],
        out_specs=pl.BlockSpec((TM_IN, TN_IN), lambda i, j: (i, j)),
        out_shape=jax.ShapeDtypeStruct((m, D_IN), F32),
        scratch_shapes=[pltpu.VMEM((TM_IN, D_MODEL), BF16), pltpu.VMEM((1, D_MODEL), F32)],
        compiler_params=_params(("parallel", "arbitrary")),
        name="inproj",
    )(x, g_pre, mod_l, mod_l, w_in)


def _ctx_attn_kernel(qkv_ref, qg_ref, kg_ref, lam_ref, dg_ref,
                     o_ref, nak_ref, nav_ref, gk_ref, gv_ref, dk_ref, dv_ref, *, lambda_init):
    hd = HEAD_DIM

    def cols(c0, h):
        return qkv_ref[:, c0 + h * hd:c0 + (h + 1) * hd]

    for ref, c0 in ((nak_ref, C_NA_K), (nav_ref, C_NA_V), (gv_ref, C_G_V),
                    (dk_ref, C_D_K), (dv_ref, C_D_V)):
        for h in range(ref.shape[1]):
            ref[:, h, :] = cols(c0, h)

    for h in range(NA_HEADS):
        q = (cols(C_NA_Q, h) * QSCALE).astype(BF16)
        k = cols(C_NA_K, h).astype(BF16)
        v1 = _with_ones(cols(C_NA_V, h).astype(BF16))
        o_ref[:, h * hd:(h + 1) * hd] = _weighted(_scores_exp(q, k), v1).astype(BF16)

    for kv in range(GQA_KV_HEADS):
        kf = _rms(cols(C_G_K, kv), kg_ref[...])
        gk_ref[:, kv, :] = kf
        k = kf.astype(BF16)
        v1 = _with_ones(cols(C_G_V, kv).astype(BF16))
        for g in range(GQA_GROUP):
            hq = kv * GQA_GROUP + g
            q = (_rms(cols(C_G_Q, hq), qg_ref[...]) * QSCALE).astype(BF16)
            o = _weighted(_scores_exp(q, k), v1)
            o_ref[:, NA_W + hq * hd:NA_W + (hq + 1) * hd] = o.astype(BF16)

    lam = _diff_lambda(lam_ref, lambda_init)
    lane = lax.broadcasted_iota(jnp.int32, (SEQ, hd), 1)
    for h in range(DIFF_HEADS):
        qf = cols(C_D_Q, h) * DIFF_QSCALE
        k = cols(C_D_K, h).astype(BF16)
        v1 = _with_ones(cols(C_D_V, h).astype(BF16))
        q0 = jnp.where(lane < DIFF_QK_DIM, qf, 0.0).astype(BF16)
        q1 = jnp.where(lane >= DIFF_QK_DIM, qf, 0.0).astype(BF16)
        od = _weighted(_scores_exp(q0, k), v1) - lam * _weighted(_scores_exp(q1, k), v1)
        od = _rms(od, dg_ref[...]) * (1.0 - lambda_init)
        c0 = NA_W + GQA_QW + h * hd
        o_ref[:, c0:c0 + hd] = od.astype(BF16)


def _ctx_attn(qkv, q_g, k_g, lam_vec, diff_g, lambda_init):
    n = qkv.shape[0]
    vec = pl.BlockSpec((1, HEAD_DIM), lambda b: (0, 0))

    def rows(w):
        return pl.BlockSpec((SEQ, w), lambda b: (b, 0))

    def heads(nh):
        return pl.BlockSpec((None, SEQ, nh, HEAD_DIM), lambda b: (b, 0, 0, 0))

    def kv_sds(nh):
        return jax.ShapeDtypeStruct((n // SEQ, SEQ, nh, HEAD_DIM), F32)

    kv_heads = (NA_HEADS, NA_HEADS, GQA_KV_HEADS, GQA_KV_HEADS, DIFF_HEADS, DIFF_HEADS)
    return pl.pallas_call(
        functools.partial(_ctx_attn_kernel, lambda_init=lambda_init),
        grid=(n // SEQ,),
        in_specs=[rows(D_IN), vec, vec,
                  pl.BlockSpec((4, DIFF_QK_DIM), lambda b: (0, 0)), vec],
        out_specs=[rows(D_MIX)] + [heads(nh) for nh in kv_heads],
        out_shape=[jax.ShapeDtypeStruct((n, D_MIX), BF16)] + [kv_sds(nh) for nh in kv_heads],
        compiler_params=_params(("parallel",)),
        name="ctx_attn",
    )(qkv, q_g, k_g, lam_vec, diff_g)


def _lat_na_kernel(q_ref, k_ref, v_ref, kc_ref, vc_ref, bias_ref, o_ref, kbig_ref, vbig_ref):
    lat0, lat1 = PAST_LEN, PAST_LEN + DEC_SEQ
    kc = kc_ref[...].astype(BF16)
    vc = _with_ones(vc_ref[...].astype(BF16))
    kbig_ref[0:lat0, :] = kc
    kbig_ref[lat0:lat1, :] = k_ref[...].astype(BF16)
    kbig_ref[lat1:, :] = kc
    vbig_ref[0:lat0, :] = vc
    vbig_ref[lat0:lat1, :] = _with_ones(v_ref[...].astype(BF16))
    vbig_ref[lat1:, :] = vc
    for j in range(NA_GROUPS):
        s0 = NA_SLAB_START[j]
        w0 = 0 if s0 == 0 else PAST_LEN + s0 * GRID_W
        q = (q_ref[j * NA_QTOK:(j + 1) * NA_QTOK, :] * QSCALE).astype(BF16)
        e = _scores_exp(q, kbig_ref[w0:w0 + NA_NKEY, :], bias_ref[j])
        o = _weighted(e, vbig_ref[w0:w0 + NA_NKEY, :])
        o_ref[j * NA_QTOK:(j + 1) * NA_QTOK, :] = o.astype(BF16)


def _lat_na(qkv, cache_k, cache_v, bias, layer):
    hd = HEAD_DIM
    cq, ck, cv = C_NA_Q // hd, C_NA_K // hd, C_NA_V // hd
    cache = pl.BlockSpec((None, None, PAST_LEN, hd), lambda h, b: (b, layer, 0, h))
    nbig = DEC_SEQ + 2 * PAST_LEN
    return pl.pallas_call(
        _lat_na_kernel,
        grid=(NA_HEADS, DEC_BATCH),
        in_specs=[
            pl.BlockSpec((DEC_SEQ, hd), lambda h, b: (b, cq + h)),
            pl.BlockSpec((DEC_SEQ, hd), lambda h, b: (b, ck + h)),
            pl.BlockSpec((DEC_SEQ, hd), lambda h, b: (b, cv + h)),
            cache, cache,
            pl.BlockSpec((None, None, NA_GROUPS, NA_QTOK, NA_NKEY),
                         lambda h, b: (layer, h, 0, 0, 0)),
        ],
        out_specs=pl.BlockSpec((DEC_SEQ, hd), lambda h, b: (b, h)),
        out_shape=jax.ShapeDtypeStruct((DEC_BATCH * DEC_SEQ, NA_W), BF16),
        scratch_shapes=[pltpu.VMEM((nbig, hd), BF16), pltpu.VMEM((nbig, 2 * hd), BF16)],
        compiler_params=_params(("parallel", "parallel")),
        name="lat_na",
    )(qkv, qkv, qkv, cache_k, cache_v, bias)


def _fill_keys(kall_ref, vall_ref, k_lat, v_ref, kc_ref, vc_ref):
    kall_ref[0:DEC_SEQ, :] = k_lat.astype(BF16)
    kall_ref[DEC_SEQ:N_KEYS, :] = kc_ref[...].astype(BF16)
    vall_ref[0:DEC_SEQ, :] = _with_ones(v_ref[...].astype(BF16))
    vall_ref[DEC_SEQ:N_KEYS, :] = _with_ones(vc_ref[...].astype(BF16))


def _lat_gqa_kernel(q_ref, k_ref, v_ref, kc_ref, vc_ref, cos_ref, sin_ref, qg_ref, kg_ref,
                    o_ref, kall_ref, vall_ref):
    hd = HEAD_DIM
    half = HEAD_DIM // 4
    qb = pl.program_id(2)

    @pl.when(qb == 0)
    def _():
        kf = _rope(_rms(k_ref[...], kg_ref[...]), cos_ref[...], sin_ref[...], half)
        _fill_keys(kall_ref, vall_ref, kf, v_ref, kc_ref, vc_ref)

    for s in range(TQ // ATT_ROWS):
        rows = slice(s * ATT_ROWS, (s + 1) * ATT_ROWS)
        row0 = pl.multiple_of(qb * TQ + s * ATT_ROWS, ATT_ROWS)
        cos = cos_ref[pl.ds(row0, ATT_ROWS), :]
        sin = sin_ref[pl.ds(row0, ATT_ROWS), :]
        for g in range(GQA_GROUP):
            qf = _rms(q_ref[rows, g * hd:(g + 1) * hd], qg_ref[...])
            q = (_rope(qf, cos, sin, half) * QSCALE).astype(BF16)
            o = _weighted(_scores_exp(q, kall_ref[...]), vall_ref[...])
            o_ref[rows, g * hd:(g + 1) * hd] = o.astype(BF16)


def _lat_gqa(qkv, cache_k, cache_v, cos, sin, q_g, k_g, layer):
    hd = HEAD_DIM
    gw = GQA_GROUP * hd
    nqb = DEC_SEQ // TQ
    cq, ck, cv = C_G_Q // gw, C_G_K // hd, C_G_V // hd
    cache = pl.BlockSpec((None, None, PAST_LEN, hd), lambda b, kv, qb: (b, layer, 0, kv))
    table = pl.BlockSpec((DEC_SEQ, hd), lambda b, kv, qb: (0, 0))
    vec = pl.BlockSpec((1, hd), lambda b, kv, qb: (0, 0))
    return pl.pallas_call(
        _lat_gqa_kernel,
        grid=(DEC_BATCH, GQA_KV_HEADS, nqb),
        in_specs=[
            pl.BlockSpec((TQ, gw), lambda b, kv, qb: (b * nqb + qb, cq + kv)),
            pl.BlockSpec((DEC_SEQ, hd), lambda b, kv, qb: (b, ck + kv)),
            pl.BlockSpec((DEC_SEQ, hd), lambda b, kv, qb: (b, cv + kv)),
            cache, cache, table, table, vec, vec,
        ],
        out_specs=pl.BlockSpec((TQ, gw), lambda b, kv, qb: (b * nqb + qb, kv)),
        out_shape=jax.ShapeDtypeStruct((DEC_BATCH * DEC_SEQ, GQA_QW), BF16),
        scratch_shapes=[pltpu.VMEM((N_KEYS, hd), BF16), pltpu.VMEM((N_KEYS, 2 * hd), BF16)],
        compiler_params=_params(("parallel", "parallel", "arbitrary")),
        name="lat_gqa",
    )(qkv, qkv, qkv, cache_k, cache_v, cos, sin, q_g, k_g)


def _lat_diff_kernel(q_ref, k_ref, v_ref, kc_ref, vc_ref, cos_ref, sin_ref, lam_ref, dg_ref,
                     o_ref, kall_ref, vall_ref, *, lambda_init):
    half = DIFF_QK_DIM // 4
    qb = pl.program_id(2)

    @pl.when(qb == 0)
    def _():
        kf = _rope(k_ref[...], cos_ref[...], sin_ref[...], half)
        _fill_keys(kall_ref, vall_ref, kf, v_ref, kc_ref, vc_ref)

    lam = _diff_lambda(lam_ref, lambda_init)
    lane = lax.broadcasted_iota(jnp.int32, (ATT_ROWS, HEAD_DIM), 1)
    for s in range(TQ // ATT_ROWS):
        rows = slice(s * ATT_ROWS, (s + 1) * ATT_ROWS)
        row0 = pl.multiple_of(qb * TQ + s * ATT_ROWS, ATT_ROWS)
        cos = cos_ref[pl.ds(row0, ATT_ROWS), :]
        sin = sin_ref[pl.ds(row0, ATT_ROWS), :]
        qf = _rope(q_ref[rows, :], cos, sin, half) * DIFF_QSCALE
        q0 = jnp.where(lane < DIFF_QK_DIM, qf, 0.0).astype(BF16)
        q1 = jnp.where(lane >= DIFF_QK_DIM, qf, 0.0).astype(BF16)
        od = (_weighted(_scores_exp(q0, kall_ref[...]), vall_ref[...])
              - lam * _weighted(_scores_exp(q1, kall_ref[...]), vall_ref[...]))
        od = _rms(od, dg_ref[...]) * (1.0 - lambda_init)
        o_ref[rows, :] = od.astype(BF16)


def _lat_diff(qkv, cache_k, cache_v, cos, sin, lam_vec, diff_g, layer, lambda_init):
    hd = HEAD_DIM
    nqb = DEC_SEQ // TQ
    cq, ck, cv = C_D_Q // hd, C_D_K // hd, C_D_V // hd
    cache = pl.BlockSpec((None, None, PAST_LEN, hd), lambda b, h, qb: (b, layer, 0, h))
    table = pl.BlockSpec((DEC_SEQ, hd), lambda b, h, qb: (0, 0))
    return pl.pallas_call(
        functools.partial(_lat_diff_kernel, lambda_init=lambda_init),
        grid=(DEC_BATCH, DIFF_HEADS, nqb),
        in_specs=[
            pl.BlockSpec((TQ, hd), lambda b, h, qb: (b * nqb + qb, cq + h)),
            pl.BlockSpec((DEC_SEQ, hd), lambda b, h, qb: (b, ck + h)),
            pl.BlockSpec((DEC_SEQ, hd), lambda b, h, qb: (b, cv + h)),
            cache, cache, table, table,
            pl.BlockSpec((4, DIFF_QK_DIM), lambda b, h, qb: (0, 0)),
            pl.BlockSpec((1, hd), lambda b, h, qb: (0, 0)),
        ],
        out_specs=pl.BlockSpec((TQ, hd), lambda b, h, qb: (b * nqb + qb, h)),
        out_shape=jax.ShapeDtypeStruct((DEC_BATCH * DEC_SEQ, DIFF_W), BF16),
        scratch_shapes=[pltpu.VMEM((N_KEYS, hd), BF16), pltpu.VMEM((N_KEYS, 2 * hd), BF16)],
        compiler_params=_params(("parallel", "parallel", "arbitrary")),
        name="lat_diff",
    )(qkv, qkv, qkv, cache_k, cache_v, cos, sin, lam_vec, diff_g)


def _outproj_kernel(*refs, n_o):
    o_refs = refs[:n_o]
    w_ref, x_ref, g_ref, gt_ref, out_ref = refs[n_o:]
    y = None
    off = 0
    for o_ref in o_refs:
        wd = o_ref.shape[1]
        part = _mm(o_ref[...], w_ref[off:off + wd, :])
        y = part if y is None else y + part
        off += wd
    out_ref[...] = x_ref[...] + gt_ref[...] * _rms(y, g_ref[...])


def _outproj(o_parts, w_out, layer, x, g_post, mod_l, rows_per_batch, first_row):
    m = x.shape[0]
    row = _row_fn(TM_OUT, rows_per_batch, first_row)
    full = pl.BlockSpec((TM_OUT, D_MODEL), lambda i: (i, 0))
    return pl.pallas_call(
        functools.partial(_outproj_kernel, n_o=len(o_parts)),
        grid=(m // TM_OUT,),
        in_specs=[pl.BlockSpec((TM_OUT, o.shape[1]), lambda i: (i, 0)) for o in o_parts] + [
            pl.BlockSpec((None, D_MIX, D_MODEL), lambda i: (layer, 0, 0)),
            full,
            pl.BlockSpec((1, D_MODEL), lambda i: (0, 0)),
            _mod_spec(2, row),
        ],
        out_specs=full,
        out_shape=jax.ShapeDtypeStruct((m, D_MODEL), F32),
        compiler_params=_params(("parallel",)),
        name="outproj",
    )(*o_parts, w_out, x, g_post, mod_l)


def _mlp_kernel(x_ref, gpre_ref, sc_ref, sh_ref, wup_ref, wdn_ref, gpost_ref, gt_ref,
                out_ref, h_ref, acc_ref, gm_ref):
    k = pl.program_id(1)

    @pl.when(k == 0)
    def _():
        _modulated_norm(h_ref, x_ref, gpre_ref, sc_ref, sh_ref, gm_ref)
        acc_ref[...] = jnp.zeros_like(acc_ref)

    u = _mm(h_ref[...], wup_ref[...])
    a = jnp.square(jnp.maximum(u, 0.0)).astype(BF16)
    acc_ref[...] += _mm(a, wdn_ref[...])

    @pl.when(k == pl.num_programs(1) - 1)
    def _():
        _gated_norm_residual(out_ref, x_ref, acc_ref, gpost_ref, gt_ref, gm_ref)


def _mlp(x, g_pre, g_post, mod_l, w_up, w_down, layer, rows_per_batch, first_row):
    m = x.shape[0]
    row = _row_fn(TM_MLP, rows_per_batch, first_row)
    full = pl.BlockSpec((TM_MLP, D_MODEL), lambda i, k: (i, 0))
    vec = pl.BlockSpec((1, D_MODEL), lambda i, k: (0, 0))
    return pl.pallas_call(
        _mlp_kernel,
        grid=(m // TM_MLP, D_FF // TF_MLP),
        in_specs=[
            full, vec, _mod_spec(4, row), _mod_spec(3, row),
            pl.BlockSpec((None, D_MODEL, TF_MLP), lambda i, k: (layer, 0, k)),
            pl.BlockSpec((None, TF_MLP, D_MODEL), lambda i, k: (layer, k, 0)),
            vec, _mod_spec(5, row),
        ],
        out_specs=full,
        out_shape=jax.ShapeDtypeStruct((m, D_MODEL), F32),
        scratch_shapes=[pltpu.VMEM((TM_MLP, D_MODEL), BF16), pltpu.VMEM((TM_MLP, D_MODEL), F32),
                        pltpu.VMEM((1, D_MODEL), F32)],
        compiler_params=_params(("parallel", "arbitrary")),
        name="mlp",
    )(x, g_pre, mod_l, mod_l, w_up, w_down, g_post, mod_l)


def _rope_tables(half, n_rep):
    t = jnp.arange(DEC_SEQ)
    inv = ROPE_THETA ** (-jnp.arange(half, dtype=F32) / half)

    def cs(pos):
        ang = pos.astype(F32)[:, None] * inv[None, :]
        c, s = jnp.cos(ang), jnp.sin(ang)
        return jnp.concatenate([c, c], axis=-1), jnp.concatenate([-s, s], axis=-1)

    cr, sr = cs(t // GRID_W)
    cc, sc = cs(t % GRID_W)
    return (jnp.concatenate([cr, cc] * n_rep, axis=-1),
            jnp.concatenate([sr, sc] * n_rep, axis=-1))


def _na_bias(rpb):
    qc = np.arange(GRID_W)[:, None]
    kc = np.arange(GRID_W)[None, :]
    ws = np.clip(qc - NA_KW // 2, 0, GRID_W - NA_KW)
    valid = (kc >= ws) & (kc < ws + NA_KW)
    dcol = np.clip(kc - qc + NA_KW - 1, 0, 2 * NA_KW - 2)
    n_dcol = 2 * NA_KW - 1
    onehot = (dcol.reshape(-1)[None, :] == np.arange(n_dcol)[:, None]).astype(np.float32)
    t = jnp.einsum("lhdc,cq->lhdq", rpb.astype(F32), jnp.asarray(onehot),
                   precision=lax.Precision.HIGHEST)
    t = t.reshape(DEPTH, NA_HEADS, 2 * NA_KH - 1, GRID_W, GRID_W) * LOG2E
    t = jnp.where(jnp.asarray(valid)[None, None, None], t, NEG_BIG)
    masked = jnp.full((DEPTH, NA_HEADS, GRID_W, GRID_W), NEG_BIG, F32)
    ctx = jnp.zeros((DEPTH, NA_HEADS, NA_QTOK, PAST_LEN), F32)
    groups = []
    for j in range(NA_GROUPS):
        s0 = NA_SLAB_START[j]
        q_rows = []
        for r in range(j * NA_QROWS, (j + 1) * NA_QROWS):
            r0 = min(max(r - NA_KH // 2, 0), GRID_ROWS - NA_KH)
            blocks = []
            for key_row in range(s0, s0 + NA_SLAB_ROWS):
                in_window = r0 <= key_row < r0 + NA_KH
                blocks.append(t[:, :, key_row - r + NA_KH - 1] if in_window else masked)
            q_rows.append(jnp.concatenate(blocks, axis=-1))
        slab = jnp.concatenate(q_rows, axis=-2)
        groups.append(jnp.concatenate([ctx, slab] if s0 == 0 else [slab, ctx], axis=-1))
    return jnp.stack(groups, axis=2)


def kernel(x_prompt, x_sample, c, cache_na_k, cache_na_v, cache_gqa_k, cache_gqa_v,
           cache_diff_k, cache_diff_v, c_ctx, w_ada, b_ada, norm_g, w_in, w_out, na_rpb,
           gqa_q_g, gqa_k_g, diff_lam, diff_g, w_up, w_down):
    np_rows = BATCH * SEQ
    ns_rows = DEC_BATCH * DEC_SEQ
    xp = x_prompt.reshape(np_rows, D_MODEL)
    xs = x_sample.reshape(ns_rows, D_MODEL)

    cv = jnp.concatenate(
        [c_ctx[None, :], c, jnp.zeros((MOD_ROWS - 1 - DEC_BATCH, D_MODEL), F32)], axis=0)
    mod = _modulation(cv, w_ada, b_ada).reshape(DEPTH, MOD_ROWS, N_MOD, 1, D_MODEL)

    w_in_b = w_in.astype(BF16)
    w_out_b = w_out.astype(BF16)
    w_up_b = w_up.astype(BF16)
    w_down_b = w_down.astype(BF16)

    def flat_cache(a):
        return a.reshape(DEC_BATCH, DEPTH, PAST_LEN, a.shape[3] * HEAD_DIM)

    c_na_k, c_na_v = flat_cache(cache_na_k), flat_cache(cache_na_v)
    c_g_k, c_g_v = flat_cache(cache_gqa_k), flat_cache(cache_gqa_v)
    c_d_k, c_d_v = flat_cache(cache_diff_k), flat_cache(cache_diff_v)

    cos_g, sin_g = _rope_tables(HEAD_DIM // 4, 1)
    cos_d, sin_d = _rope_tables(DIFF_QK_DIM // 4, 2)
    na_bias = _na_bias(na_rpb)

    new_kv = [[] for _ in range(6)]
    for l in range(DEPTH):
        lambda_init = 0.8 - 0.6 * math.exp(-0.3 * l)
        mod_l = mod[l]
        g = norm_g[l].reshape(4, 1, D_MODEL)
        q_g = gqa_q_g[l].reshape(1, HEAD_DIM)
        k_g = gqa_k_g[l].reshape(1, HEAD_DIM)
        d_g = diff_g[l].reshape(1, HEAD_DIM)
        lam_vec = diff_lam[l]

        qkv_p = _inproj(xp, g[0], mod_l, w_in_b, l, None, 0)
        o_p, *kv = _ctx_attn(qkv_p, q_g, k_g, lam_vec, d_g, lambda_init)
        for i in range(6):
            new_kv[i].append(kv[i])
        xp = _outproj([o_p], w_out_b, l, xp, g[1], mod_l, None, 0)
        xp = _mlp(xp, g[2], g[3], mod_l, w_up_b, w_down_b, l, None, 0)

        qkv_s = _inproj(xs, g[0], mod_l, w_in_b, l, DEC_SEQ, 1)
        o_na = _lat_na(qkv_s, c_na_k, c_na_v, na_bias, l)
        o_gqa = _lat_gqa(qkv_s, c_g_k, c_g_v, cos_g, sin_g, q_g, k_g, l)
        o_diff = _lat_diff(qkv_s, c_d_k, c_d_v, cos_d, sin_d, lam_vec, d_g, l, lambda_init)
        xs = _outproj([o_na, o_gqa, o_diff], w_out_b, l, xs, g[1], mod_l, DEC_SEQ, 1)
        xs = _mlp(xs, g[2], g[3], mod_l, w_up_b, w_down_b, l, DEC_SEQ, 1)

    def stack(parts, heads):
        return jnp.stack(parts, axis=1)

    return (xp.reshape(BATCH, SEQ, D_MODEL), xs.reshape(DEC_BATCH, DEC_SEQ, D_MODEL),
            stack(new_kv[0], NA_HEADS), stack(new_kv[1], NA_HEADS),
            stack(new_kv[2], GQA_KV_HEADS), stack(new_kv[3], GQA_KV_HEADS),
            stack(new_kv[4], DIFF_HEADS), stack(new_kv[5], DIFF_HEADS))
```

```python
import functools
import math

import jax
import jax.numpy as jnp
import numpy as np
from jax import lax
from jax.experimental import pallas as pl
from jax.experimental.pallas import tpu as pltpu

D_MODEL = 2048
BATCH = 16
SEQ = 256
DEPTH = 4
DEC_BATCH = 8
DEC_SEQ = 1024
PAST_LEN = 256
GRID_W = 64
GRID_ROWS = DEC_SEQ // GRID_W
HEAD_DIM = 128
NA_HEADS = 4
GQA_Q_HEADS = 8
GQA_KV_HEADS = 2
GQA_GROUP = GQA_Q_HEADS // GQA_KV_HEADS
DIFF_HEADS = 4
DIFF_QK_DIM = HEAD_DIM // 2
NA_KH = 8
NA_KW = 16
D_FF = 4 * D_MODEL
ROPE_THETA = 10000.0
EPS = 1e-6
N_MOD = 6
NEG_BIG = -1e30

NA_W = NA_HEADS * HEAD_DIM
GQA_QW = GQA_Q_HEADS * HEAD_DIM
GQA_KVW = GQA_KV_HEADS * HEAD_DIM
DIFF_W = DIFF_HEADS * HEAD_DIM
D_IN = 3 * NA_W + GQA_QW + 2 * GQA_KVW + 3 * DIFF_W
D_MIX = NA_W + GQA_QW + DIFF_W
C_NA_Q, C_NA_K, C_NA_V = 0, NA_W, 2 * NA_W
C_G_Q = 3 * NA_W
C_G_K = C_G_Q + GQA_QW
C_G_V = C_G_K + GQA_KVW
C_D_Q = C_G_V + GQA_KVW
C_D_K = C_D_Q + DIFF_W
C_D_V = C_D_K + DIFF_W

KV_HEADS = (NA_HEADS, NA_HEADS, GQA_KV_HEADS, GQA_KV_HEADS, DIFF_HEADS, DIFF_HEADS)
N_KEYS = DEC_SEQ + PAST_LEN
LOG2E = 1.4426950408889634
QSCALE = HEAD_DIM ** -0.5 * LOG2E
DIFF_QSCALE = DIFF_QK_DIM ** -0.5 * LOG2E

NA_QROWS = 4
NA_QTOK = NA_QROWS * GRID_W
NA_GROUPS = GRID_ROWS // NA_QROWS
NA_SLAB_ROWS = 12
NA_SLAB = NA_SLAB_ROWS * GRID_W
NA_NKEY = NA_SLAB + PAST_LEN
NA_SLAB_START = (0, 0, 4, 4)
MOD_ROWS = 16

F32 = jnp.float32
BF16 = jnp.bfloat16

VMEM_LIMIT = 52 * 1024 * 1024

TM_IN = 1024
TN_IN = 512
TM_OUT = 512
TM_MLP = 512
TF_MLP = 1024
TN_ADA = 1024
TQ = 512
ATT_ROWS = 256
NORM_ROWS = 128


def _params(sem):
    return pltpu.CompilerParams(dimension_semantics=sem, vmem_limit_bytes=VMEM_LIMIT)


def _rms(x, g):
    ms = jnp.mean(x * x, axis=-1, keepdims=True)
    return x * lax.rsqrt(ms + EPS) * g


def _nt(a, b):
    return lax.dot_general(a, b, (((1,), (1,)), ((), ())), preferred_element_type=F32)


def _mm(a, b):
    return jnp.dot(a, b, preferred_element_type=F32)


def _scores_exp(q, k, bias=None):
    s = _nt(q, k)
    if bias is not None:
        s = s + bias
    return jnp.exp2(s - jnp.max(s, axis=-1, keepdims=True)).astype(BF16)


def _weighted(e, v_ones):
    oa = _mm(e, v_ones)
    d = v_ones.shape[1] // 2
    return oa[:, :d] / oa[:, d:]


def _with_ones(v):
    return jnp.concatenate([v, jnp.ones_like(v)], axis=1)


def _rope(x, cos, sin_signed, half):
    n = x.shape[-1]
    lane = lax.broadcasted_iota(jnp.int32, x.shape, 1)
    first = (lane % (2 * half)) < half
    rot = jnp.where(first, pltpu.roll(x, n - half, 1), pltpu.roll(x, half, 1))
    return x * cos + rot * sin_signed


def _diff_lambda(lam_ref, lambda_init):
    lf = lam_ref[...]
    a = jnp.sum(lf[0:1] * lf[1:2], axis=-1, keepdims=True)
    b = jnp.sum(lf[2:3] * lf[3:4], axis=-1, keepdims=True)
    return jnp.exp(a) - jnp.exp(b) + lambda_init


def _mod_kernel(cv_ref, w_ref, b_ref, o_ref):
    cv = cv_ref[...]
    s = cv / (1.0 + jnp.exp(-cv))
    o_ref[...] = _mm(s.astype(BF16), w_ref[...].astype(BF16)) + b_ref[...]


def _modulation(cv, w_ada, b_ada):
    n = N_MOD * D_MODEL
    return pl.pallas_call(
        _mod_kernel,
        grid=(DEPTH, n // TN_ADA),
        in_specs=[
            pl.BlockSpec((MOD_ROWS, D_MODEL), lambda l, j: (0, 0)),
            pl.BlockSpec((None, D_MODEL, TN_ADA), lambda l, j: (l, 0, j)),
            pl.BlockSpec((None, 1, TN_ADA), lambda l, j: (l, 0, j)),
        ],
        out_specs=pl.BlockSpec((None, MOD_ROWS, TN_ADA), lambda l, j: (l, 0, j)),
        out_shape=jax.ShapeDtypeStruct((DEPTH, MOD_ROWS, n), F32),
        compiler_params=_params(("parallel", "parallel")),
        name="modulation",
    )(cv, w_ada, b_ada.reshape(DEPTH, 1, n))


def _mod_spec(chunk, row_fn):
    return pl.BlockSpec((None, None, 1, D_MODEL), lambda i, *_: (row_fn(i), chunk, 0, 0))


def _row_fn(tm, rows_per_batch, first_row):
    if rows_per_batch is None:
        return lambda i: first_row
    return lambda i: first_row + (i * tm) // rows_per_batch


def _row_chunks(n_rows, body):
    def step(c, carry):
        body(pl.ds(pl.multiple_of(c * NORM_ROWS, NORM_ROWS), NORM_ROWS))
        return carry

    lax.fori_loop(0, n_rows // NORM_ROWS, step, 0)


def _modulated_norm(h_ref, x_ref, g_ref, sc_ref, sh_ref, gm_ref):
    gm_ref[...] = g_ref[...] * (1.0 + sc_ref[...])

    def body(rows):
        x = x_ref[rows, :]
        r = lax.rsqrt(jnp.mean(x * x, axis=-1, keepdims=True) + EPS)
        h_ref[rows, :] = (x * r * gm_ref[...] + sh_ref[...]).astype(BF16)

    _row_chunks(x_ref.shape[0], body)


def _gated_norm_residual(out_ref, x_ref, y_ref, g_ref, gt_ref, gm_ref):
    gm_ref[...] = gt_ref[...] * g_ref[...]

    def body(rows):
        y = y_ref[rows, :]
        r = lax.rsqrt(jnp.mean(y * y, axis=-1, keepdims=True) + EPS)
        out_ref[rows, :] = x_ref[rows, :] + y * r * gm_ref[...]

    _row_chunks(x_ref.shape[0], body)


def _inproj_kernel(x_ref, g_ref, sc_ref, sh_ref, w_ref, o_ref, h_ref, gm_ref):
    @pl.when(pl.program_id(1) == 0)
    def _():
        _modulated_norm(h_ref, x_ref, g_ref, sc_ref, sh_ref, gm_ref)

    o_ref[...] = _mm(h_ref[...], w_ref[...])


def _inproj(x, g_pre, mod_l, w_in, layer, rows_per_batch, first_row):
    m = x.shape[0]
    row = _row_fn(TM_IN, rows_per_batch, first_row)
    return pl.pallas_call(
        _inproj_kernel,
        grid=(m // TM_IN, D_IN // TN_IN),
        in_specs=[
            pl.BlockSpec((TM_IN, D_MODEL), lambda i, j: (i, 0)),
            pl.BlockSpec((1, D_MODEL), lambda i, j: (0, 0)),
            _mod_spec(1, row),
            _mod_spec(0, row),
            pl.BlockSpec((None, D_MODEL, TN_IN), lambda i, j: (layer, 0, j)),
        ],
        out_specs=pl.BlockSpec((TM_IN, TN_IN), lambda i, j: (i, j)),
        out_shape=jax.ShapeDtypeStruct((m, D_IN), F32),
        scratch_shapes=[pltpu.VMEM((TM_IN, D_MODEL), BF16), pltpu.VMEM((1, D_MODEL), F32)],
        compiler_params=_params(("parallel", "arbitrary")),
        name="inproj",
    )(x, g_pre, mod_l, mod_l, w_in)


def _ctx_attn_kernel(qkv_ref, qg_ref, kg_ref, lam_ref, dg_ref, *rest, lambda_init):
    o_ref, nak_ref, nav_ref, gk_ref, gv_ref, dk_ref, dv_ref = rest[len(KV_HEADS):]
    hd = HEAD_DIM

    def cols(c0, h):
        return qkv_ref[:, c0 + h * hd:c0 + (h + 1) * hd]

    for ref, c0 in ((nak_ref, C_NA_K), (nav_ref, C_NA_V), (gv_ref, C_G_V),
                    (dk_ref, C_D_K), (dv_ref, C_D_V)):
        for h in range(ref.shape[1]):
            ref[:, h, :] = cols(c0, h)

    for h in range(NA_HEADS):
        q = (cols(C_NA_Q, h) * QSCALE).astype(BF16)
        k = cols(C_NA_K, h).astype(BF16)
        v1 = _with_ones(cols(C_NA_V, h).astype(BF16))
        o_ref[:, h * hd:(h + 1) * hd] = _weighted(_scores_exp(q, k), v1).astype(BF16)

    for kv in range(GQA_KV_HEADS):
        kf = _rms(cols(C_G_K, kv), kg_ref[...])
        gk_ref[:, kv, :] = kf
        k = kf.astype(BF16)
        v1 = _with_ones(cols(C_G_V, kv).astype(BF16))
        for g in range(GQA_GROUP):
            hq = kv * GQA_GROUP + g
            q = (_rms(cols(C_G_Q, hq), qg_ref[...]) * QSCALE).astype(BF16)
            o = _weighted(_scores_exp(q, k), v1)
            o_ref[:, NA_W + hq * hd:NA_W + (hq + 1) * hd] = o.astype(BF16)

    lam = _diff_lambda(lam_ref, lambda_init)
    lane = lax.broadcasted_iota(jnp.int32, (SEQ, hd), 1)
    for h in range(DIFF_HEADS):
        qf = cols(C_D_Q, h) * DIFF_QSCALE
        k = cols(C_D_K, h).astype(BF16)
        v1 = _with_ones(cols(C_D_V, h).astype(BF16))
        q0 = jnp.where(lane < DIFF_QK_DIM, qf, 0.0).astype(BF16)
        q1 = jnp.where(lane >= DIFF_QK_DIM, qf, 0.0).astype(BF16)
        od = _weighted(_scores_exp(q0, k), v1) - lam * _weighted(_scores_exp(q1, k), v1)
        od = _rms(od, dg_ref[...]) * (1.0 - lambda_init)
        c0 = NA_W + GQA_QW + h * hd
        o_ref[:, c0:c0 + hd] = od.astype(BF16)


def _ctx_attn(qkv, q_g, k_g, lam_vec, diff_g, kv_bufs, layer, lambda_init):
    n = qkv.shape[0]
    vec = pl.BlockSpec((1, HEAD_DIM), lambda b: (0, 0))
    n_in = 5

    def rows(w):
        return pl.BlockSpec((SEQ, w), lambda b: (b, 0))

    def heads(nh):
        return pl.BlockSpec((None, None, SEQ, nh, HEAD_DIM), lambda b: (b, layer, 0, 0, 0))

    return pl.pallas_call(
        functools.partial(_ctx_attn_kernel, lambda_init=lambda_init),
        grid=(n // SEQ,),
        in_specs=[rows(D_IN), vec, vec,
                  pl.BlockSpec((4, DIFF_QK_DIM), lambda b: (0, 0)), vec]
                 + [pl.BlockSpec(memory_space=pl.ANY)] * len(kv_bufs),
        out_specs=[rows(D_MIX)] + [heads(nh) for nh in KV_HEADS],
        out_shape=[jax.ShapeDtypeStruct((n, D_MIX), BF16)]
                  + [jax.ShapeDtypeStruct(a.shape, a.dtype) for a in kv_bufs],
        input_output_aliases={n_in + i: 1 + i for i in range(len(kv_bufs))},
        compiler_params=_params(("parallel",)),
        name="ctx_attn",
    )(qkv, q_g, k_g, lam_vec, diff_g, *kv_bufs)


def _lat_na_kernel(q_ref, k_ref, v_ref, kc_ref, vc_ref, bias_ref, o_ref, kbig_ref, vbig_ref):
    lat0, lat1 = PAST_LEN, PAST_LEN + DEC_SEQ
    kc = kc_ref[...].astype(BF16)
    vc = _with_ones(vc_ref[...].astype(BF16))
    kbig_ref[0:lat0, :] = kc
    kbig_ref[lat0:lat1, :] = k_ref[...].astype(BF16)
    kbig_ref[lat1:, :] = kc
    vbig_ref[0:lat0, :] = vc
    vbig_ref[lat0:lat1, :] = _with_ones(v_ref[...].astype(BF16))
    vbig_ref[lat1:, :] = vc
    for j in range(NA_GROUPS):
        s0 = NA_SLAB_START[j]
        w0 = 0 if s0 == 0 else PAST_LEN + s0 * GRID_W
        q = (q_ref[j * NA_QTOK:(j + 1) * NA_QTOK, :] * QSCALE).astype(BF16)
        e = _scores_exp(q, kbig_ref[w0:w0 + NA_NKEY, :], bias_ref[j])
        o = _weighted(e, vbig_ref[w0:w0 + NA_NKEY, :])
        o_ref[j * NA_QTOK:(j + 1) * NA_QTOK, :] = o.astype(BF16)


def _lat_na(qkv, cache_k, cache_v, bias, layer):
    hd = HEAD_DIM
    cq, ck, cv = C_NA_Q // hd, C_NA_K // hd, C_NA_V // hd
    cache = pl.BlockSpec((None, None, PAST_LEN, hd), lambda h, b: (b, layer, 0, h))
    nbig = DEC_SEQ + 2 * PAST_LEN
    return pl.pallas_call(
        _lat_na_kernel,
        grid=(NA_HEADS, DEC_BATCH),
        in_specs=[
            pl.BlockSpec((DEC_SEQ, hd), lambda h, b: (b, cq + h)),
            pl.BlockSpec((DEC_SEQ, hd), lambda h, b: (b, ck + h)),
            pl.BlockSpec((DEC_SEQ, hd), lambda h, b: (b, cv + h)),
            cache, cache,
            pl.BlockSpec((None, None, NA_GROUPS, NA_QTOK, NA_NKEY),
                         lambda h, b: (layer, h, 0, 0, 0)),
        ],
        out_specs=pl.BlockSpec((DEC_SEQ, hd), lambda h, b: (b, h)),
        out_shape=jax.ShapeDtypeStruct((DEC_BATCH * DEC_SEQ, NA_W), BF16),
        scratch_shapes=[pltpu.VMEM((nbig, hd), BF16), pltpu.VMEM((nbig, 2 * hd), BF16)],
        compiler_params=_params(("parallel", "parallel")),
        name="lat_na",
    )(qkv, qkv, qkv, cache_k, cache_v, bias)


def _fill_keys(kall_ref, vall_ref, k_lat, v_ref, kc_ref, vc_ref):
    kall_ref[0:DEC_SEQ, :] = k_lat.astype(BF16)
    kall_ref[DEC_SEQ:N_KEYS, :] = kc_ref[...].astype(BF16)
    vall_ref[0:DEC_SEQ, :] = _with_ones(v_ref[...].astype(BF16))
    vall_ref[DEC_SEQ:N_KEYS, :] = _with_ones(vc_ref[...].astype(BF16))


def _lat_gqa_kernel(q_ref, k_ref, v_ref, kc_ref, vc_ref, cos_ref, sin_ref, qg_ref, kg_ref,
                    o_ref, kall_ref, vall_ref):
    hd = HEAD_DIM
    half = HEAD_DIM // 4
    qb = pl.program_id(2)

    @pl.when(qb == 0)
    def _():
        kf = _rope(_rms(k_ref[...], kg_ref[...]), cos_ref[...], sin_ref[...], half)
        _fill_keys(kall_ref, vall_ref, kf, v_ref, kc_ref, vc_ref)

    for s in range(TQ // ATT_ROWS):
        rows = slice(s * ATT_ROWS, (s + 1) * ATT_ROWS)
        row0 = pl.multiple_of(qb * TQ + s * ATT_ROWS, ATT_ROWS)
        cos = cos_ref[pl.ds(row0, ATT_ROWS), :]
        sin = sin_ref[pl.ds(row0, ATT_ROWS), :]
        for g in range(GQA_GROUP):
            qf = _rms(q_ref[rows, g * hd:(g + 1) * hd], qg_ref[...])
            q = (_rope(qf, cos, sin, half) * QSCALE).astype(BF16)
            o = _weighted(_scores_exp(q, kall_ref[...]), vall_ref[...])
            o_ref[rows, g * hd:(g + 1) * hd] = o.astype(BF16)


def _lat_gqa(qkv, cache_k, cache_v, cos, sin, q_g, k_g, layer):
    hd = HEAD_DIM
    gw = GQA_GROUP * hd
    nqb = DEC_SEQ // TQ
    cq, ck, cv = C_G_Q // gw, C_G_K // hd, C_G_V // hd
    cache = pl.BlockSpec((None, None, PAST_LEN, hd), lambda b, kv, qb: (b, layer, 0, kv))
    table = pl.BlockSpec((DEC_SEQ, hd), lambda b, kv, qb: (0, 0))
    vec = pl.BlockSpec((1, hd), lambda b, kv, qb: (0, 0))
    return pl.pallas_call(
        _lat_gqa_kernel,
        grid=(DEC_BATCH, GQA_KV_HEADS, nqb),
        in_specs=[
            pl.BlockSpec((TQ, gw), lambda b, kv, qb: (b * nqb + qb, cq + kv)),
            pl.BlockSpec((DEC_SEQ, hd), lambda b, kv, qb: (b, ck + kv)),
            pl.BlockSpec((DEC_SEQ, hd), lambda b, kv, qb: (b, cv + kv)),
            cache, cache, table, table, vec, vec,
        ],
        out_specs=pl.BlockSpec((TQ, gw), lambda b, kv, qb: (b * nqb + qb, kv)),
        out_shape=jax.ShapeDtypeStruct((DEC_BATCH * DEC_SEQ, GQA_QW), BF16),
        scratch_shapes=[pltpu.VMEM((N_KEYS, hd), BF16), pltpu.VMEM((N_KEYS, 2 * hd), BF16)],
        compiler_params=_params(("parallel", "parallel", "arbitrary")),
        name="lat_gqa",
    )(qkv, qkv, qkv, cache_k, cache_v, cos, sin, q_g, k_g)


def _lat_diff_kernel(q_ref, k_ref, v_ref, kc_ref, vc_ref, cos_ref, sin_ref, lam_ref, dg_ref,
                     o_ref, kall_ref, vall_ref, *, lambda_init):
    half = DIFF_QK_DIM // 4
    qb = pl.program_id(2)

    @pl.when(qb == 0)
    def _():
        kf = _rope(k_ref[...], cos_ref[...], sin_ref[...], half)
        _fill_keys(kall_ref, vall_ref, kf, v_ref, kc_ref, vc_ref)

    lam = _diff_lambda(lam_ref, lambda_init)
    lane = lax.broadcasted_iota(jnp.int32, (ATT_ROWS, HEAD_DIM), 1)
    for s in range(TQ // ATT_ROWS):
        rows = slice(s * ATT_ROWS, (s + 1) * ATT_ROWS)
        row0 = pl.multiple_of(qb * TQ + s * ATT_ROWS, ATT_ROWS)
        cos = cos_ref[pl.ds(row0, ATT_ROWS), :]
        sin = sin_ref[pl.ds(row0, ATT_ROWS), :]
        qf = _rope(q_ref[rows, :], cos, sin, half) * DIFF_QSCALE
        q0 = jnp.where(lane < DIFF_QK_DIM, qf, 0.0).astype(BF16)
        q1 = jnp.where(lane >= DIFF_QK_DIM, qf, 0.0).astype(BF16)
        od = (_weighted(_scores_exp(q0, kall_ref[...]), vall_ref[...])
              - lam * _weighted(_scores_exp(q1, kall_ref[...]), vall_ref[...]))
        od = _rms(od, dg_ref[...]) * (1.0 - lambda_init)
        o_ref[rows, :] = od.astype(BF16)


def _lat_diff(qkv, cache_k, cache_v, cos, sin, lam_vec, diff_g, layer, lambda_init):
    hd = HEAD_DIM
    nqb = DEC_SEQ // TQ
    cq, ck, cv = C_D_Q // hd, C_D_K // hd, C_D_V // hd
    cache = pl.BlockSpec((None, None, PAST_LEN, hd), lambda b, h, qb: (b, layer, 0, h))
    table = pl.BlockSpec((DEC_SEQ, hd), lambda b, h, qb: (0, 0))
    return pl.pallas_call(
        functools.partial(_lat_diff_kernel, lambda_init=lambda_init),
        grid=(DEC_BATCH, DIFF_HEADS, nqb),
        in_specs=[
            pl.BlockSpec((TQ, hd), lambda b, h, qb: (b * nqb + qb, cq + h)),
            pl.BlockSpec((DEC_SEQ, hd), lambda b, h, qb: (b, ck + h)),
            pl.BlockSpec((DEC_SEQ, hd), lambda b, h, qb: (b, cv + h)),
            cache, cache, table, table,
            pl.BlockSpec((4, DIFF_QK_DIM), lambda b, h, qb: (0, 0)),
            pl.BlockSpec((1, hd), lambda b, h, qb: (0, 0)),
        ],
        out_specs=pl.BlockSpec((TQ, hd), lambda b, h, qb: (b * nqb + qb, h)),
        out_shape=jax.ShapeDtypeStruct((DEC_BATCH * DEC_SEQ, DIFF_W), BF16),
        scratch_shapes=[pltpu.VMEM((N_KEYS, hd), BF16), pltpu.VMEM((N_KEYS, 2 * hd), BF16)],
        compiler_params=_params(("parallel", "parallel", "arbitrary")),
        name="lat_diff",
    )(qkv, qkv, qkv, cache_k, cache_v, cos, sin, lam_vec, diff_g)


def _outproj_kernel(*refs, n_o):
    o_refs = refs[:n_o]
    w_ref, x_ref, g_ref, gt_ref, out_ref = refs[n_o:]
    y = None
    off = 0
    for o_ref in o_refs:
        wd = o_ref.shape[1]
        part = _mm(o_ref[...], w_ref[off:off + wd, :])
        y = part if y is None else y + part
        off += wd
    out_ref[...] = x_ref[...] + gt_ref[...] * _rms(y, g_ref[...])


def _outproj(o_parts, w_out, layer, x, g_post, mod_l, rows_per_batch, first_row):
    m = x.shape[0]
    row = _row_fn(TM_OUT, rows_per_batch, first_row)
    full = pl.BlockSpec((TM_OUT, D_MODEL), lambda i: (i, 0))
    return pl.pallas_call(
        functools.partial(_outproj_kernel, n_o=len(o_parts)),
        grid=(m // TM_OUT,),
        in_specs=[pl.BlockSpec((TM_OUT, o.shape[1]), lambda i: (i, 0)) for o in o_parts] + [
            pl.BlockSpec((None, D_MIX, D_MODEL), lambda i: (layer, 0, 0)),
            full,
            pl.BlockSpec((1, D_MODEL), lambda i: (0, 0)),
            _mod_spec(2, row),
        ],
        out_specs=full,
        out_shape=jax.ShapeDtypeStruct((m, D_MODEL), F32),
        compiler_params=_params(("parallel",)),
        name="outproj",
    )(*o_parts, w_out, x, g_post, mod_l)


def _mlp_kernel(x_ref, gpre_ref, sc_ref, sh_ref, wup_ref, wdn_ref, gpost_ref, gt_ref,
                out_ref, h_ref, acc_ref, gm_ref):
    k = pl.program_id(1)

    @pl.when(k == 0)
    def _():
        _modulated_norm(h_ref, x_ref, gpre_ref, sc_ref, sh_ref, gm_ref)
        acc_ref[...] = jnp.zeros_like(acc_ref)

    u = _mm(h_ref[...], wup_ref[...])
    a = jnp.square(jnp.maximum(u, 0.0)).astype(BF16)
    acc_ref[...] += _mm(a, wdn_ref[...])

    @pl.when(k == pl.num_programs(1) - 1)
    def _():
        _gated_norm_residual(out_ref, x_ref, acc_ref, gpost_ref, gt_ref, gm_ref)


def _mlp(x, g_pre, g_post, mod_l, w_up, w_down, layer, rows_per_batch, first_row):
    m = x.shape[0]
    row = _row_fn(TM_MLP, rows_per_batch, first_row)
    full = pl.BlockSpec((TM_MLP, D_MODEL), lambda i, k: (i, 0))
    vec = pl.BlockSpec((1, D_MODEL), lambda i, k: (0, 0))
    return pl.pallas_call(
        _mlp_kernel,
        grid=(m // TM_MLP, D_FF // TF_MLP),
        in_specs=[
            full, vec, _mod_spec(4, row), _mod_spec(3, row),
            pl.BlockSpec((None, D_MODEL, TF_MLP), lambda i, k: (layer, 0, k)),
            pl.BlockSpec((None, TF_MLP, D_MODEL), lambda i, k: (layer, k, 0)),
            vec, _mod_spec(5, row),
        ],
        out_specs=full,
        out_shape=jax.ShapeDtypeStruct((m, D_MODEL), F32),
        scratch_shapes=[pltpu.VMEM((TM_MLP, D_MODEL), BF16), pltpu.VMEM((TM_MLP, D_MODEL), F32),
                        pltpu.VMEM((1, D_MODEL), F32)],
        compiler_params=_params(("parallel", "arbitrary")),
        name="mlp",
    )(x, g_pre, mod_l, mod_l, w_up, w_down, g_post, mod_l)


def _rope_tables(half, n_rep):
    t = jnp.arange(DEC_SEQ)
    inv = ROPE_THETA ** (-jnp.arange(half, dtype=F32) / half)

    def cs(pos):
        ang = pos.astype(F32)[:, None] * inv[None, :]
        c, s = jnp.cos(ang), jnp.sin(ang)
        return jnp.concatenate([c, c], axis=-1), jnp.concatenate([-s, s], axis=-1)

    cr, sr = cs(t // GRID_W)
    cc, sc = cs(t % GRID_W)
    return (jnp.concatenate([cr, cc] * n_rep, axis=-1),
            jnp.concatenate([sr, sc] * n_rep, axis=-1))


def _na_bias(rpb):
    qc = np.arange(GRID_W)[:, None]
    kc = np.arange(GRID_W)[None, :]
    ws = np.clip(qc - NA_KW // 2, 0, GRID_W - NA_KW)
    valid = (kc >= ws) & (kc < ws + NA_KW)
    dcol = np.clip(kc - qc + NA_KW - 1, 0, 2 * NA_KW - 2)
    n_dcol = 2 * NA_KW - 1
    onehot = (dcol.reshape(-1)[None, :] == np.arange(n_dcol)[:, None]).astype(np.float32)
    t = jnp.einsum("lhdc,cq->lhdq", rpb.astype(F32), jnp.asarray(onehot),
                   precision=lax.Precision.HIGHEST)
    t = t.reshape(DEPTH, NA_HEADS, 2 * NA_KH - 1, GRID_W, GRID_W) * LOG2E
    t = jnp.where(jnp.asarray(valid)[None, None, None], t, NEG_BIG)
    masked = jnp.full((DEPTH, NA_HEADS, GRID_W, GRID_W), NEG_BIG, F32)
    ctx = jnp.zeros((DEPTH, NA_HEADS, NA_QTOK, PAST_LEN), F32)
    groups = []
    for j in range(NA_GROUPS):
        s0 = NA_SLAB_START[j]
        q_rows = []
        for r in range(j * NA_QROWS, (j + 1) * NA_QROWS):
            r0 = min(max(r - NA_KH // 2, 0), GRID_ROWS - NA_KH)
            blocks = []
            for key_row in range(s0, s0 + NA_SLAB_ROWS):
                in_window = r0 <= key_row < r0 + NA_KH
                blocks.append(t[:, :, key_row - r + NA_KH - 1] if in_window else masked)
            q_rows.append(jnp.concatenate(blocks, axis=-1))
        slab = jnp.concatenate(q_rows, axis=-2)
        groups.append(jnp.concatenate([ctx, slab] if s0 == 0 else [slab, ctx], axis=-1))
    return jnp.stack(groups, axis=2)


def kernel(x_prompt, x_sample, c, cache_na_k, cache_na_v, cache_gqa_k, cache_gqa_v,
           cache_diff_k, cache_diff_v, c_ctx, w_ada, b_ada, norm_g, w_in, w_out, na_rpb,
           gqa_q_g, gqa_k_g, diff_lam, diff_g, w_up, w_down):
    np_rows = BATCH * SEQ
    ns_rows = DEC_BATCH * DEC_SEQ
    xp = x_prompt.reshape(np_rows, D_MODEL)
    xs = x_sample.reshape(ns_rows, D_MODEL)

    cv = jnp.concatenate(
        [c_ctx[None, :], c, jnp.zeros((MOD_ROWS - 1 - DEC_BATCH, D_MODEL), F32)], axis=0)
    mod = _modulation(cv, w_ada, b_ada).reshape(DEPTH, MOD_ROWS, N_MOD, 1, D_MODEL)

    w_in_b = w_in.astype(BF16)
    w_out_b = w_out.astype(BF16)
    w_up_b = w_up.astype(BF16)
    w_down_b = w_down.astype(BF16)

    def flat_cache(a):
        return a.reshape(DEC_BATCH, DEPTH, PAST_LEN, a.shape[3] * HEAD_DIM)

    c_na_k, c_na_v = flat_cache(cache_na_k), flat_cache(cache_na_v)
    c_g_k, c_g_v = flat_cache(cache_gqa_k), flat_cache(cache_gqa_v)
    c_d_k, c_d_v = flat_cache(cache_diff_k), flat_cache(cache_diff_v)

    cos_g, sin_g = _rope_tables(HEAD_DIM // 4, 1)
    cos_d, sin_d = _rope_tables(DIFF_QK_DIM // 4, 2)
    na_bias = _na_bias(na_rpb)

    new_kv = [jnp.zeros((BATCH, DEPTH, SEQ, nh, HEAD_DIM), F32) for nh in KV_HEADS]
    for l in range(DEPTH):
        lambda_init = 0.8 - 0.6 * math.exp(-0.3 * l)
        mod_l = mod[l]
        g = norm_g[l].reshape(4, 1, D_MODEL)
        q_g = gqa_q_g[l].reshape(1, HEAD_DIM)
        k_g = gqa_k_g[l].reshape(1, HEAD_DIM)
        d_g = diff_g[l].reshape(1, HEAD_DIM)
        lam_vec = diff_lam[l]

        qkv_p = _inproj(xp, g[0], mod_l, w_in_b, l, None, 0)
        o_p, *new_kv = _ctx_attn(qkv_p, q_g, k_g, lam_vec, d_g, new_kv, l, lambda_init)
        xp = _outproj([o_p], w_out_b, l, xp, g[1], mod_l, None, 0)
        xp = _mlp(xp, g[2], g[3], mod_l, w_up_b, w_down_b, l, None, 0)

        qkv_s = _inproj(xs, g[0], mod_l, w_in_b, l, DEC_SEQ, 1)
        o_na = _lat_na(qkv_s, c_na_k, c_na_v, na_bias, l)
        o_gqa = _lat_gqa(qkv_s, c_g_k, c_g_v, cos_g, sin_g, q_g, k_g, l)
        o_diff = _lat_diff(qkv_s, c_d_k, c_d_v, cos_d, sin_d, lam_vec, d_g, l, lambda_init)
        xs = _outproj([o_na, o_gqa, o_diff], w_out_b, l, xs, g[1], mod_l, DEC_SEQ, 1)
        xs = _mlp(xs, g[2], g[3], mod_l, w_up_b, w_down_b, l, DEC_SEQ, 1)

    return (xp.reshape(BATCH, SEQ, D_MODEL), xs.reshape(DEC_BATCH, DEC_SEQ, D_MODEL), *new_kv)
```

```python
import functools
import math

import jax
import jax.numpy as jnp
import numpy as np
from jax import lax
from jax.experimental import pallas as pl
from jax.experimental.pallas import tpu as pltpu

D_MODEL = 2048
BATCH = 16
SEQ = 256
DEPTH = 4
DEC_BATCH = 8
DEC_SEQ = 1024
PAST_LEN = 256
GRID_W = 64
GRID_ROWS = DEC_SEQ // GRID_W
HEAD_DIM = 128
NA_HEADS = 4
GQA_Q_HEADS = 8
GQA_KV_HEADS = 2
GQA_GROUP = GQA_Q_HEADS // GQA_KV_HEADS
DIFF_HEADS = 4
DIFF_QK_DIM = HEAD_DIM // 2
NA_KH = 8
NA_KW = 16
D_FF = 4 * D_MODEL
ROPE_THETA = 10000.0
EPS = 1e-6
N_MOD = 6
NEG_BIG = -1e30

NA_W = NA_HEADS * HEAD_DIM
GQA_QW = GQA_Q_HEADS * HEAD_DIM
GQA_KVW = GQA_KV_HEADS * HEAD_DIM
DIFF_W = DIFF_HEADS * HEAD_DIM
D_IN = 3 * NA_W + GQA_QW + 2 * GQA_KVW + 3 * DIFF_W
D_MIX = NA_W + GQA_QW + DIFF_W
C_NA_Q, C_NA_K, C_NA_V = 0, NA_W, 2 * NA_W
C_G_Q = 3 * NA_W
C_G_K = C_G_Q + GQA_QW
C_G_V = C_G_K + GQA_KVW
C_D_Q = C_G_V + GQA_KVW
C_D_K = C_D_Q + DIFF_W
C_D_V = C_D_K + DIFF_W

KV_HEADS = (NA_HEADS, NA_HEADS, GQA_KV_HEADS, GQA_KV_HEADS, DIFF_HEADS, DIFF_HEADS)
N_KEYS = DEC_SEQ + PAST_LEN
LOG2E = 1.4426950408889634
QSCALE = HEAD_DIM ** -0.5 * LOG2E
DIFF_QSCALE = DIFF_QK_DIM ** -0.5 * LOG2E

NA_QROWS = 4
NA_QTOK = NA_QROWS * GRID_W
NA_GROUPS = GRID_ROWS // NA_QROWS
NA_SLAB_ROWS = 12
NA_SLAB = NA_SLAB_ROWS * GRID_W
NA_NKEY = NA_SLAB + PAST_LEN
NA_SLAB_START = (0, 0, 4, 4)
MOD_ROWS = 16

F32 = jnp.float32
BF16 = jnp.bfloat16

VMEM_LIMIT = 52 * 1024 * 1024

TM_IN = 1024
TN_IN = 1536
TM_OUT = 512
TM_MLP = 512
TF_MLP = 1024
TN_ADA = 1024
TQ = 512
ATT_ROWS = 256
NORM_ROWS = 128


def _params(sem):
    return pltpu.CompilerParams(dimension_semantics=sem, vmem_limit_bytes=VMEM_LIMIT)


def _rms(x, g):
    ms = jnp.mean(x * x, axis=-1, keepdims=True)
    return x * lax.rsqrt(ms + EPS) * g


def _nt(a, b):
    return lax.dot_general(a, b, (((1,), (1,)), ((), ())), preferred_element_type=F32)


def _mm(a, b):
    return jnp.dot(a, b, preferred_element_type=F32)


def _scores_exp(q, k, bias=None):
    s = _nt(q, k)
    if bias is not None:
        s = s + bias
    return jnp.exp2(s - jnp.max(s, axis=-1, keepdims=True)).astype(BF16)


def _weighted(e, v_ones):
    oa = _mm(e, v_ones)
    d = v_ones.shape[1] // 2
    return oa[:, :d] / oa[:, d:]


def _with_ones(v):
    return jnp.concatenate([v, jnp.ones_like(v)], axis=1)

def _rope(x, cos, sin_signed, half):
    n = x.shape[-1]
    lane = lax.broadcasted_iota(jnp.int32, x.shape, 1)
    first = (lane % (2 * half)) < half
    rot = jnp.where(first, pltpu.roll(x, n - half, 1), pltpu.roll(x, half, 1))
    return x * cos + rot * sin_signed


def _diff_lambda(lam_ref, lambda_init):
    lf = lam_ref[...]
    a = jnp.sum(lf[0:1] * lf[1:2], axis=-1, keepdims=True)
    b = jnp.sum(lf[2:3] * lf[3:4], axis=-1, keepdims=True)
    return jnp.exp(a) - jnp.exp(b) + lambda_init


def _mod_kernel(cv_ref, w_ref, b_ref, o_ref):
    cv = cv_ref[...]
    s = cv / (1.0 + jnp.exp(-cv))
    o_ref[...] = _mm(s.astype(BF16), w_ref[...].astype(BF16)) + b_ref[...]


def _modulation(cv, w_ada, b_ada):
    n = N_MOD * D_MODEL
    return pl.pallas_call(
        _mod_kernel,
        grid=(DEPTH, n // TN_ADA),
        in_specs=[
            pl.BlockSpec((MOD_ROWS, D_MODEL), lambda l, j: (0, 0)),
            pl.BlockSpec((None, D_MODEL, TN_ADA), lambda l, j: (l, 0, j)),
            pl.BlockSpec((None, 1, TN_ADA), lambda l, j: (l, 0, j)),
        ],
        out_specs=pl.BlockSpec((None, MOD_ROWS, TN_ADA), lambda l, j: (l, 0, j)),
        out_shape=jax.ShapeDtypeStruct((DEPTH, MOD_ROWS, n), F32),
        compiler_params=_params(("parallel", "parallel")),
        name="modulation",
    )(cv, w_ada, b_ada.reshape(DEPTH, 1, n))


def _mod_spec(chunk, row_fn):
    return pl.BlockSpec((None, None, 1, D_MODEL), lambda i, *_: (row_fn(i), chunk, 0, 0))


def _row_fn(tm, rows_per_batch, first_row):
    if rows_per_batch is None:
        return lambda i: first_row
    return lambda i: first_row + (i * tm) // rows_per_batch


def _row_chunks(n_rows, body):
    def step(c, carry):
        body(pl.ds(pl.multiple_of(c * NORM_ROWS, NORM_ROWS), NORM_ROWS))
        return carry

    lax.fori_loop(0, n_rows // NORM_ROWS, step, 0)


def _modulated_norm(h_ref, x_ref, g_ref, sc_ref, sh_ref, gm_ref):
    gm_ref[...] = g_ref[...] * (1.0 + sc_ref[...])

    def body(rows):
        x = x_ref[rows, :]
        r = lax.rsqrt(jnp.mean(x * x, axis=-1, keepdims=True) + EPS)
        h_ref[rows, :] = (x * r * gm_ref[...] + sh_ref[...]).astype(BF16)

    _row_chunks(x_ref.shape[0], body)


def _gated_norm_residual(out_ref, x_ref, y_ref, g_ref, gt_ref, gm_ref):
    gm_ref[...] = gt_ref[...] * g_ref[...]

    def body(rows):
        y = y_ref[rows, :]
        r = lax.rsqrt(jnp.mean(y * y, axis=-1, keepdims=True) + EPS)
        out_ref[rows, :] = x_ref[rows, :] + y * r * gm_ref[...]

    _row_chunks(x_ref.shape[0], body)


def _inproj_kernel(x_ref, g_ref, sc_ref, sh_ref, w_ref, o_ref, h_ref, gm_ref):
    @pl.when(pl.program_id(1) == 0)
    def _():
        _modulated_norm(h_ref, x_ref, g_ref, sc_ref, sh_ref, gm_ref)

    o_ref[...] = _mm(h_ref[...], w_ref[...])


def _inproj(x, g_pre, mod_l, w_in, layer, rows_per_batch, first_row):
    m = x.shape[0]
    row = _row_fn(TM_IN, rows_per_batch, first_row)
    return pl.pallas_call(
        _inproj_kernel,
        grid=(m // TM_IN, D_IN // TN_IN),
        in_specs=[
            pl.BlockSpec((TM_IN, D_MODEL), lambda i, j: (i, 0)),
            pl.BlockSpec((1, D_MODEL), lambda i, j: (0, 0)),
            _mod_spec(1, row),
            _mod_spec(0, row),
            pl.BlockSpec((None, D_MODEL, TN_IN), lambda i, j: (layer, 0, j)),
        ],
        out_specs=pl.BlockSpec((TM_IN, TN_IN), lambda i, j: (i, j)),
        out_shape=jax.ShapeDtypeStruct((m, D_IN), F32),
        scratch_shapes=[pltpu.VMEM((TM_IN, D_MODEL), BF16), pltpu.VMEM((1, D_MODEL), F32)],
        compiler_params=_params(("parallel", "arbitrary")),
        name="inproj",
    )(x, g_pre, mod_l, mod_l, w_in)


def _ctx_attn_kernel(qkv_ref, qg_ref, kg_ref, lam_ref, dg_ref, *rest, lambda_init):
    o_ref, nak_ref, nav_ref, gk_ref, gv_ref, dk_ref, dv_ref = rest[len(KV_HEADS):]
    hd = HEAD_DIM

    def cols(c0, h):
        return qkv_ref[:, c0 + h * hd:c0 + (h + 1) * hd]

    for ref, c0 in ((nak_ref, C_NA_K), (nav_ref, C_NA_V), (gv_ref, C_G_V),
                    (dk_ref, C_D_K), (dv_ref, C_D_V)):
        for h in range(ref.shape[1]):
            ref[:, h, :] = cols(c0, h)

    for h in range(NA_HEADS):
        q = (cols(C_NA_Q, h) * QSCALE).astype(BF16)
        k = cols(C_NA_K, h).astype(BF16)
        v1 = _with_ones(cols(C_NA_V, h).astype(BF16))
        o_ref[:, h * hd:(h + 1) * hd] = _weighted(_scores_exp(q, k), v1).astype(BF16)

    for kv in range(GQA_KV_HEADS):
        kf = _rms(cols(C_G_K, kv), kg_ref[...])
        gk_ref[:, kv, :] = kf
        k = kf.astype(BF16)
        v1 = _with_ones(cols(C_G_V, kv).astype(BF16))
        for g in range(GQA_GROUP):
            hq = kv * GQA_GROUP + g
            q = (_rms(cols(C_G_Q, hq), qg_ref[...]) * QSCALE).astype(BF16)
            o = _weighted(_scores_exp(q, k), v1)
            o_ref[:, NA_W + hq * hd:NA_W + (hq + 1) * hd] = o.astype(BF16)

    lam = _diff_lambda(lam_ref, lambda_init)
    lane = lax.broadcasted_iota(jnp.int32, (SEQ, hd), 1)
    for h in range(DIFF_HEADS):
        qf = cols(C_D_Q, h) * DIFF_QSCALE
        k = cols(C_D_K, h).astype(BF16)
        v1 = _with_ones(cols(C_D_V, h).astype(BF16))
        q0 = jnp.where(lane < DIFF_QK_DIM, qf, 0.0).astype(BF16)
        q1 = jnp.where(lane >= DIFF_QK_DIM, qf, 0.0).astype(BF16)
        od = _weighted(_scores_exp(q0, k), v1) - lam * _weighted(_scores_exp(q1, k), v1)
        od = _rms(od, dg_ref[...]) * (1.0 - lambda_init)
        c0 = NA_W + GQA_QW + h * hd
        o_ref[:, c0:c0 + hd] = od.astype(BF16)


def _ctx_attn(qkv, q_g, k_g, lam_vec, diff_g, kv_bufs, layer, lambda_init):
    n = qkv.shape[0]
    vec = pl.BlockSpec((1, HEAD_DIM), lambda b: (0, 0))
    n_in = 5

    def rows(w):
        return pl.BlockSpec((SEQ, w), lambda b: (b, 0))

    def heads(nh):
        return pl.BlockSpec((None, None, SEQ, nh, HEAD_DIM), lambda b: (b, layer, 0, 0, 0))

    return pl.pallas_call(
        functools.partial(_ctx_attn_kernel, lambda_init=lambda_init),
        grid=(n // SEQ,),
        in_specs=[rows(D_IN), vec, vec,
                  pl.BlockSpec((4, DIFF_QK_DIM), lambda b: (0, 0)), vec]
                 + [pl.BlockSpec(memory_space=pl.ANY)] * len(kv_bufs),
        out_specs=[rows(D_MIX)] + [heads(nh) for nh in KV_HEADS],
        out_shape=[jax.ShapeDtypeStruct((n, D_MIX), BF16)]
                  + [jax.ShapeDtypeStruct(a.shape, a.dtype) for a in kv_bufs],
        input_output_aliases={n_in + i: 1 + i for i in range(len(kv_bufs))},
        compiler_params=_params(("parallel",)),
        name="ctx_attn",
    )(qkv, q_g, k_g, lam_vec, diff_g, *kv_bufs)


def _lat_na_kernel(q_ref, k_ref, v_ref, kc_ref, vc_ref, bias_ref, o_ref, kbig_ref, vbig_ref):
    lat0, lat1 = PAST_LEN, PAST_LEN + DEC_SEQ
    kc = kc_ref[...].astype(BF16)
    vc = _with_ones(vc_ref[...].astype(BF16))
    kbig_ref[0:lat0, :] = kc
    kbig_ref[lat0:lat1, :] = k_ref[...].astype(BF16)
    kbig_ref[lat1:, :] = kc
    vbig_ref[0:lat0, :] = vc
    vbig_ref[lat0:lat1, :] = _with_ones(v_ref[...].astype(BF16))
    vbig_ref[lat1:, :] = vc
    for j in range(NA_GROUPS):
        s0 = NA_SLAB_START[j]
        w0 = 0 if s0 == 0 else PAST_LEN + s0 * GRID_W
        q = (q_ref[j * NA_QTOK:(j + 1) * NA_QTOK, :] * QSCALE).astype(BF16)
        e = _scores_exp(q, kbig_ref[w0:w0 + NA_NKEY, :], bias_ref[j])
        o = _weighted(e, vbig_ref[w0:w0 + NA_NKEY, :])
        o_ref[j * NA_QTOK:(j + 1) * NA_QTOK, :] = o.astype(BF16)


def _lat_na(qkv, cache_k, cache_v, bias, layer):
    hd = HEAD_DIM
    cq, ck, cv = C_NA_Q // hd, C_NA_K // hd, C_NA_V // hd
    cache = pl.BlockSpec((None, None, PAST_LEN, hd), lambda h, b: (b, layer, 0, h))
    nbig = DEC_SEQ + 2 * PAST_LEN
    return pl.pallas_call(
        _lat_na_kernel,
        grid=(NA_HEADS, DEC_BATCH),
        in_specs=[
            pl.BlockSpec((DEC_SEQ, hd), lambda h, b: (b, cq + h)),
            pl.BlockSpec((DEC_SEQ, hd), lambda h, b: (b, ck + h)),
            pl.BlockSpec((DEC_SEQ, hd), lambda h, b: (b, cv + h)),
            cache, cache,
            pl.BlockSpec((None, None, NA_GROUPS, NA_QTOK, NA_NKEY),
                         lambda h, b: (layer, h, 0, 0, 0)),
        ],
        out_specs=pl.BlockSpec((DEC_SEQ, hd), lambda h, b: (b, h)),
        out_shape=jax.ShapeDtypeStruct((DEC_BATCH * DEC_SEQ, NA_W), BF16),
        scratch_shapes=[pltpu.VMEM((nbig, hd), BF16), pltpu.VMEM((nbig, 2 * hd), BF16)],
        compiler_params=_params(("parallel", "parallel")),
        name="lat_na",
    )(qkv, qkv, qkv, cache_k, cache_v, bias)


def _fill_keys(kall_ref, vall_ref, k_lat, v_ref, kc_ref, vc_ref):
    kall_ref[0:DEC_SEQ, :] = k_lat.astype(BF16)
    kall_ref[DEC_SEQ:N_KEYS, :] = kc_ref[...].astype(BF16)
    vall_ref[0:DEC_SEQ, :] = _with_ones(v_ref[...].astype(BF16))
    vall_ref[DEC_SEQ:N_KEYS, :] = _with_ones(vc_ref[...].astype(BF16))


def _lat_gqa_kernel(q_ref, k_ref, v_ref, kc_ref, vc_ref, cos_ref, sin_ref, qg_ref, kg_ref,
                    o_ref, kall_ref, vall_ref):
    hd = HEAD_DIM
    half = HEAD_DIM // 4
    qb = pl.program_id(2)

    @pl.when(qb == 0)
    def _():
        kf = _rope(_rms(k_ref[...], kg_ref[...]), cos_ref[...], sin_ref[...], half)
        _fill_keys(kall_ref, vall_ref, kf, v_ref, kc_ref, vc_ref)

    for s in range(TQ // ATT_ROWS):
        rows = slice(s * ATT_ROWS, (s + 1) * ATT_ROWS)
        row0 = pl.multiple_of(qb * TQ + s * ATT_ROWS, ATT_ROWS)
        cos = cos_ref[pl.ds(row0, ATT_ROWS), :]
        sin = sin_ref[pl.ds(row0, ATT_ROWS), :]
        for g in range(GQA_GROUP):
            qf = _rms(q_ref[rows, g * hd:(g + 1) * hd], qg_ref[...])
            q = (_rope(qf, cos, sin, half) * QSCALE).astype(BF16)
            o = _weighted(_scores_exp(q, kall_ref[...]), vall_ref[...])
            o_ref[rows, g * hd:(g + 1) * hd] = o.astype(BF16)


def _lat_gqa(qkv, cache_k, cache_v, cos, sin, q_g, k_g, layer):
    hd = HEAD_DIM
    gw = GQA_GROUP * hd
    nqb = DEC_SEQ // TQ
    cq, ck, cv = C_G_Q // gw, C_G_K // hd, C_G_V // hd
    cache = pl.BlockSpec((None, None, PAST_LEN, hd), lambda b, kv, qb: (b, layer, 0, kv))
    table = pl.BlockSpec((DEC_SEQ, hd), lambda b, kv, qb: (0, 0))
    vec = pl.BlockSpec((1, hd), lambda b, kv, qb: (0, 0))
    return pl.pallas_call(
        _lat_gqa_kernel,
        grid=(DEC_BATCH, GQA_KV_HEADS, nqb),
        in_specs=[
            pl.BlockSpec((TQ, gw), lambda b, kv, qb: (b * nqb + qb, cq + kv)),
            pl.BlockSpec((DEC_SEQ, hd), lambda b, kv, qb: (b, ck + kv)),
            pl.BlockSpec((DEC_SEQ, hd), lambda b, kv, qb: (b, cv + kv)),
            cache, cache, table, table, vec, vec,
        ],
        out_specs=pl.BlockSpec((TQ, gw), lambda b, kv, qb: (b * nqb + qb, kv)),
        out_shape=jax.ShapeDtypeStruct((DEC_BATCH * DEC_SEQ, GQA_QW), BF16),
        scratch_shapes=[pltpu.VMEM((N_KEYS, hd), BF16), pltpu.VMEM((N_KEYS, 2 * hd), BF16)],
        compiler_params=_params(("parallel", "parallel", "arbitrary")),
        name="lat_gqa",
    )(qkv, qkv, qkv, cache_k, cache_v, cos, sin, q_g, k_g)


def _lat_diff_kernel(q_ref, k_ref, v_ref, kc_ref, vc_ref, cos_ref, sin_ref, lam_ref, dg_ref,
                     o_ref, kall_ref, vall_ref, *, lambda_init):
    half = DIFF_QK_DIM // 4
    qb = pl.program_id(2)

    @pl.when(qb == 0)
    def _():
        kf = _rope(k_ref[...], cos_ref[...], sin_ref[...], half)
        _fill_keys(kall_ref, vall_ref, kf, v_ref, kc_ref, vc_ref)

    lam = _diff_lambda(lam_ref, lambda_init)
    lane = lax.broadcasted_iota(jnp.int32, (ATT_ROWS, HEAD_DIM), 1)
    for s in range(TQ // ATT_ROWS):
        rows = slice(s * ATT_ROWS, (s + 1) * ATT_ROWS)
        row0 = pl.multiple_of(qb * TQ + s * ATT_ROWS, ATT_ROWS)
        cos = cos_ref[pl.ds(row0, ATT_ROWS), :]
        sin = sin_ref[pl.ds(row0, ATT_ROWS), :]
        qf = _rope(q_ref[rows, :], cos, sin, half) * DIFF_QSCALE
        q0 = jnp.where(lane < DIFF_QK_DIM, qf, 0.0).astype(BF16)
        q1 = jnp.where(lane >= DIFF_QK_DIM, qf, 0.0).astype(BF16)
        od = (_weighted(_scores_exp(q0, kall_ref[...]), vall_ref[...])
              - lam * _weighted(_scores_exp(q1, kall_ref[...]), vall_ref[...]))
        od = _rms(od, dg_ref[...]) * (1.0 - lambda_init)
        o_ref[rows, :] = od.astype(BF16)


def _lat_diff(qkv, cache_k, cache_v, cos, sin, lam_vec, diff_g, layer, lambda_init):
    hd = HEAD_DIM
    nqb = DEC_SEQ // TQ
    cq, ck, cv = C_D_Q // hd, C_D_K // hd, C_D_V // hd
    cache = pl.BlockSpec((None, None, PAST_LEN, hd), lambda b, h, qb: (b, layer, 0, h))
    table = pl.BlockSpec((DEC_SEQ, hd), lambda b, h, qb: (0, 0))
    return pl.pallas_call(
        functools.partial(_lat_diff_kernel, lambda_init=lambda_init),
        grid=(DEC_BATCH, DIFF_HEADS, nqb),
        in_specs=[
            pl.BlockSpec((TQ, hd), lambda b, h, qb: (b * nqb + qb, cq + h)),
            pl.BlockSpec((DEC_SEQ, hd), lambda b, h, qb: (b, ck + h)),
            pl.BlockSpec((DEC_SEQ, hd), lambda b, h, qb: (b, cv + h)),
            cache, cache, table, table,
            pl.BlockSpec((4, DIFF_QK_DIM), lambda b, h, qb: (0, 0)),
            pl.BlockSpec((1, hd), lambda b, h, qb: (0, 0)),
        ],
        out_specs=pl.BlockSpec((TQ, hd), lambda b, h, qb: (b * nqb + qb, h)),
        out_shape=jax.ShapeDtypeStruct((DEC_BATCH * DEC_SEQ, DIFF_W), BF16),
        scratch_shapes=[pltpu.VMEM((N_KEYS, hd), BF16), pltpu.VMEM((N_KEYS, 2 * hd), BF16)],
        compiler_params=_params(("parallel", "parallel", "arbitrary")),
        name="lat_diff",
    )(qkv, qkv, qkv, cache_k, cache_v, cos, sin, lam_vec, diff_g)


def _outproj_kernel(*refs, n_o):
    o_refs = refs[:n_o]
    w_ref, x_ref, g_ref, gt_ref, out_ref = refs[n_o:]
    y = None
    off = 0
    for o_ref in o_refs:
        wd = o_ref.shape[1]
        part = _mm(o_ref[...], w_ref[off:off + wd, :])
        y = part if y is None else y + part
        off += wd
    out_ref[...] = x_ref[...] + gt_ref[...] * _rms(y, g_ref[...])


def _outproj(o_parts, w_out, layer, x, g_post, mod_l, rows_per_batch, first_row):
    m = x.shape[0]
    row = _row_fn(TM_OUT, rows_per_batch, first_row)
    full = pl.BlockSpec((TM_OUT, D_MODEL), lambda i: (i, 0))
    return pl.pallas_call(
        functools.partial(_outproj_kernel, n_o=len(o_parts)),
        grid=(m // TM_OUT,),
        in_specs=[pl.BlockSpec((TM_OUT, o.shape[1]), lambda i: (i, 0)) for o in o_parts] + [
            pl.BlockSpec((None, D_MIX, D_MODEL), lambda i: (layer, 0, 0)),
            full,
            pl.BlockSpec((1, D_MODEL), lambda i: (0, 0)),
            _mod_spec(2, row),
        ],
        out_specs=full,
        out_shape=jax.ShapeDtypeStruct((m, D_MODEL), F32),
        compiler_params=_params(("parallel",)),
        name="outproj",
    )(*o_parts, w_out, x, g_post, mod_l)


def _mlp_kernel(x_ref, gpre_ref, sc_ref, sh_ref, wup_ref, wdn_ref, gpost_ref, gt_ref,
                out_ref, h_ref, acc_ref, gm_ref):
    k = pl.program_id(1)

    @pl.when(k == 0)
    def _():
        _modulated_norm(h_ref, x_ref, gpre_ref, sc_ref, sh_ref, gm_ref)
        acc_ref[...] = jnp.zeros_like(acc_ref)

    u = _mm(h_ref[...], wup_ref[...])
    a = jnp.square(jnp.maximum(u, 0.0)).astype(BF16)
    acc_ref[...] += _mm(a, wdn_ref[...])

    @pl.when(k == pl.num_programs(1) - 1)
    def _():
        _gated_norm_residual(out_ref, x_ref, acc_ref, gpost_ref, gt_ref, gm_ref)


def _mlp(x, g_pre, g_post, mod_l, w_up, w_down, layer, rows_per_batch, first_row):
    m = x.shape[0]
    row = _row_fn(TM_MLP, rows_per_batch, first_row)
    full = pl.BlockSpec((TM_MLP, D_MODEL), lambda i, k: (i, 0))
    vec = pl.BlockSpec((1, D_MODEL), lambda i, k: (0, 0))
    return pl.pallas_call(
        _mlp_kernel,
        grid=(m // TM_MLP, D_FF // TF_MLP),
        in_specs=[
            full, vec, _mod_spec(4, row), _mod_spec(3, row),
            pl.BlockSpec((None, D_MODEL, TF_MLP), lambda i, k: (layer, 0, k)),
            pl.BlockSpec((None, TF_MLP, D_MODEL), lambda i, k: (layer, k, 0)),
            vec, _mod_spec(5, row),
        ],
        out_specs=full,
        out_shape=jax.ShapeDtypeStruct((m, D_MODEL), F32),
        scratch_shapes=[pltpu.VMEM((TM_MLP, D_MODEL), BF16), pltpu.VMEM((TM_MLP, D_MODEL), F32),
                        pltpu.VMEM((1, D_MODEL), F32)],
        compiler_params=_params(("parallel", "arbitrary")),
        name="mlp",
    )(x, g_pre, mod_l, mod_l, w_up, w_down, g_post, mod_l)


def _rope_tables(half, n_rep):
    t = jnp.arange(DEC_SEQ)
    inv = ROPE_THETA ** (-jnp.arange(half, dtype=F32) / half)

    def cs(pos):
        ang = pos.astype(F32)[:, None] * inv[None, :]
        c, s = jnp.cos(ang), jnp.sin(ang)
        return jnp.concatenate([c, c], axis=-1), jnp.concatenate([-s, s], axis=-1)

    cr, sr = cs(t // GRID_W)
    cc, sc = cs(t % GRID_W)
    return (jnp.concatenate([cr, cc] * n_rep, axis=-1),
            jnp.concatenate([sr, sc] * n_rep, axis=-1))


def _na_bias(rpb):
    qc = np.arange(GRID_W)[:, None]
    kc = np.arange(GRID_W)[None, :]
    ws = np.clip(qc - NA_KW // 2, 0, GRID_W - NA_KW)
    valid = (kc >= ws) & (kc < ws + NA_KW)
    dcol = np.clip(kc - qc + NA_KW - 1, 0, 2 * NA_KW - 2)
    n_dcol = 2 * NA_KW - 1
    onehot = (dcol.reshape(-1)[None, :] == np.arange(n_dcol)[:, None]).astype(np.float32)
    t = jnp.einsum("lhdc,cq->lhdq", rpb.astype(F32), jnp.asarray(onehot),
                   precision=lax.Precision.HIGHEST)
    t = t.reshape(DEPTH, NA_HEADS, 2 * NA_KH - 1, GRID_W, GRID_W) * LOG2E
    t = jnp.where(jnp.asarray(valid)[None, None, None], t, NEG_BIG)
    masked = jnp.full((DEPTH, NA_HEADS, GRID_W, GRID_W), NEG_BIG, F32)
    ctx = jnp.zeros((DEPTH, NA_HEADS, NA_QTOK, PAST_LEN), F32)
    groups = []
    for j in range(NA_GROUPS):
        s0 = NA_SLAB_START[j]
        q_rows = []
        for r in range(j * NA_QROWS, (j + 1) * NA_QROWS):
            r0 = min(max(r - NA_KH // 2, 0), GRID_ROWS - NA_KH)
            blocks = []
            for key_row in range(s0, s0 + NA_SLAB_ROWS):
                in_window = r0 <= key_row < r0 + NA_KH
                blocks.append(t[:, :, key_row - r + NA_KH - 1] if in_window else masked)
            q_rows.append(jnp.concatenate(blocks, axis=-1))
        slab = jnp.concatenate(q_rows, axis=-2)
        groups.append(jnp.concatenate([ctx, slab] if s0 == 0 else [slab, ctx], axis=-1))
    return jnp.stack(groups, axis=2)


def kernel(x_prompt, x_sample, c, cache_na_k, cache_na_v, cache_gqa_k, cache_gqa_v,
           cache_diff_k, cache_diff_v, c_ctx, w_ada, b_ada, norm_g, w_in, w_out, na_rpb,
           gqa_q_g, gqa_k_g, diff_lam, diff_g, w_up, w_down):
    np_rows = BATCH * SEQ
    ns_rows = DEC_BATCH * DEC_SEQ
    xp = x_prompt.reshape(np_rows, D_MODEL)
    xs = x_sample.reshape(ns_rows, D_MODEL)

    cv = jnp.concatenate(
        [c_ctx[None, :], c, jnp.zeros((MOD_ROWS - 1 - DEC_BATCH, D_MODEL), F32)], axis=0)
    mod = _modulation(cv, w_ada, b_ada).reshape(DEPTH, MOD_ROWS, N_MOD, 1, D_MODEL)

    w_in_b = w_in.astype(BF16)
    w_out_b = w_out.astype(BF16)
    w_up_b = w_up.astype(BF16)
    w_down_b = w_down.astype(BF16)

    def flat_cache(a):
        return a.reshape(DEC_BATCH, DEPTH, PAST_LEN, a.shape[3] * HEAD_DIM)

    c_na_k, c_na_v = flat_cache(cache_na_k), flat_cache(cache_na_v)
    c_g_k, c_g_v = flat_cache(cache_gqa_k), flat_cache(cache_gqa_v)
    c_d_k, c_d_v = flat_cache(cache_diff_k), flat_cache(cache_diff_v)

    cos_g, sin_g = _rope_tables(HEAD_DIM // 4, 1)
    cos_d, sin_d = _rope_tables(DIFF_QK_DIM // 4, 2)
    na_bias = _na_bias(na_rpb)

    new_kv = [jnp.zeros((BATCH, DEPTH, SEQ, nh, HEAD_DIM), F32) for nh in KV_HEADS]
    for l in range(DEPTH):
        lambda_init = 0.8 - 0.6 * math.exp(-0.3 * l)
        mod_l = mod[l]
        g = norm_g[l].reshape(4, 1, D_MODEL)
        q_g = gqa_q_g[l].reshape(1, HEAD_DIM)
        k_g = gqa_k_g[l].reshape(1, HEAD_DIM)
        d_g = diff_g[l].reshape(1, HEAD_DIM)
        lam_vec = diff_lam[l]

        qkv_p = _inproj(xp, g[0], mod_l, w_in_b, l, None, 0)
        o_p, *new_kv = _ctx_attn(qkv_p, q_g, k_g, lam_vec, d_g, new_kv, l, lambda_init)
        xp = _outproj([o_p], w_out_b, l, xp, g[1], mod_l, None, 0)
        xp = _mlp(xp, g[2], g[3], mod_l, w_up_b, w_down_b, l, None, 0)

        qkv_s = _inproj(xs, g[0], mod_l, w_in_b, l, DEC_SEQ, 1)
        o_na = _lat_na(qkv_s, c_na_k, c_na_v, na_bias, l)
        o_gqa = _lat_gqa(qkv_s, c_g_k, c_g_v, cos_g, sin_g, q_g, k_g, l)
        o_diff = _lat_diff(qkv_s, c_d_k, c_d_v, cos_d, sin_d, lam_vec, d_g, l, lambda_init)
        xs = _outproj([o_na, o_gqa, o_diff], w_out_b, l, xs, g[1], mod_l, DEC_SEQ, 1)
        xs = _mlp(xs, g[2], g[3], mod_l, w_up_b, w_down_b, l, DEC_SEQ, 1)

    return (xp.reshape(BATCH, SEQ, D_MODEL), xs.reshape(DEC_BATCH, DEC_SEQ, D_MODEL), *new_kv)
```

```python
import functools
import math

import jax
import jax.numpy as jnp
import numpy as np
from jax import lax
from jax.experimental import pallas as pl
from jax.experimental.pallas import tpu as pltpu

D_MODEL = 2048
BATCH = 16
SEQ = 256
DEPTH = 4
DEC_BATCH = 8
DEC_SEQ = 1024
PAST_LEN = 256
GRID_W = 64
GRID_ROWS = DEC_SEQ // GRID_W
HEAD_DIM = 128
NA_HEADS = 4
GQA_Q_HEADS = 8
GQA_KV_HEADS = 2
GQA_GROUP = GQA_Q_HEADS // GQA_KV_HEADS
DIFF_HEADS = 4
DIFF_QK_DIM = HEAD_DIM // 2
NA_KH = 8
NA_KW = 16
D_FF = 4 * D_MODEL
ROPE_THETA = 10000.0
EPS = 1e-6
N_MOD = 6
NEG_BIG = -1e30

NA_W = NA_HEADS * HEAD_DIM
GQA_QW = GQA_Q_HEADS * HEAD_DIM
GQA_KVW = GQA_KV_HEADS * HEAD_DIM
DIFF_W = DIFF_HEADS * HEAD_DIM
D_IN = 3 * NA_W + GQA_QW + 2 * GQA_KVW + 3 * DIFF_W
D_MIX = NA_W + GQA_QW + DIFF_W
C_NA_Q, C_NA_K, C_NA_V = 0, NA_W, 2 * NA_W
C_G_Q = 3 * NA_W
C_G_K = C_G_Q + GQA_QW
C_G_V = C_G_K + GQA_KVW
C_D_Q = C_G_V + GQA_KVW
C_D_K = C_D_Q + DIFF_W
C_D_V = C_D_K + DIFF_W

KV_HEADS = (NA_HEADS, NA_HEADS, GQA_KV_HEADS, GQA_KV_HEADS, DIFF_HEADS, DIFF_HEADS)
N_KEYS = DEC_SEQ + PAST_LEN
LOG2E = 1.4426950408889634
QSCALE = HEAD_DIM ** -0.5 * LOG2E
DIFF_QSCALE = DIFF_QK_DIM ** -0.5 * LOG2E

NA_QROWS = 4
NA_QTOK = NA_QROWS * GRID_W
NA_GROUPS = GRID_ROWS // NA_QROWS
NA_SLAB_ROWS = 12
NA_SLAB = NA_SLAB_ROWS * GRID_W
NA_NKEY = NA_SLAB + PAST_LEN
NA_SLAB_START = (0, 0, 4, 4)
MOD_ROWS = 16

F32 = jnp.float32
BF16 = jnp.bfloat16

VMEM_LIMIT = 52 * 1024 * 1024

TM_IN = 1024
TN_IN = 1536
TM_OUT = 512
TM_MLP = 512
TF_MLP = 1024
TN_ADA = 1024
TQ = 512
ATT_ROWS = 256
NORM_ROWS = 128


def _params(sem):
    return pltpu.CompilerParams(dimension_semantics=sem, vmem_limit_bytes=VMEM_LIMIT)


def _rms(x, g):
    ms = jnp.mean(x * x, axis=-1, keepdims=True)
    return x * lax.rsqrt(ms + EPS) * g


def _nt(a, b):
    return lax.dot_general(a, b, (((1,), (1,)), ((), ())), preferred_element_type=F32)


def _mm(a, b):
    return jnp.dot(a, b, preferred_element_type=F32)


def _scores_exp(q, k, bias=None):
    s = _nt(q, k)
    if bias is not None:
        s = s + bias
    return jnp.exp2(s - jnp.max(s, axis=-1, keepdims=True)).astype(BF16)


def _weighted(e, v_ones):
    oa = _mm(e, v_ones)
    d = v_ones.shape[1] // 2
    return oa[:, :d] / oa[:, d:]


def _with_ones(v):
    return jnp.concatenate([v, jnp.ones_like(v)], axis=1)

def _rope(x, cos, sin_signed, half):
    n = x.shape[-1]
    lane = lax.broadcasted_iota(jnp.int32, x.shape, 1)
    first = (lane % (2 * half)) < half
    rot = jnp.where(first, pltpu.roll(x, n - half, 1), pltpu.roll(x, half, 1))
    return x * cos + rot * sin_signed


def _diff_lambda(lam_ref, lambda_init):
    lf = lam_ref[...]
    a = jnp.sum(lf[0:1] * lf[1:2], axis=-1, keepdims=True)
    b = jnp.sum(lf[2:3] * lf[3:4], axis=-1, keepdims=True)
    return jnp.exp(a) - jnp.exp(b) + lambda_init


def _mod_kernel(cv_ref, w_ref, b_ref, o_ref):
    cv = cv_ref[...]
    s = cv / (1.0 + jnp.exp(-cv))
    o_ref[...] = _mm(s.astype(BF16), w_ref[...].astype(BF16)) + b_ref[...]


def _modulation(cv, w_ada, b_ada):
    n = N_MOD * D_MODEL
    return pl.pallas_call(
        _mod_kernel,
        grid=(DEPTH, n // TN_ADA),
        in_specs=[
            pl.BlockSpec((MOD_ROWS, D_MODEL), lambda l, j: (0, 0)),
            pl.BlockSpec((None, D_MODEL, TN_ADA), lambda l, j: (l, 0, j)),
            pl.BlockSpec((None, 1, TN_ADA), lambda l, j: (l, 0, j)),
        ],
        out_specs=pl.BlockSpec((None, MOD_ROWS, TN_ADA), lambda l, j: (l, 0, j)),
        out_shape=jax.ShapeDtypeStruct((DEPTH, MOD_ROWS, n), F32),
        compiler_params=_params(("parallel", "parallel")),
        name="modulation",
    )(cv, w_ada, b_ada.reshape(DEPTH, 1, n))


def _mod_spec(chunk, row_fn):
    return pl.BlockSpec((None, None, 1, D_MODEL), lambda i, *_: (row_fn(i), chunk, 0, 0))


def _row_fn(tm, rows_per_batch, first_row):
    if rows_per_batch is None:
        return lambda i: first_row
    return lambda i: first_row + (i * tm) // rows_per_batch


def _row_chunks(n_rows, body):
    def step(c, carry):
        body(pl.ds(pl.multiple_of(c * NORM_ROWS, NORM_ROWS), NORM_ROWS))
        return carry

    lax.fori_loop(0, n_rows // NORM_ROWS, step, 0)


def _modulated_norm(h_ref, x_ref, g_ref, sc_ref, sh_ref, gm_ref):
    gm_ref[...] = g_ref[...] * (1.0 + sc_ref[...])

    def body(rows):
        x = x_ref[rows, :]
        r = lax.rsqrt(jnp.mean(x * x, axis=-1, keepdims=True) + EPS)
        h_ref[rows, :] = (x * r * gm_ref[...] + sh_ref[...]).astype(BF16)

    _row_chunks(x_ref.shape[0], body)


def _inproj_kernel(x_ref, g_ref, sc_ref, sh_ref, w_ref, o_ref, h_ref, gm_ref):
    @pl.when(pl.program_id(1) == 0)
    def _():
        _modulated_norm(h_ref, x_ref, g_ref, sc_ref, sh_ref, gm_ref)

    o_ref[...] = _mm(h_ref[...], w_ref[...])


def _inproj(x, g_pre, mod_l, w_in, layer, rows_per_batch, first_row):
    m = x.shape[0]
    row = _row_fn(TM_IN, rows_per_batch, first_row)
    return pl.pallas_call(
        _inproj_kernel,
        grid=(m // TM_IN, D_IN // TN_IN),
        in_specs=[
            pl.BlockSpec((TM_IN, D_MODEL), lambda i, j: (i, 0)),
            pl.BlockSpec((1, D_MODEL), lambda i, j: (0, 0)),
            _mod_spec(1, row),
            _mod_spec(0, row),
            pl.BlockSpec((None, D_MODEL, TN_IN), lambda i, j: (layer, 0, j)),
        ],
        out_specs=pl.BlockSpec((TM_IN, TN_IN), lambda i, j: (i, j)),
        out_shape=jax.ShapeDtypeStruct((m, D_IN), F32),
        scratch_shapes=[pltpu.VMEM((TM_IN, D_MODEL), BF16), pltpu.VMEM((1, D_MODEL), F32)],
        compiler_params=_params(("parallel", "arbitrary")),
        name="inproj",
    )(x, g_pre, mod_l, mod_l, w_in)


def _inproj_h_kernel(h_ref, w_ref, o_ref):
    o_ref[...] = _mm(h_ref[...], w_ref[...])


def _inproj_h(h, w_in, layer):
    m = h.shape[0]
    return pl.pallas_call(
        _inproj_h_kernel,
        grid=(m // TM_IN, D_IN // TN_IN),
        in_specs=[
            pl.BlockSpec((TM_IN, D_MODEL), lambda i, j: (i, 0)),
            pl.BlockSpec((None, D_MODEL, TN_IN), lambda i, j: (layer, 0, j)),
        ],
        out_specs=pl.BlockSpec((TM_IN, TN_IN), lambda i, j: (i, j)),
        out_shape=jax.ShapeDtypeStruct((m, D_IN), F32),
        compiler_params=_params(("parallel", "parallel")),
        name="inproj_h",
    )(h, w_in)


def _ctx_attn_kernel(qkv_ref, qg_ref, kg_ref, lam_ref, dg_ref, *rest, lambda_init):
    o_ref, nak_ref, nav_ref, gk_ref, gv_ref, dk_ref, dv_ref = rest[len(KV_HEADS):]
    hd = HEAD_DIM

    def cols(c0, h):
        return qkv_ref[:, c0 + h * hd:c0 + (h + 1) * hd]

    for ref, c0 in ((nak_ref, C_NA_K), (nav_ref, C_NA_V), (gv_ref, C_G_V),
                    (dk_ref, C_D_K), (dv_ref, C_D_V)):
        for h in range(ref.shape[1]):
            ref[:, h, :] = cols(c0, h)

    for h in range(NA_HEADS):
        q = (cols(C_NA_Q, h) * QSCALE).astype(BF16)
        k = cols(C_NA_K, h).astype(BF16)
        v1 = _with_ones(cols(C_NA_V, h).astype(BF16))
        o_ref[:, h * hd:(h + 1) * hd] = _weighted(_scores_exp(q, k), v1).astype(BF16)

    for kv in range(GQA_KV_HEADS):
        kf = _rms(cols(C_G_K, kv), kg_ref[...])
        gk_ref[:, kv, :] = kf
        k = kf.astype(BF16)
        v1 = _with_ones(cols(C_G_V, kv).astype(BF16))
        for g in range(GQA_GROUP):
            hq = kv * GQA_GROUP + g
            q = (_rms(cols(C_G_Q, hq), qg_ref[...]) * QSCALE).astype(BF16)
            o = _weighted(_scores_exp(q, k), v1)
            o_ref[:, NA_W + hq * hd:NA_W + (hq + 1) * hd] = o.astype(BF16)

    lam = _diff_lambda(lam_ref, lambda_init)
    lane = lax.broadcasted_iota(jnp.int32, (SEQ, hd), 1)
    for h in range(DIFF_HEADS):
        qf = cols(C_D_Q, h) * DIFF_QSCALE
        k = cols(C_D_K, h).astype(BF16)
        v1 = _with_ones(cols(C_D_V, h).astype(BF16))
        q0 = jnp.where(lane < DIFF_QK_DIM, qf, 0.0).astype(BF16)
        q1 = jnp.where(lane >= DIFF_QK_DIM, qf, 0.0).astype(BF16)
        od = _weighted(_scores_exp(q0, k), v1) - lam * _weighted(_scores_exp(q1, k), v1)
        od = _rms(od, dg_ref[...]) * (1.0 - lambda_init)
        c0 = NA_W + GQA_QW + h * hd
        o_ref[:, c0:c0 + hd] = od.astype(BF16)


def _ctx_attn(qkv, q_g, k_g, lam_vec, diff_g, kv_bufs, layer, lambda_init):
    n = qkv.shape[0]
    vec = pl.BlockSpec((1, HEAD_DIM), lambda b: (0, 0))
    n_in = 5

    def rows(w):
        return pl.BlockSpec((SEQ, w), lambda b: (b, 0))

    def heads(nh):
        return pl.BlockSpec((None, None, SEQ, nh, HEAD_DIM), lambda b: (b, layer, 0, 0, 0))

    return pl.pallas_call(
        functools.partial(_ctx_attn_kernel, lambda_init=lambda_init),
        grid=(n // SEQ,),
        in_specs=[rows(D_IN), vec, vec,
                  pl.BlockSpec((4, DIFF_QK_DIM), lambda b: (0, 0)), vec]
                 + [pl.BlockSpec(memory_space=pl.ANY)] * len(kv_bufs),
        out_specs=[rows(D_MIX)] + [heads(nh) for nh in KV_HEADS],
        out_shape=[jax.ShapeDtypeStruct((n, D_MIX), BF16)]
                  + [jax.ShapeDtypeStruct(a.shape, a.dtype) for a in kv_bufs],
        input_output_aliases={n_in + i: 1 + i for i in range(len(kv_bufs))},
        compiler_params=_params(("parallel",)),
        name="ctx_attn",
    )(qkv, q_g, k_g, lam_vec, diff_g, *kv_bufs)


def _lat_na_kernel(q_ref, k_ref, v_ref, kc_ref, vc_ref, bias_ref, o_ref, kbig_ref, vbig_ref):
    lat0, lat1 = PAST_LEN, PAST_LEN + DEC_SEQ
    kc = kc_ref[...].astype(BF16)
    vc = _with_ones(vc_ref[...].astype(BF16))
    kbig_ref[0:lat0, :] = kc
    kbig_ref[lat0:lat1, :] = k_ref[...].astype(BF16)
    kbig_ref[lat1:, :] = kc
    vbig_ref[0:lat0, :] = vc
    vbig_ref[lat0:lat1, :] = _with_ones(v_ref[...].astype(BF16))
    vbig_ref[lat1:, :] = vc
    for j in range(NA_GROUPS):
        s0 = NA_SLAB_START[j]
        w0 = 0 if s0 == 0 else PAST_LEN + s0 * GRID_W
        q = (q_ref[j * NA_QTOK:(j + 1) * NA_QTOK, :] * QSCALE).astype(BF16)
        e = _scores_exp(q, kbig_ref[w0:w0 + NA_NKEY, :], bias_ref[j])
        o = _weighted(e, vbig_ref[w0:w0 + NA_NKEY, :])
        o_ref[j * NA_QTOK:(j + 1) * NA_QTOK, :] = o.astype(BF16)


def _lat_na(qkv, cache_k, cache_v, bias, layer):
    hd = HEAD_DIM
    cq, ck, cv = C_NA_Q // hd, C_NA_K // hd, C_NA_V // hd
    cache = pl.BlockSpec((None, None, PAST_LEN, hd), lambda h, b: (b, layer, 0, h))
    nbig = DEC_SEQ + 2 * PAST_LEN
    return pl.pallas_call(
        _lat_na_kernel,
        grid=(NA_HEADS, DEC_BATCH),
        in_specs=[
            pl.BlockSpec((DEC_SEQ, hd), lambda h, b: (b, cq + h)),
            pl.BlockSpec((DEC_SEQ, hd), lambda h, b: (b, ck + h)),
            pl.BlockSpec((DEC_SEQ, hd), lambda h, b: (b, cv + h)),
            cache, cache,
            pl.BlockSpec((None, None, NA_GROUPS, NA_QTOK, NA_NKEY),
                         lambda h, b: (layer, h, 0, 0, 0)),
        ],
        out_specs=pl.BlockSpec((DEC_SEQ, hd), lambda h, b: (b, h)),
        out_shape=jax.ShapeDtypeStruct((DEC_BATCH * DEC_SEQ, NA_W), BF16),
        scratch_shapes=[pltpu.VMEM((nbig, hd), BF16), pltpu.VMEM((nbig, 2 * hd), BF16)],
        compiler_params=_params(("parallel", "parallel")),
        name="lat_na",
    )(qkv, qkv, qkv, cache_k, cache_v, bias)


def _fill_keys(kall_ref, vall_ref, k_lat, v_ref, kc_ref, vc_ref):
    kall_ref[0:DEC_SEQ, :] = k_lat.astype(BF16)
    kall_ref[DEC_SEQ:N_KEYS, :] = kc_ref[...].astype(BF16)
    vall_ref[0:DEC_SEQ, :] = _with_ones(v_ref[...].astype(BF16))
    vall_ref[DEC_SEQ:N_KEYS, :] = _with_ones(vc_ref[...].astype(BF16))


def _lat_gqa_kernel(q_ref, k_ref, v_ref, kc_ref, vc_ref, cos_ref, sin_ref, qg_ref, kg_ref,
                    o_ref, kall_ref, vall_ref):
    hd = HEAD_DIM
    half = HEAD_DIM // 4
    qb = pl.program_id(2)

    @pl.when(qb == 0)
    def _():
        kf = _rope(_rms(k_ref[...], kg_ref[...]), cos_ref[...], sin_ref[...], half)
        _fill_keys(kall_ref, vall_ref, kf, v_ref, kc_ref, vc_ref)

    for s in range(TQ // ATT_ROWS):
        rows = slice(s * ATT_ROWS, (s + 1) * ATT_ROWS)
        row0 = pl.multiple_of(qb * TQ + s * ATT_ROWS, ATT_ROWS)
        cos = cos_ref[pl.ds(row0, ATT_ROWS), :]
        sin = sin_ref[pl.ds(row0, ATT_ROWS), :]
        for g in range(GQA_GROUP):
            qf = _rms(q_ref[rows, g * hd:(g + 1) * hd], qg_ref[...])
            q = (_rope(qf, cos, sin, half) * QSCALE).astype(BF16)
            o = _weighted(_scores_exp(q, kall_ref[...]), vall_ref[...])
            o_ref[rows, g * hd:(g + 1) * hd] = o.astype(BF16)


def _lat_gqa(qkv, cache_k, cache_v, cos, sin, q_g, k_g, layer):
    hd = HEAD_DIM
    gw = GQA_GROUP * hd
    nqb = DEC_SEQ // TQ
    cq, ck, cv = C_G_Q // gw, C_G_K // hd, C_G_V // hd
    cache = pl.BlockSpec((None, None, PAST_LEN, hd), lambda b, kv, qb: (b, layer, 0, kv))
    table = pl.BlockSpec((DEC_SEQ, hd), lambda b, kv, qb: (0, 0))
    vec = pl.BlockSpec((1, hd), lambda b, kv, qb: (0, 0))
    return pl.pallas_call(
        _lat_gqa_kernel,
        grid=(DEC_BATCH, GQA_KV_HEADS, nqb),
        in_specs=[
            pl.BlockSpec((TQ, gw), lambda b, kv, qb: (b * nqb + qb, cq + kv)),
            pl.BlockSpec((DEC_SEQ, hd), lambda b, kv, qb: (b, ck + kv)),
            pl.BlockSpec((DEC_SEQ, hd), lambda b, kv, qb: (b, cv + kv)),
            cache, cache, table, table, vec, vec,
        ],
        out_specs=pl.BlockSpec((TQ, gw), lambda b, kv, qb: (b * nqb + qb, kv)),
        out_shape=jax.ShapeDtypeStruct((DEC_BATCH * DEC_SEQ, GQA_QW), BF16),
        scratch_shapes=[pltpu.VMEM((N_KEYS, hd), BF16), pltpu.VMEM((N_KEYS, 2 * hd), BF16)],
        compiler_params=_params(("parallel", "parallel", "arbitrary")),
        name="lat_gqa",
    )(qkv, qkv, qkv, cache_k, cache_v, cos, sin, q_g, k_g)


def _lat_diff_kernel(q_ref, k_ref, v_ref, kc_ref, vc_ref, cos_ref, sin_ref, lam_ref, dg_ref,
                     o_ref, kall_ref, vall_ref, *, lambda_init):
    half = DIFF_QK_DIM // 4
    qb = pl.program_id(2)

    @pl.when(qb == 0)
    def _():
        kf = _rope(k_ref[...], cos_ref[...], sin_ref[...], half)
        _fill_keys(kall_ref, vall_ref, kf, v_ref, kc_ref, vc_ref)

    lam = _diff_lambda(lam_ref, lambda_init)
    lane = lax.broadcasted_iota(jnp.int32, (ATT_ROWS, HEAD_DIM), 1)
    for s in range(TQ // ATT_ROWS):
        rows = slice(s * ATT_ROWS, (s + 1) * ATT_ROWS)
        row0 = pl.multiple_of(qb * TQ + s * ATT_ROWS, ATT_ROWS)
        cos = cos_ref[pl.ds(row0, ATT_ROWS), :]
        sin = sin_ref[pl.ds(row0, ATT_ROWS), :]
        qf = _rope(q_ref[rows, :], cos, sin, half) * DIFF_QSCALE
        q0 = jnp.where(lane < DIFF_QK_DIM, qf, 0.0).astype(BF16)
        q1 = jnp.where(lane >= DIFF_QK_DIM, qf, 0.0).astype(BF16)
        od = (_weighted(_scores_exp(q0, kall_ref[...]), vall_ref[...])
              - lam * _weighted(_scores_exp(q1, kall_ref[...]), vall_ref[...]))
        od = _rms(od, dg_ref[...]) * (1.0 - lambda_init)
        o_ref[rows, :] = od.astype(BF16)


def _lat_diff(qkv, cache_k, cache_v, cos, sin, lam_vec, diff_g, layer, lambda_init):
    hd = HEAD_DIM
    nqb = DEC_SEQ // TQ
    cq, ck, cv = C_D_Q // hd, C_D_K // hd, C_D_V // hd
    cache = pl.BlockSpec((None, None, PAST_LEN, hd), lambda b, h, qb: (b, layer, 0, h))
    table = pl.BlockSpec((DEC_SEQ, hd), lambda b, h, qb: (0, 0))
    return pl.pallas_call(
        functools.partial(_lat_diff_kernel, lambda_init=lambda_init),
        grid=(DEC_BATCH, DIFF_HEADS, nqb),
        in_specs=[
            pl.BlockSpec((TQ, hd), lambda b, h, qb: (b * nqb + qb, cq + h)),
            pl.BlockSpec((DEC_SEQ, hd), lambda b, h, qb: (b, ck + h)),
            pl.BlockSpec((DEC_SEQ, hd), lambda b, h, qb: (b, cv + h)),
            cache, cache, table, table,
            pl.BlockSpec((4, DIFF_QK_DIM), lambda b, h, qb: (0, 0)),
            pl.BlockSpec((1, hd), lambda b, h, qb: (0, 0)),
        ],
        out_specs=pl.BlockSpec((TQ, hd), lambda b, h, qb: (b * nqb + qb, h)),
        out_shape=jax.ShapeDtypeStruct((DEC_BATCH * DEC_SEQ, DIFF_W), BF16),
        scratch_shapes=[pltpu.VMEM((N_KEYS, hd), BF16), pltpu.VMEM((N_KEYS, 2 * hd), BF16)],
        compiler_params=_params(("parallel", "parallel", "arbitrary")),
        name="lat_diff",
    )(qkv, qkv, qkv, cache_k, cache_v, cos, sin, lam_vec, diff_g)


def _outproj_kernel(*refs, n_o):
    o_refs = refs[:n_o]
    w_ref, x_ref, g_ref, gt_ref, gpre_ref, sc_ref, sh_ref, out_ref, h_ref = refs[n_o:]
    y = None
    off = 0
    for o_ref in o_refs:
        wd = o_ref.shape[1]
        part = _mm(o_ref[...], w_ref[off:off + wd, :])
        y = part if y is None else y + part
        off += wd
    x1 = x_ref[...] + gt_ref[...] * _rms(y, g_ref[...])
    out_ref[...] = x1
    h_ref[...] = (_rms(x1, gpre_ref[...]) * (1.0 + sc_ref[...]) + sh_ref[...]).astype(BF16)


def _outproj(o_parts, w_out, layer, x, g_post, g_pre_mlp, mod_l, rows_per_batch, first_row):
    m = x.shape[0]
    row = _row_fn(TM_OUT, rows_per_batch, first_row)
    full = pl.BlockSpec((TM_OUT, D_MODEL), lambda i: (i, 0))
    vec = pl.BlockSpec((1, D_MODEL), lambda i: (0, 0))
    return pl.pallas_call(
        functools.partial(_outproj_kernel, n_o=len(o_parts)),
        grid=(m // TM_OUT,),
        in_specs=[pl.BlockSpec((TM_OUT, o.shape[1]), lambda i: (i, 0)) for o in o_parts] + [
            pl.BlockSpec((None, D_MIX, D_MODEL), lambda i: (layer, 0, 0)),
            full, vec, _mod_spec(2, row), vec, _mod_spec(4, row), _mod_spec(3, row),
        ],
        out_specs=[full, full],
        out_shape=[jax.ShapeDtypeStruct((m, D_MODEL), F32),
                   jax.ShapeDtypeStruct((m, D_MODEL), BF16)],
        compiler_params=_params(("parallel",)),
        name="outproj",
    )(*o_parts, w_out, x, g_post, mod_l, g_pre_mlp, mod_l, mod_l)


def _mlp_kernel(*refs, emit_next):
    h_ref, x_ref, wup_ref, wdn_ref, gpost_ref, gt_ref = refs[:6]
    if emit_next:
        gnext_ref, scn_ref, shn_ref, out_ref, hnext_ref, acc_ref, gm_ref, gmn_ref = refs[6:]
    else:
        out_ref, acc_ref, gm_ref = refs[6:]
    k = pl.program_id(1)

    @pl.when(k == 0)
    def _():
        acc_ref[...] = jnp.zeros_like(acc_ref)

    u = _mm(h_ref[...], wup_ref[...])
    a = jnp.square(jnp.maximum(u, 0.0)).astype(BF16)
    acc_ref[...] += _mm(a, wdn_ref[...])

    @pl.when(k == pl.num_programs(1) - 1)
    def _():
        gm_ref[...] = gt_ref[...] * gpost_ref[...]
        if emit_next:
            gmn_ref[...] = gnext_ref[...] * (1.0 + scn_ref[...])

        def body(rows):
            y = acc_ref[rows, :]
            r = lax.rsqrt(jnp.mean(y * y, axis=-1, keepdims=True) + EPS)
            x2 = x_ref[rows, :] + y * r * gm_ref[...]
            out_ref[rows, :] = x2
            if emit_next:
                rn = lax.rsqrt(jnp.mean(x2 * x2, axis=-1, keepdims=True) + EPS)
                hnext_ref[rows, :] = (x2 * rn * gmn_ref[...] + shn_ref[...]).astype(BF16)

        _row_chunks(x_ref.shape[0], body)


def _mlp(h, x, g_post, mod_l, w_up, w_down, layer, rows_per_batch, first_row,
         g_next=None, mod_next=None):
    m = x.shape[0]
    emit_next = g_next is not None
    row = _row_fn(TM_MLP, rows_per_batch, first_row)
    full = pl.BlockSpec((TM_MLP, D_MODEL), lambda i, k: (i, 0))
    vec = pl.BlockSpec((1, D_MODEL), lambda i, k: (0, 0))
    in_specs = [
        full, full,
        pl.BlockSpec((None, D_MODEL, TF_MLP), lambda i, k: (layer, 0, k)),
        pl.BlockSpec((None, TF_MLP, D_MODEL), lambda i, k: (layer, k, 0)),
        vec, _mod_spec(5, row),
    ]
    args = [h, x, w_up, w_down, g_post, mod_l]
    out_specs = [full]
    out_shape = [jax.ShapeDtypeStruct((m, D_MODEL), F32)]
    scratch = [pltpu.VMEM((TM_MLP, D_MODEL), F32), pltpu.VMEM((1, D_MODEL), F32)]
    if emit_next:
        in_specs += [vec, _mod_spec(1, row), _mod_spec(0, row)]
        args += [g_next, mod_next, mod_next]
        out_specs.append(full)
        out_shape.append(jax.ShapeDtypeStruct((m, D_MODEL), BF16))
        scratch.append(pltpu.VMEM((1, D_MODEL), F32))
    res = pl.pallas_call(
        functools.partial(_mlp_kernel, emit_next=emit_next),
        grid=(m // TM_MLP, D_FF // TF_MLP),
        in_specs=in_specs,
        out_specs=out_specs,
        out_shape=out_shape,
        scratch_shapes=scratch,
        compiler_params=_params(("parallel", "arbitrary")),
        name="mlp",
    )(*args)
    return res if emit_next else (res[0], None)


def _rope_tables(half, n_rep):
    t = jnp.arange(DEC_SEQ)
    inv = ROPE_THETA ** (-jnp.arange(half, dtype=F32) / half)

    def cs(pos):
        ang = pos.astype(F32)[:, None] * inv[None, :]
        c, s = jnp.cos(ang), jnp.sin(ang)
        return jnp.concatenate([c, c], axis=-1), jnp.concatenate([-s, s], axis=-1)

    cr, sr = cs(t // GRID_W)
    cc, sc = cs(t % GRID_W)
    return (jnp.concatenate([cr, cc] * n_rep, axis=-1),
            jnp.concatenate([sr, sc] * n_rep, axis=-1))


def _na_bias(rpb):
    qc = np.arange(GRID_W)[:, None]
    kc = np.arange(GRID_W)[None, :]
    ws = np.clip(qc - NA_KW // 2, 0, GRID_W - NA_KW)
    valid = (kc >= ws) & (kc < ws + NA_KW)
    dcol = np.clip(kc - qc + NA_KW - 1, 0, 2 * NA_KW - 2)
    n_dcol = 2 * NA_KW - 1
    onehot = (dcol.reshape(-1)[None, :] == np.arange(n_dcol)[:, None]).astype(np.float32)
    t = jnp.einsum("lhdc,cq->lhdq", rpb.astype(F32), jnp.asarray(onehot),
                   precision=lax.Precision.HIGHEST)
    t = t.reshape(DEPTH, NA_HEADS, 2 * NA_KH - 1, GRID_W, GRID_W) * LOG2E
    t = jnp.where(jnp.asarray(valid)[None, None, None], t, NEG_BIG)
    masked = jnp.full((DEPTH, NA_HEADS, GRID_W, GRID_W), NEG_BIG, F32)
    ctx = jnp.zeros((DEPTH, NA_HEADS, NA_QTOK, PAST_LEN), F32)
    groups = []
    for j in range(NA_GROUPS):
        s0 = NA_SLAB_START[j]
        q_rows = []
        for r in range(j * NA_QROWS, (j + 1) * NA_QROWS):
            r0 = min(max(r - NA_KH // 2, 0), GRID_ROWS - NA_KH)
            blocks = []
            for key_row in range(s0, s0 + NA_SLAB_ROWS):
                in_window = r0 <= key_row < r0 + NA_KH
                blocks.append(t[:, :, key_row - r + NA_KH - 1] if in_window else masked)
            q_rows.append(jnp.concatenate(blocks, axis=-1))
        slab = jnp.concatenate(q_rows, axis=-2)
        groups.append(jnp.concatenate([ctx, slab] if s0 == 0 else [slab, ctx], axis=-1))
    return jnp.stack(groups, axis=2)


def kernel(x_prompt, x_sample, c, cache_na_k, cache_na_v, cache_gqa_k, cache_gqa_v,
           cache_diff_k, cache_diff_v, c_ctx, w_ada, b_ada, norm_g, w_in, w_out, na_rpb,
           gqa_q_g, gqa_k_g, diff_lam, diff_g, w_up, w_down):
    np_rows = BATCH * SEQ
    ns_rows = DEC_BATCH * DEC_SEQ
    xp = x_prompt.reshape(np_rows, D_MODEL)
    xs = x_sample.reshape(ns_rows, D_MODEL)

    cv = jnp.concatenate(
        [c_ctx[None, :], c, jnp.zeros((MOD_ROWS - 1 - DEC_BATCH, D_MODEL), F32)], axis=0)
    mod = _modulation(cv, w_ada, b_ada).reshape(DEPTH, MOD_ROWS, N_MOD, 1, D_MODEL)

    w_in_b = w_in.astype(BF16)
    w_out_b = w_out.astype(BF16)
    w_up_b = w_up.astype(BF16)
    w_down_b = w_down.astype(BF16)

    def flat_cache(a):
        return a.reshape(DEC_BATCH, DEPTH, PAST_LEN, a.shape[3] * HEAD_DIM)

    c_na_k, c_na_v = flat_cache(cache_na_k), flat_cache(cache_na_v)
    c_g_k, c_g_v = flat_cache(cache_gqa_k), flat_cache(cache_gqa_v)
    c_d_k, c_d_v = flat_cache(cache_diff_k), flat_cache(cache_diff_v)

    cos_g, sin_g = _rope_tables(HEAD_DIM // 4, 1)
    cos_d, sin_d = _rope_tables(DIFF_QK_DIM // 4, 2)
    na_bias = _na_bias(na_rpb)

    new_kv = [jnp.zeros((BATCH, DEPTH, SEQ, nh, HEAD_DIM), F32) for nh in KV_HEADS]
    g_all = norm_g.reshape(DEPTH, 4, 1, D_MODEL)
    hp = hs = None
    for l in range(DEPTH):
        lambda_init = 0.8 - 0.6 * math.exp(-0.3 * l)
        mod_l = mod[l]
        g = g_all[l]
        q_g = gqa_q_g[l].reshape(1, HEAD_DIM)
        k_g = gqa_k_g[l].reshape(1, HEAD_DIM)
        d_g = diff_g[l].reshape(1, HEAD_DIM)
        lam_vec = diff_lam[l]
        last = l == DEPTH - 1
        g_next = None if last else g_all[l + 1, 0]
        mod_next = None if last else mod[l + 1]

        if l == 0:
            qkv_p = _inproj(xp, g[0], mod_l, w_in_b, l, None, 0)
        else:
            qkv_p = _inproj_h(hp, w_in_b, l)
        o_p, *new_kv = _ctx_attn(qkv_p, q_g, k_g, lam_vec, d_g, new_kv, l, lambda_init)
        xp, hm = _outproj([o_p], w_out_b, l, xp, g[1], g[2], mod_l, None, 0)
        xp, hp = _mlp(hm, xp, g[3], mod_l, w_up_b, w_down_b, l, None, 0, g_next, mod_next)

        if l == 0:
            qkv_s = _inproj(xs, g[0], mod_l, w_in_b, l, DEC_SEQ, 1)
        else:
            qkv_s = _inproj_h(hs, w_in_b, l)
        o_na = _lat_na(qkv_s, c_na_k, c_na_v, na_bias, l)
        o_gqa = _lat_gqa(qkv_s, c_g_k, c_g_v, cos_g, sin_g, q_g, k_g, l)
        o_diff = _lat_diff(qkv_s, c_d_k, c_d_v, cos_d, sin_d, lam_vec, d_g, l, lambda_init)
        xs, hm = _outproj([o_na, o_gqa, o_diff], w_out_b, l, xs, g[1], g[2], mod_l, DEC_SEQ, 1)
        xs, hs = _mlp(hm, xs, g[3], mod_l, w_up_b, w_down_b, l, DEC_SEQ, 1, g_next, mod_next)

    return (xp.reshape(BATCH, SEQ, D_MODEL), xs.reshape(DEC_BATCH, DEC_SEQ, D_MODEL), *new_kv)
```

```python
import functools
import math

import jax
import jax.numpy as jnp
import numpy as np
from jax import lax
from jax.experimental import pallas as pl
from jax.experimental.pallas import tpu as pltpu

D_MODEL = 2048
BATCH = 16
SEQ = 256
DEPTH = 4
DEC_BATCH = 8
DEC_SEQ = 1024
PAST_LEN = 256
GRID_W = 64
GRID_ROWS = DEC_SEQ // GRID_W
HEAD_DIM = 128
NA_HEADS = 4
GQA_Q_HEADS = 8
GQA_KV_HEADS = 2
GQA_GROUP = GQA_Q_HEADS // GQA_KV_HEADS
DIFF_HEADS = 4
DIFF_QK_DIM = HEAD_DIM // 2
NA_KH = 8
NA_KW = 16
D_FF = 4 * D_MODEL
ROPE_THETA = 10000.0
EPS = 1e-6
N_MOD = 6
NEG_BIG = -1e30

NA_W = NA_HEADS * HEAD_DIM
GQA_QW = GQA_Q_HEADS * HEAD_DIM
GQA_KVW = GQA_KV_HEADS * HEAD_DIM
DIFF_W = DIFF_HEADS * HEAD_DIM
D_IN = 3 * NA_W + GQA_QW + 2 * GQA_KVW + 3 * DIFF_W
D_MIX = NA_W + GQA_QW + DIFF_W
C_NA_Q, C_NA_K, C_NA_V = 0, NA_W, 2 * NA_W
C_G_Q = 3 * NA_W
C_G_K = C_G_Q + GQA_QW
C_G_V = C_G_K + GQA_KVW
C_D_Q = C_G_V + GQA_KVW
C_D_K = C_D_Q + DIFF_W
C_D_V = C_D_K + DIFF_W

KV_HEADS = (NA_HEADS, NA_HEADS, GQA_KV_HEADS, GQA_KV_HEADS, DIFF_HEADS, DIFF_HEADS)
N_KEYS = DEC_SEQ + PAST_LEN
LOG2E = 1.4426950408889634
QSCALE = HEAD_DIM ** -0.5 * LOG2E
DIFF_QSCALE = DIFF_QK_DIM ** -0.5 * LOG2E

NA_QROWS = 4
NA_QTOK = NA_QROWS * GRID_W
NA_GROUPS = GRID_ROWS // NA_QROWS
NA_SLAB_ROWS = 12
NA_SLAB = NA_SLAB_ROWS * GRID_W
NA_NKEY = NA_SLAB + PAST_LEN
NA_SLAB_START = (0, 0, 4, 4)
MOD_ROWS = 16

F32 = jnp.float32
BF16 = jnp.bfloat16

VMEM_LIMIT = 52 * 1024 * 1024

TM_IN = 1024
TN_IN = 1536
TM_OUT = 512
TM_MLP = 512
TF_MLP = 1024
TN_ADA = 1024
TQ = 512
ATT_ROWS = 256
NORM_ROWS = 128


def _params(sem):
    return pltpu.CompilerParams(dimension_semantics=sem, vmem_limit_bytes=VMEM_LIMIT)


def _rms(x, g):
    ms = jnp.mean(x * x, axis=-1, keepdims=True)
    return x * lax.rsqrt(ms + EPS) * g


def _nt(a, b):
    return lax.dot_general(a, b, (((1,), (1,)), ((), ())), preferred_element_type=F32)


def _mm(a, b):
    return jnp.dot(a, b, preferred_element_type=F32)


def _scores_exp(q, k, bias=None):
    s = _nt(q, k)
    if bias is not None:
        s = s + bias
    return jnp.exp2(s - jnp.max(s, axis=-1, keepdims=True)).astype(BF16)


def _weighted(e, v_ones):
    oa = _mm(e, v_ones)
    d = v_ones.shape[1] // 2
    return oa[:, :d] / oa[:, d:]


def _with_ones(v):
    return jnp.concatenate([v, jnp.ones_like(v)], axis=1)


def _rope(x, cos, sin_signed, half):
    n = x.shape[-1]
    lane = lax.broadcasted_iota(jnp.int32, x.shape, 1)
    first = (lane % (2 * half)) < half
    rot = jnp.where(first, pltpu.roll(x, n - half, 1), pltpu.roll(x, half, 1))
    return x * cos + rot * sin_signed


def _diff_lambda(lam_ref, lambda_init):
    lf = lam_ref[...]
    a = jnp.sum(lf[0:1] * lf[1:2], axis=-1, keepdims=True)
    b = jnp.sum(lf[2:3] * lf[3:4], axis=-1, keepdims=True)
    return jnp.exp(a) - jnp.exp(b) + lambda_init


def _mod_kernel(cv_ref, w_ref, b_ref, o_ref):
    cv = cv_ref[...]
    s = cv / (1.0 + jnp.exp(-cv))
    o_ref[...] = _mm(s.astype(BF16), w_ref[...].astype(BF16)) + b_ref[...]


def _modulation(cv, w_ada, b_ada):
    n = N_MOD * D_MODEL
    return pl.pallas_call(
        _mod_kernel,
        grid=(DEPTH, n // TN_ADA),
        in_specs=[
            pl.BlockSpec((MOD_ROWS, D_MODEL), lambda l, j: (0, 0)),
            pl.BlockSpec((None, D_MODEL, TN_ADA), lambda l, j: (l, 0, j)),
            pl.BlockSpec((None, 1, TN_ADA), lambda l, j: (l, 0, j)),
        ],
        out_specs=pl.BlockSpec((None, MOD_ROWS, TN_ADA), lambda l, j: (l, 0, j)),
        out_shape=jax.ShapeDtypeStruct((DEPTH, MOD_ROWS, n), F32),
        compiler_params=_params(("parallel", "parallel")),
        name="modulation",
    )(cv, w_ada, b_ada.reshape(DEPTH, 1, n))


def _mod_spec(chunk, row_fn):
    return pl.BlockSpec((None, None, 1, D_MODEL), lambda i, *_: (row_fn(i), chunk, 0, 0))


def _row_fn(tm, rows_per_batch, first_row):
    if rows_per_batch is None:
        return lambda i: first_row
    return lambda i: first_row + (i * tm) // rows_per_batch


def _row_chunks(n_rows, body):
    def step(c, carry):
        body(pl.ds(pl.multiple_of(c * NORM_ROWS, NORM_ROWS), NORM_ROWS))
        return carry

    lax.fori_loop(0, n_rows // NORM_ROWS, step, 0)


def _modulated_norm(h_ref, x_ref, g_ref, sc_ref, sh_ref, gm_ref):
    gm_ref[...] = g_ref[...] * (1.0 + sc_ref[...])

    def body(rows):
        x = x_ref[rows, :]
        r = lax.rsqrt(jnp.mean(x * x, axis=-1, keepdims=True) + EPS)
        h_ref[rows, :] = (x * r * gm_ref[...] + sh_ref[...]).astype(BF16)

    _row_chunks(x_ref.shape[0], body)


def _gated_norm_residual(out_ref, x_ref, y_ref, g_ref, gt_ref, gm_ref):
    gm_ref[...] = gt_ref[...] * g_ref[...]

    def body(rows):
        y = y_ref[rows, :]
        r = lax.rsqrt(jnp.mean(y * y, axis=-1, keepdims=True) + EPS)
        out_ref[rows, :] = x_ref[rows, :] + y * r * gm_ref[...]

    _row_chunks(x_ref.shape[0], body)


def _inproj_kernel(x_ref, g_ref, sc_ref, sh_ref, w_ref, o_ref, h_ref, gm_ref):
    @pl.when(pl.program_id(1) == 0)
    def _():
        _modulated_norm(h_ref, x_ref, g_ref, sc_ref, sh_ref, gm_ref)

    o_ref[...] = _mm(h_ref[...], w_ref[...])


def _inproj(x, g_pre, mod_l, w_in, rows_per_batch, first_row):
    m = x.shape[0]
    row = _row_fn(TM_IN, rows_per_batch, first_row)
    return pl.pallas_call(
        _inproj_kernel,
        grid=(m // TM_IN, D_IN // TN_IN),
        in_specs=[
            pl.BlockSpec((TM_IN, D_MODEL), lambda i, j: (i, 0)),
            pl.BlockSpec((1, D_MODEL), lambda i, j: (0, 0)),
            _mod_spec(1, row),
            _mod_spec(0, row),
            pl.BlockSpec((D_MODEL, TN_IN), lambda i, j: (0, j)),
        ],
        out_specs=pl.BlockSpec((TM_IN, TN_IN), lambda i, j: (i, j)),
        out_shape=jax.ShapeDtypeStruct((m, D_IN), F32),
        scratch_shapes=[pltpu.VMEM((TM_IN, D_MODEL), BF16), pltpu.VMEM((1, D_MODEL), F32)],
        compiler_params=_params(("parallel", "arbitrary")),
        name="inproj",
    )(x, g_pre, mod_l, mod_l, w_in)


def _ctx_attn_kernel(qkv_ref, qg_ref, kg_ref, lam_ref, dg_ref, *rest, lambda_init):
    o_ref, nak_ref, nav_ref, gk_ref, gv_ref, dk_ref, dv_ref = rest[len(KV_HEADS):]
    hd = HEAD_DIM

    def cols(c0, h):
        return qkv_ref[:, c0 + h * hd:c0 + (h + 1) * hd]

    for ref, c0 in ((nak_ref, C_NA_K), (nav_ref, C_NA_V), (gv_ref, C_G_V),
                    (dk_ref, C_D_K), (dv_ref, C_D_V)):
        for h in range(ref.shape[1]):
            ref[:, h, :] = cols(c0, h)

    for h in range(NA_HEADS):
        q = (cols(C_NA_Q, h) * QSCALE).astype(BF16)
        k = cols(C_NA_K, h).astype(BF16)
        v1 = _with_ones(cols(C_NA_V, h).astype(BF16))
        o_ref[:, h * hd:(h + 1) * hd] = _weighted(_scores_exp(q, k), v1).astype(BF16)

    for kv in range(GQA_KV_HEADS):
        kf = _rms(cols(C_G_K, kv), kg_ref[...])
        gk_ref[:, kv, :] = kf
        k = kf.astype(BF16)
        v1 = _with_ones(cols(C_G_V, kv).astype(BF16))
        for g in range(GQA_GROUP):
            hq = kv * GQA_GROUP + g
            q = (_rms(cols(C_G_Q, hq), qg_ref[...]) * QSCALE).astype(BF16)
            o = _weighted(_scores_exp(q, k), v1)
            o_ref[:, NA_W + hq * hd:NA_W + (hq + 1) * hd] = o.astype(BF16)

    lam = _diff_lambda(lam_ref, lambda_init)
    lane = lax.broadcasted_iota(jnp.int32, (SEQ, hd), 1)
    for h in range(DIFF_HEADS):
        qf = cols(C_D_Q, h) * DIFF_QSCALE
        k = cols(C_D_K, h).astype(BF16)
        v1 = _with_ones(cols(C_D_V, h).astype(BF16))
        q0 = jnp.where(lane < DIFF_QK_DIM, qf, 0.0).astype(BF16)
        q1 = jnp.where(lane >= DIFF_QK_DIM, qf, 0.0).astype(BF16)
        od = _weighted(_scores_exp(q0, k), v1) - lam * _weighted(_scores_exp(q1, k), v1)
        od = _rms(od, dg_ref[...]) * (1.0 - lambda_init)
        c0 = NA_W + GQA_QW + h * hd
        o_ref[:, c0:c0 + hd] = od.astype(BF16)


def _ctx_attn(qkv, q_g, k_g, lam_vec, diff_g, kv_bufs, layer, lambda_init):
    n = qkv.shape[0]
    vec = pl.BlockSpec((1, HEAD_DIM), lambda b: (0, 0))
    n_in = 5

    def rows(w):
        return pl.BlockSpec((SEQ, w), lambda b: (b, 0))

    def heads(nh):
        return pl.BlockSpec((None, None, SEQ, nh, HEAD_DIM), lambda b: (b, layer, 0, 0, 0))

    return pl.pallas_call(
        functools.partial(_ctx_attn_kernel, lambda_init=lambda_init),
        grid=(n // SEQ,),
        in_specs=[rows(D_IN), vec, vec,
                  pl.BlockSpec((4, DIFF_QK_DIM), lambda b: (0, 0)), vec]
                 + [pl.BlockSpec(memory_space=pl.ANY)] * len(kv_bufs),
        out_specs=[rows(D_MIX)] + [heads(nh) for nh in KV_HEADS],
        out_shape=[jax.ShapeDtypeStruct((n, D_MIX), BF16)]
                  + [jax.ShapeDtypeStruct(a.shape, a.dtype) for a in kv_bufs],
        input_output_aliases={n_in + i: 1 + i for i in range(len(kv_bufs))},
        compiler_params=_params(("parallel",)),
        name="ctx_attn",
    )(qkv, q_g, k_g, lam_vec, diff_g, *kv_bufs)


def _lat_na_kernel(q_ref, k_ref, v_ref, kc_ref, vc_ref, bias_ref, o_ref, kbig_ref, vbig_ref):
    lat0, lat1 = PAST_LEN, PAST_LEN + DEC_SEQ
    kc = kc_ref[...].astype(BF16)
    vc = _with_ones(vc_ref[...].astype(BF16))
    kbig_ref[0:lat0, :] = kc
    kbig_ref[lat0:lat1, :] = k_ref[...].astype(BF16)
    kbig_ref[lat1:, :] = kc
    vbig_ref[0:lat0, :] = vc
    vbig_ref[lat0:lat1, :] = _with_ones(v_ref[...].astype(BF16))
    vbig_ref[lat1:, :] = vc
    for j in range(NA_GROUPS):
        s0 = NA_SLAB_START[j]
        w0 = 0 if s0 == 0 else PAST_LEN + s0 * GRID_W
        q = (q_ref[j * NA_QTOK:(j + 1) * NA_QTOK, :] * QSCALE).astype(BF16)
        e = _scores_exp(q, kbig_ref[w0:w0 + NA_NKEY, :], bias_ref[j])
        o = _weighted(e, vbig_ref[w0:w0 + NA_NKEY, :])
        o_ref[j * NA_QTOK:(j + 1) * NA_QTOK, :] = o.astype(BF16)


def _lat_na(qkv, cache_k, cache_v, bias, layer):
    hd = HEAD_DIM
    cq, ck, cv = C_NA_Q // hd, C_NA_K // hd, C_NA_V // hd
    cache = pl.BlockSpec((None, None, PAST_LEN, hd), lambda h, b: (b, layer, 0, h))
    nbig = DEC_SEQ + 2 * PAST_LEN
    return pl.pallas_call(
        _lat_na_kernel,
        grid=(NA_HEADS, DEC_BATCH),
        in_specs=[
            pl.BlockSpec((DEC_SEQ, hd), lambda h, b: (b, cq + h)),
            pl.BlockSpec((DEC_SEQ, hd), lambda h, b: (b, ck + h)),
            pl.BlockSpec((DEC_SEQ, hd), lambda h, b: (b, cv + h)),
            cache, cache,
            pl.BlockSpec((None, None, NA_GROUPS, NA_QTOK, NA_NKEY),
                         lambda h, b: (layer, h, 0, 0, 0)),
        ],
        out_specs=pl.BlockSpec((DEC_SEQ, hd), lambda h, b: (b, h)),
        out_shape=jax.ShapeDtypeStruct((DEC_BATCH * DEC_SEQ, NA_W), BF16),
        scratch_shapes=[pltpu.VMEM((nbig, hd), BF16), pltpu.VMEM((nbig, 2 * hd), BF16)],
        compiler_params=_params(("parallel", "parallel")),
        name="lat_na",
    )(qkv, qkv, qkv, cache_k, cache_v, bias)


def _fill_keys(kall_ref, vall_ref, k_lat, v_ref, kc_ref, vc_ref):
    kall_ref[0:DEC_SEQ, :] = k_lat.astype(BF16)
    kall_ref[DEC_SEQ:N_KEYS, :] = kc_ref[...].astype(BF16)
    vall_ref[0:DEC_SEQ, :] = _with_ones(v_ref[...].astype(BF16))
    vall_ref[DEC_SEQ:N_KEYS, :] = _with_ones(vc_ref[...].astype(BF16))


def _lat_gqa_kernel(q_ref, k_ref, v_ref, kc_ref, vc_ref, cos_ref, sin_ref, qg_ref, kg_ref,
                    o_ref, kall_ref, vall_ref):
    hd = HEAD_DIM
    half = HEAD_DIM // 4
    qb = pl.program_id(2)

    @pl.when(qb == 0)
    def _():
        kf = _rope(_rms(k_ref[...], kg_ref[...]), cos_ref[...], sin_ref[...], half)
        _fill_keys(kall_ref, vall_ref, kf, v_ref, kc_ref, vc_ref)

    for s in range(TQ // ATT_ROWS):
        rows = slice(s * ATT_ROWS, (s + 1) * ATT_ROWS)
        row0 = pl.multiple_of(qb * TQ + s * ATT_ROWS, ATT_ROWS)
        cos = cos_ref[pl.ds(row0, ATT_ROWS), :]
        sin = sin_ref[pl.ds(row0, ATT_ROWS), :]
        for g in range(GQA_GROUP):
            qf = _rms(q_ref[rows, g * hd:(g + 1) * hd], qg_ref[...])
            q = (_rope(qf, cos, sin, half) * QSCALE).astype(BF16)
            o = _weighted(_scores_exp(q, kall_ref[...]), vall_ref[...])
            o_ref[rows, g * hd:(g + 1) * hd] = o.astype(BF16)


def _lat_gqa(qkv, cache_k, cache_v, cos, sin, q_g, k_g, layer):
    hd = HEAD_DIM
    gw = GQA_GROUP * hd
    nqb = DEC_SEQ // TQ
    cq, ck, cv = C_G_Q // gw, C_G_K // hd, C_G_V // hd
    cache = pl.BlockSpec((None, None, PAST_LEN, hd), lambda b, kv, qb: (b, layer, 0, kv))
    table = pl.BlockSpec((DEC_SEQ, hd), lambda b, kv, qb: (0, 0))
    vec = pl.BlockSpec((1, hd), lambda b, kv, qb: (0, 0))
    return pl.pallas_call(
        _lat_gqa_kernel,
        grid=(DEC_BATCH, GQA_KV_HEADS, nqb),
        in_specs=[
            pl.BlockSpec((TQ, gw), lambda b, kv, qb: (b * nqb + qb, cq + kv)),
            pl.BlockSpec((DEC_SEQ, hd), lambda b, kv, qb: (b, ck + kv)),
            pl.BlockSpec((DEC_SEQ, hd), lambda b, kv, qb: (b, cv + kv)),
            cache, cache, table, table, vec, vec,
        ],
        out_specs=pl.BlockSpec((TQ, gw), lambda b, kv, qb: (b * nqb + qb, kv)),
        out_shape=jax.ShapeDtypeStruct((DEC_BATCH * DEC_SEQ, GQA_QW), BF16),
        scratch_shapes=[pltpu.VMEM((N_KEYS, hd), BF16), pltpu.VMEM((N_KEYS, 2 * hd), BF16)],
        compiler_params=_params(("parallel", "parallel", "arbitrary")),
        name="lat_gqa",
    )(qkv, qkv, qkv, cache_k, cache_v, cos, sin, q_g, k_g)


def _lat_diff_kernel(q_ref, k_ref, v_ref, kc_ref, vc_ref, cos_ref, sin_ref, lam_ref, dg_ref,
                     o_ref, kall_ref, vall_ref, *, lambda_init):
    half = DIFF_QK_DIM // 4
    qb = pl.program_id(2)

    @pl.when(qb == 0)
    def _():
        kf = _rope(k_ref[...], cos_ref[...], sin_ref[...], half)
        _fill_keys(kall_ref, vall_ref, kf, v_ref, kc_ref, vc_ref)

    lam = _diff_lambda(lam_ref, lambda_init)
    lane = lax.broadcasted_iota(jnp.int32, (ATT_ROWS, HEAD_DIM), 1)
    for s in range(TQ // ATT_ROWS):
        rows = slice(s * ATT_ROWS, (s + 1) * ATT_ROWS)
        row0 = pl.multiple_of(qb * TQ + s * ATT_ROWS, ATT_ROWS)
        cos = cos_ref[pl.ds(row0, ATT_ROWS), :]
        sin = sin_ref[pl.ds(row0, ATT_ROWS), :]
        qf = _rope(q_ref[rows, :], cos, sin, half) * DIFF_QSCALE
        q0 = jnp.where(lane < DIFF_QK_DIM, qf, 0.0).astype(BF16)
        q1 = jnp.where(lane >= DIFF_QK_DIM, qf, 0.0).astype(BF16)
        od = (_weighted(_scores_exp(q0, kall_ref[...]), vall_ref[...])
              - lam * _weighted(_scores_exp(q1, kall_ref[...]), vall_ref[...]))
        od = _rms(od, dg_ref[...]) * (1.0 - lambda_init)
        o_ref[rows, :] = od.astype(BF16)


def _lat_diff(qkv, cache_k, cache_v, cos, sin, lam_vec, diff_g, layer, lambda_init):
    hd = HEAD_DIM
    nqb = DEC_SEQ // TQ
    cq, ck, cv = C_D_Q // hd, C_D_K // hd, C_D_V // hd
    cache = pl.BlockSpec((None, None, PAST_LEN, hd), lambda b, h, qb: (b, layer, 0, h))
    table = pl.BlockSpec((DEC_SEQ, hd), lambda b, h, qb: (0, 0))
    return pl.pallas_call(
        functools.partial(_lat_diff_kernel, lambda_init=lambda_init),
        grid=(DEC_BATCH, DIFF_HEADS, nqb),
        in_specs=[
            pl.BlockSpec((TQ, hd), lambda b, h, qb: (b * nqb + qb, cq + h)),
            pl.BlockSpec((DEC_SEQ, hd), lambda b, h, qb: (b, ck + h)),
            pl.BlockSpec((DEC_SEQ, hd), lambda b, h, qb: (b, cv + h)),
            cache, cache, table, table,
            pl.BlockSpec((4, DIFF_QK_DIM), lambda b, h, qb: (0, 0)),
            pl.BlockSpec((1, hd), lambda b, h, qb: (0, 0)),
        ],
        out_specs=pl.BlockSpec((TQ, hd), lambda b, h, qb: (b * nqb + qb, h)),
        out_shape=jax.ShapeDtypeStruct((DEC_BATCH * DEC_SEQ, DIFF_W), BF16),
        scratch_shapes=[pltpu.VMEM((N_KEYS, hd), BF16), pltpu.VMEM((N_KEYS, 2 * hd), BF16)],
        compiler_params=_params(("parallel", "parallel", "arbitrary")),
        name="lat_diff",
    )(qkv, qkv, qkv, cache_k, cache_v, cos, sin, lam_vec, diff_g)


def _outproj_kernel(*refs, n_o):
    o_refs = refs[:n_o]
    w_ref, x_ref, g_ref, gt_ref, out_ref = refs[n_o:]
    y = None
    off = 0
    for o_ref in o_refs:
        wd = o_ref.shape[1]
        part = _mm(o_ref[...], w_ref[off:off + wd, :])
        y = part if y is None else y + part
        off += wd
    out_ref[...] = x_ref[...] + gt_ref[...] * _rms(y, g_ref[...])


def _outproj(o_parts, w_out, x, g_post, mod_l, rows_per_batch, first_row):
    m = x.shape[0]
    row = _row_fn(TM_OUT, rows_per_batch, first_row)
    full = pl.BlockSpec((TM_OUT, D_MODEL), lambda i: (i, 0))
    return pl.pallas_call(
        functools.partial(_outproj_kernel, n_o=len(o_parts)),
        grid=(m // TM_OUT,),
        in_specs=[pl.BlockSpec((TM_OUT, o.shape[1]), lambda i: (i, 0)) for o in o_parts] + [
            pl.BlockSpec((D_MIX, D_MODEL), lambda i: (0, 0)),
            full,
            pl.BlockSpec((1, D_MODEL), lambda i: (0, 0)),
            _mod_spec(2, row),
        ],
        out_specs=full,
        out_shape=jax.ShapeDtypeStruct((m, D_MODEL), F32),
        compiler_params=_params(("parallel",)),
        name="outproj",
    )(*o_parts, w_out, x, g_post, mod_l)


def _mlp_kernel(x_ref, gpre_ref, sc_ref, sh_ref, wup_ref, wdn_ref, gpost_ref, gt_ref, *rest):
    n_cast = (len(rest) - 4) // 2
    cast_in = rest[:n_cast]
    out_ref = rest[n_cast]
    cast_out = rest[n_cast + 1:2 * n_cast + 1]
    h_ref, acc_ref, gm_ref = rest[2 * n_cast + 1:]
    k = pl.program_id(1)

    @pl.when(k == 0)
    def _():
        _modulated_norm(h_ref, x_ref, gpre_ref, sc_ref, sh_ref, gm_ref)
        acc_ref[...] = jnp.zeros_like(acc_ref)

    u = _mm(h_ref[...], wup_ref[...])
    a = jnp.square(jnp.maximum(u, 0.0)).astype(BF16)
    acc_ref[...] += _mm(a, wdn_ref[...])

    for src, dst in zip(cast_in, cast_out):
        dst[...] = src[...].astype(BF16)

    @pl.when(k == pl.num_programs(1) - 1)
    def _():
        _gated_norm_residual(out_ref, x_ref, acc_ref, gpost_ref, gt_ref, gm_ref)


def _mlp(x, g_pre, g_post, mod_l, w_up, w_down, rows_per_batch, first_row, cast=None):
    m = x.shape[0]
    row = _row_fn(TM_MLP, rows_per_batch, first_row)
    full = pl.BlockSpec((TM_MLP, D_MODEL), lambda i, k: (i, 0))
    vec = pl.BlockSpec((1, D_MODEL), lambda i, k: (0, 0))
    n_k = D_FF // TF_MLP
    n_steps = (m // TM_MLP) * n_k
    cast_in_specs, cast_out_specs, cast_shapes, cast_args = [], [], [], []
    if cast is not None:
        stacks, layer = cast
        for w in stacks:
            rows, cols = w.shape[1], w.shape[2]
            rb = rows // n_steps
            cast_in_specs.append(
                pl.BlockSpec((None, rb, cols), lambda i, k: (layer, i * n_k + k, 0)))
            cast_out_specs.append(pl.BlockSpec((rb, cols), lambda i, k: (i * n_k + k, 0)))
            cast_shapes.append(jax.ShapeDtypeStruct((rows, cols), BF16))
            cast_args.append(w)
    res = pl.pallas_call(
        _mlp_kernel,
        grid=(m // TM_MLP, n_k),
        in_specs=[
            full, vec, _mod_spec(4, row), _mod_spec(3, row),
            pl.BlockSpec((D_MODEL, TF_MLP), lambda i, k: (0, k)),
            pl.BlockSpec((TF_MLP, D_MODEL), lambda i, k: (k, 0)),
            vec, _mod_spec(5, row),
        ] + cast_in_specs,
        out_specs=[full] + cast_out_specs,
        out_shape=[jax.ShapeDtypeStruct((m, D_MODEL), F32)] + cast_shapes,
        scratch_shapes=[pltpu.VMEM((TM_MLP, D_MODEL), BF16), pltpu.VMEM((TM_MLP, D_MODEL), F32),
                        pltpu.VMEM((1, D_MODEL), F32)],
        compiler_params=_params(("parallel", "arbitrary")),
        name="mlp",
    )(x, g_pre, mod_l, mod_l, w_up, w_down, g_post, mod_l, *cast_args)
    return res[0], res[1:]


def _rope_tables(half, n_rep):
    t = jnp.arange(DEC_SEQ)
    inv = ROPE_THETA ** (-jnp.arange(half, dtype=F32) / half)

    def cs(pos):
        ang = pos.astype(F32)[:, None] * inv[None, :]
        c, s = jnp.cos(ang), jnp.sin(ang)
        return jnp.concatenate([c, c], axis=-1), jnp.concatenate([-s, s], axis=-1)

    cr, sr = cs(t // GRID_W)
    cc, sc = cs(t % GRID_W)
    return (jnp.concatenate([cr, cc] * n_rep, axis=-1),
            jnp.concatenate([sr, sc] * n_rep, axis=-1))


def _na_bias(rpb):
    qc = np.arange(GRID_W)[:, None]
    kc = np.arange(GRID_W)[None, :]
    ws = np.clip(qc - NA_KW // 2, 0, GRID_W - NA_KW)
    valid = (kc >= ws) & (kc < ws + NA_KW)
    dcol = np.clip(kc - qc + NA_KW - 1, 0, 2 * NA_KW - 2)
    n_dcol = 2 * NA_KW - 1
    onehot = (dcol.reshape(-1)[None, :] == np.arange(n_dcol)[:, None]).astype(np.float32)
    t = jnp.einsum("lhdc,cq->lhdq", rpb.astype(F32), jnp.asarray(onehot),
                   precision=lax.Precision.HIGHEST)
    t = t.reshape(DEPTH, NA_HEADS, 2 * NA_KH - 1, GRID_W, GRID_W) * LOG2E
    t = jnp.where(jnp.asarray(valid)[None, None, None], t, NEG_BIG)
    masked = jnp.full((DEPTH, NA_HEADS, GRID_W, GRID_W), NEG_BIG, F32)
    ctx = jnp.zeros((DEPTH, NA_HEADS, NA_QTOK, PAST_LEN), F32)
    groups = []
    for j in range(NA_GROUPS):
        s0 = NA_SLAB_START[j]
        q_rows = []
        for r in range(j * NA_QROWS, (j + 1) * NA_QROWS):
            r0 = min(max(r - NA_KH // 2, 0), GRID_ROWS - NA_KH)
            blocks = []
            for key_row in range(s0, s0 + NA_SLAB_ROWS):
                in_window = r0 <= key_row < r0 + NA_KH
                blocks.append(t[:, :, key_row - r + NA_KH - 1] if in_window else masked)
            q_rows.append(jnp.concatenate(blocks, axis=-1))
        slab = jnp.concatenate(q_rows, axis=-2)
        groups.append(jnp.concatenate([ctx, slab] if s0 == 0 else [slab, ctx], axis=-1))
    return jnp.stack(groups, axis=2)


def kernel(x_prompt, x_sample, c, cache_na_k, cache_na_v, cache_gqa_k, cache_gqa_v,
           cache_diff_k, cache_diff_v, c_ctx, w_ada, b_ada, norm_g, w_in, w_out, na_rpb,
           gqa_q_g, gqa_k_g, diff_lam, diff_g, w_up, w_down):
    np_rows = BATCH * SEQ
    ns_rows = DEC_BATCH * DEC_SEQ
    xp = x_prompt.reshape(np_rows, D_MODEL)
    xs = x_sample.reshape(ns_rows, D_MODEL)

    cv = jnp.concatenate(
        [c_ctx[None, :], c, jnp.zeros((MOD_ROWS - 1 - DEC_BATCH, D_MODEL), F32)], axis=0)
    mod = _modulation(cv, w_ada, b_ada).reshape(DEPTH, MOD_ROWS, N_MOD, 1, D_MODEL)

    w_stacks = [w_in, w_out, w_up, w_down]
    w_in_b, w_out_b, w_up_b, w_down_b = [w[0].astype(BF16) for w in w_stacks]

    def flat_cache(a):
        return a.reshape(DEC_BATCH, DEPTH, PAST_LEN, a.shape[3] * HEAD_DIM)

    c_na_k, c_na_v = flat_cache(cache_na_k), flat_cache(cache_na_v)
    c_g_k, c_g_v = flat_cache(cache_gqa_k), flat_cache(cache_gqa_v)
    c_d_k, c_d_v = flat_cache(cache_diff_k), flat_cache(cache_diff_v)

    cos_g, sin_g = _rope_tables(HEAD_DIM // 4, 1)
    cos_d, sin_d = _rope_tables(DIFF_QK_DIM // 4, 2)
    na_bias = _na_bias(na_rpb)

    new_kv = [jnp.zeros((BATCH, DEPTH, SEQ, nh, HEAD_DIM), F32) for nh in KV_HEADS]
    for l in range(DEPTH):
        lambda_init = 0.8 - 0.6 * math.exp(-0.3 * l)
        mod_l = mod[l]
        g = norm_g[l].reshape(4, 1, D_MODEL)
        q_g = gqa_q_g[l].reshape(1, HEAD_DIM)
        k_g = gqa_k_g[l].reshape(1, HEAD_DIM)
        d_g = diff_g[l].reshape(1, HEAD_DIM)
        lam_vec = diff_lam[l]

        qkv_p = _inproj(xp, g[0], mod_l, w_in_b, None, 0)
        o_p, *new_kv = _ctx_attn(qkv_p, q_g, k_g, lam_vec, d_g, new_kv, l, lambda_init)
        xp = _outproj([o_p], w_out_b, xp, g[1], mod_l, None, 0)
        xp, _ = _mlp(xp, g[2], g[3], mod_l, w_up_b, w_down_b, None, 0)

        qkv_s = _inproj(xs, g[0], mod_l, w_in_b, DEC_SEQ, 1)
        o_na = _lat_na(qkv_s, c_na_k, c_na_v, na_bias, l)
        o_gqa = _lat_gqa(qkv_s, c_g_k, c_g_v, cos_g, sin_g, q_g, k_g, l)
        o_diff = _lat_diff(qkv_s, c_d_k, c_d_v, cos_d, sin_d, lam_vec, d_g, l, lambda_init)
        xs = _outproj([o_na, o_gqa, o_diff], w_out_b, xs, g[1], mod_l, DEC_SEQ, 1)
        cast = (w_stacks, l + 1) if l + 1 < DEPTH else None
        xs, nxt = _mlp(xs, g[2], g[3], mod_l, w_up_b, w_down_b, DEC_SEQ, 1, cast)
        if nxt:
            w_in_b, w_out_b, w_up_b, w_down_b = nxt

    return (xp.reshape(BATCH, SEQ, D_MODEL), xs.reshape(DEC_BATCH, DEC_SEQ, D_MODEL), *new_kv)
```

```python
import functools
import math

import jax
import jax.numpy as jnp
import numpy as np
from jax import lax
from jax.experimental import pallas as pl
from jax.experimental.pallas import tpu as pltpu

D_MODEL = 2048
BATCH = 16
SEQ = 256
DEPTH = 4
DEC_BATCH = 8
DEC_SEQ = 1024
PAST_LEN = 256
GRID_W = 64
GRID_ROWS = DEC_SEQ // GRID_W
HEAD_DIM = 128
NA_HEADS = 4
GQA_Q_HEADS = 8
GQA_KV_HEADS = 2
GQA_GROUP = GQA_Q_HEADS // GQA_KV_HEADS
DIFF_HEADS = 4
DIFF_QK_DIM = HEAD_DIM // 2
NA_KH = 8
NA_KW = 16
D_FF = 4 * D_MODEL
ROPE_THETA = 10000.0
EPS = 1e-6
N_MOD = 6
NEG_BIG = -1e30

NA_W = NA_HEADS * HEAD_DIM
GQA_QW = GQA_Q_HEADS * HEAD_DIM
GQA_KVW = GQA_KV_HEADS * HEAD_DIM
DIFF_W = DIFF_HEADS * HEAD_DIM
D_IN = 3 * NA_W + GQA_QW + 2 * GQA_KVW + 3 * DIFF_W
D_MIX = NA_W + GQA_QW + DIFF_W
C_NA_Q, C_NA_K, C_NA_V = 0, NA_W, 2 * NA_W
C_G_Q = 3 * NA_W
C_G_K = C_G_Q + GQA_QW
C_G_V = C_G_K + GQA_KVW
C_D_Q = C_G_V + GQA_KVW
C_D_K = C_D_Q + DIFF_W
C_D_V = C_D_K + DIFF_W

KV_HEADS = (NA_HEADS, NA_HEADS, GQA_KV_HEADS, GQA_KV_HEADS, DIFF_HEADS, DIFF_HEADS)
N_KEYS = DEC_SEQ + PAST_LEN
LOG2E = 1.4426950408889634
QSCALE = HEAD_DIM ** -0.5 * LOG2E
DIFF_QSCALE = DIFF_QK_DIM ** -0.5 * LOG2E

NA_QROWS = 4
NA_QTOK = NA_QROWS * GRID_W
NA_GROUPS = GRID_ROWS // NA_QROWS
NA_SLAB_ROWS = 12
NA_SLAB = NA_SLAB_ROWS * GRID_W
NA_NKEY = NA_SLAB + PAST_LEN
NA_SLAB_START = (0, 0, 4, 4)
NA_PAIRS_BOTH = 2 * NA_KH - 2
MOD_ROWS = 16

F32 = jnp.float32
BF16 = jnp.bfloat16

VMEM_LIMIT = 52 * 1024 * 1024

TM_IN = 1024
TN_IN = 1536
TM_OUT = 512
TM_MLP = 512
TF_MLP = 1024
TN_ADA = 1024
TQ = 512
ATT_ROWS = 256
NORM_ROWS = 128

def _params(sem):
    return pltpu.CompilerParams(dimension_semantics=sem, vmem_limit_bytes=VMEM_LIMIT)


def _rms(x, g):
    ms = jnp.mean(x * x, axis=-1, keepdims=True)
    return x * lax.rsqrt(ms + EPS) * g


def _nt(a, b):
    return lax.dot_general(a, b, (((1,), (1,)), ((), ())), preferred_element_type=F32)


def _mm(a, b):
    return jnp.dot(a, b, preferred_element_type=F32)


def _scores_exp(q, k, bias=None):
    s = _nt(q, k)
    if bias is not None:
        s = s + bias
    return jnp.exp2(s - jnp.max(s, axis=-1, keepdims=True)).astype(BF16)


def _weighted(e, v_ones):
    oa = _mm(e, v_ones)
    d = v_ones.shape[1] // 2
    return oa[:, :d] / oa[:, d:]


def _with_ones(v):
    return jnp.concatenate([v, jnp.ones_like(v)], axis=1)


def _rope(x, cos, sin_signed, half):
    n = x.shape[-1]
    lane = lax.broadcasted_iota(jnp.int32, x.shape, 1)
    first = (lane % (2 * half)) < half
    rot = jnp.where(first, pltpu.roll(x, n - half, 1), pltpu.roll(x, half, 1))
    return x * cos + rot * sin_signed


def _diff_lambda(lam_ref, lambda_init):
    lf = lam_ref[...]
    a = jnp.sum(lf[0:1] * lf[1:2], axis=-1, keepdims=True)
    b = jnp.sum(lf[2:3] * lf[3:4], axis=-1, keepdims=True)
    return jnp.exp(a) - jnp.exp(b) + lambda_init


def _mod_kernel(cv_ref, w_ref, b_ref, o_ref):
    cv = cv_ref[...]
    s = cv / (1.0 + jnp.exp(-cv))
    o_ref[...] = _mm(s.astype(BF16), w_ref[...].astype(BF16)) + b_ref[...]


def _modulation(cv, w_ada, b_ada):
    n = N_MOD * D_MODEL
    return pl.pallas_call(
        _mod_kernel,
        grid=(DEPTH, n // TN_ADA),
        in_specs=[
            pl.BlockSpec((MOD_ROWS, D_MODEL), lambda l, j: (0, 0)),
            pl.BlockSpec((None, D_MODEL, TN_ADA), lambda l, j: (l, 0, j)),
            pl.BlockSpec((None, 1, TN_ADA), lambda l, j: (l, 0, j)),
        ],
        out_specs=pl.BlockSpec((None, MOD_ROWS, TN_ADA), lambda l, j: (l, 0, j)),
        out_shape=jax.ShapeDtypeStruct((DEPTH, MOD_ROWS, n), F32),
        compiler_params=_params(("parallel", "parallel")),
        name="modulation",
    )(cv, w_ada, b_ada.reshape(DEPTH, 1, n))


def _mod_spec(chunk, row_fn):
    return pl.BlockSpec((None, None, 1, D_MODEL), lambda i, *_: (row_fn(i), chunk, 0, 0))


def _row_fn(tm, rows_per_batch, first_row):
    if rows_per_batch is None:
        return lambda i: first_row
    return lambda i: first_row + (i * tm) // rows_per_batch


def _row_chunks(n_rows, body):
    def step(c, carry):
        body(pl.ds(pl.multiple_of(c * NORM_ROWS, NORM_ROWS), NORM_ROWS))
        return carry

    lax.fori_loop(0, n_rows // NORM_ROWS, step, 0)


def _modulated_norm(h_ref, x_ref, g_ref, sc_ref, sh_ref, gm_ref):
    gm_ref[...] = g_ref[...] * (1.0 + sc_ref[...])

    def body(rows):
        x = x_ref[rows, :]
        r = lax.rsqrt(jnp.mean(x * x, axis=-1, keepdims=True) + EPS)
        h_ref[rows, :] = (x * r * gm_ref[...] + sh_ref[...]).astype(BF16)

    _row_chunks(x_ref.shape[0], body)


def _gated_norm_residual(out_ref, x_ref, y_ref, g_ref, gt_ref, gm_ref):
    gm_ref[...] = gt_ref[...] * g_ref[...]

    def body(rows):
        y = y_ref[rows, :]
        r = lax.rsqrt(jnp.mean(y * y, axis=-1, keepdims=True) + EPS)
        out_ref[rows, :] = x_ref[rows, :] + y * r * gm_ref[...]

    _row_chunks(x_ref.shape[0], body)


def _inproj_kernel(x_ref, g_ref, sc_ref, sh_ref, w_ref, o_ref, h_ref, gm_ref):
    @pl.when(pl.program_id(1) == 0)
    def _():
        _modulated_norm(h_ref, x_ref, g_ref, sc_ref, sh_ref, gm_ref)

    o_ref[...] = _mm(h_ref[...], w_ref[...])


def _inproj(x, g_pre, mod_l, w_in, rows_per_batch, first_row):
    m = x.shape[0]
    row = _row_fn(TM_IN, rows_per_batch, first_row)
    return pl.pallas_call(
        _inproj_kernel,
        grid=(m // TM_IN, D_IN // TN_IN),
        in_specs=[
            pl.BlockSpec((TM_IN, D_MODEL), lambda i, j: (i, 0)),
            pl.BlockSpec((1, D_MODEL), lambda i, j: (0, 0)),
            _mod_spec(1, row),
            _mod_spec(0, row),
            pl.BlockSpec((D_MODEL, TN_IN), lambda i, j: (0, j)),
        ],
        out_specs=pl.BlockSpec((TM_IN, TN_IN), lambda i, j: (i, j)),
        out_shape=jax.ShapeDtypeStruct((m, D_IN), F32),
        scratch_shapes=[pltpu.VMEM((TM_IN, D_MODEL), BF16), pltpu.VMEM((1, D_MODEL), F32)],
        compiler_params=_params(("parallel", "arbitrary")),
        name="inproj",
    )(x, g_pre, mod_l, mod_l, w_in)


def _ctx_attn_kernel(qkv_ref, qg_ref, kg_ref, lam_ref, dg_ref, *rest, lambda_init, first):
    if first:
        o_ref, *kv_refs = rest
        for ref in kv_refs:
            ref[1:] = jnp.zeros((DEPTH - 1,) + ref.shape[1:], F32)
        nak_ref, nav_ref, gk_ref, gv_ref, dk_ref, dv_ref = [ref.at[0] for ref in kv_refs]
    else:
        o_ref, nak_ref, nav_ref, gk_ref, gv_ref, dk_ref, dv_ref = rest[len(KV_HEADS):]
    hd = HEAD_DIM

    def cols(c0, h):
        return qkv_ref[:, c0 + h * hd:c0 + (h + 1) * hd]

    for ref, c0 in ((nak_ref, C_NA_K), (nav_ref, C_NA_V), (gv_ref, C_G_V),
                    (dk_ref, C_D_K), (dv_ref, C_D_V)):
        for h in range(ref.shape[1]):
            ref[:, h, :] = cols(c0, h)

    for h in range(NA_HEADS):
        q = (cols(C_NA_Q, h) * QSCALE).astype(BF16)
        k = cols(C_NA_K, h).astype(BF16)
        v1 = _with_ones(cols(C_NA_V, h).astype(BF16))
        o_ref[:, h * hd:(h + 1) * hd] = _weighted(_scores_exp(q, k), v1).astype(BF16)

    for kv in range(GQA_KV_HEADS):
        kf = _rms(cols(C_G_K, kv), kg_ref[...])
        gk_ref[:, kv, :] = kf
        k = kf.astype(BF16)
        v1 = _with_ones(cols(C_G_V, kv).astype(BF16))
        for g in range(GQA_GROUP):
            hq = kv * GQA_GROUP + g
            q = (_rms(cols(C_G_Q, hq), qg_ref[...]) * QSCALE).astype(BF16)
            o = _weighted(_scores_exp(q, k), v1)
            o_ref[:, NA_W + hq * hd:NA_W + (hq + 1) * hd] = o.astype(BF16)

    lam = _diff_lambda(lam_ref, lambda_init)
    lane = lax.broadcasted_iota(jnp.int32, (SEQ, hd), 1)
    for h in range(DIFF_HEADS):
        qf = cols(C_D_Q, h) * DIFF_QSCALE
        k = cols(C_D_K, h).astype(BF16)
        v1 = _with_ones(cols(C_D_V, h).astype(BF16))
        q0 = jnp.where(lane < DIFF_QK_DIM, qf, 0.0).astype(BF16)
        q1 = jnp.where(lane >= DIFF_QK_DIM, qf, 0.0).astype(BF16)
        od = _weighted(_scores_exp(q0, k), v1) - lam * _weighted(_scores_exp(q1, k), v1)
        od = _rms(od, dg_ref[...]) * (1.0 - lambda_init)
        c0 = NA_W + GQA_QW + h * hd
        o_ref[:, c0:c0 + hd] = od.astype(BF16)


def _ctx_attn(qkv, q_g, k_g, lam_vec, diff_g, kv_bufs, layer, lambda_init):
    n = qkv.shape[0]
    vec = pl.BlockSpec((1, HEAD_DIM), lambda b: (0, 0))
    n_in = 5
    first = kv_bufs is None

    def rows(w):
        return pl.BlockSpec((SEQ, w), lambda b: (b, 0))

    def heads(nh):
        if first:
            return pl.BlockSpec((None, DEPTH, SEQ, nh, HEAD_DIM), lambda b: (b, 0, 0, 0, 0))
        return pl.BlockSpec((None, None, SEQ, nh, HEAD_DIM), lambda b: (b, layer, 0, 0, 0))

    bufs = [] if first else list(kv_bufs)
    return pl.pallas_call(
        functools.partial(_ctx_attn_kernel, lambda_init=lambda_init, first=first),
        grid=(n // SEQ,),
        in_specs=[rows(D_IN), vec, vec,
                  pl.BlockSpec((4, DIFF_QK_DIM), lambda b: (0, 0)), vec]
                 + [pl.BlockSpec(memory_space=pl.ANY)] * len(bufs),
        out_specs=[rows(D_MIX)] + [heads(nh) for nh in KV_HEADS],
        out_shape=[jax.ShapeDtypeStruct((n, D_MIX), BF16)]
                  + [jax.ShapeDtypeStruct((BATCH, DEPTH, SEQ, nh, HEAD_DIM), F32)
                     for nh in KV_HEADS],
        input_output_aliases={n_in + i: 1 + i for i in range(len(bufs))},
        compiler_params=_params(("parallel",)),
        name="ctx_attn",
    )(qkv, q_g, k_g, lam_vec, diff_g, *bufs)


def _lat_na_kernel(q_ref, k_ref, v_ref, kc_ref, vc_ref, bias_ref, o_ref, kbig_ref, vbig_ref):
    lat0, lat1 = PAST_LEN, PAST_LEN + DEC_SEQ
    kc = kc_ref[...].astype(BF16)
    vc = _with_ones(vc_ref[...].astype(BF16))
    kbig_ref[0:lat0, :] = kc
    kbig_ref[lat0:lat1, :] = k_ref[...].astype(BF16)
    kbig_ref[lat1:, :] = kc
    vbig_ref[0:lat0, :] = vc
    vbig_ref[lat0:lat1, :] = _with_ones(v_ref[...].astype(BF16))
    vbig_ref[lat1:, :] = vc
    for j in range(NA_GROUPS):
        s0 = NA_SLAB_START[j]
        w0 = 0 if s0 == 0 else PAST_LEN + s0 * GRID_W
        q = (q_ref[j * NA_QTOK:(j + 1) * NA_QTOK, :] * QSCALE).astype(BF16)
        ctx0 = jnp.zeros((GRID_W, PAST_LEN), F32)
        bias_rows = []
        for r in range(j * NA_QROWS, (j + 1) * NA_QROWS):
            slab = [bias_ref[_na_pair_index(r, kr)] for kr in range(s0, s0 + NA_SLAB_ROWS, 2)]
            bias_rows.append(jnp.concatenate([ctx0] + slab if s0 == 0 else slab + [ctx0], axis=1))
        bias = jnp.concatenate(bias_rows, axis=0)
        e = _scores_exp(q, kbig_ref[w0:w0 + NA_NKEY, :], bias)
        o = _weighted(e, vbig_ref[w0:w0 + NA_NKEY, :])
        o_ref[j * NA_QTOK:(j + 1) * NA_QTOK, :] = o.astype(BF16)


def _lat_na(qkv, cache_k, cache_v, bias, layer):
    hd = HEAD_DIM
    cq, ck, cv = C_NA_Q // hd, C_NA_K // hd, C_NA_V // hd
    cache = pl.BlockSpec((None, None, PAST_LEN, hd), lambda h, b: (b, layer, 0, h))
    nbig = DEC_SEQ + 2 * PAST_LEN
    return pl.pallas_call(
        _lat_na_kernel,
        grid=(NA_HEADS, DEC_BATCH),
        in_specs=[
            pl.BlockSpec((DEC_SEQ, hd), lambda h, b: (b, cq + h)),
            pl.BlockSpec((DEC_SEQ, hd), lambda h, b: (b, ck + h)),
            pl.BlockSpec((DEC_SEQ, hd), lambda h, b: (b, cv + h)),
            cache, cache,
            pl.BlockSpec((None, None, NA_PAIRS_BOTH + 3, GRID_W, 2 * GRID_W),
                         lambda h, b: (layer, h, 0, 0, 0)),
        ],
        out_specs=pl.BlockSpec((DEC_SEQ, hd), lambda h, b: (b, h)),
        out_shape=jax.ShapeDtypeStruct((DEC_BATCH * DEC_SEQ, NA_W), BF16),
        scratch_shapes=[pltpu.VMEM((nbig, hd), BF16), pltpu.VMEM((nbig, 2 * hd), BF16)],
        compiler_params=_params(("parallel", "parallel")),
        name="lat_na",
    )(qkv, qkv, qkv, cache_k, cache_v, bias)


def _fill_keys(kall_ref, vall_ref, k_lat, v_ref, kc_ref, vc_ref):
    kall_ref[0:DEC_SEQ, :] = k_lat.astype(BF16)
    kall_ref[DEC_SEQ:N_KEYS, :] = kc_ref[...].astype(BF16)
    vall_ref[0:DEC_SEQ, :] = _with_ones(v_ref[...].astype(BF16))
    vall_ref[DEC_SEQ:N_KEYS, :] = _with_ones(vc_ref[...].astype(BF16))


def _lat_gqa_kernel(q_ref, k_ref, v_ref, kc_ref, vc_ref, cos_ref, sin_ref, qg_ref, kg_ref,
                    o_ref, kall_ref, vall_ref):
    hd = HEAD_DIM
    half = HEAD_DIM // 4
    qb = pl.program_id(2)

    @pl.when(qb == 0)
    def _():
        kf = _rope(_rms(k_ref[...], kg_ref[...]), cos_ref[...], sin_ref[...], half)
        _fill_keys(kall_ref, vall_ref, kf, v_ref, kc_ref, vc_ref)

    for s in range(TQ // ATT_ROWS):
        rows = slice(s * ATT_ROWS, (s + 1) * ATT_ROWS)
        row0 = pl.multiple_of(qb * TQ + s * ATT_ROWS, ATT_ROWS)
        cos = cos_ref[pl.ds(row0, ATT_ROWS), :]
        sin = sin_ref[pl.ds(row0, ATT_ROWS), :]
        for g in range(GQA_GROUP):
            qf = _rms(q_ref[rows, g * hd:(g + 1) * hd], qg_ref[...])
            q = (_rope(qf, cos, sin, half) * QSCALE).astype(BF16)
            o = _weighted(_scores_exp(q, kall_ref[...]), vall_ref[...])
            o_ref[rows, g * hd:(g + 1) * hd] = o.astype(BF16)


def _lat_gqa(qkv, cache_k, cache_v, cos, sin, q_g, k_g, layer):
    hd = HEAD_DIM
    gw = GQA_GROUP * hd
    nqb = DEC_SEQ // TQ
    cq, ck, cv = C_G_Q // gw, C_G_K // hd, C_G_V // hd
    cache = pl.BlockSpec((None, None, PAST_LEN, hd), lambda b, kv, qb: (b, layer, 0, kv))
    table = pl.BlockSpec((DEC_SEQ, hd), lambda b, kv, qb: (0, 0))
    vec = pl.BlockSpec((1, hd), lambda b, kv, qb: (0, 0))
    return pl.pallas_call(
        _lat_gqa_kernel,
        grid=(DEC_BATCH, GQA_KV_HEADS, nqb),
        in_specs=[
            pl.BlockSpec((TQ, gw), lambda b, kv, qb: (b * nqb + qb, cq + kv)),
            pl.BlockSpec((DEC_SEQ, hd), lambda b, kv, qb: (b, ck + kv)),
            pl.BlockSpec((DEC_SEQ, hd), lambda b, kv, qb: (b, cv + kv)),
            cache, cache, table, table, vec, vec,
        ],
        out_specs=pl.BlockSpec((TQ, gw), lambda b, kv, qb: (b * nqb + qb, kv)),
        out_shape=jax.ShapeDtypeStruct((DEC_BATCH * DEC_SEQ, GQA_QW), BF16),
        scratch_shapes=[pltpu.VMEM((N_KEYS, hd), BF16), pltpu.VMEM((N_KEYS, 2 * hd), BF16)],
        compiler_params=_params(("parallel", "parallel", "arbitrary")),
        name="lat_gqa",
    )(qkv, qkv, qkv, cache_k, cache_v, cos, sin, q_g, k_g)


def _lat_diff_kernel(q_ref, k_ref, v_ref, kc_ref, vc_ref, cos_ref, sin_ref, lam_ref, dg_ref,
                     o_ref, kall_ref, vall_ref, *, lambda_init):
    half = DIFF_QK_DIM // 4
    qb = pl.program_id(2)

    @pl.when(qb == 0)
    def _():
        kf = _rope(k_ref[...], cos_ref[...], sin_ref[...], half)
        _fill_keys(kall_ref, vall_ref, kf, v_ref, kc_ref, vc_ref)

    lam = _diff_lambda(lam_ref, lambda_init)
    lane = lax.broadcasted_iota(jnp.int32, (ATT_ROWS, HEAD_DIM), 1)
    for s in range(TQ // ATT_ROWS):
        rows = slice(s * ATT_ROWS, (s + 1) * ATT_ROWS)
        row0 = pl.multiple_of(qb * TQ + s * ATT_ROWS, ATT_ROWS)
        cos = cos_ref[pl.ds(row0, ATT_ROWS), :]
        sin = sin_ref[pl.ds(row0, ATT_ROWS), :]
        qf = _rope(q_ref[rows, :], cos, sin, half) * DIFF_QSCALE
        q0 = jnp.where(lane < DIFF_QK_DIM, qf, 0.0).astype(BF16)
        q1 = jnp.where(lane >= DIFF_QK_DIM, qf, 0.0).astype(BF16)
        od = (_weighted(_scores_exp(q0, kall_ref[...]), vall_ref[...])
              - lam * _weighted(_scores_exp(q1, kall_ref[...]), vall_ref[...]))
        od = _rms(od, dg_ref[...]) * (1.0 - lambda_init)
        o_ref[rows, :] = od.astype(BF16)


def _lat_diff(qkv, cache_k, cache_v, cos, sin, lam_vec, diff_g, layer, lambda_init):
    hd = HEAD_DIM
    nqb = DEC_SEQ // TQ
    cq, ck, cv = C_D_Q // hd, C_D_K // hd, C_D_V // hd
    cache = pl.BlockSpec((None, None, PAST_LEN, hd), lambda b, h, qb: (b, layer, 0, h))
    table = pl.BlockSpec((DEC_SEQ, hd), lambda b, h, qb: (0, 0))
    return pl.pallas_call(
        functools.partial(_lat_diff_kernel, lambda_init=lambda_init),
        grid=(DEC_BATCH, DIFF_HEADS, nqb),
        in_specs=[
            pl.BlockSpec((TQ, hd), lambda b, h, qb: (b * nqb + qb, cq + h)),
            pl.BlockSpec((DEC_SEQ, hd), lambda b, h, qb: (b, ck + h)),
            pl.BlockSpec((DEC_SEQ, hd), lambda b, h, qb: (b, cv + h)),
            cache, cache, table, table,
            pl.BlockSpec((4, DIFF_QK_DIM), lambda b, h, qb: (0, 0)),
            pl.BlockSpec((1, hd), lambda b, h, qb: (0, 0)),
        ],
        out_specs=pl.BlockSpec((TQ, hd), lambda b, h, qb: (b * nqb + qb, h)),
        out_shape=jax.ShapeDtypeStruct((DEC_BATCH * DEC_SEQ, DIFF_W), BF16),
        scratch_shapes=[pltpu.VMEM((N_KEYS, hd), BF16), pltpu.VMEM((N_KEYS, 2 * hd), BF16)],
        compiler_params=_params(("parallel", "parallel", "arbitrary")),
        name="lat_diff",
    )(qkv, qkv, qkv, cache_k, cache_v, cos, sin, lam_vec, diff_g)


def _outproj_kernel(*refs, n_o):
    o_refs = refs[:n_o]
    w_ref, x_ref, g_ref, gt_ref, out_ref = refs[n_o:]
    y = None
    off = 0
    for o_ref in o_refs:
        wd = o_ref.shape[1]
        part = _mm(o_ref[...], w_ref[off:off + wd, :])
        y = part if y is None else y + part
        off += wd
    out_ref[...] = x_ref[...] + gt_ref[...] * _rms(y, g_ref[...])


def _outproj(o_parts, w_out, x, g_post, mod_l, rows_per_batch, first_row):
    m = x.shape[0]
    row = _row_fn(TM_OUT, rows_per_batch, first_row)
    full = pl.BlockSpec((TM_OUT, D_MODEL), lambda i: (i, 0))
    return pl.pallas_call(
        functools.partial(_outproj_kernel, n_o=len(o_parts)),
        grid=(m // TM_OUT,),
        in_specs=[pl.BlockSpec((TM_OUT, o.shape[1]), lambda i: (i, 0)) for o in o_parts] + [
            pl.BlockSpec((D_MIX, D_MODEL), lambda i: (0, 0)),
            full,
            pl.BlockSpec((1, D_MODEL), lambda i: (0, 0)),
            _mod_spec(2, row),
        ],
        out_specs=full,
        out_shape=jax.ShapeDtypeStruct((m, D_MODEL), F32),
        compiler_params=_params(("parallel",)),
        name="outproj",
    )(*o_parts, w_out, x, g_post, mod_l)


def _mlp_kernel(x_ref, gpre_ref, sc_ref, sh_ref, wup_ref, wdn_ref, gpost_ref, gt_ref, *rest):
    n_cast = (len(rest) - 4) // 2
    cast_in = rest[:n_cast]
    out_ref = rest[n_cast]
    cast_out = rest[n_cast + 1:2 * n_cast + 1]
    h_ref, acc_ref, gm_ref = rest[2 * n_cast + 1:]
    k = pl.program_id(1)

    @pl.when(k == 0)
    def _():
        _modulated_norm(h_ref, x_ref, gpre_ref, sc_ref, sh_ref, gm_ref)
        acc_ref[...] = jnp.zeros_like(acc_ref)

    u = _mm(h_ref[...], wup_ref[...])
    a = jnp.square(jnp.maximum(u, 0.0)).astype(BF16)
    acc_ref[...] += _mm(a, wdn_ref[...])

    for src, dst in zip(cast_in, cast_out):
        dst[...] = src[...].astype(BF16)

    @pl.when(k == pl.num_programs(1) - 1)
    def _():
        _gated_norm_residual(out_ref, x_ref, acc_ref, gpost_ref, gt_ref, gm_ref)


def _mlp(x, g_pre, g_post, mod_l, w_up, w_down, rows_per_batch, first_row, cast=None):
    m = x.shape[0]
    row = _row_fn(TM_MLP, rows_per_batch, first_row)
    full = pl.BlockSpec((TM_MLP, D_MODEL), lambda i, k: (i, 0))
    vec = pl.BlockSpec((1, D_MODEL), lambda i, k: (0, 0))
    n_k = D_FF // TF_MLP
    n_steps = (m // TM_MLP) * n_k
    cast_in_specs, cast_out_specs, cast_shapes, cast_args = [], [], [], []
    if cast is not None:
        stacks, layer = cast
        for w in stacks:
            rows, cols = w.shape[1], w.shape[2]
            rb = rows // n_steps
            cast_in_specs.append(
                pl.BlockSpec((None, rb, cols), lambda i, k: (layer, i * n_k + k, 0)))
            cast_out_specs.append(pl.BlockSpec((rb, cols), lambda i, k: (i * n_k + k, 0)))
            cast_shapes.append(jax.ShapeDtypeStruct((rows, cols), BF16))
            cast_args.append(w)
    res = pl.pallas_call(
        _mlp_kernel,
        grid=(m // TM_MLP, n_k),
        in_specs=[
            full, vec, _mod_spec(4, row), _mod_spec(3, row),
            pl.BlockSpec((D_MODEL, TF_MLP), lambda i, k: (0, k)),
            pl.BlockSpec((TF_MLP, D_MODEL), lambda i, k: (k, 0)),
            vec, _mod_spec(5, row),
        ] + cast_in_specs,
        out_specs=[full] + cast_out_specs,
        out_shape=[jax.ShapeDtypeStruct((m, D_MODEL), F32)] + cast_shapes,
        scratch_shapes=[pltpu.VMEM((TM_MLP, D_MODEL), BF16), pltpu.VMEM((TM_MLP, D_MODEL), F32),
                        pltpu.VMEM((1, D_MODEL), F32)],
        compiler_params=_params(("parallel", "arbitrary")),
        name="mlp",
    )(x, g_pre, mod_l, mod_l, w_up, w_down, g_post, mod_l, *cast_args)
    return res[0], res[1:]


def _rope_tables(half, n_rep):
    t = jnp.arange(DEC_SEQ)
    inv = ROPE_THETA ** (-jnp.arange(half, dtype=F32) / half)

    def cs(pos):
        ang = pos.astype(F32)[:, None] * inv[None, :]
        c, s = jnp.cos(ang), jnp.sin(ang)
        return jnp.concatenate([c, c], axis=-1), jnp.concatenate([-s, s], axis=-1)

    cr, sr = cs(t // GRID_W)
    cc, sc = cs(t % GRID_W)
    return (jnp.concatenate([cr, cc] * n_rep, axis=-1),
            jnp.concatenate([sr, sc] * n_rep, axis=-1))


def _na_bias(rpb):
    qc = np.arange(GRID_W)[:, None]
    kc = np.arange(GRID_W)[None, :]
    ws = np.clip(qc - NA_KW // 2, 0, GRID_W - NA_KW)
    valid = (kc >= ws) & (kc < ws + NA_KW)
    dcol = np.clip(kc - qc + NA_KW - 1, 0, 2 * NA_KW - 2)
    n_dcol = 2 * NA_KW - 1
    onehot = (dcol.reshape(-1)[None, :] == np.arange(n_dcol)[:, None]).astype(np.float32)
    t = jnp.einsum("lhdc,cq->lhdq", rpb.astype(F32), jnp.asarray(onehot),
                   precision=lax.Precision.HIGHEST)
    t = t.reshape(DEPTH, NA_HEADS, 2 * NA_KH - 1, GRID_W, GRID_W) * LOG2E
    t = jnp.where(jnp.asarray(valid)[None, None, None], t, NEG_BIG)
    masked = jnp.full((DEPTH, NA_HEADS, 1, GRID_W, GRID_W), NEG_BIG, F32)
    lo, hi = NA_KH // 2 - 1, NA_KH + NA_KH // 2 - 2
    both = jnp.concatenate([t[:, :, :-1], t[:, :, 1:]], axis=-1)
    second = jnp.concatenate([masked, t[:, :, lo:lo + 1]], axis=-1)
    first = jnp.concatenate([t[:, :, hi:hi + 1], masked], axis=-1)
    none = jnp.concatenate([masked, masked], axis=-1)
    return jnp.concatenate([both, second, first, none], axis=2)


def _na_pair_index(r, key_row):
    r0 = min(max(r - NA_KH // 2, 0), GRID_ROWS - NA_KH)
    in_a = r0 <= key_row < r0 + NA_KH
    in_b = r0 <= key_row + 1 < r0 + NA_KH
    d = key_row - r + NA_KH - 1
    if in_a and in_b:
        return d
    if in_b:
        assert d + 1 == NA_KH // 2 - 1
        return NA_PAIRS_BOTH
    if in_a:
        assert d == NA_KH + NA_KH // 2 - 2
        return NA_PAIRS_BOTH + 1
    return NA_PAIRS_BOTH + 2


def kernel(x_prompt, x_sample, c, cache_na_k, cache_na_v, cache_gqa_k, cache_gqa_v,
           cache_diff_k, cache_diff_v, c_ctx, w_ada, b_ada, norm_g, w_in, w_out, na_rpb,
           gqa_q_g, gqa_k_g, diff_lam, diff_g, w_up, w_down):
    np_rows = BATCH * SEQ
    ns_rows = DEC_BATCH * DEC_SEQ
    xp = x_prompt.reshape(np_rows, D_MODEL)
    xs = x_sample.reshape(ns_rows, D_MODEL)

    cv = jnp.concatenate(
        [c_ctx[None, :], c, jnp.zeros((MOD_ROWS - 1 - DEC_BATCH, D_MODEL), F32)], axis=0)
    mod = _modulation(cv, w_ada, b_ada).reshape(DEPTH, MOD_ROWS, N_MOD, 1, D_MODEL)

    w_stacks = [w_in, w_out, w_up, w_down]
    w_in_b, w_out_b, w_up_b, w_down_b = [w[0].astype(BF16) for w in w_stacks]

    def flat_cache(a):
        return a.reshape(DEC_BATCH, DEPTH, PAST_LEN, a.shape[3] * HEAD_DIM)

    c_na_k, c_na_v = flat_cache(cache_na_k), flat_cache(cache_na_v)
    c_g_k, c_g_v = flat_cache(cache_gqa_k), flat_cache(cache_gqa_v)
    c_d_k, c_d_v = flat_cache(cache_diff_k), flat_cache(cache_diff_v)

    cos_g, sin_g = _rope_tables(HEAD_DIM // 4, 1)
    cos_d, sin_d = _rope_tables(DIFF_QK_DIM // 4, 2)
    na_bias = _na_bias(na_rpb)

    new_kv = None
    for l in range(DEPTH):
        lambda_init = 0.8 - 0.6 * math.exp(-0.3 * l)
        mod_l = mod[l]
        g = norm_g[l].reshape(4, 1, D_MODEL)
        q_g = gqa_q_g[l].reshape(1, HEAD_DIM)
        k_g = gqa_k_g[l].reshape(1, HEAD_DIM)
        d_g = diff_g[l].reshape(1, HEAD_DIM)
        lam_vec = diff_lam[l]

        qkv_p = _inproj(xp, g[0], mod_l, w_in_b, None, 0)
        o_p, *new_kv = _ctx_attn(qkv_p, q_g, k_g, lam_vec, d_g, new_kv, l, lambda_init)
        xp = _outproj([o_p], w_out_b, xp, g[1], mod_l, None, 0)
        xp, _ = _mlp(xp, g[2], g[3], mod_l, w_up_b, w_down_b, None, 0)

        qkv_s = _inproj(xs, g[0], mod_l, w_in_b, DEC_SEQ, 1)
        o_na = _lat_na(qkv_s, c_na_k, c_na_v, na_bias, l)
        o_gqa = _lat_gqa(qkv_s, c_g_k, c_g_v, cos_g, sin_g, q_g, k_g, l)
        o_diff = _lat_diff(qkv_s, c_d_k, c_d_v, cos_d, sin_d, lam_vec, d_g, l, lambda_init)
        xs = _outproj([o_na, o_gqa, o_diff], w_out_b, xs, g[1], mod_l, DEC_SEQ, 1)
        cast = (w_stacks, l + 1) if l + 1 < DEPTH else None
        xs, nxt = _mlp(xs, g[2], g[3], mod_l, w_up_b, w_down_b, DEC_SEQ, 1, cast)
        if nxt:
            w_in_b, w_out_b, w_up_b, w_down_b = nxt

    return (xp.reshape(BATCH, SEQ, D_MODEL), xs.reshape(DEC_BATCH, DEC_SEQ, D_MODEL), *new_kv)
```

```python
import functools
import math

import jax
import jax.numpy as jnp
import numpy as np
from jax import lax
from jax.experimental import pallas as pl
from jax.experimental.pallas import tpu as pltpu

D_MODEL = 2048
BATCH = 16
SEQ = 256
DEPTH = 4
DEC_BATCH = 8
DEC_SEQ = 1024
PAST_LEN = 256
GRID_W = 64
GRID_ROWS = DEC_SEQ // GRID_W
HEAD_DIM = 128
NA_HEADS = 4
GQA_Q_HEADS = 8
GQA_KV_HEADS = 2
GQA_GROUP = GQA_Q_HEADS // GQA_KV_HEADS
DIFF_HEADS = 4
DIFF_QK_DIM = HEAD_DIM // 2
NA_KH = 8
NA_KW = 16
D_FF = 4 * D_MODEL
ROPE_THETA = 10000.0
EPS = 1e-6
N_MOD = 6
NEG_BIG = -1e30

NA_W = NA_HEADS * HEAD_DIM
GQA_QW = GQA_Q_HEADS * HEAD_DIM
GQA_KVW = GQA_KV_HEADS * HEAD_DIM
DIFF_W = DIFF_HEADS * HEAD_DIM
D_IN = 3 * NA_W + GQA_QW + 2 * GQA_KVW + 3 * DIFF_W
D_MIX = NA_W + GQA_QW + DIFF_W
C_NA_Q, C_NA_K, C_NA_V = 0, NA_W, 2 * NA_W
C_G_Q = 3 * NA_W
C_G_K = C_G_Q + GQA_QW
C_G_V = C_G_K + GQA_KVW
C_D_Q = C_G_V + GQA_KVW
C_D_K = C_D_Q + DIFF_W
C_D_V = C_D_K + DIFF_W

KV_HEADS = (NA_HEADS, NA_HEADS, GQA_KV_HEADS, GQA_KV_HEADS, DIFF_HEADS, DIFF_HEADS)
N_KEYS = DEC_SEQ + PAST_LEN
LOG2E = 1.4426950408889634
QSCALE = HEAD_DIM ** -0.5 * LOG2E
DIFF_QSCALE = DIFF_QK_DIM ** -0.5 * LOG2E

NA_QROWS = 4
NA_QTOK = NA_QROWS * GRID_W
NA_GROUPS = GRID_ROWS // NA_QROWS
NA_SLAB_ROWS = 12
NA_SLAB = NA_SLAB_ROWS * GRID_W
NA_NKEY = NA_SLAB + PAST_LEN
NA_SLAB_START = (0, 0, 4, 4)
NA_PAIRS_BOTH = 2 * NA_KH - 2
MOD_ROWS = 16

F32 = jnp.float32
BF16 = jnp.bfloat16

VMEM_LIMIT = 52 * 1024 * 1024

TM_IN = 1024
TN_IN = 1536
TM_OUT = 512
TM_MLP = 512
TF_MLP = 1024
TN_ADA = 1024
TQ = 512
ATT_ROWS = 256
NORM_ROWS = 128

def _params(sem):
    return pltpu.CompilerParams(dimension_semantics=sem, vmem_limit_bytes=VMEM_LIMIT)


def _rms(x, g):
    ms = jnp.mean(x * x, axis=-1, keepdims=True)
    return x * lax.rsqrt(ms + EPS) * g


def _nt(a, b):
    return lax.dot_general(a, b, (((1,), (1,)), ((), ())), preferred_element_type=F32)


def _mm(a, b):
    return jnp.dot(a, b, preferred_element_type=F32)


def _scores_exp(q, k, bias=None):
    s = _nt(q, k)
    if bias is not None:
        s = s + bias
    return jnp.exp2(s - jnp.max(s, axis=-1, keepdims=True)).astype(BF16)


def _weighted(e, v_ones):
    oa = _mm(e, v_ones)
    d = v_ones.shape[1] // 2
    return oa[:, :d] / oa[:, d:]


def _with_ones(v):
    return jnp.concatenate([v, jnp.ones_like(v)], axis=1)


def _rope(x, cos, sin_signed, half):
    n = x.shape[-1]
    lane = lax.broadcasted_iota(jnp.int32, x.shape, 1)
    first = (lane % (2 * half)) < half
    rot = jnp.where(first, pltpu.roll(x, n - half, 1), pltpu.roll(x, half, 1))
    return x * cos + rot * sin_signed


def _diff_lambda(lam_ref, lambda_init):
    lf = lam_ref[...]
    a = jnp.sum(lf[0:1] * lf[1:2], axis=-1, keepdims=True)
    b = jnp.sum(lf[2:3] * lf[3:4], axis=-1, keepdims=True)
    return jnp.exp(a) - jnp.exp(b) + lambda_init


def _mod_kernel(cv_ref, w_ref, b_ref, o_ref):
    cv = cv_ref[...]
    s = cv / (1.0 + jnp.exp(-cv))
    o_ref[...] = _mm(s.astype(BF16), w_ref[...].astype(BF16)) + b_ref[...]


def _modulation(cv, w_ada, b_ada):
    n = N_MOD * D_MODEL
    return pl.pallas_call(
        _mod_kernel,
        grid=(DEPTH, n // TN_ADA),
        in_specs=[
            pl.BlockSpec((MOD_ROWS, D_MODEL), lambda l, j: (0, 0)),
            pl.BlockSpec((None, D_MODEL, TN_ADA), lambda l, j: (l, 0, j)),
            pl.BlockSpec((None, 1, TN_ADA), lambda l, j: (l, 0, j)),
        ],
        out_specs=pl.BlockSpec((None, MOD_ROWS, TN_ADA), lambda l, j: (l, 0, j)),
        out_shape=jax.ShapeDtypeStruct((DEPTH, MOD_ROWS, n), F32),
        compiler_params=_params(("parallel", "parallel")),
        name="modulation",
    )(cv, w_ada, b_ada.reshape(DEPTH, 1, n))


def _mod_spec(chunk, row_fn):
    return pl.BlockSpec((None, None, 1, D_MODEL), lambda i, *_: (row_fn(i), chunk, 0, 0))


def _row_fn(tm, rows_per_batch, first_row):
    if rows_per_batch is None:
        return lambda i: first_row
    return lambda i: first_row + (i * tm) // rows_per_batch


def _row_chunks(n_rows, body):
    def step(c, carry):
        body(pl.ds(pl.multiple_of(c * NORM_ROWS, NORM_ROWS), NORM_ROWS))
        return carry

    lax.fori_loop(0, n_rows // NORM_ROWS, step, 0)


def _modulated_norm(h_ref, x_ref, g_ref, sc_ref, sh_ref, gm_ref):
    gm_ref[...] = g_ref[...] * (1.0 + sc_ref[...])

    def body(rows):
        x = x_ref[rows, :]
        r = lax.rsqrt(jnp.mean(x * x, axis=-1, keepdims=True) + EPS)
        h_ref[rows, :] = (x * r * gm_ref[...] + sh_ref[...]).astype(BF16)

    _row_chunks(x_ref.shape[0], body)


def _gated_norm_residual(out_ref, x_ref, y_ref, g_ref, gt_ref, gm_ref):
    gm_ref[...] = gt_ref[...] * g_ref[...]

    def body(rows):
        y = y_ref[rows, :]
        r = lax.rsqrt(jnp.mean(y * y, axis=-1, keepdims=True) + EPS)
        out_ref[rows, :] = x_ref[rows, :] + y * r * gm_ref[...]

    _row_chunks(x_ref.shape[0], body)


def _inproj_kernel(x_ref, g_ref, sc_ref, sh_ref, w_ref, o_ref, h_ref, gm_ref):
    @pl.when(pl.program_id(1) == 0)
    def _():
        _modulated_norm(h_ref, x_ref, g_ref, sc_ref, sh_ref, gm_ref)

    o_ref[...] = _mm(h_ref[...], w_ref[...])


def _inproj(x, g_pre, mod_l, w_in, rows_per_batch, first_row):
    m = x.shape[0]
    row = _row_fn(TM_IN, rows_per_batch, first_row)
    return pl.pallas_call(
        _inproj_kernel,
        grid=(m // TM_IN, D_IN // TN_IN),
        in_specs=[
            pl.BlockSpec((TM_IN, D_MODEL), lambda i, j: (i, 0)),
            pl.BlockSpec((1, D_MODEL), lambda i, j: (0, 0)),
            _mod_spec(1, row),
            _mod_spec(0, row),
            pl.BlockSpec((D_MODEL, TN_IN), lambda i, j: (0, j)),
        ],
        out_specs=pl.BlockSpec((TM_IN, TN_IN), lambda i, j: (i, j)),
        out_shape=jax.ShapeDtypeStruct((m, D_IN), F32),
        scratch_shapes=[pltpu.VMEM((TM_IN, D_MODEL), BF16), pltpu.VMEM((1, D_MODEL), F32)],
        compiler_params=_params(("parallel", "arbitrary")),
        name="inproj",
    )(x, g_pre, mod_l, mod_l, w_in)


def _ctx_attn_kernel(qkv_ref, qg_ref, kg_ref, lam_ref, dg_ref, *rest, lambda_init, first):
    if first:
        o_ref, *kv_refs = rest
        for ref in kv_refs:
            ref[1:] = jnp.zeros((DEPTH - 1,) + ref.shape[1:], F32)
        nak_ref, nav_ref, gk_ref, gv_ref, dk_ref, dv_ref = [ref.at[0] for ref in kv_refs]
    else:
        o_ref, nak_ref, nav_ref, gk_ref, gv_ref, dk_ref, dv_ref = rest[len(KV_HEADS):]
    hd = HEAD_DIM

    def cols(c0, h):
        return qkv_ref[:, c0 + h * hd:c0 + (h + 1) * hd]

    for ref, c0 in ((nak_ref, C_NA_K), (nav_ref, C_NA_V), (gv_ref, C_G_V),
                    (dk_ref, C_D_K), (dv_ref, C_D_V)):
        for h in range(ref.shape[1]):
            ref[:, h, :] = cols(c0, h)

    for h in range(NA_HEADS):
        q = (cols(C_NA_Q, h) * QSCALE).astype(BF16)
        k = cols(C_NA_K, h).astype(BF16)
        v1 = _with_ones(cols(C_NA_V, h).astype(BF16))
        o_ref[:, h * hd:(h + 1) * hd] = _weighted(_scores_exp(q, k), v1).astype(BF16)

    for kv in range(GQA_KV_HEADS):
        kf = _rms(cols(C_G_K, kv), kg_ref[...])
        gk_ref[:, kv, :] = kf
        k = kf.astype(BF16)
        v1 = _with_ones(cols(C_G_V, kv).astype(BF16))
        for g in range(GQA_GROUP):
            hq = kv * GQA_GROUP + g
            q = (_rms(cols(C_G_Q, hq), qg_ref[...]) * QSCALE).astype(BF16)
            o = _weighted(_scores_exp(q, k), v1)
            o_ref[:, NA_W + hq * hd:NA_W + (hq + 1) * hd] = o.astype(BF16)

    lam = _diff_lambda(lam_ref, lambda_init)
    lane = lax.broadcasted_iota(jnp.int32, (SEQ, hd), 1)
    for h in range(DIFF_HEADS):
        qf = cols(C_D_Q, h) * DIFF_QSCALE
        k = cols(C_D_K, h).astype(BF16)
        v1 = _with_ones(cols(C_D_V, h).astype(BF16))
        q0 = jnp.where(lane < DIFF_QK_DIM, qf, 0.0).astype(BF16)
        q1 = jnp.where(lane >= DIFF_QK_DIM, qf, 0.0).astype(BF16)
        od = _weighted(_scores_exp(q0, k), v1) - lam * _weighted(_scores_exp(q1, k), v1)
        od = _rms(od, dg_ref[...]) * (1.0 - lambda_init)
        c0 = NA_W + GQA_QW + h * hd
        o_ref[:, c0:c0 + hd] = od.astype(BF16)


def _ctx_attn(qkv, q_g, k_g, lam_vec, diff_g, kv_bufs, layer, lambda_init):
    n = qkv.shape[0]
    vec = pl.BlockSpec((1, HEAD_DIM), lambda b: (0, 0))
    n_in = 5
    first = kv_bufs is None

    def rows(w):
        return pl.BlockSpec((SEQ, w), lambda b: (b, 0))

    def heads(nh):
        if first:
            return pl.BlockSpec((None, DEPTH, SEQ, nh, HEAD_DIM), lambda b: (b, 0, 0, 0, 0))
        return pl.BlockSpec((None, None, SEQ, nh, HEAD_DIM), lambda b: (b, layer, 0, 0, 0))

    bufs = [] if first else list(kv_bufs)
    return pl.pallas_call(
        functools.partial(_ctx_attn_kernel, lambda_init=lambda_init, first=first),
        grid=(n // SEQ,),
        in_specs=[rows(D_IN), vec, vec,
                  pl.BlockSpec((4, DIFF_QK_DIM), lambda b: (0, 0)), vec]
                 + [pl.BlockSpec(memory_space=pl.ANY)] * len(bufs),
        out_specs=[rows(D_MIX)] + [heads(nh) for nh in KV_HEADS],
        out_shape=[jax.ShapeDtypeStruct((n, D_MIX), BF16)]
                  + [jax.ShapeDtypeStruct((BATCH, DEPTH, SEQ, nh, HEAD_DIM), F32)
                     for nh in KV_HEADS],
        input_output_aliases={n_in + i: 1 + i for i in range(len(bufs))},
        compiler_params=_params(("parallel",)),
        name="ctx_attn",
    )(qkv, q_g, k_g, lam_vec, diff_g, *bufs)


def _lat_na_kernel(q_ref, k_ref, v_ref, kc_ref, vc_ref, bias_ref, o_ref, kbig_ref, vbig_ref):
    lat0, lat1 = PAST_LEN, PAST_LEN + DEC_SEQ
    kc = kc_ref[...].astype(BF16)
    vc = _with_ones(vc_ref[...].astype(BF16))
    kbig_ref[0:lat0, :] = kc
    kbig_ref[lat0:lat1, :] = k_ref[...].astype(BF16)
    kbig_ref[lat1:, :] = kc
    vbig_ref[0:lat0, :] = vc
    vbig_ref[lat0:lat1, :] = _with_ones(v_ref[...].astype(BF16))
    vbig_ref[lat1:, :] = vc
    for j in range(NA_GROUPS):
        s0 = NA_SLAB_START[j]
        w0 = 0 if s0 == 0 else PAST_LEN + s0 * GRID_W
        q = (q_ref[j * NA_QTOK:(j + 1) * NA_QTOK, :] * QSCALE).astype(BF16)
        ctx0 = jnp.zeros((GRID_W, PAST_LEN), F32)
        bias_rows = []
        for r in range(j * NA_QROWS, (j + 1) * NA_QROWS):
            slab = [bias_ref[_na_pair_index(r, kr)] for kr in range(s0, s0 + NA_SLAB_ROWS, 2)]
            bias_rows.append(jnp.concatenate([ctx0] + slab if s0 == 0 else slab + [ctx0], axis=1))
        bias = jnp.concatenate(bias_rows, axis=0)
        e = _scores_exp(q, kbig_ref[w0:w0 + NA_NKEY, :], bias)
        o = _weighted(e, vbig_ref[w0:w0 + NA_NKEY, :])
        o_ref[j * NA_QTOK:(j + 1) * NA_QTOK, :] = o.astype(BF16)


def _lat_na(qkv, cache_k, cache_v, bias, layer):
    hd = HEAD_DIM
    cq, ck, cv = C_NA_Q // hd, C_NA_K // hd, C_NA_V // hd
    cache = pl.BlockSpec((None, None, PAST_LEN, hd), lambda h, b: (b, layer, 0, h))
    nbig = DEC_SEQ + 2 * PAST_LEN
    return pl.pallas_call(
        _lat_na_kernel,
        grid=(NA_HEADS, DEC_BATCH),
        in_specs=[
            pl.BlockSpec((DEC_SEQ, hd), lambda h, b: (b, cq + h)),
            pl.BlockSpec((DEC_SEQ, hd), lambda h, b: (b, ck + h)),
            pl.BlockSpec((DEC_SEQ, hd), lambda h, b: (b, cv + h)),
            cache, cache,
            pl.BlockSpec((None, None, NA_PAIRS_BOTH + 3, GRID_W, 2 * GRID_W),
                         lambda h, b: (layer, h, 0, 0, 0)),
        ],
        out_specs=pl.BlockSpec((DEC_SEQ, hd), lambda h, b: (b, h)),
        out_shape=jax.ShapeDtypeStruct((DEC_BATCH * DEC_SEQ, NA_W), BF16),
        scratch_shapes=[pltpu.VMEM((nbig, hd), BF16), pltpu.VMEM((nbig, 2 * hd), BF16)],
        compiler_params=_params(("parallel", "parallel")),
        name="lat_na",
    )(qkv, qkv, qkv, cache_k, cache_v, bias)


def _fill_keys(kall_ref, vall_ref, k_lat, v_ref, kc_ref, vc_ref):
    kall_ref[0:DEC_SEQ, :] = k_lat.astype(BF16)
    kall_ref[DEC_SEQ:N_KEYS, :] = kc_ref[...].astype(BF16)
    vall_ref[0:DEC_SEQ, :] = _with_ones(v_ref[...].astype(BF16))
    vall_ref[DEC_SEQ:N_KEYS, :] = _with_ones(vc_ref[...].astype(BF16))


def _lat_gqa_kernel(q_ref, k_ref, v_ref, kc_ref, vc_ref, cos_ref, sin_ref, qg_ref, kg_ref,
                    o_ref, kall_ref, vall_ref):
    hd = HEAD_DIM
    half = HEAD_DIM // 4
    qb = pl.program_id(2)

    @pl.when(qb == 0)
    def _():
        kf = _rope(_rms(k_ref[...], kg_ref[...]), cos_ref[...], sin_ref[...], half)
        _fill_keys(kall_ref, vall_ref, kf, v_ref, kc_ref, vc_ref)

    for s in range(TQ // ATT_ROWS):
        rows = slice(s * ATT_ROWS, (s + 1) * ATT_ROWS)
        row0 = pl.multiple_of(qb * TQ + s * ATT_ROWS, ATT_ROWS)
        cos = cos_ref[pl.ds(row0, ATT_ROWS), :]
        sin = sin_ref[pl.ds(row0, ATT_ROWS), :]
        for g in range(GQA_GROUP):
            qf = _rms(q_ref[rows, g * hd:(g + 1) * hd], qg_ref[...])
            q = (_rope(qf, cos, sin, half) * QSCALE).astype(BF16)
            o = _weighted(_scores_exp(q, kall_ref[...]), vall_ref[...])
            o_ref[rows, g * hd:(g + 1) * hd] = o.astype(BF16)


def _lat_gqa(qkv, cache_k, cache_v, cos, sin, q_g, k_g, layer):
    hd = HEAD_DIM
    gw = GQA_GROUP * hd
    nqb = DEC_SEQ // TQ
    cq, ck, cv = C_G_Q // gw, C_G_K // hd, C_G_V // hd
    cache = pl.BlockSpec((None, None, PAST_LEN, hd), lambda b, kv, qb: (b, layer, 0, kv))
    table = pl.BlockSpec((DEC_SEQ, hd), lambda b, kv, qb: (0, 0))
    vec = pl.BlockSpec((1, hd), lambda b, kv, qb: (0, 0))
    return pl.pallas_call(
        _lat_gqa_kernel,
        grid=(DEC_BATCH, GQA_KV_HEADS, nqb),
        in_specs=[
            pl.BlockSpec((TQ, gw), lambda b, kv, qb: (b * nqb + qb, cq + kv)),
            pl.BlockSpec((DEC_SEQ, hd), lambda b, kv, qb: (b, ck + kv)),
            pl.BlockSpec((DEC_SEQ, hd), lambda b, kv, qb: (b, cv + kv)),
            cache, cache, table, table, vec, vec,
        ],
        out_specs=pl.BlockSpec((TQ, gw), lambda b, kv, qb: (b * nqb + qb, kv)),
        out_shape=jax.ShapeDtypeStruct((DEC_BATCH * DEC_SEQ, GQA_QW), BF16),
        scratch_shapes=[pltpu.VMEM((N_KEYS, hd), BF16), pltpu.VMEM((N_KEYS, 2 * hd), BF16)],
        compiler_params=_params(("parallel", "parallel", "arbitrary")),
        name="lat_gqa",
    )(qkv, qkv, qkv, cache_k, cache_v, cos, sin, q_g, k_g)


def _lat_diff_kernel(q_ref, k_ref, v_ref, kc_ref, vc_ref, cos_ref, sin_ref, lam_ref, dg_ref,
                     o_ref, kall_ref, vall_ref, *, lambda_init):
    half = DIFF_QK_DIM // 4
    qb = pl.program_id(2)

    @pl.when(qb == 0)
    def _():
        kf = _rope(k_ref[...], cos_ref[...], sin_ref[...], half)
        _fill_keys(kall_ref, vall_ref, kf, v_ref, kc_ref, vc_ref)

    lam = _diff_lambda(lam_ref, lambda_init)
    lane = lax.broadcasted_iota(jnp.int32, (ATT_ROWS, HEAD_DIM), 1)
    for s in range(TQ // ATT_ROWS):
        rows = slice(s * ATT_ROWS, (s + 1) * ATT_ROWS)
        row0 = pl.multiple_of(qb * TQ + s * ATT_ROWS, ATT_ROWS)
        cos = cos_ref[pl.ds(row0, ATT_ROWS), :]
        sin = sin_ref[pl.ds(row0, ATT_ROWS), :]
        qf = _rope(q_ref[rows, :], cos, sin, half) * DIFF_QSCALE
        q0 = jnp.where(lane < DIFF_QK_DIM, qf, 0.0).astype(BF16)
        q1 = jnp.where(lane >= DIFF_QK_DIM, qf, 0.0).astype(BF16)
        od = (_weighted(_scores_exp(q0, kall_ref[...]), vall_ref[...])
              - lam * _weighted(_scores_exp(q1, kall_ref[...]), vall_ref[...]))
        od = _rms(od, dg_ref[...]) * (1.0 - lambda_init)
        o_ref[rows, :] = od.astype(BF16)


def _lat_diff(qkv, cache_k, cache_v, cos, sin, lam_vec, diff_g, layer, lambda_init):
    hd = HEAD_DIM
    nqb = DEC_SEQ // TQ
    cq, ck, cv = C_D_Q // hd, C_D_K // hd, C_D_V // hd
    cache = pl.BlockSpec((None, None, PAST_LEN, hd), lambda b, h, qb: (b, layer, 0, h))
    table = pl.BlockSpec((DEC_SEQ, hd), lambda b, h, qb: (0, 0))
    return pl.pallas_call(
        functools.partial(_lat_diff_kernel, lambda_init=lambda_init),
        grid=(DEC_BATCH, DIFF_HEADS, nqb),
        in_specs=[
            pl.BlockSpec((TQ, hd), lambda b, h, qb: (b * nqb + qb, cq + h)),
            pl.BlockSpec((DEC_SEQ, hd), lambda b, h, qb: (b, ck + h)),
            pl.BlockSpec((DEC_SEQ, hd), lambda b, h, qb: (b, cv + h)),
            cache, cache, table, table,
            pl.BlockSpec((4, DIFF_QK_DIM), lambda b, h, qb: (0, 0)),
            pl.BlockSpec((1, hd), lambda b, h, qb: (0, 0)),
        ],
        out_specs=pl.BlockSpec((TQ, hd), lambda b, h, qb: (b * nqb + qb, h)),
        out_shape=jax.ShapeDtypeStruct((DEC_BATCH * DEC_SEQ, DIFF_W), BF16),
        scratch_shapes=[pltpu.VMEM((N_KEYS, hd), BF16), pltpu.VMEM((N_KEYS, 2 * hd), BF16)],
        compiler_params=_params(("parallel", "parallel", "arbitrary")),
        name="lat_diff",
    )(qkv, qkv, qkv, cache_k, cache_v, cos, sin, lam_vec, diff_g)


def _outproj_kernel(*refs, n_o):
    o_refs = refs[:n_o]
    w_ref, x_ref, g_ref, gt_ref, out_ref = refs[n_o:]
    y = None
    off = 0
    for o_ref in o_refs:
        wd = o_ref.shape[1]
        part = _mm(o_ref[...], w_ref[off:off + wd, :])
        y = part if y is None else y + part
        off += wd
    out_ref[...] = x_ref[...] + gt_ref[...] * _rms(y, g_ref[...])


def _outproj(o_parts, w_out, x, g_post, mod_l, rows_per_batch, first_row):
    m = x.shape[0]
    row = _row_fn(TM_OUT, rows_per_batch, first_row)
    full = pl.BlockSpec((TM_OUT, D_MODEL), lambda i: (i, 0))
    return pl.pallas_call(
        functools.partial(_outproj_kernel, n_o=len(o_parts)),
        grid=(m // TM_OUT,),
        in_specs=[pl.BlockSpec((TM_OUT, o.shape[1]), lambda i: (i, 0)) for o in o_parts] + [
            pl.BlockSpec((D_MIX, D_MODEL), lambda i: (0, 0)),
            full,
            pl.BlockSpec((1, D_MODEL), lambda i: (0, 0)),
            _mod_spec(2, row),
        ],
        out_specs=full,
        out_shape=jax.ShapeDtypeStruct((m, D_MODEL), F32),
        compiler_params=_params(("parallel",)),
        name="outproj",
    )(*o_parts, w_out, x, g_post, mod_l)


def _mlp_kernel(x_ref, gpre_ref, sc_ref, sh_ref, wup_ref, wdn_ref, gpost_ref, gt_ref, *rest,
                n_cast, n_flat):
    n_side = n_cast + n_flat
    cast_in, flat_in = rest[:n_cast], rest[n_cast:n_side]
    out_ref = rest[n_side]
    cast_out = rest[n_side + 1:n_side + 1 + n_cast]
    flat_out = rest[n_side + 1 + n_cast:2 * n_side + 1]
    h_ref, acc_ref, gm_ref = rest[2 * n_side + 1:]
    k = pl.program_id(1)

    @pl.when(k == 0)
    def _():
        _modulated_norm(h_ref, x_ref, gpre_ref, sc_ref, sh_ref, gm_ref)
        acc_ref[...] = jnp.zeros_like(acc_ref)

    u = _mm(h_ref[...], wup_ref[...])
    a = jnp.square(jnp.maximum(u, 0.0)).astype(BF16)
    acc_ref[...] += _mm(a, wdn_ref[...])

    for src, dst in zip(cast_in, cast_out):
        dst[...] = src[...].astype(BF16)
    for src, dst in zip(flat_in, flat_out):
        for h in range(src.shape[1]):
            dst[:, h * HEAD_DIM:(h + 1) * HEAD_DIM] = src[:, h, :]

    @pl.when(k == pl.num_programs(1) - 1)
    def _():
        _gated_norm_residual(out_ref, x_ref, acc_ref, gpost_ref, gt_ref, gm_ref)


def _mlp(x, g_pre, g_post, mod_l, w_up, w_down, rows_per_batch, first_row, cast=None,
         flatten=None):
    m = x.shape[0]
    row = _row_fn(TM_MLP, rows_per_batch, first_row)
    full = pl.BlockSpec((TM_MLP, D_MODEL), lambda i, k: (i, 0))
    vec = pl.BlockSpec((1, D_MODEL), lambda i, k: (0, 0))
    n_k = D_FF // TF_MLP
    n_steps = (m // TM_MLP) * n_k
    cast_in_specs, cast_out_specs, cast_shapes, cast_args = [], [], [], []
    if cast is not None:
        stacks, layer = cast
        for w in stacks:
            rows, cols = w.shape[1], w.shape[2]
            rb = rows // n_steps
            cast_in_specs.append(
                pl.BlockSpec((None, rb, cols), lambda i, k: (layer, i * n_k + k, 0)))
            cast_out_specs.append(pl.BlockSpec((rb, cols), lambda i, k: (i * n_k + k, 0)))
            cast_shapes.append(jax.ShapeDtypeStruct((rows, cols), BF16))
            cast_args.append(w)
    flat_in_specs, flat_out_specs, flat_shapes = [], [], []
    flatten = flatten or []
    if flatten:
        tok = DEC_BATCH * DEPTH * PAST_LEN // n_steps
        per = PAST_LEN // tok

        def where(i, k):
            s = i * n_k + k
            return s // (DEPTH * per), (s // per) % DEPTH, s % per

        for a in flatten:
            nh = a.shape[3]
            flat_in_specs.append(pl.BlockSpec((None, None, tok, nh, HEAD_DIM),
                                              lambda i, k: (*where(i, k), 0, 0)))
            flat_out_specs.append(pl.BlockSpec((None, None, tok, nh * HEAD_DIM),
                                               lambda i, k: (*where(i, k), 0)))
            flat_shapes.append(
                jax.ShapeDtypeStruct((DEC_BATCH, DEPTH, PAST_LEN, nh * HEAD_DIM), a.dtype))
    n_cast = len(cast_args)
    res = pl.pallas_call(
        functools.partial(_mlp_kernel, n_cast=n_cast, n_flat=len(flatten)),
        grid=(m // TM_MLP, n_k),
        in_specs=[
            full, vec, _mod_spec(4, row), _mod_spec(3, row),
            pl.BlockSpec((D_MODEL, TF_MLP), lambda i, k: (0, k)),
            pl.BlockSpec((TF_MLP, D_MODEL), lambda i, k: (k, 0)),
            vec, _mod_spec(5, row),
        ] + cast_in_specs + flat_in_specs,
        out_specs=[full] + cast_out_specs + flat_out_specs,
        out_shape=[jax.ShapeDtypeStruct((m, D_MODEL), F32)] + cast_shapes + flat_shapes,
        scratch_shapes=[pltpu.VMEM((TM_MLP, D_MODEL), BF16), pltpu.VMEM((TM_MLP, D_MODEL), F32),
                        pltpu.VMEM((1, D_MODEL), F32)],
        compiler_params=_params(("parallel", "arbitrary")),
        name="mlp",
    )(x, g_pre, mod_l, mod_l, w_up, w_down, g_post, mod_l, *cast_args, *flatten)
    return res[0], res[1:1 + n_cast], res[1 + n_cast:]


def _rope_tables(half, n_rep):
    t = jnp.arange(DEC_SEQ)
    inv = ROPE_THETA ** (-jnp.arange(half, dtype=F32) / half)

    def cs(pos):
        ang = pos.astype(F32)[:, None] * inv[None, :]
        c, s = jnp.cos(ang), jnp.sin(ang)
        return jnp.concatenate([c, c], axis=-1), jnp.concatenate([-s, s], axis=-1)

    cr, sr = cs(t // GRID_W)
    cc, sc = cs(t % GRID_W)
    return (jnp.concatenate([cr, cc] * n_rep, axis=-1),
            jnp.concatenate([sr, sc] * n_rep, axis=-1))


def _na_bias(rpb):
    qc = np.arange(GRID_W)[:, None]
    kc = np.arange(GRID_W)[None, :]
    ws = np.clip(qc - NA_KW // 2, 0, GRID_W - NA_KW)
    valid = (kc >= ws) & (kc < ws + NA_KW)
    dcol = np.clip(kc - qc + NA_KW - 1, 0, 2 * NA_KW - 2)
    n_dcol = 2 * NA_KW - 1
    onehot = (dcol.reshape(-1)[None, :] == np.arange(n_dcol)[:, None]).astype(np.float32)
    t = jnp.einsum("lhdc,cq->lhdq", rpb.astype(F32), jnp.asarray(onehot),
                   precision=lax.Precision.HIGHEST)
    t = t.reshape(DEPTH, NA_HEADS, 2 * NA_KH - 1, GRID_W, GRID_W) * LOG2E
    t = jnp.where(jnp.asarray(valid)[None, None, None], t, NEG_BIG)
    masked = jnp.full((DEPTH, NA_HEADS, 1, GRID_W, GRID_W), NEG_BIG, F32)
    lo, hi = NA_KH // 2 - 1, NA_KH + NA_KH // 2 - 2
    both = jnp.concatenate([t[:, :, :-1], t[:, :, 1:]], axis=-1)
    second = jnp.concatenate([masked, t[:, :, lo:lo + 1]], axis=-1)
    first = jnp.concatenate([t[:, :, hi:hi + 1], masked], axis=-1)
    none = jnp.concatenate([masked, masked], axis=-1)
    return jnp.concatenate([both, second, first, none], axis=2)


def _na_pair_index(r, key_row):
    r0 = min(max(r - NA_KH // 2, 0), GRID_ROWS - NA_KH)
    in_a = r0 <= key_row < r0 + NA_KH
    in_b = r0 <= key_row + 1 < r0 + NA_KH
    d = key_row - r + NA_KH - 1
    if in_a and in_b:
        return d
    if in_b:
        assert d + 1 == NA_KH // 2 - 1
        return NA_PAIRS_BOTH
    if in_a:
        assert d == NA_KH + NA_KH // 2 - 2
        return NA_PAIRS_BOTH + 1
    return NA_PAIRS_BOTH + 2


def kernel(x_prompt, x_sample, c, cache_na_k, cache_na_v, cache_gqa_k, cache_gqa_v,
           cache_diff_k, cache_diff_v, c_ctx, w_ada, b_ada, norm_g, w_in, w_out, na_rpb,
           gqa_q_g, gqa_k_g, diff_lam, diff_g, w_up, w_down):
    np_rows = BATCH * SEQ
    ns_rows = DEC_BATCH * DEC_SEQ
    xp = x_prompt.reshape(np_rows, D_MODEL)
    xs = x_sample.reshape(ns_rows, D_MODEL)

    cv = jnp.concatenate(
        [c_ctx[None, :], c, jnp.zeros((MOD_ROWS - 1 - DEC_BATCH, D_MODEL), F32)], axis=0)
    mod = _modulation(cv, w_ada, b_ada).reshape(DEPTH, MOD_ROWS, N_MOD, 1, D_MODEL)

    w_stacks = [w_in, w_out, w_up, w_down]
    w_in_b, w_out_b, w_up_b, w_down_b = [w[0].astype(BF16) for w in w_stacks]

    caches = [cache_na_k, cache_na_v, cache_gqa_k, cache_gqa_v, cache_diff_k, cache_diff_v]

    cos_g, sin_g = _rope_tables(HEAD_DIM // 4, 1)
    cos_d, sin_d = _rope_tables(DIFF_QK_DIM // 4, 2)
    na_bias = _na_bias(na_rpb)

    new_kv = None
    for l in range(DEPTH):
        lambda_init = 0.8 - 0.6 * math.exp(-0.3 * l)
        mod_l = mod[l]
        g = norm_g[l].reshape(4, 1, D_MODEL)
        q_g = gqa_q_g[l].reshape(1, HEAD_DIM)
        k_g = gqa_k_g[l].reshape(1, HEAD_DIM)
        d_g = diff_g[l].reshape(1, HEAD_DIM)
        lam_vec = diff_lam[l]

        qkv_p = _inproj(xp, g[0], mod_l, w_in_b, None, 0)
        o_p, *new_kv = _ctx_attn(qkv_p, q_g, k_g, lam_vec, d_g, new_kv, l, lambda_init)
        xp = _outproj([o_p], w_out_b, xp, g[1], mod_l, None, 0)
        xp, _, flat = _mlp(xp, g[2], g[3], mod_l, w_up_b, w_down_b, None, 0,
                           flatten=caches if l == 0 else None)
        if l == 0:
            c_na_k, c_na_v, c_g_k, c_g_v, c_d_k, c_d_v = flat

        qkv_s = _inproj(xs, g[0], mod_l, w_in_b, DEC_SEQ, 1)
        o_na = _lat_na(qkv_s, c_na_k, c_na_v, na_bias, l)
        o_gqa = _lat_gqa(qkv_s, c_g_k, c_g_v, cos_g, sin_g, q_g, k_g, l)
        o_diff = _lat_diff(qkv_s, c_d_k, c_d_v, cos_d, sin_d, lam_vec, d_g, l, lambda_init)
        xs = _outproj([o_na, o_gqa, o_diff], w_out_b, xs, g[1], mod_l, DEC_SEQ, 1)
        cast = (w_stacks, l + 1) if l + 1 < DEPTH else None
        xs, nxt, _ = _mlp(xs, g[2], g[3], mod_l, w_up_b, w_down_b, DEC_SEQ, 1, cast)
        if nxt:
            w_in_b, w_out_b, w_up_b, w_down_b = nxt

    return (xp.reshape(BATCH, SEQ, D_MODEL), xs.reshape(DEC_BATCH, DEC_SEQ, D_MODEL), *new_kv)
```

```python
import functools
import math

import jax
import jax.numpy as jnp
import numpy as np
from jax import lax
from jax.experimental import pallas as pl
from jax.experimental.pallas import tpu as pltpu

D_MODEL = 2048
BATCH = 16
SEQ = 256
DEPTH = 4
DEC_BATCH = 8
DEC_SEQ = 1024
PAST_LEN = 256
GRID_W = 64
GRID_ROWS = DEC_SEQ // GRID_W
HEAD_DIM = 128
NA_HEADS = 4
GQA_Q_HEADS = 8
GQA_KV_HEADS = 2
GQA_GROUP = GQA_Q_HEADS // GQA_KV_HEADS
DIFF_HEADS = 4
DIFF_QK_DIM = HEAD_DIM // 2
NA_KH = 8
NA_KW = 16
D_FF = 4 * D_MODEL
ROPE_THETA = 10000.0
EPS = 1e-6
N_MOD = 6
NEG_BIG = -1e30

NA_W = NA_HEADS * HEAD_DIM
GQA_QW = GQA_Q_HEADS * HEAD_DIM
GQA_KVW = GQA_KV_HEADS * HEAD_DIM
DIFF_W = DIFF_HEADS * HEAD_DIM
D_IN = 3 * NA_W + GQA_QW + 2 * GQA_KVW + 3 * DIFF_W
D_MIX = NA_W + GQA_QW + DIFF_W
C_NA_Q, C_NA_K, C_NA_V = 0, NA_W, 2 * NA_W
C_G_Q = 3 * NA_W
C_G_K = C_G_Q + GQA_QW
C_G_V = C_G_K + GQA_KVW
C_D_Q = C_G_V + GQA_KVW
C_D_K = C_D_Q + DIFF_W
C_D_V = C_D_K + DIFF_W

KV_HEADS = (NA_HEADS, NA_HEADS, GQA_KV_HEADS, GQA_KV_HEADS, DIFF_HEADS, DIFF_HEADS)
N_KEYS = DEC_SEQ + PAST_LEN
LOG2E = 1.4426950408889634
QSCALE = HEAD_DIM ** -0.5 * LOG2E
DIFF_QSCALE = DIFF_QK_DIM ** -0.5 * LOG2E

NA_QROWS = 4
NA_QTOK = NA_QROWS * GRID_W
NA_GROUPS = GRID_ROWS // NA_QROWS
NA_SLAB_ROWS = 12
NA_SLAB = NA_SLAB_ROWS * GRID_W
NA_NKEY = NA_SLAB + PAST_LEN
NA_SLAB_START = (0, 0, 4, 4)
NA_PAIRS_BOTH = 2 * NA_KH - 2
MOD_ROWS = 16

F32 = jnp.float32
BF16 = jnp.bfloat16

VMEM_LIMIT = 52 * 1024 * 1024

TM_IN = 1024
TN_IN = 1536
TM_OUT = 512
TM_MLP = 512
TF_MLP = 1024
TN_ADA = 1024
TQ = 1024
ATT_ROWS = 256
NORM_ROWS = 128

def _params(sem):
    return pltpu.CompilerParams(dimension_semantics=sem, vmem_limit_bytes=VMEM_LIMIT)


def _rms(x, g):
    ms = jnp.mean(x * x, axis=-1, keepdims=True)
    return x * lax.rsqrt(ms + EPS) * g


def _nt(a, b):
    return lax.dot_general(a, b, (((1,), (1,)), ((), ())), preferred_element_type=F32)


def _mm(a, b):
    return jnp.dot(a, b, preferred_element_type=F32)


def _scores_exp(q, k, bias=None):
    s = _nt(q, k)
    if bias is not None:
        s = s + bias
    return jnp.exp2(s - jnp.max(s, axis=-1, keepdims=True)).astype(BF16)


def _weighted(e, v_ones):
    oa = _mm(e, v_ones)
    d = v_ones.shape[1] // 2
    return oa[:, :d] / oa[:, d:]


def _with_ones(v):
    return jnp.concatenate([v, jnp.ones_like(v)], axis=1)


def _rope(x, cos, sin_signed, half):
    n = x.shape[-1]
    lane = lax.broadcasted_iota(jnp.int32, x.shape, 1)
    first = (lane % (2 * half)) < half
    rot = jnp.where(first, pltpu.roll(x, n - half, 1), pltpu.roll(x, half, 1))
    return x * cos + rot * sin_signed


def _diff_lambda(lam_ref, lambda_init):
    lf = lam_ref[...]
    a = jnp.sum(lf[0:1] * lf[1:2], axis=-1, keepdims=True)
    b = jnp.sum(lf[2:3] * lf[3:4], axis=-1, keepdims=True)
    return jnp.exp(a) - jnp.exp(b) + lambda_init


def _mod_kernel(cv_ref, w_ref, b_ref, o_ref):
    cv = cv_ref[...]
    s = cv / (1.0 + jnp.exp(-cv))
    o_ref[...] = _mm(s.astype(BF16), w_ref[...].astype(BF16)) + b_ref[...]


def _modulation(cv, w_ada, b_ada):
    n = N_MOD * D_MODEL
    return pl.pallas_call(
        _mod_kernel,
        grid=(DEPTH, n // TN_ADA),
        in_specs=[
            pl.BlockSpec((MOD_ROWS, D_MODEL), lambda l, j: (0, 0)),
            pl.BlockSpec((None, D_MODEL, TN_ADA), lambda l, j: (l, 0, j)),
            pl.BlockSpec((None, 1, TN_ADA), lambda l, j: (l, 0, j)),
        ],
        out_specs=pl.BlockSpec((None, MOD_ROWS, TN_ADA), lambda l, j: (l, 0, j)),
        out_shape=jax.ShapeDtypeStruct((DEPTH, MOD_ROWS, n), F32),
        compiler_params=_params(("parallel", "parallel")),
        name="modulation",
    )(cv, w_ada, b_ada.reshape(DEPTH, 1, n))


def _mod_spec(chunk, row_fn):
    return pl.BlockSpec((None, None, 1, D_MODEL), lambda i, *_: (row_fn(i), chunk, 0, 0))


def _row_fn(tm, rows_per_batch, first_row):
    if rows_per_batch is None:
        return lambda i: first_row
    return lambda i: first_row + (i * tm) // rows_per_batch


def _row_chunks(n_rows, body):
    def step(c, carry):
        body(pl.ds(pl.multiple_of(c * NORM_ROWS, NORM_ROWS), NORM_ROWS))
        return carry

    lax.fori_loop(0, n_rows // NORM_ROWS, step, 0)


def _modulated_norm(h_ref, x_ref, g_ref, sc_ref, sh_ref, gm_ref):
    gm_ref[...] = g_ref[...] * (1.0 + sc_ref[...])

    def body(rows):
        x = x_ref[rows, :]
        r = lax.rsqrt(jnp.mean(x * x, axis=-1, keepdims=True) + EPS)
        h_ref[rows, :] = (x * r * gm_ref[...] + sh_ref[...]).astype(BF16)

    _row_chunks(x_ref.shape[0], body)


def _gated_norm_residual(out_ref, x_ref, y_ref, g_ref, gt_ref, gm_ref):
    gm_ref[...] = gt_ref[...] * g_ref[...]

    def body(rows):
        y = y_ref[rows, :]
        r = lax.rsqrt(jnp.mean(y * y, axis=-1, keepdims=True) + EPS)
        out_ref[rows, :] = x_ref[rows, :] + y * r * gm_ref[...]

    _row_chunks(x_ref.shape[0], body)


def _inproj_kernel(x_ref, g_ref, sc_ref, sh_ref, w_ref, o_ref, h_ref, gm_ref):
    @pl.when(pl.program_id(1) == 0)
    def _():
        _modulated_norm(h_ref, x_ref, g_ref, sc_ref, sh_ref, gm_ref)

    o_ref[...] = _mm(h_ref[...], w_ref[...])


def _inproj(x, g_pre, mod_l, w_in, rows_per_batch, first_row):
    m = x.shape[0]
    row = _row_fn(TM_IN, rows_per_batch, first_row)
    return pl.pallas_call(
        _inproj_kernel,
        grid=(m // TM_IN, D_IN // TN_IN),
        in_specs=[
            pl.BlockSpec((TM_IN, D_MODEL), lambda i, j: (i, 0)),
            pl.BlockSpec((1, D_MODEL), lambda i, j: (0, 0)),
            _mod_spec(1, row),
            _mod_spec(0, row),
            pl.BlockSpec((D_MODEL, TN_IN), lambda i, j: (0, j)),
        ],
        out_specs=pl.BlockSpec((TM_IN, TN_IN), lambda i, j: (i, j)),
        out_shape=jax.ShapeDtypeStruct((m, D_IN), F32),
        scratch_shapes=[pltpu.VMEM((TM_IN, D_MODEL), BF16), pltpu.VMEM((1, D_MODEL), F32)],
        compiler_params=_params(("parallel", "arbitrary")),
        name="inproj",
    )(x, g_pre, mod_l, mod_l, w_in)


def _ctx_attn_kernel(qkv_ref, qg_ref, kg_ref, lam_ref, dg_ref, *rest, lambda_init, first):
    if first:
        o_ref, *kv_refs = rest
        for ref in kv_refs:
            ref[1:] = jnp.zeros((DEPTH - 1,) + ref.shape[1:], F32)
        nak_ref, nav_ref, gk_ref, gv_ref, dk_ref, dv_ref = [ref.at[0] for ref in kv_refs]
    else:
        o_ref, nak_ref, nav_ref, gk_ref, gv_ref, dk_ref, dv_ref = rest[len(KV_HEADS):]
    hd = HEAD_DIM

    def cols(c0, h):
        return qkv_ref[:, c0 + h * hd:c0 + (h + 1) * hd]

    for ref, c0 in ((nak_ref, C_NA_K), (nav_ref, C_NA_V), (gv_ref, C_G_V),
                    (dk_ref, C_D_K), (dv_ref, C_D_V)):
        for h in range(ref.shape[1]):
            ref[:, h, :] = cols(c0, h)

    for h in range(NA_HEADS):
        q = (cols(C_NA_Q, h) * QSCALE).astype(BF16)
        k = cols(C_NA_K, h).astype(BF16)
        v1 = _with_ones(cols(C_NA_V, h).astype(BF16))
        o_ref[:, h * hd:(h + 1) * hd] = _weighted(_scores_exp(q, k), v1).astype(BF16)

    for kv in range(GQA_KV_HEADS):
        kf = _rms(cols(C_G_K, kv), kg_ref[...])
        gk_ref[:, kv, :] = kf
        k = kf.astype(BF16)
        v1 = _with_ones(cols(C_G_V, kv).astype(BF16))
        for g in range(GQA_GROUP):
            hq = kv * GQA_GROUP + g
            q = (_rms(cols(C_G_Q, hq), qg_ref[...]) * QSCALE).astype(BF16)
            o = _weighted(_scores_exp(q, k), v1)
            o_ref[:, NA_W + hq * hd:NA_W + (hq + 1) * hd] = o.astype(BF16)

    lam = _diff_lambda(lam_ref, lambda_init)
    lane = lax.broadcasted_iota(jnp.int32, (SEQ, hd), 1)
    for h in range(DIFF_HEADS):
        qf = cols(C_D_Q, h) * DIFF_QSCALE
        k = cols(C_D_K, h).astype(BF16)
        v1 = _with_ones(cols(C_D_V, h).astype(BF16))
        q0 = jnp.where(lane < DIFF_QK_DIM, qf, 0.0).astype(BF16)
        q1 = jnp.where(lane >= DIFF_QK_DIM, qf, 0.0).astype(BF16)
        od = _weighted(_scores_exp(q0, k), v1) - lam * _weighted(_scores_exp(q1, k), v1)
        od = _rms(od, dg_ref[...]) * (1.0 - lambda_init)
        c0 = NA_W + GQA_QW + h * hd
        o_ref[:, c0:c0 + hd] = od.astype(BF16)


def _ctx_attn(qkv, q_g, k_g, lam_vec, diff_g, kv_bufs, layer, lambda_init):
    n = qkv.shape[0]
    vec = pl.BlockSpec((1, HEAD_DIM), lambda b: (0, 0))
    n_in = 5
    first = kv_bufs is None

    def rows(w):
        return pl.BlockSpec((SEQ, w), lambda b: (b, 0))

    def heads(nh):
        if first:
            return pl.BlockSpec((None, DEPTH, SEQ, nh, HEAD_DIM), lambda b: (b, 0, 0, 0, 0))
        return pl.BlockSpec((None, None, SEQ, nh, HEAD_DIM), lambda b: (b, layer, 0, 0, 0))

    bufs = [] if first else list(kv_bufs)
    return pl.pallas_call(
        functools.partial(_ctx_attn_kernel, lambda_init=lambda_init, first=first),
        grid=(n // SEQ,),
        in_specs=[rows(D_IN), vec, vec,
                  pl.BlockSpec((4, DIFF_QK_DIM), lambda b: (0, 0)), vec]
                 + [pl.BlockSpec(memory_space=pl.ANY)] * len(bufs),
        out_specs=[rows(D_MIX)] + [heads(nh) for nh in KV_HEADS],
        out_shape=[jax.ShapeDtypeStruct((n, D_MIX), BF16)]
                  + [jax.ShapeDtypeStruct((BATCH, DEPTH, SEQ, nh, HEAD_DIM), F32)
                     for nh in KV_HEADS],
        input_output_aliases={n_in + i: 1 + i for i in range(len(bufs))},
        compiler_params=_params(("parallel",)),
        name="ctx_attn",
    )(qkv, q_g, k_g, lam_vec, diff_g, *bufs)


def _lat_na_kernel(q_ref, k_ref, v_ref, kc_ref, vc_ref, bias_ref, o_ref, kbig_ref, vbig_ref):
    lat0, lat1 = PAST_LEN, PAST_LEN + DEC_SEQ
    kc = kc_ref[...].astype(BF16)
    vc = _with_ones(vc_ref[...].astype(BF16))
    kbig_ref[0:lat0, :] = kc
    kbig_ref[lat0:lat1, :] = k_ref[...].astype(BF16)
    kbig_ref[lat1:, :] = kc
    vbig_ref[0:lat0, :] = vc
    vbig_ref[lat0:lat1, :] = _with_ones(v_ref[...].astype(BF16))
    vbig_ref[lat1:, :] = vc
    for j in range(NA_GROUPS):
        s0 = NA_SLAB_START[j]
        w0 = 0 if s0 == 0 else PAST_LEN + s0 * GRID_W
        q = (q_ref[j * NA_QTOK:(j + 1) * NA_QTOK, :] * QSCALE).astype(BF16)
        ctx0 = jnp.zeros((GRID_W, PAST_LEN), F32)
        bias_rows = []
        for r in range(j * NA_QROWS, (j + 1) * NA_QROWS):
            slab = [bias_ref[_na_pair_index(r, kr)] for kr in range(s0, s0 + NA_SLAB_ROWS, 2)]
            bias_rows.append(jnp.concatenate([ctx0] + slab if s0 == 0 else slab + [ctx0], axis=1))
        bias = jnp.concatenate(bias_rows, axis=0)
        e = _scores_exp(q, kbig_ref[w0:w0 + NA_NKEY, :], bias)
        o = _weighted(e, vbig_ref[w0:w0 + NA_NKEY, :])
        o_ref[j * NA_QTOK:(j + 1) * NA_QTOK, :] = o.astype(BF16)


def _lat_na(qkv, cache_k, cache_v, bias, layer):
    hd = HEAD_DIM
    cq, ck, cv = C_NA_Q // hd, C_NA_K // hd, C_NA_V // hd
    cache = pl.BlockSpec((None, None, PAST_LEN, hd), lambda h, b: (b, layer, 0, h))
    nbig = DEC_SEQ + 2 * PAST_LEN
    return pl.pallas_call(
        _lat_na_kernel,
        grid=(NA_HEADS, DEC_BATCH),
        in_specs=[
            pl.BlockSpec((DEC_SEQ, hd), lambda h, b: (b, cq + h)),
            pl.BlockSpec((DEC_SEQ, hd), lambda h, b: (b, ck + h)),
            pl.BlockSpec((DEC_SEQ, hd), lambda h, b: (b, cv + h)),
            cache, cache,
            pl.BlockSpec((None, None, NA_PAIRS_BOTH + 3, GRID_W, 2 * GRID_W),
                         lambda h, b: (layer, h, 0, 0, 0)),
        ],
        out_specs=pl.BlockSpec((DEC_SEQ, hd), lambda h, b: (b, h)),
        out_shape=jax.ShapeDtypeStruct((DEC_BATCH * DEC_SEQ, NA_W), BF16),
        scratch_shapes=[pltpu.VMEM((nbig, hd), BF16), pltpu.VMEM((nbig, 2 * hd), BF16)],
        compiler_params=_params(("parallel", "parallel")),
        name="lat_na",
    )(qkv, qkv, qkv, cache_k, cache_v, bias)


def _fill_keys(kall_ref, vall_ref, k_lat, v_ref, kc_ref, vc_ref):
    kall_ref[0:DEC_SEQ, :] = k_lat.astype(BF16)
    kall_ref[DEC_SEQ:N_KEYS, :] = kc_ref[...].astype(BF16)
    vall_ref[0:DEC_SEQ, :] = _with_ones(v_ref[...].astype(BF16))
    vall_ref[DEC_SEQ:N_KEYS, :] = _with_ones(vc_ref[...].astype(BF16))


def _lat_gqa_kernel(q_ref, k_ref, v_ref, kc_ref, vc_ref, cos_ref, sin_ref, qg_ref, kg_ref,
                    o_ref, kall_ref, vall_ref):
    hd = HEAD_DIM
    half = HEAD_DIM // 4
    qb = pl.program_id(2)

    @pl.when(qb == 0)
    def _():
        kf = _rope(_rms(k_ref[...], kg_ref[...]), cos_ref[...], sin_ref[...], half)
        _fill_keys(kall_ref, vall_ref, kf, v_ref, kc_ref, vc_ref)

    for s in range(TQ // ATT_ROWS):
        rows = slice(s * ATT_ROWS, (s + 1) * ATT_ROWS)
        row0 = pl.multiple_of(qb * TQ + s * ATT_ROWS, ATT_ROWS)
        cos = cos_ref[pl.ds(row0, ATT_ROWS), :]
        sin = sin_ref[pl.ds(row0, ATT_ROWS), :]
        for g in range(GQA_GROUP):
            qf = _rms(q_ref[rows, g * hd:(g + 1) * hd], qg_ref[...])
            q = (_rope(qf, cos, sin, half) * QSCALE).astype(BF16)
            o = _weighted(_scores_exp(q, kall_ref[...]), vall_ref[...])
            o_ref[rows, g * hd:(g + 1) * hd] = o.astype(BF16)


def _lat_gqa(qkv, cache_k, cache_v, cos, sin, q_g, k_g, layer):
    hd = HEAD_DIM
    gw = GQA_GROUP * hd
    nqb = DEC_SEQ // TQ
    cq, ck, cv = C_G_Q // gw, C_G_K // hd, C_G_V // hd
    cache = pl.BlockSpec((None, None, PAST_LEN, hd), lambda b, kv, qb: (b, layer, 0, kv))
    table = pl.BlockSpec((DEC_SEQ, hd), lambda b, kv, qb: (0, 0))
    vec = pl.BlockSpec((1, hd), lambda b, kv, qb: (0, 0))
    return pl.pallas_call(
        _lat_gqa_kernel,
        grid=(DEC_BATCH, GQA_KV_HEADS, nqb),
        in_specs=[
            pl.BlockSpec((TQ, gw), lambda b, kv, qb: (b * nqb + qb, cq + kv)),
            pl.BlockSpec((DEC_SEQ, hd), lambda b, kv, qb: (b, ck + kv)),
            pl.BlockSpec((DEC_SEQ, hd), lambda b, kv, qb: (b, cv + kv)),
            cache, cache, table, table, vec, vec,
        ],
        out_specs=pl.BlockSpec((TQ, gw), lambda b, kv, qb: (b * nqb + qb, kv)),
        out_shape=jax.ShapeDtypeStruct((DEC_BATCH * DEC_SEQ, GQA_QW), BF16),
        scratch_shapes=[pltpu.VMEM((N_KEYS, hd), BF16), pltpu.VMEM((N_KEYS, 2 * hd), BF16)],
        compiler_params=_params(("parallel", "parallel", "arbitrary")),
        name="lat_gqa",
    )(qkv, qkv, qkv, cache_k, cache_v, cos, sin, q_g, k_g)


def _lat_diff_kernel(q_ref, k_ref, v_ref, kc_ref, vc_ref, cos_ref, sin_ref, lam_ref, dg_ref,
                     o_ref, kall_ref, vall_ref, *, lambda_init):
    half = DIFF_QK_DIM // 4
    qb = pl.program_id(2)

    @pl.when(qb == 0)
    def _():
        kf = _rope(k_ref[...], cos_ref[...], sin_ref[...], half)
        _fill_keys(kall_ref, vall_ref, kf, v_ref, kc_ref, vc_ref)

    lam = _diff_lambda(lam_ref, lambda_init)
    lane = lax.broadcasted_iota(jnp.int32, (ATT_ROWS, HEAD_DIM), 1)
    for s in range(TQ // ATT_ROWS):
        rows = slice(s * ATT_ROWS, (s + 1) * ATT_ROWS)
        row0 = pl.multiple_of(qb * TQ + s * ATT_ROWS, ATT_ROWS)
        cos = cos_ref[pl.ds(row0, ATT_ROWS), :]
        sin = sin_ref[pl.ds(row0, ATT_ROWS), :]
        qf = _rope(q_ref[rows, :], cos, sin, half) * DIFF_QSCALE
        q0 = jnp.where(lane < DIFF_QK_DIM, qf, 0.0).astype(BF16)
        q1 = jnp.where(lane >= DIFF_QK_DIM, qf, 0.0).astype(BF16)
        od = (_weighted(_scores_exp(q0, kall_ref[...]), vall_ref[...])
              - lam * _weighted(_scores_exp(q1, kall_ref[...]), vall_ref[...]))
        od = _rms(od, dg_ref[...]) * (1.0 - lambda_init)
        o_ref[rows, :] = od.astype(BF16)


def _lat_diff(qkv, cache_k, cache_v, cos, sin, lam_vec, diff_g, layer, lambda_init):
    hd = HEAD_DIM
    nqb = DEC_SEQ // TQ
    cq, ck, cv = C_D_Q // hd, C_D_K // hd, C_D_V // hd
    cache = pl.BlockSpec((None, None, PAST_LEN, hd), lambda b, h, qb: (b, layer, 0, h))
    table = pl.BlockSpec((DEC_SEQ, hd), lambda b, h, qb: (0, 0))
    return pl.pallas_call(
        functools.partial(_lat_diff_kernel, lambda_init=lambda_init),
        grid=(DEC_BATCH, DIFF_HEADS, nqb),
        in_specs=[
            pl.BlockSpec((TQ, hd), lambda b, h, qb: (b * nqb + qb, cq + h)),
            pl.BlockSpec((DEC_SEQ, hd), lambda b, h, qb: (b, ck + h)),
            pl.BlockSpec((DEC_SEQ, hd), lambda b, h, qb: (b, cv + h)),
            cache, cache, table, table,
            pl.BlockSpec((4, DIFF_QK_DIM), lambda b, h, qb: (0, 0)),
            pl.BlockSpec((1, hd), lambda b, h, qb: (0, 0)),
        ],
        out_specs=pl.BlockSpec((TQ, hd), lambda b, h, qb: (b * nqb + qb, h)),
        out_shape=jax.ShapeDtypeStruct((DEC_BATCH * DEC_SEQ, DIFF_W), BF16),
        scratch_shapes=[pltpu.VMEM((N_KEYS, hd), BF16), pltpu.VMEM((N_KEYS, 2 * hd), BF16)],
        compiler_params=_params(("parallel", "parallel", "arbitrary")),
        name="lat_diff",
    )(qkv, qkv, qkv, cache_k, cache_v, cos, sin, lam_vec, diff_g)


def _outproj_kernel(*refs, n_o):
    o_refs = refs[:n_o]
    w_ref, x_ref, g_ref, gt_ref, out_ref = refs[n_o:]
    y = None
    off = 0
    for o_ref in o_refs:
        wd = o_ref.shape[1]
        part = _mm(o_ref[...], w_ref[off:off + wd, :])
        y = part if y is None else y + part
        off += wd
    out_ref[...] = x_ref[...] + gt_ref[...] * _rms(y, g_ref[...])


def _outproj(o_parts, w_out, x, g_post, mod_l, rows_per_batch, first_row):
    m = x.shape[0]
    row = _row_fn(TM_OUT, rows_per_batch, first_row)
    full = pl.BlockSpec((TM_OUT, D_MODEL), lambda i: (i, 0))
    return pl.pallas_call(
        functools.partial(_outproj_kernel, n_o=len(o_parts)),
        grid=(m // TM_OUT,),
        in_specs=[pl.BlockSpec((TM_OUT, o.shape[1]), lambda i: (i, 0)) for o in o_parts] + [
            pl.BlockSpec((D_MIX, D_MODEL), lambda i: (0, 0)),
            full,
            pl.BlockSpec((1, D_MODEL), lambda i: (0, 0)),
            _mod_spec(2, row),
        ],
        out_specs=full,
        out_shape=jax.ShapeDtypeStruct((m, D_MODEL), F32),
        compiler_params=_params(("parallel",)),
        name="outproj",
    )(*o_parts, w_out, x, g_post, mod_l)


def _mlp_kernel(x_ref, gpre_ref, sc_ref, sh_ref, wup_ref, wdn_ref, gpost_ref, gt_ref, *rest,
                n_cast, n_flat):
    n_side = n_cast + n_flat
    cast_in, flat_in = rest[:n_cast], rest[n_cast:n_side]
    out_ref = rest[n_side]
    cast_out = rest[n_side + 1:n_side + 1 + n_cast]
    flat_out = rest[n_side + 1 + n_cast:2 * n_side + 1]
    h_ref, acc_ref, gm_ref = rest[2 * n_side + 1:]
    k = pl.program_id(1)

    @pl.when(k == 0)
    def _():
        _modulated_norm(h_ref, x_ref, gpre_ref, sc_ref, sh_ref, gm_ref)
        acc_ref[...] = jnp.zeros_like(acc_ref)

    u = _mm(h_ref[...], wup_ref[...])
    a = jnp.square(jnp.maximum(u, 0.0)).astype(BF16)
    acc_ref[...] += _mm(a, wdn_ref[...])

    for src, dst in zip(cast_in, cast_out):
        dst[...] = src[...].astype(BF16)
    for src, dst in zip(flat_in, flat_out):
        for h in range(src.shape[1]):
            dst[:, h * HEAD_DIM:(h + 1) * HEAD_DIM] = src[:, h, :]

    @pl.when(k == pl.num_programs(1) - 1)
    def _():
        _gated_norm_residual(out_ref, x_ref, acc_ref, gpost_ref, gt_ref, gm_ref)


def _mlp(x, g_pre, g_post, mod_l, w_up, w_down, rows_per_batch, first_row, cast=None,
         flatten=None):
    m = x.shape[0]
    row = _row_fn(TM_MLP, rows_per_batch, first_row)
    full = pl.BlockSpec((TM_MLP, D_MODEL), lambda i, k: (i, 0))
    vec = pl.BlockSpec((1, D_MODEL), lambda i, k: (0, 0))
    n_k = D_FF // TF_MLP
    n_steps = (m // TM_MLP) * n_k
    cast_in_specs, cast_out_specs, cast_shapes, cast_args = [], [], [], []
    if cast is not None:
        stacks, layer = cast
        for w in stacks:
            rows, cols = w.shape[1], w.shape[2]
            rb = rows // n_steps
            cast_in_specs.append(
                pl.BlockSpec((None, rb, cols), lambda i, k: (layer, i * n_k + k, 0)))
            cast_out_specs.append(pl.BlockSpec((rb, cols), lambda i, k: (i * n_k + k, 0)))
            cast_shapes.append(jax.ShapeDtypeStruct((rows, cols), BF16))
            cast_args.append(w)
    flat_in_specs, flat_out_specs, flat_shapes = [], [], []
    flatten = flatten or []
    if flatten:
        tok = DEC_BATCH * DEPTH * PAST_LEN // n_steps
        per = PAST_LEN // tok

        def where(i, k):
            s = i * n_k + k
            return s // (DEPTH * per), (s // per) % DEPTH, s % per

        for a in flatten:
            nh = a.shape[3]
            flat_in_specs.append(pl.BlockSpec((None, None, tok, nh, HEAD_DIM),
                                              lambda i, k: (*where(i, k), 0, 0)))
            flat_out_specs.append(pl.BlockSpec((None, None, tok, nh * HEAD_DIM),
                                               lambda i, k: (*where(i, k), 0)))
            flat_shapes.append(
                jax.ShapeDtypeStruct((DEC_BATCH, DEPTH, PAST_LEN, nh * HEAD_DIM), a.dtype))
    n_cast = len(cast_args)
    res = pl.pallas_call(
        functools.partial(_mlp_kernel, n_cast=n_cast, n_flat=len(flatten)),
        grid=(m // TM_MLP, n_k),
        in_specs=[
            full, vec, _mod_spec(4, row), _mod_spec(3, row),
            pl.BlockSpec((D_MODEL, TF_MLP), lambda i, k: (0, k)),
            pl.BlockSpec((TF_MLP, D_MODEL), lambda i, k: (k, 0)),
            vec, _mod_spec(5, row),
        ] + cast_in_specs + flat_in_specs,
        out_specs=[full] + cast_out_specs + flat_out_specs,
        out_shape=[jax.ShapeDtypeStruct((m, D_MODEL), F32)] + cast_shapes + flat_shapes,
        scratch_shapes=[pltpu.VMEM((TM_MLP, D_MODEL), BF16), pltpu.VMEM((TM_MLP, D_MODEL), F32),
                        pltpu.VMEM((1, D_MODEL), F32)],
        compiler_params=_params(("parallel", "arbitrary")),
        name="mlp",
    )(x, g_pre, mod_l, mod_l, w_up, w_down, g_post, mod_l, *cast_args, *flatten)
    return res[0], res[1:1 + n_cast], res[1 + n_cast:]


def _rope_tables(half, n_rep):
    t = jnp.arange(DEC_SEQ)
    inv = ROPE_THETA ** (-jnp.arange(half, dtype=F32) / half)

    def cs(pos):
        ang = pos.astype(F32)[:, None] * inv[None, :]
        c, s = jnp.cos(ang), jnp.sin(ang)
        return jnp.concatenate([c, c], axis=-1), jnp.concatenate([-s, s], axis=-1)

    cr, sr = cs(t // GRID_W)
    cc, sc = cs(t % GRID_W)
    return (jnp.concatenate([cr, cc] * n_rep, axis=-1),
            jnp.concatenate([sr, sc] * n_rep, axis=-1))


def _na_bias(rpb):
    qc = np.arange(GRID_W)[:, None]
    kc = np.arange(GRID_W)[None, :]
    ws = np.clip(qc - NA_KW // 2, 0, GRID_W - NA_KW)
    valid = (kc >= ws) & (kc < ws + NA_KW)
    dcol = np.clip(kc - qc + NA_KW - 1, 0, 2 * NA_KW - 2)
    n_dcol = 2 * NA_KW - 1
    onehot = (dcol.reshape(-1)[None, :] == np.arange(n_dcol)[:, None]).astype(np.float32)
    t = jnp.einsum("lhdc,cq->lhdq", rpb.astype(F32), jnp.asarray(onehot),
                   precision=lax.Precision.HIGHEST)
    t = t.reshape(DEPTH, NA_HEADS, 2 * NA_KH - 1, GRID_W, GRID_W) * LOG2E
    t = jnp.where(jnp.asarray(valid)[None, None, None], t, NEG_BIG)
    masked = jnp.full((DEPTH, NA_HEADS, 1, GRID_W, GRID_W), NEG_BIG, F32)
    lo, hi = NA_KH // 2 - 1, NA_KH + NA_KH // 2 - 2
    both = jnp.concatenate([t[:, :, :-1], t[:, :, 1:]], axis=-1)
    second = jnp.concatenate([masked, t[:, :, lo:lo + 1]], axis=-1)
    first = jnp.concatenate([t[:, :, hi:hi + 1], masked], axis=-1)
    none = jnp.concatenate([masked, masked], axis=-1)
    return jnp.concatenate([both, second, first, none], axis=2)


def _na_pair_index(r, key_row):
    r0 = min(max(r - NA_KH // 2, 0), GRID_ROWS - NA_KH)
    in_a = r0 <= key_row < r0 + NA_KH
    in_b = r0 <= key_row + 1 < r0 + NA_KH
    d = key_row - r + NA_KH - 1
    if in_a and in_b:
        return d
    if in_b:
        assert d + 1 == NA_KH // 2 - 1
        return NA_PAIRS_BOTH
    if in_a:
        assert d == NA_KH + NA_KH // 2 - 2
        return NA_PAIRS_BOTH + 1
    return NA_PAIRS_BOTH + 2


def kernel(x_prompt, x_sample, c, cache_na_k, cache_na_v, cache_gqa_k, cache_gqa_v,
           cache_diff_k, cache_diff_v, c_ctx, w_ada, b_ada, norm_g, w_in, w_out, na_rpb,
           gqa_q_g, gqa_k_g, diff_lam, diff_g, w_up, w_down):
    np_rows = BATCH * SEQ
    ns_rows = DEC_BATCH * DEC_SEQ
    xp = x_prompt.reshape(np_rows, D_MODEL)
    xs = x_sample.reshape(ns_rows, D_MODEL)

    cv = jnp.concatenate(
        [c_ctx[None, :], c, jnp.zeros((MOD_ROWS - 1 - DEC_BATCH, D_MODEL), F32)], axis=0)
    mod = _modulation(cv, w_ada, b_ada).reshape(DEPTH, MOD_ROWS, N_MOD, 1, D_MODEL)

    w_stacks = [w_in, w_out, w_up, w_down]
    w_in_b, w_out_b, w_up_b, w_down_b = [w[0].astype(BF16) for w in w_stacks]

    caches = [cache_na_k, cache_na_v, cache_gqa_k, cache_gqa_v, cache_diff_k, cache_diff_v]

    cos_g, sin_g = _rope_tables(HEAD_DIM // 4, 1)
    cos_d, sin_d = _rope_tables(DIFF_QK_DIM // 4, 2)
    na_bias = _na_bias(na_rpb)

    new_kv = None
    for l in range(DEPTH):
        lambda_init = 0.8 - 0.6 * math.exp(-0.3 * l)
        mod_l = mod[l]
        g = norm_g[l].reshape(4, 1, D_MODEL)
        q_g = gqa_q_g[l].reshape(1, HEAD_DIM)
        k_g = gqa_k_g[l].reshape(1, HEAD_DIM)
        d_g = diff_g[l].reshape(1, HEAD_DIM)
        lam_vec = diff_lam[l]

        qkv_p = _inproj(xp, g[0], mod_l, w_in_b, None, 0)
        o_p, *new_kv = _ctx_attn(qkv_p, q_g, k_g, lam_vec, d_g, new_kv, l, lambda_init)
        xp = _outproj([o_p], w_out_b, xp, g[1], mod_l, None, 0)
        xp, _, flat = _mlp(xp, g[2], g[3], mod_l, w_up_b, w_down_b, None, 0,
                           flatten=caches if l == 0 else None)
        if l == 0:
            c_na_k, c_na_v, c_g_k, c_g_v, c_d_k, c_d_v = flat

        qkv_s = _inproj(xs, g[0], mod_l, w_in_b, DEC_SEQ, 1)
        o_na = _lat_na(qkv_s, c_na_k, c_na_v, na_bias, l)
        o_gqa = _lat_gqa(qkv_s, c_g_k, c_g_v, cos_g, sin_g, q_g, k_g, l)
        o_diff = _lat_diff(qkv_s, c_d_k, c_d_v, cos_d, sin_d, lam_vec, d_g, l, lambda_init)
        xs = _outproj([o_na, o_gqa, o_diff], w_out_b, xs, g[1], mod_l, DEC_SEQ, 1)
        cast = (w_stacks, l + 1) if l + 1 < DEPTH else None
        xs, nxt, _ = _mlp(xs, g[2], g[3], mod_l, w_up_b, w_down_b, DEC_SEQ, 1, cast)
        if nxt:
            w_in_b, w_out_b, w_up_b, w_down_b = nxt

    return (xp.reshape(BATCH, SEQ, D_MODEL), xs.reshape(DEC_BATCH, DEC_SEQ, D_MODEL), *new_kv)
```

```python
import functools
import math

import jax
import jax.numpy as jnp
import numpy as np
from jax import lax
from jax.experimental import pallas as pl
from jax.experimental.pallas import tpu as pltpu

D_MODEL = 2048
BATCH = 16
SEQ = 256
DEPTH = 4
DEC_BATCH = 8
DEC_SEQ = 1024
PAST_LEN = 256
GRID_W = 64
GRID_ROWS = DEC_SEQ // GRID_W
HEAD_DIM = 128
NA_HEADS = 4
GQA_Q_HEADS = 8
GQA_KV_HEADS = 2
GQA_GROUP = GQA_Q_HEADS // GQA_KV_HEADS
DIFF_HEADS = 4
DIFF_QK_DIM = HEAD_DIM // 2
NA_KH = 8
NA_KW = 16
D_FF = 4 * D_MODEL
ROPE_THETA = 10000.0
EPS = 1e-6
N_MOD = 6
NEG_BIG = -1e30

NA_W = NA_HEADS * HEAD_DIM
GQA_QW = GQA_Q_HEADS * HEAD_DIM
GQA_KVW = GQA_KV_HEADS * HEAD_DIM
DIFF_W = DIFF_HEADS * HEAD_DIM
D_IN = 3 * NA_W + GQA_QW + 2 * GQA_KVW + 3 * DIFF_W
D_MIX = NA_W + GQA_QW + DIFF_W
C_NA_Q, C_NA_K, C_NA_V = 0, NA_W, 2 * NA_W
C_G_Q = 3 * NA_W
C_G_K = C_G_Q + GQA_QW
C_G_V = C_G_K + GQA_KVW
C_D_Q = C_G_V + GQA_KVW
C_D_K = C_D_Q + DIFF_W
C_D_V = C_D_K + DIFF_W

KV_HEADS = (NA_HEADS, NA_HEADS, GQA_KV_HEADS, GQA_KV_HEADS, DIFF_HEADS, DIFF_HEADS)
N_KEYS = DEC_SEQ + PAST_LEN
LOG2E = 1.4426950408889634
QSCALE = HEAD_DIM ** -0.5 * LOG2E
DIFF_QSCALE = DIFF_QK_DIM ** -0.5 * LOG2E

NA_QROWS = 4
NA_QTOK = NA_QROWS * GRID_W
NA_GROUPS = GRID_ROWS // NA_QROWS
NA_SLAB_ROWS = 12
NA_SLAB = NA_SLAB_ROWS * GRID_W
NA_NKEY = NA_SLAB + PAST_LEN
NA_SLAB_START = (0, 0, 4, 4)
NA_PAIRS_BOTH = 2 * NA_KH - 2
MOD_ROWS = 16

F32 = jnp.float32
BF16 = jnp.bfloat16

VMEM_LIMIT = 52 * 1024 * 1024

TM_IN = 1024
TN_IN = 1536
TM_OUT = 512
TM_MLP = 512
TF_MLP = 1024
TN_ADA = 1024
ATT_ROWS = 256
NORM_ROWS = 128

def _params(sem):
    return pltpu.CompilerParams(dimension_semantics=sem, vmem_limit_bytes=VMEM_LIMIT)


def _rms(x, g):
    ms = jnp.mean(x * x, axis=-1, keepdims=True)
    return x * lax.rsqrt(ms + EPS) * g


def _nt(a, b):
    return lax.dot_general(a, b, (((1,), (1,)), ((), ())), preferred_element_type=F32)


def _mm(a, b):
    return jnp.dot(a, b, preferred_element_type=F32)


def _scores_exp(q, k, bias=None):
    s = _nt(q, k)
    if bias is not None:
        s = s + bias
    return jnp.exp2(s - jnp.max(s, axis=-1, keepdims=True)).astype(BF16)


def _weighted(e, v_ones):
    oa = _mm(e, v_ones)
    d = v_ones.shape[1] // 2
    return oa[:, :d] / oa[:, d:]


def _with_ones(v):
    return jnp.concatenate([v, jnp.ones_like(v)], axis=1)


def _rope(x, cos, sin_signed, half):
    n = x.shape[-1]
    lane = lax.broadcasted_iota(jnp.int32, x.shape, 1)
    first = (lane % (2 * half)) < half
    rot = jnp.where(first, pltpu.roll(x, n - half, 1), pltpu.roll(x, half, 1))
    return x * cos + rot * sin_signed


def _diff_lambda(lam_ref, lambda_init):
    lf = lam_ref[...]
    a = jnp.sum(lf[0:1] * lf[1:2], axis=-1, keepdims=True)
    b = jnp.sum(lf[2:3] * lf[3:4], axis=-1, keepdims=True)
    return jnp.exp(a) - jnp.exp(b) + lambda_init


def _mod_kernel(cv_ref, w_ref, b_ref, o_ref):
    cv = cv_ref[...]
    s = cv / (1.0 + jnp.exp(-cv))
    o_ref[...] = _mm(s.astype(BF16), w_ref[...].astype(BF16)) + b_ref[...]


def _modulation(cv, w_ada, b_ada):
    n = N_MOD * D_MODEL
    return pl.pallas_call(
        _mod_kernel,
        grid=(DEPTH, n // TN_ADA),
        in_specs=[
            pl.BlockSpec((MOD_ROWS, D_MODEL), lambda l, j: (0, 0)),
            pl.BlockSpec((None, D_MODEL, TN_ADA), lambda l, j: (l, 0, j)),
            pl.BlockSpec((None, 1, TN_ADA), lambda l, j: (l, 0, j)),
        ],
        out_specs=pl.BlockSpec((None, MOD_ROWS, TN_ADA), lambda l, j: (l, 0, j)),
        out_shape=jax.ShapeDtypeStruct((DEPTH, MOD_ROWS, n), F32),
        compiler_params=_params(("parallel", "parallel")),
        name="modulation",
    )(cv, w_ada, b_ada.reshape(DEPTH, 1, n))


def _mod_spec(chunk, row_fn):
    return pl.BlockSpec((None, None, 1, D_MODEL), lambda i, *_: (row_fn(i), chunk, 0, 0))


def _row_fn(tm, rows_per_batch, first_row):
    if rows_per_batch is None:
        return lambda i: first_row
    return lambda i: first_row + (i * tm) // rows_per_batch


def _row_chunks(n_rows, body):
    def step(c, carry):
        body(pl.ds(pl.multiple_of(c * NORM_ROWS, NORM_ROWS), NORM_ROWS))
        return carry

    lax.fori_loop(0, n_rows // NORM_ROWS, step, 0)


def _modulated_norm(h_ref, x_ref, g_ref, sc_ref, sh_ref, gm_ref):
    gm_ref[...] = g_ref[...] * (1.0 + sc_ref[...])

    def body(rows):
        x = x_ref[rows, :]
        r = lax.rsqrt(jnp.mean(x * x, axis=-1, keepdims=True) + EPS)
        h_ref[rows, :] = (x * r * gm_ref[...] + sh_ref[...]).astype(BF16)

    _row_chunks(x_ref.shape[0], body)


def _gated_norm_residual(out_ref, x_ref, y_ref, g_ref, gt_ref, gm_ref):
    gm_ref[...] = gt_ref[...] * g_ref[...]

    def body(rows):
        y = y_ref[rows, :]
        r = lax.rsqrt(jnp.mean(y * y, axis=-1, keepdims=True) + EPS)
        out_ref[rows, :] = x_ref[rows, :] + y * r * gm_ref[...]

    _row_chunks(x_ref.shape[0], body)


def _inproj_kernel(x_ref, g_ref, sc_ref, sh_ref, w_ref, o_ref, h_ref, gm_ref):
    @pl.when(pl.program_id(1) == 0)
    def _():
        _modulated_norm(h_ref, x_ref, g_ref, sc_ref, sh_ref, gm_ref)

    o_ref[...] = _mm(h_ref[...], w_ref[...])


def _inproj(x, g_pre, mod_l, w_in, rows_per_batch, first_row):
    m = x.shape[0]
    row = _row_fn(TM_IN, rows_per_batch, first_row)
    return pl.pallas_call(
        _inproj_kernel,
        grid=(m // TM_IN, D_IN // TN_IN),
        in_specs=[
            pl.BlockSpec((TM_IN, D_MODEL), lambda i, j: (i, 0)),
            pl.BlockSpec((1, D_MODEL), lambda i, j: (0, 0)),
            _mod_spec(1, row),
            _mod_spec(0, row),
            pl.BlockSpec((D_MODEL, TN_IN), lambda i, j: (0, j)),
        ],
        out_specs=pl.BlockSpec((TM_IN, TN_IN), lambda i, j: (i, j)),
        out_shape=jax.ShapeDtypeStruct((m, D_IN), F32),
        scratch_shapes=[pltpu.VMEM((TM_IN, D_MODEL), BF16), pltpu.VMEM((1, D_MODEL), F32)],
        compiler_params=_params(("parallel", "arbitrary")),
        name="inproj",
    )(x, g_pre, mod_l, mod_l, w_in)


def _ctx_attn_kernel(qkv_ref, qg_ref, kg_ref, lam_ref, dg_ref, *rest, lambda_init, first):
    if first:
        o_ref, *kv_refs = rest
        for ref in kv_refs:
            ref[1:] = jnp.zeros((DEPTH - 1,) + ref.shape[1:], F32)
        nak_ref, nav_ref, gk_ref, gv_ref, dk_ref, dv_ref = [ref.at[0] for ref in kv_refs]
    else:
        o_ref, nak_ref, nav_ref, gk_ref, gv_ref, dk_ref, dv_ref = rest[len(KV_HEADS):]
    hd = HEAD_DIM

    def cols(c0, h):
        return qkv_ref[:, c0 + h * hd:c0 + (h + 1) * hd]

    for ref, c0 in ((nak_ref, C_NA_K), (nav_ref, C_NA_V), (gv_ref, C_G_V),
                    (dk_ref, C_D_K), (dv_ref, C_D_V)):
        for h in range(ref.shape[1]):
            ref[:, h, :] = cols(c0, h)

    for h in range(NA_HEADS):
        q = (cols(C_NA_Q, h) * QSCALE).astype(BF16)
        k = cols(C_NA_K, h).astype(BF16)
        v1 = _with_ones(cols(C_NA_V, h).astype(BF16))
        o_ref[:, h * hd:(h + 1) * hd] = _weighted(_scores_exp(q, k), v1).astype(BF16)

    for kv in range(GQA_KV_HEADS):
        kf = _rms(cols(C_G_K, kv), kg_ref[...])
        gk_ref[:, kv, :] = kf
        k = kf.astype(BF16)
        v1 = _with_ones(cols(C_G_V, kv).astype(BF16))
        for g in range(GQA_GROUP):
            hq = kv * GQA_GROUP + g
            q = (_rms(cols(C_G_Q, hq), qg_ref[...]) * QSCALE).astype(BF16)
            o = _weighted(_scores_exp(q, k), v1)
            o_ref[:, NA_W + hq * hd:NA_W + (hq + 1) * hd] = o.astype(BF16)

    lam = _diff_lambda(lam_ref, lambda_init)
    lane = lax.broadcasted_iota(jnp.int32, (SEQ, hd), 1)
    for h in range(DIFF_HEADS):
        qf = cols(C_D_Q, h) * DIFF_QSCALE
        k = cols(C_D_K, h).astype(BF16)
        v1 = _with_ones(cols(C_D_V, h).astype(BF16))
        q0 = jnp.where(lane < DIFF_QK_DIM, qf, 0.0).astype(BF16)
        q1 = jnp.where(lane >= DIFF_QK_DIM, qf, 0.0).astype(BF16)
        od = _weighted(_scores_exp(q0, k), v1) - lam * _weighted(_scores_exp(q1, k), v1)
        od = _rms(od, dg_ref[...]) * (1.0 - lambda_init)
        c0 = NA_W + GQA_QW + h * hd
        o_ref[:, c0:c0 + hd] = od.astype(BF16)


def _ctx_attn(qkv, q_g, k_g, lam_vec, diff_g, kv_bufs, layer, lambda_init):
    n = qkv.shape[0]
    vec = pl.BlockSpec((1, HEAD_DIM), lambda b: (0, 0))
    n_in = 5
    first = kv_bufs is None

    def rows(w):
        return pl.BlockSpec((SEQ, w), lambda b: (b, 0))

    def heads(nh):
        if first:
            return pl.BlockSpec((None, DEPTH, SEQ, nh, HEAD_DIM), lambda b: (b, 0, 0, 0, 0))
        return pl.BlockSpec((None, None, SEQ, nh, HEAD_DIM), lambda b: (b, layer, 0, 0, 0))

    bufs = [] if first else list(kv_bufs)
    return pl.pallas_call(
        functools.partial(_ctx_attn_kernel, lambda_init=lambda_init, first=first),
        grid=(n // SEQ,),
        in_specs=[rows(D_IN), vec, vec,
                  pl.BlockSpec((4, DIFF_QK_DIM), lambda b: (0, 0)), vec]
                 + [pl.BlockSpec(memory_space=pl.ANY)] * len(bufs),
        out_specs=[rows(D_MIX)] + [heads(nh) for nh in KV_HEADS],
        out_shape=[jax.ShapeDtypeStruct((n, D_MIX), BF16)]
                  + [jax.ShapeDtypeStruct((BATCH, DEPTH, SEQ, nh, HEAD_DIM), F32)
                     for nh in KV_HEADS],
        input_output_aliases={n_in + i: 1 + i for i in range(len(bufs))},
        compiler_params=_params(("parallel",)),
        name="ctx_attn",
    )(qkv, q_g, k_g, lam_vec, diff_g, *bufs)


def _lat_na_kernel(q_ref, k_ref, v_ref, kc_ref, vc_ref, bias_ref, o_ref, kbig_ref, vbig_ref):
    lat0, lat1 = PAST_LEN, PAST_LEN + DEC_SEQ
    kc = kc_ref[...].astype(BF16)
    vc = _with_ones(vc_ref[...].astype(BF16))
    kbig_ref[0:lat0, :] = kc
    kbig_ref[lat0:lat1, :] = k_ref[...].astype(BF16)
    kbig_ref[lat1:, :] = kc
    vbig_ref[0:lat0, :] = vc
    vbig_ref[lat0:lat1, :] = _with_ones(v_ref[...].astype(BF16))
    vbig_ref[lat1:, :] = vc
    for j in range(NA_GROUPS):
        s0 = NA_SLAB_START[j]
        w0 = 0 if s0 == 0 else PAST_LEN + s0 * GRID_W
        q = (q_ref[j * NA_QTOK:(j + 1) * NA_QTOK, :] * QSCALE).astype(BF16)
        ctx0 = jnp.zeros((GRID_W, PAST_LEN), F32)
        bias_rows = []
        for r in range(j * NA_QROWS, (j + 1) * NA_QROWS):
            slab = [bias_ref[_na_pair_index(r, kr)] for kr in range(s0, s0 + NA_SLAB_ROWS, 2)]
            bias_rows.append(jnp.concatenate([ctx0] + slab if s0 == 0 else slab + [ctx0], axis=1))
        bias = jnp.concatenate(bias_rows, axis=0)
        e = _scores_exp(q, kbig_ref[w0:w0 + NA_NKEY, :], bias)
        o = _weighted(e, vbig_ref[w0:w0 + NA_NKEY, :])
        o_ref[j * NA_QTOK:(j + 1) * NA_QTOK, :] = o.astype(BF16)


def _lat_na(qkv, cache_k, cache_v, bias, layer):
    hd = HEAD_DIM
    cq, ck, cv = C_NA_Q // hd, C_NA_K // hd, C_NA_V // hd
    cache = pl.BlockSpec((None, None, PAST_LEN, hd), lambda h, b: (b, layer, 0, h))
    nbig = DEC_SEQ + 2 * PAST_LEN
    return pl.pallas_call(
        _lat_na_kernel,
        grid=(NA_HEADS, DEC_BATCH),
        in_specs=[
            pl.BlockSpec((DEC_SEQ, hd), lambda h, b: (b, cq + h)),
            pl.BlockSpec((DEC_SEQ, hd), lambda h, b: (b, ck + h)),
            pl.BlockSpec((DEC_SEQ, hd), lambda h, b: (b, cv + h)),
            cache, cache,
            pl.BlockSpec((None, None, NA_PAIRS_BOTH + 3, GRID_W, 2 * GRID_W),
                         lambda h, b: (layer, h, 0, 0, 0)),
        ],
        out_specs=pl.BlockSpec((DEC_SEQ, hd), lambda h, b: (b, h)),
        out_shape=jax.ShapeDtypeStruct((DEC_BATCH * DEC_SEQ, NA_W), BF16),
        scratch_shapes=[pltpu.VMEM((nbig, hd), BF16), pltpu.VMEM((nbig, 2 * hd), BF16)],
        compiler_params=_params(("parallel", "parallel")),
        name="lat_na",
    )(qkv, qkv, qkv, cache_k, cache_v, bias)


def _fill_keys(kall_ref, vall_ref, k_lat, v_ref, kc_ref, vc_ref):
    kall_ref[0:DEC_SEQ, :] = k_lat.astype(BF16)
    kall_ref[DEC_SEQ:N_KEYS, :] = kc_ref[...].astype(BF16)
    vall_ref[0:DEC_SEQ, :] = _with_ones(v_ref[...].astype(BF16))
    vall_ref[DEC_SEQ:N_KEYS, :] = _with_ones(vc_ref[...].astype(BF16))


def _lat_gqa_kernel(q_ref, k_ref, v_ref, kc_ref, vc_ref, cos_ref, sin_ref, qg_ref, kg_ref,
                    o_ref, kall_ref, vall_ref):
    hd = HEAD_DIM
    half = HEAD_DIM // 4
    kf = _rope(_rms(k_ref[...], kg_ref[...]), cos_ref[...], sin_ref[...], half)
    _fill_keys(kall_ref, vall_ref, kf, v_ref, kc_ref, vc_ref)
    for s in range(DEC_SEQ // ATT_ROWS):
        rows = slice(s * ATT_ROWS, (s + 1) * ATT_ROWS)
        cos = cos_ref[rows, :]
        sin = sin_ref[rows, :]
        for g in range(GQA_GROUP):
            qf = _rms(q_ref[rows, g * hd:(g + 1) * hd], qg_ref[...])
            q = (_rope(qf, cos, sin, half) * QSCALE).astype(BF16)
            o = _weighted(_scores_exp(q, kall_ref[...]), vall_ref[...])
            o_ref[rows, g * hd:(g + 1) * hd] = o.astype(BF16)


def _lat_gqa(qkv, cache_k, cache_v, cos, sin, q_g, k_g, layer):
    hd = HEAD_DIM
    gw = GQA_GROUP * hd
    cq, ck, cv = C_G_Q // gw, C_G_K // hd, C_G_V // hd
    cache = pl.BlockSpec((None, None, PAST_LEN, hd), lambda b, kv: (b, layer, 0, kv))
    table = pl.BlockSpec((DEC_SEQ, hd), lambda b, kv: (0, 0))
    vec = pl.BlockSpec((1, hd), lambda b, kv: (0, 0))
    return pl.pallas_call(
        _lat_gqa_kernel,
        grid=(DEC_BATCH, GQA_KV_HEADS),
        in_specs=[
            pl.BlockSpec((DEC_SEQ, gw), lambda b, kv: (b, cq + kv)),
            pl.BlockSpec((DEC_SEQ, hd), lambda b, kv: (b, ck + kv)),
            pl.BlockSpec((DEC_SEQ, hd), lambda b, kv: (b, cv + kv)),
            cache, cache, table, table, vec, vec,
        ],
        out_specs=pl.BlockSpec((DEC_SEQ, gw), lambda b, kv: (b, kv)),
        out_shape=jax.ShapeDtypeStruct((DEC_BATCH * DEC_SEQ, GQA_QW), BF16),
        scratch_shapes=[pltpu.VMEM((N_KEYS, hd), BF16), pltpu.VMEM((N_KEYS, 2 * hd), BF16)],
        compiler_params=_params(("parallel", "parallel")),
        name="lat_gqa",
    )(qkv, qkv, qkv, cache_k, cache_v, cos, sin, q_g, k_g)


def _lat_diff_kernel(q_ref, k_ref, v_ref, kc_ref, vc_ref, cos_ref, sin_ref, lam_ref, dg_ref,
                     o_ref, kall_ref, vall_ref, *, lambda_init):
    half = DIFF_QK_DIM // 4
    kf = _rope(k_ref[...], cos_ref[...], sin_ref[...], half)
    _fill_keys(kall_ref, vall_ref, kf, v_ref, kc_ref, vc_ref)
    lam = _diff_lambda(lam_ref, lambda_init)
    lane = lax.broadcasted_iota(jnp.int32, (ATT_ROWS, HEAD_DIM), 1)
    for s in range(DEC_SEQ // ATT_ROWS):
        rows = slice(s * ATT_ROWS, (s + 1) * ATT_ROWS)
        qf = _rope(q_ref[rows, :], cos_ref[rows, :], sin_ref[rows, :], half) * DIFF_QSCALE
        q0 = jnp.where(lane < DIFF_QK_DIM, qf, 0.0).astype(BF16)
        q1 = jnp.where(lane >= DIFF_QK_DIM, qf, 0.0).astype(BF16)
        od = (_weighted(_scores_exp(q0, kall_ref[...]), vall_ref[...])
              - lam * _weighted(_scores_exp(q1, kall_ref[...]), vall_ref[...]))
        od = _rms(od, dg_ref[...]) * (1.0 - lambda_init)
        o_ref[rows, :] = od.astype(BF16)


def _lat_diff(qkv, cache_k, cache_v, cos, sin, lam_vec, diff_g, layer, lambda_init):
    hd = HEAD_DIM
    cq, ck, cv = C_D_Q // hd, C_D_K // hd, C_D_V // hd
    cache = pl.BlockSpec((None, None, PAST_LEN, hd), lambda b, h: (b, layer, 0, h))
    table = pl.BlockSpec((DEC_SEQ, hd), lambda b, h: (0, 0))
    return pl.pallas_call(
        functools.partial(_lat_diff_kernel, lambda_init=lambda_init),
        grid=(DEC_BATCH, DIFF_HEADS),
        in_specs=[
            pl.BlockSpec((DEC_SEQ, hd), lambda b, h: (b, cq + h)),
            pl.BlockSpec((DEC_SEQ, hd), lambda b, h: (b, ck + h)),
            pl.BlockSpec((DEC_SEQ, hd), lambda b, h: (b, cv + h)),
            cache, cache, table, table,
            pl.BlockSpec((4, DIFF_QK_DIM), lambda b, h: (0, 0)),
            pl.BlockSpec((1, hd), lambda b, h: (0, 0)),
        ],
        out_specs=pl.BlockSpec((DEC_SEQ, hd), lambda b, h: (b, h)),
        out_shape=jax.ShapeDtypeStruct((DEC_BATCH * DEC_SEQ, DIFF_W), BF16),
        scratch_shapes=[pltpu.VMEM((N_KEYS, hd), BF16), pltpu.VMEM((N_KEYS, 2 * hd), BF16)],
        compiler_params=_params(("parallel", "parallel")),
        name="lat_diff",
    )(qkv, qkv, qkv, cache_k, cache_v, cos, sin, lam_vec, diff_g)


def _outproj_kernel(*refs, n_o):
    o_refs = refs[:n_o]
    w_ref, x_ref, g_ref, gt_ref, out_ref = refs[n_o:]
    y = None
    off = 0
    for o_ref in o_refs:
        wd = o_ref.shape[1]
        part = _mm(o_ref[...], w_ref[off:off + wd, :])
        y = part if y is None else y + part
        off += wd
    out_ref[...] = x_ref[...] + gt_ref[...] * _rms(y, g_ref[...])


def _outproj(o_parts, w_out, x, g_post, mod_l, rows_per_batch, first_row):
    m = x.shape[0]
    row = _row_fn(TM_OUT, rows_per_batch, first_row)
    full = pl.BlockSpec((TM_OUT, D_MODEL), lambda i: (i, 0))
    return pl.pallas_call(
        functools.partial(_outproj_kernel, n_o=len(o_parts)),
        grid=(m // TM_OUT,),
        in_specs=[pl.BlockSpec((TM_OUT, o.shape[1]), lambda i: (i, 0)) for o in o_parts] + [
            pl.BlockSpec((D_MIX, D_MODEL), lambda i: (0, 0)),
            full,
            pl.BlockSpec((1, D_MODEL), lambda i: (0, 0)),
            _mod_spec(2, row),
        ],
        out_specs=full,
        out_shape=jax.ShapeDtypeStruct((m, D_MODEL), F32),
        compiler_params=_params(("parallel",)),
        name="outproj",
    )(*o_parts, w_out, x, g_post, mod_l)


def _mlp_kernel(x_ref, gpre_ref, sc_ref, sh_ref, wup_ref, wdn_ref, gpost_ref, gt_ref, *rest,
                n_cast, n_flat):
    n_side = n_cast + n_flat
    cast_in, flat_in = rest[:n_cast], rest[n_cast:n_side]
    out_ref = rest[n_side]
    cast_out = rest[n_side + 1:n_side + 1 + n_cast]
    flat_out = rest[n_side + 1 + n_cast:2 * n_side + 1]
    h_ref, acc_ref, gm_ref = rest[2 * n_side + 1:]
    k = pl.program_id(1)

    @pl.when(k == 0)
    def _():
        _modulated_norm(h_ref, x_ref, gpre_ref, sc_ref, sh_ref, gm_ref)
        acc_ref[...] = jnp.zeros_like(acc_ref)

    u = _mm(h_ref[...], wup_ref[...])
    a = jnp.square(jnp.maximum(u, 0.0)).astype(BF16)
    acc_ref[...] += _mm(a, wdn_ref[...])

    for src, dst in zip(cast_in, cast_out):
        dst[...] = src[...].astype(BF16)
    for src, dst in zip(flat_in, flat_out):
        for h in range(src.shape[1]):
            dst[:, h * HEAD_DIM:(h + 1) * HEAD_DIM] = src[:, h, :]

    @pl.when(k == pl.num_programs(1) - 1)
    def _():
        _gated_norm_residual(out_ref, x_ref, acc_ref, gpost_ref, gt_ref, gm_ref)


def _mlp(x, g_pre, g_post, mod_l, w_up, w_down, rows_per_batch, first_row, cast=None,
         flatten=None):
    m = x.shape[0]
    row = _row_fn(TM_MLP, rows_per_batch, first_row)
    full = pl.BlockSpec((TM_MLP, D_MODEL), lambda i, k: (i, 0))
    vec = pl.BlockSpec((1, D_MODEL), lambda i, k: (0, 0))
    n_k = D_FF // TF_MLP
    n_steps = (m // TM_MLP) * n_k
    cast_in_specs, cast_out_specs, cast_shapes, cast_args = [], [], [], []
    if cast is not None:
        stacks, layer = cast
        for w in stacks:
            rows, cols = w.shape[1], w.shape[2]
            rb = rows // n_steps
            cast_in_specs.append(
                pl.BlockSpec((None, rb, cols), lambda i, k: (layer, i * n_k + k, 0)))
            cast_out_specs.append(pl.BlockSpec((rb, cols), lambda i, k: (i * n_k + k, 0)))
            cast_shapes.append(jax.ShapeDtypeStruct((rows, cols), BF16))
            cast_args.append(w)
    flat_in_specs, flat_out_specs, flat_shapes = [], [], []
    flatten = flatten or []
    if flatten:
        tok = DEC_BATCH * DEPTH * PAST_LEN // n_steps
        per = PAST_LEN // tok

        def where(i, k):
            s = i * n_k + k
            return s // (DEPTH * per), (s // per) % DEPTH, s % per

        for a in flatten:
            nh = a.shape[3]
            flat_in_specs.append(pl.BlockSpec((None, None, tok, nh, HEAD_DIM),
                                              lambda i, k: (*where(i, k), 0, 0)))
            flat_out_specs.append(pl.BlockSpec((None, None, tok, nh * HEAD_DIM),
                                               lambda i, k: (*where(i, k), 0)))
            flat_shapes.append(
                jax.ShapeDtypeStruct((DEC_BATCH, DEPTH, PAST_LEN, nh * HEAD_DIM), a.dtype))
    n_cast = len(cast_args)
    res = pl.pallas_call(
        functools.partial(_mlp_kernel, n_cast=n_cast, n_flat=len(flatten)),
        grid=(m // TM_MLP, n_k),
        in_specs=[
            full, vec, _mod_spec(4, row), _mod_spec(3, row),
            pl.BlockSpec((D_MODEL, TF_MLP), lambda i, k: (0, k)),
            pl.BlockSpec((TF_MLP, D_MODEL), lambda i, k: (k, 0)),
            vec, _mod_spec(5, row),
        ] + cast_in_specs + flat_in_specs,
        out_specs=[full] + cast_out_specs + flat_out_specs,
        out_shape=[jax.ShapeDtypeStruct((m, D_MODEL), F32)] + cast_shapes + flat_shapes,
        scratch_shapes=[pltpu.VMEM((TM_MLP, D_MODEL), BF16), pltpu.VMEM((TM_MLP, D_MODEL), F32),
                        pltpu.VMEM((1, D_MODEL), F32)],
        compiler_params=_params(("parallel", "arbitrary")),
        name="mlp",
    )(x, g_pre, mod_l, mod_l, w_up, w_down, g_post, mod_l, *cast_args, *flatten)
    return res[0], res[1:1 + n_cast], res[1 + n_cast:]


def _rope_tables(half, n_rep):
    t = jnp.arange(DEC_SEQ)
    inv = ROPE_THETA ** (-jnp.arange(half, dtype=F32) / half)

    def cs(pos):
        ang = pos.astype(F32)[:, None] * inv[None, :]
        c, s = jnp.cos(ang), jnp.sin(ang)
        return jnp.concatenate([c, c], axis=-1), jnp.concatenate([-s, s], axis=-1)

    cr, sr = cs(t // GRID_W)
    cc, sc = cs(t % GRID_W)
    return (jnp.concatenate([cr, cc] * n_rep, axis=-1),
            jnp.concatenate([sr, sc] * n_rep, axis=-1))


def _na_bias(rpb):
    qc = np.arange(GRID_W)[:, None]
    kc = np.arange(GRID_W)[None, :]
    ws = np.clip(qc - NA_KW // 2, 0, GRID_W - NA_KW)
    valid = (kc >= ws) & (kc < ws + NA_KW)
    dcol = np.clip(kc - qc + NA_KW - 1, 0, 2 * NA_KW - 2)
    n_dcol = 2 * NA_KW - 1
    onehot = (dcol.reshape(-1)[None, :] == np.arange(n_dcol)[:, None]).astype(np.float32)
    t = jnp.einsum("lhdc,cq->lhdq", rpb.astype(F32), jnp.asarray(onehot),
                   precision=lax.Precision.HIGHEST)
    t = t.reshape(DEPTH, NA_HEADS, 2 * NA_KH - 1, GRID_W, GRID_W) * LOG2E
    t = jnp.where(jnp.asarray(valid)[None, None, None], t, NEG_BIG)
    masked = jnp.full((DEPTH, NA_HEADS, 1, GRID_W, GRID_W), NEG_BIG, F32)
    lo, hi = NA_KH // 2 - 1, NA_KH + NA_KH // 2 - 2
    both = jnp.concatenate([t[:, :, :-1], t[:, :, 1:]], axis=-1)
    second = jnp.concatenate([masked, t[:, :, lo:lo + 1]], axis=-1)
    first = jnp.concatenate([t[:, :, hi:hi + 1], masked], axis=-1)
    none = jnp.concatenate([masked, masked], axis=-1)
    return jnp.concatenate([both, second, first, none], axis=2)


def _na_pair_index(r, key_row):
    r0 = min(max(r - NA_KH // 2, 0), GRID_ROWS - NA_KH)
    in_a = r0 <= key_row < r0 + NA_KH
    in_b = r0 <= key_row + 1 < r0 + NA_KH
    d = key_row - r + NA_KH - 1
    if in_a and in_b:
        return d
    if in_b:
        assert d + 1 == NA_KH // 2 - 1
        return NA_PAIRS_BOTH
    if in_a:
        assert d == NA_KH + NA_KH // 2 - 2
        return NA_PAIRS_BOTH + 1
    return NA_PAIRS_BOTH + 2


def kernel(x_prompt, x_sample, c, cache_na_k, cache_na_v, cache_gqa_k, cache_gqa_v,
           cache_diff_k, cache_diff_v, c_ctx, w_ada, b_ada, norm_g, w_in, w_out, na_rpb,
           gqa_q_g, gqa_k_g, diff_lam, diff_g, w_up, w_down):
    np_rows = BATCH * SEQ
    ns_rows = DEC_BATCH * DEC_SEQ
    xp = x_prompt.reshape(np_rows, D_MODEL)
    xs = x_sample.reshape(ns_rows, D_MODEL)

    cv = jnp.concatenate(
        [c_ctx[None, :], c, jnp.zeros((MOD_ROWS - 1 - DEC_BATCH, D_MODEL), F32)], axis=0)
    mod = _modulation(cv, w_ada, b_ada).reshape(DEPTH, MOD_ROWS, N_MOD, 1, D_MODEL)

    w_stacks = [w_in, w_out, w_up, w_down]
    w_in_b, w_out_b, w_up_b, w_down_b = [w[0].astype(BF16) for w in w_stacks]

    caches = [cache_na_k, cache_na_v, cache_gqa_k, cache_gqa_v, cache_diff_k, cache_diff_v]

    cos_g, sin_g = _rope_tables(HEAD_DIM // 4, 1)
    cos_d, sin_d = _rope_tables(DIFF_QK_DIM // 4, 2)
    na_bias = _na_bias(na_rpb)

    new_kv = None
    for l in range(DEPTH):
        lambda_init = 0.8 - 0.6 * math.exp(-0.3 * l)
        mod_l = mod[l]
        g = norm_g[l].reshape(4, 1, D_MODEL)
        q_g = gqa_q_g[l].reshape(1, HEAD_DIM)
        k_g = gqa_k_g[l].reshape(1, HEAD_DIM)
        d_g = diff_g[l].reshape(1, HEAD_DIM)
        lam_vec = diff_lam[l]

        qkv_p = _inproj(xp, g[0], mod_l, w_in_b, None, 0)
        o_p, *new_kv = _ctx_attn(qkv_p, q_g, k_g, lam_vec, d_g, new_kv, l, lambda_init)
        xp = _outproj([o_p], w_out_b, xp, g[1], mod_l, None, 0)
        xp, _, flat = _mlp(xp, g[2], g[3], mod_l, w_up_b, w_down_b, None, 0,
                           flatten=caches if l == 0 else None)
        if l == 0:
            c_na_k, c_na_v, c_g_k, c_g_v, c_d_k, c_d_v = flat

        qkv_s = _inproj(xs, g[0], mod_l, w_in_b, DEC_SEQ, 1)
        o_na = _lat_na(qkv_s, c_na_k, c_na_v, na_bias, l)
        o_gqa = _lat_gqa(qkv_s, c_g_k, c_g_v, cos_g, sin_g, q_g, k_g, l)
        o_diff = _lat_diff(qkv_s, c_d_k, c_d_v, cos_d, sin_d, lam_vec, d_g, l, lambda_init)
        xs = _outproj([o_na, o_gqa, o_diff], w_out_b, xs, g[1], mod_l, DEC_SEQ, 1)
        cast = (w_stacks, l + 1) if l + 1 < DEPTH else None
        xs, nxt, _ = _mlp(xs, g[2], g[3], mod_l, w_up_b, w_down_b, DEC_SEQ, 1, cast)
        if nxt:
            w_in_b, w_out_b, w_up_b, w_down_b = nxt

    return (xp.reshape(BATCH, SEQ, D_MODEL), xs.reshape(DEC_BATCH, DEC_SEQ, D_MODEL), *new_kv)
```

```python
import functools
import math

import jax
import jax.numpy as jnp
import numpy as np
from jax import lax
from jax.experimental import pallas as pl
from jax.experimental.pallas import tpu as pltpu

D_MODEL = 2048
BATCH = 16
SEQ = 256
DEPTH = 4
DEC_BATCH = 8
DEC_SEQ = 1024
PAST_LEN = 256
GRID_W = 64
GRID_ROWS = DEC_SEQ // GRID_W
HEAD_DIM = 128
NA_HEADS = 4
GQA_Q_HEADS = 8
GQA_KV_HEADS = 2
GQA_GROUP = GQA_Q_HEADS // GQA_KV_HEADS
DIFF_HEADS = 4
DIFF_QK_DIM = HEAD_DIM // 2
NA_KH = 8
NA_KW = 16
D_FF = 4 * D_MODEL
ROPE_THETA = 10000.0
EPS = 1e-6
N_MOD = 6
NEG_BIG = -1e30

NA_W = NA_HEADS * HEAD_DIM
GQA_QW = GQA_Q_HEADS * HEAD_DIM
GQA_KVW = GQA_KV_HEADS * HEAD_DIM
DIFF_W = DIFF_HEADS * HEAD_DIM
D_IN = 3 * NA_W + GQA_QW + 2 * GQA_KVW + 3 * DIFF_W
D_MIX = NA_W + GQA_QW + DIFF_W
C_NA_Q, C_NA_K, C_NA_V = 0, NA_W, 2 * NA_W
C_G_Q = 3 * NA_W
C_G_K = C_G_Q + GQA_QW
C_G_V = C_G_K + GQA_KVW
C_D_Q = C_G_V + GQA_KVW
C_D_K = C_D_Q + DIFF_W
C_D_V = C_D_K + DIFF_W

KV_HEADS = (NA_HEADS, NA_HEADS, GQA_KV_HEADS, GQA_KV_HEADS, DIFF_HEADS, DIFF_HEADS)
N_KEYS = DEC_SEQ + PAST_LEN
LOG2E = 1.4426950408889634
QSCALE = HEAD_DIM ** -0.5 * LOG2E
DIFF_QSCALE = DIFF_QK_DIM ** -0.5 * LOG2E

NA_QROWS = 4
NA_QTOK = NA_QROWS * GRID_W
NA_GROUPS = GRID_ROWS // NA_QROWS
NA_SLAB_ROWS = 12
NA_SLAB = NA_SLAB_ROWS * GRID_W
NA_NKEY = NA_SLAB + PAST_LEN
NA_SLAB_START = (0, 0, 4, 4)
NA_PAIRS_BOTH = 2 * NA_KH - 2
MOD_ROWS = 16

F32 = jnp.float32
BF16 = jnp.bfloat16

VMEM_LIMIT = 52 * 1024 * 1024

TM_IN = 1024
TN_IN = 1536
TM_OUT = 512
TM_MLP = 512
TF_MLP = 1024
TN_ADA = 1024
ATT_ROWS = 256
NORM_ROWS = 128

def _params(sem):
    return pltpu.CompilerParams(dimension_semantics=sem, vmem_limit_bytes=VMEM_LIMIT)


def _rms(x, g):
    ms = jnp.mean(x * x, axis=-1, keepdims=True)
    return x * lax.rsqrt(ms + EPS) * g


def _nt(a, b):
    return lax.dot_general(a, b, (((1,), (1,)), ((), ())), preferred_element_type=F32)


def _mm(a, b):
    return jnp.dot(a, b, preferred_element_type=F32)


def _scores_exp(q, k, bias=None):
    s = _nt(q, k)
    if bias is not None:
        s = s + bias
    return jnp.exp2(s - jnp.max(s, axis=-1, keepdims=True)).astype(BF16)


def _weighted(e, v_ones):
    oa = _mm(e, v_ones)
    d = v_ones.shape[1] // 2
    return oa[:, :d] / oa[:, d:]


def _with_ones(v):
    return jnp.concatenate([v, jnp.ones_like(v)], axis=1)


def _rope(x, cos, sin_signed, half):
    n = x.shape[-1]
    lane = lax.broadcasted_iota(jnp.int32, x.shape, 1)
    first = (lane % (2 * half)) < half
    rot = jnp.where(first, pltpu.roll(x, n - half, 1), pltpu.roll(x, half, 1))
    return x * cos + rot * sin_signed


def _diff_lambda(lam_ref, lambda_init):
    lf = lam_ref[...]
    a = jnp.sum(lf[0:1] * lf[1:2], axis=-1, keepdims=True)
    b = jnp.sum(lf[2:3] * lf[3:4], axis=-1, keepdims=True)
    return jnp.exp(a) - jnp.exp(b) + lambda_init


def _mod_kernel(cv_ref, w_ref, b_ref, o_ref):
    cv = cv_ref[...]
    s = cv / (1.0 + jnp.exp(-cv))
    o_ref[...] = _mm(s.astype(BF16), w_ref[...].astype(BF16)) + b_ref[...]


def _modulation(cv, w_ada, b_ada):
    n = N_MOD * D_MODEL
    return pl.pallas_call(
        _mod_kernel,
        grid=(DEPTH, n // TN_ADA),
        in_specs=[
            pl.BlockSpec((MOD_ROWS, D_MODEL), lambda l, j: (0, 0)),
            pl.BlockSpec((None, D_MODEL, TN_ADA), lambda l, j: (l, 0, j)),
            pl.BlockSpec((None, 1, TN_ADA), lambda l, j: (l, 0, j)),
        ],
        out_specs=pl.BlockSpec((None, MOD_ROWS, TN_ADA), lambda l, j: (l, 0, j)),
        out_shape=jax.ShapeDtypeStruct((DEPTH, MOD_ROWS, n), F32),
        compiler_params=_params(("parallel", "parallel")),
        name="modulation",
    )(cv, w_ada, b_ada.reshape(DEPTH, 1, n))


def _mod_spec(chunk, row_fn):
    return pl.BlockSpec((None, None, 1, D_MODEL), lambda i, *_: (row_fn(i), chunk, 0, 0))


def _row_fn(tm, rows_per_batch, first_row):
    if rows_per_batch is None:
        return lambda i: first_row
    return lambda i: first_row + (i * tm) // rows_per_batch


def _row_chunks(n_rows, body):
    def step(c, carry):
        body(pl.ds(pl.multiple_of(c * NORM_ROWS, NORM_ROWS), NORM_ROWS))
        return carry

    lax.fori_loop(0, n_rows // NORM_ROWS, step, 0)


def _modulated_norm(h_ref, x_ref, g_ref, sc_ref, sh_ref, gm_ref):
    gm_ref[...] = g_ref[...] * (1.0 + sc_ref[...])

    def body(rows):
        x = x_ref[rows, :]
        r = lax.rsqrt(jnp.mean(x * x, axis=-1, keepdims=True) + EPS)
        h_ref[rows, :] = (x * r * gm_ref[...] + sh_ref[...]).astype(BF16)

    _row_chunks(x_ref.shape[0], body)


def _gated_norm_residual(out_ref, x_ref, y_ref, g_ref, gt_ref, gm_ref):
    gm_ref[...] = gt_ref[...] * g_ref[...]

    def body(rows):
        y = y_ref[rows, :]
        r = lax.rsqrt(jnp.mean(y * y, axis=-1, keepdims=True) + EPS)
        out_ref[rows, :] = x_ref[rows, :] + y * r * gm_ref[...]

    _row_chunks(x_ref.shape[0], body)


def _inproj_kernel(x_hbm, g_ref, sc_ref, sh_ref, w_ref, o_ref, xbuf_ref, sem_ref, h_ref, gm_ref):
    i = pl.program_id(0)
    j = pl.program_id(1)
    slot = i % 2

    def x_copy(block, to_slot):
        rows = pl.ds(pl.multiple_of(block * TM_IN, TM_IN), TM_IN)
        return pltpu.make_async_copy(x_hbm.at[rows, :], xbuf_ref.at[to_slot], sem_ref.at[to_slot])

    @pl.when((i == 0) & (j == 0))
    def _():
        x_copy(0, 0).start()

    @pl.when((j == 0) & (i + 1 < pl.num_programs(0)))
    def _():
        x_copy(i + 1, 1 - slot).start()

    @pl.when(j == 0)
    def _():
        x_copy(i, slot).wait()
        _modulated_norm(h_ref, xbuf_ref.at[slot], g_ref, sc_ref, sh_ref, gm_ref)

    o_ref[...] = _mm(h_ref[...], w_ref[...])


def _inproj(x, g_pre, mod_l, w_in, rows_per_batch, first_row):
    m = x.shape[0]
    row = _row_fn(TM_IN, rows_per_batch, first_row)
    return pl.pallas_call(
        _inproj_kernel,
        grid=(m // TM_IN, D_IN // TN_IN),
        in_specs=[
            pl.BlockSpec(memory_space=pl.ANY),
            pl.BlockSpec((1, D_MODEL), lambda i, j: (0, 0)),
            _mod_spec(1, row),
            _mod_spec(0, row),
            pl.BlockSpec((D_MODEL, TN_IN), lambda i, j: (0, j)),
        ],
        out_specs=pl.BlockSpec((TM_IN, TN_IN), lambda i, j: (i, j)),
        out_shape=jax.ShapeDtypeStruct((m, D_IN), F32),
        scratch_shapes=[pltpu.VMEM((2, TM_IN, D_MODEL), F32), pltpu.SemaphoreType.DMA((2,)),
                        pltpu.VMEM((TM_IN, D_MODEL), BF16), pltpu.VMEM((1, D_MODEL), F32)],
        compiler_params=_params(("arbitrary", "arbitrary")),
        name="inproj",
    )(x, g_pre, mod_l, mod_l, w_in)


def _ctx_attn_kernel(qkv_ref, qg_ref, kg_ref, lam_ref, dg_ref, *rest, lambda_init, first):
    if first:
        o_ref, *kv_refs = rest
        for ref in kv_refs:
            ref[1:] = jnp.zeros((DEPTH - 1,) + ref.shape[1:], F32)
        nak_ref, nav_ref, gk_ref, gv_ref, dk_ref, dv_ref = [ref.at[0] for ref in kv_refs]
    else:
        o_ref, nak_ref, nav_ref, gk_ref, gv_ref, dk_ref, dv_ref = rest[len(KV_HEADS):]
    hd = HEAD_DIM

    def cols(c0, h):
        return qkv_ref[:, c0 + h * hd:c0 + (h + 1) * hd]

    for ref, c0 in ((nak_ref, C_NA_K), (nav_ref, C_NA_V), (gv_ref, C_G_V),
                    (dk_ref, C_D_K), (dv_ref, C_D_V)):
        for h in range(ref.shape[1]):
            ref[:, h, :] = cols(c0, h)

    for h in range(NA_HEADS):
        q = (cols(C_NA_Q, h) * QSCALE).astype(BF16)
        k = cols(C_NA_K, h).astype(BF16)
        v1 = _with_ones(cols(C_NA_V, h).astype(BF16))
        o_ref[:, h * hd:(h + 1) * hd] = _weighted(_scores_exp(q, k), v1).astype(BF16)

    for kv in range(GQA_KV_HEADS):
        kf = _rms(cols(C_G_K, kv), kg_ref[...])
        gk_ref[:, kv, :] = kf
        k = kf.astype(BF16)
        v1 = _with_ones(cols(C_G_V, kv).astype(BF16))
        for g in range(GQA_GROUP):
            hq = kv * GQA_GROUP + g
            q = (_rms(cols(C_G_Q, hq), qg_ref[...]) * QSCALE).astype(BF16)
            o = _weighted(_scores_exp(q, k), v1)
            o_ref[:, NA_W + hq * hd:NA_W + (hq + 1) * hd] = o.astype(BF16)

    lam = _diff_lambda(lam_ref, lambda_init)
    lane = lax.broadcasted_iota(jnp.int32, (SEQ, hd), 1)
    for h in range(DIFF_HEADS):
        qf = cols(C_D_Q, h) * DIFF_QSCALE
        k = cols(C_D_K, h).astype(BF16)
        v1 = _with_ones(cols(C_D_V, h).astype(BF16))
        q0 = jnp.where(lane < DIFF_QK_DIM, qf, 0.0).astype(BF16)
        q1 = jnp.where(lane >= DIFF_QK_DIM, qf, 0.0).astype(BF16)
        od = _weighted(_scores_exp(q0, k), v1) - lam * _weighted(_scores_exp(q1, k), v1)
        od = _rms(od, dg_ref[...]) * (1.0 - lambda_init)
        c0 = NA_W + GQA_QW + h * hd
        o_ref[:, c0:c0 + hd] = od.astype(BF16)


def _ctx_attn(qkv, q_g, k_g, lam_vec, diff_g, kv_bufs, layer, lambda_init):
    n = qkv.shape[0]
    vec = pl.BlockSpec((1, HEAD_DIM), lambda b: (0, 0))
    n_in = 5
    first = kv_bufs is None

    def rows(w):
        return pl.BlockSpec((SEQ, w), lambda b: (b, 0))

    def heads(nh):
        if first:
            return pl.BlockSpec((None, DEPTH, SEQ, nh, HEAD_DIM), lambda b: (b, 0, 0, 0, 0))
        return pl.BlockSpec((None, None, SEQ, nh, HEAD_DIM), lambda b: (b, layer, 0, 0, 0))

    bufs = [] if first else list(kv_bufs)
    return pl.pallas_call(
        functools.partial(_ctx_attn_kernel, lambda_init=lambda_init, first=first),
        grid=(n // SEQ,),
        in_specs=[rows(D_IN), vec, vec,
                  pl.BlockSpec((4, DIFF_QK_DIM), lambda b: (0, 0)), vec]
                 + [pl.BlockSpec(memory_space=pl.ANY)] * len(bufs),
        out_specs=[rows(D_MIX)] + [heads(nh) for nh in KV_HEADS],
        out_shape=[jax.ShapeDtypeStruct((n, D_MIX), BF16)]
                  + [jax.ShapeDtypeStruct((BATCH, DEPTH, SEQ, nh, HEAD_DIM), F32)
                     for nh in KV_HEADS],
        input_output_aliases={n_in + i: 1 + i for i in range(len(bufs))},
        compiler_params=_params(("parallel",)),
        name="ctx_attn",
    )(qkv, q_g, k_g, lam_vec, diff_g, *bufs)


def _lat_na_kernel(q_ref, k_ref, v_ref, kc_ref, vc_ref, bias_ref, o_ref, kbig_ref, vbig_ref):
    lat0, lat1 = PAST_LEN, PAST_LEN + DEC_SEQ
    kc = kc_ref[...].astype(BF16)
    vc = _with_ones(vc_ref[...].astype(BF16))
    kbig_ref[0:lat0, :] = kc
    kbig_ref[lat0:lat1, :] = k_ref[...].astype(BF16)
    kbig_ref[lat1:, :] = kc
    vbig_ref[0:lat0, :] = vc
    vbig_ref[lat0:lat1, :] = _with_ones(v_ref[...].astype(BF16))
    vbig_ref[lat1:, :] = vc
    for j in range(NA_GROUPS):
        s0 = NA_SLAB_START[j]
        w0 = 0 if s0 == 0 else PAST_LEN + s0 * GRID_W
        q = (q_ref[j * NA_QTOK:(j + 1) * NA_QTOK, :] * QSCALE).astype(BF16)
        ctx0 = jnp.zeros((GRID_W, PAST_LEN), F32)
        bias_rows = []
        for r in range(j * NA_QROWS, (j + 1) * NA_QROWS):
            slab = [bias_ref[_na_pair_index(r, kr)] for kr in range(s0, s0 + NA_SLAB_ROWS, 2)]
            bias_rows.append(jnp.concatenate([ctx0] + slab if s0 == 0 else slab + [ctx0], axis=1))
        bias = jnp.concatenate(bias_rows, axis=0)
        e = _scores_exp(q, kbig_ref[w0:w0 + NA_NKEY, :], bias)
        o = _weighted(e, vbig_ref[w0:w0 + NA_NKEY, :])
        o_ref[j * NA_QTOK:(j + 1) * NA_QTOK, :] = o.astype(BF16)


def _lat_na(qkv, cache_k, cache_v, bias, layer):
    hd = HEAD_DIM
    cq, ck, cv = C_NA_Q // hd, C_NA_K // hd, C_NA_V // hd
    cache = pl.BlockSpec((None, None, PAST_LEN, hd), lambda h, b: (b, layer, 0, h))
    nbig = DEC_SEQ + 2 * PAST_LEN
    return pl.pallas_call(
        _lat_na_kernel,
        grid=(NA_HEADS, DEC_BATCH),
        in_specs=[
            pl.BlockSpec((DEC_SEQ, hd), lambda h, b: (b, cq + h)),
            pl.BlockSpec((DEC_SEQ, hd), lambda h, b: (b, ck + h)),
            pl.BlockSpec((DEC_SEQ, hd), lambda h, b: (b, cv + h)),
            cache, cache,
            pl.BlockSpec((None, None, NA_PAIRS_BOTH + 3, GRID_W, 2 * GRID_W),
                         lambda h, b: (layer, h, 0, 0, 0)),
        ],
        out_specs=pl.BlockSpec((DEC_SEQ, hd), lambda h, b: (b, h)),
        out_shape=jax.ShapeDtypeStruct((DEC_BATCH * DEC_SEQ, NA_W), BF16),
        scratch_shapes=[pltpu.VMEM((nbig, hd), BF16), pltpu.VMEM((nbig, 2 * hd), BF16)],
        compiler_params=_params(("parallel", "parallel")),
        name="lat_na",
    )(qkv, qkv, qkv, cache_k, cache_v, bias)


def _fill_keys(kall_ref, vall_ref, k_lat, v_ref, kc_ref, vc_ref):
    kall_ref[0:DEC_SEQ, :] = k_lat.astype(BF16)
    kall_ref[DEC_SEQ:N_KEYS, :] = kc_ref[...].astype(BF16)
    vall_ref[0:DEC_SEQ, :] = _with_ones(v_ref[...].astype(BF16))
    vall_ref[DEC_SEQ:N_KEYS, :] = _with_ones(vc_ref[...].astype(BF16))


def _lat_gqa_kernel(q_ref, k_ref, v_ref, kc_ref, vc_ref, cos_ref, sin_ref, qg_ref, kg_ref,
                    o_ref, kall_ref, vall_ref):
    hd = HEAD_DIM
    half = HEAD_DIM // 4
    kf = _rope(_rms(k_ref[...], kg_ref[...]), cos_ref[...], sin_ref[...], half)
    _fill_keys(kall_ref, vall_ref, kf, v_ref, kc_ref, vc_ref)
    for s in range(DEC_SEQ // ATT_ROWS):
        rows = slice(s * ATT_ROWS, (s + 1) * ATT_ROWS)
        cos = cos_ref[rows, :]
        sin = sin_ref[rows, :]
        for g in range(GQA_GROUP):
            qf = _rms(q_ref[rows, g * hd:(g + 1) * hd], qg_ref[...])
            q = (_rope(qf, cos, sin, half) * QSCALE).astype(BF16)
            o = _weighted(_scores_exp(q, kall_ref[...]), vall_ref[...])
            o_ref[rows, g * hd:(g + 1) * hd] = o.astype(BF16)


def _lat_gqa(qkv, cache_k, cache_v, cos, sin, q_g, k_g, layer):
    hd = HEAD_DIM
    gw = GQA_GROUP * hd
    cq, ck, cv = C_G_Q // gw, C_G_K // hd, C_G_V // hd
    cache = pl.BlockSpec((None, None, PAST_LEN, hd), lambda b, kv: (b, layer, 0, kv))
    table = pl.BlockSpec((DEC_SEQ, hd), lambda b, kv: (0, 0))
    vec = pl.BlockSpec((1, hd), lambda b, kv: (0, 0))
    return pl.pallas_call(
        _lat_gqa_kernel,
        grid=(DEC_BATCH, GQA_KV_HEADS),
        in_specs=[
            pl.BlockSpec((DEC_SEQ, gw), lambda b, kv: (b, cq + kv)),
            pl.BlockSpec((DEC_SEQ, hd), lambda b, kv: (b, ck + kv)),
            pl.BlockSpec((DEC_SEQ, hd), lambda b, kv: (b, cv + kv)),
            cache, cache, table, table, vec, vec,
        ],
        out_specs=pl.BlockSpec((DEC_SEQ, gw), lambda b, kv: (b, kv)),
        out_shape=jax.ShapeDtypeStruct((DEC_BATCH * DEC_SEQ, GQA_QW), BF16),
        scratch_shapes=[pltpu.VMEM((N_KEYS, hd), BF16), pltpu.VMEM((N_KEYS, 2 * hd), BF16)],
        compiler_params=_params(("parallel", "parallel")),
        name="lat_gqa",
    )(qkv, qkv, qkv, cache_k, cache_v, cos, sin, q_g, k_g)


def _lat_diff_kernel(q_ref, k_ref, v_ref, kc_ref, vc_ref, cos_ref, sin_ref, lam_ref, dg_ref,
                     o_ref, kall_ref, vall_ref, *, lambda_init):
    half = DIFF_QK_DIM // 4
    kf = _rope(k_ref[...], cos_ref[...], sin_ref[...], half)
    _fill_keys(kall_ref, vall_ref, kf, v_ref, kc_ref, vc_ref)
    lam = _diff_lambda(lam_ref, lambda_init)
    lane = lax.broadcasted_iota(jnp.int32, (ATT_ROWS, HEAD_DIM), 1)
    for s in range(DEC_SEQ // ATT_ROWS):
        rows = slice(s * ATT_ROWS, (s + 1) * ATT_ROWS)
        qf = _rope(q_ref[rows, :], cos_ref[rows, :], sin_ref[rows, :], half) * DIFF_QSCALE
        q0 = jnp.where(lane < DIFF_QK_DIM, qf, 0.0).astype(BF16)
        q1 = jnp.where(lane >= DIFF_QK_DIM, qf, 0.0).astype(BF16)
        od = (_weighted(_scores_exp(q0, kall_ref[...]), vall_ref[...])
              - lam * _weighted(_scores_exp(q1, kall_ref[...]), vall_ref[...]))
        od = _rms(od, dg_ref[...]) * (1.0 - lambda_init)
        o_ref[rows, :] = od.astype(BF16)


def _lat_diff(qkv, cache_k, cache_v, cos, sin, lam_vec, diff_g, layer, lambda_init):
    hd = HEAD_DIM
    cq, ck, cv = C_D_Q // hd, C_D_K // hd, C_D_V // hd
    cache = pl.BlockSpec((None, None, PAST_LEN, hd), lambda b, h: (b, layer, 0, h))
    table = pl.BlockSpec((DEC_SEQ, hd), lambda b, h: (0, 0))
    return pl.pallas_call(
        functools.partial(_lat_diff_kernel, lambda_init=lambda_init),
        grid=(DEC_BATCH, DIFF_HEADS),
        in_specs=[
            pl.BlockSpec((DEC_SEQ, hd), lambda b, h: (b, cq + h)),
            pl.BlockSpec((DEC_SEQ, hd), lambda b, h: (b, ck + h)),
            pl.BlockSpec((DEC_SEQ, hd), lambda b, h: (b, cv + h)),
            cache, cache, table, table,
            pl.BlockSpec((4, DIFF_QK_DIM), lambda b, h: (0, 0)),
            pl.BlockSpec((1, hd), lambda b, h: (0, 0)),
        ],
        out_specs=pl.BlockSpec((DEC_SEQ, hd), lambda b, h: (b, h)),
        out_shape=jax.ShapeDtypeStruct((DEC_BATCH * DEC_SEQ, DIFF_W), BF16),
        scratch_shapes=[pltpu.VMEM((N_KEYS, hd), BF16), pltpu.VMEM((N_KEYS, 2 * hd), BF16)],
        compiler_params=_params(("parallel", "parallel")),
        name="lat_diff",
    )(qkv, qkv, qkv, cache_k, cache_v, cos, sin, lam_vec, diff_g)


def _outproj_kernel(*refs, n_o):
    o_refs = refs[:n_o]
    w_ref, x_ref, g_ref, gt_ref, out_ref = refs[n_o:]
    y = None
    off = 0
    for o_ref in o_refs:
        wd = o_ref.shape[1]
        part = _mm(o_ref[...], w_ref[off:off + wd, :])
        y = part if y is None else y + part
        off += wd
    out_ref[...] = x_ref[...] + gt_ref[...] * _rms(y, g_ref[...])


def _outproj(o_parts, w_out, x, g_post, mod_l, rows_per_batch, first_row):
    m = x.shape[0]
    row = _row_fn(TM_OUT, rows_per_batch, first_row)
    full = pl.BlockSpec((TM_OUT, D_MODEL), lambda i: (i, 0))
    return pl.pallas_call(
        functools.partial(_outproj_kernel, n_o=len(o_parts)),
        grid=(m // TM_OUT,),
        in_specs=[pl.BlockSpec((TM_OUT, o.shape[1]), lambda i: (i, 0)) for o in o_parts] + [
            pl.BlockSpec((D_MIX, D_MODEL), lambda i: (0, 0)),
            full,
            pl.BlockSpec((1, D_MODEL), lambda i: (0, 0)),
            _mod_spec(2, row),
        ],
        out_specs=full,
        out_shape=jax.ShapeDtypeStruct((m, D_MODEL), F32),
        compiler_params=_params(("parallel",)),
        name="outproj",
    )(*o_parts, w_out, x, g_post, mod_l)


def _mlp_kernel(x_ref, gpre_ref, sc_ref, sh_ref, wup_ref, wdn_ref, gpost_ref, gt_ref, *rest,
                n_cast, n_flat):
    n_side = n_cast + n_flat
    cast_in, flat_in = rest[:n_cast], rest[n_cast:n_side]
    out_ref = rest[n_side]
    cast_out = rest[n_side + 1:n_side + 1 + n_cast]
    flat_out = rest[n_side + 1 + n_cast:2 * n_side + 1]
    h_ref, acc_ref, gm_ref = rest[2 * n_side + 1:]
    k = pl.program_id(1)

    @pl.when(k == 0)
    def _():
        _modulated_norm(h_ref, x_ref, gpre_ref, sc_ref, sh_ref, gm_ref)
        acc_ref[...] = jnp.zeros_like(acc_ref)

    u = _mm(h_ref[...], wup_ref[...])
    a = jnp.square(jnp.maximum(u, 0.0)).astype(BF16)
    acc_ref[...] += _mm(a, wdn_ref[...])

    for src, dst in zip(cast_in, cast_out):
        dst[...] = src[...].astype(BF16)
    for src, dst in zip(flat_in, flat_out):
        for h in range(src.shape[1]):
            dst[:, h * HEAD_DIM:(h + 1) * HEAD_DIM] = src[:, h, :]

    @pl.when(k == pl.num_programs(1) - 1)
    def _():
        _gated_norm_residual(out_ref, x_ref, acc_ref, gpost_ref, gt_ref, gm_ref)


def _mlp(x, g_pre, g_post, mod_l, w_up, w_down, rows_per_batch, first_row, cast=None,
         flatten=None):
    m = x.shape[0]
    row = _row_fn(TM_MLP, rows_per_batch, first_row)
    full = pl.BlockSpec((TM_MLP, D_MODEL), lambda i, k: (i, 0))
    vec = pl.BlockSpec((1, D_MODEL), lambda i, k: (0, 0))
    n_k = D_FF // TF_MLP
    n_steps = (m // TM_MLP) * n_k
    cast_in_specs, cast_out_specs, cast_shapes, cast_args = [], [], [], []
    if cast is not None:
        stacks, layer = cast
        for w in stacks:
            rows, cols = w.shape[1], w.shape[2]
            rb = rows // n_steps
            cast_in_specs.append(
                pl.BlockSpec((None, rb, cols), lambda i, k: (layer, i * n_k + k, 0)))
            cast_out_specs.append(pl.BlockSpec((rb, cols), lambda i, k: (i * n_k + k, 0)))
            cast_shapes.append(jax.ShapeDtypeStruct((rows, cols), BF16))
            cast_args.append(w)
    flat_in_specs, flat_out_specs, flat_shapes = [], [], []
    flatten = flatten or []
    if flatten:
        tok = DEC_BATCH * DEPTH * PAST_LEN // n_steps
        per = PAST_LEN // tok

        def where(i, k):
            s = i * n_k + k
            return s // (DEPTH * per), (s // per) % DEPTH, s % per

        for a in flatten:
            nh = a.shape[3]
            flat_in_specs.append(pl.BlockSpec((None, None, tok, nh, HEAD_DIM),
                                              lambda i, k: (*where(i, k), 0, 0)))
            flat_out_specs.append(pl.BlockSpec((None, None, tok, nh * HEAD_DIM),
                                               lambda i, k: (*where(i, k), 0)))
            flat_shapes.append(
                jax.ShapeDtypeStruct((DEC_BATCH, DEPTH, PAST_LEN, nh * HEAD_DIM), a.dtype))
    n_cast = len(cast_args)
    res = pl.pallas_call(
        functools.partial(_mlp_kernel, n_cast=n_cast, n_flat=len(flatten)),
        grid=(m // TM_MLP, n_k),
        in_specs=[
            full, vec, _mod_spec(4, row), _mod_spec(3, row),
            pl.BlockSpec((D_MODEL, TF_MLP), lambda i, k: (0, k)),
            pl.BlockSpec((TF_MLP, D_MODEL), lambda i, k: (k, 0)),
            vec, _mod_spec(5, row),
        ] + cast_in_specs + flat_in_specs,
        out_specs=[full] + cast_out_specs + flat_out_specs,
        out_shape=[jax.ShapeDtypeStruct((m, D_MODEL), F32)] + cast_shapes + flat_shapes,
        scratch_shapes=[pltpu.VMEM((TM_MLP, D_MODEL), BF16), pltpu.VMEM((TM_MLP, D_MODEL), F32),
                        pltpu.VMEM((1, D_MODEL), F32)],
        compiler_params=_params(("parallel", "arbitrary")),
        name="mlp",
    )(x, g_pre, mod_l, mod_l, w_up, w_down, g_post, mod_l, *cast_args, *flatten)
    return res[0], res[1:1 + n_cast], res[1 + n_cast:]


def _rope_tables(half, n_rep):
    t = jnp.arange(DEC_SEQ)
    inv = ROPE_THETA ** (-jnp.arange(half, dtype=F32) / half)

    def cs(pos):
        ang = pos.astype(F32)[:, None] * inv[None, :]
        c, s = jnp.cos(ang), jnp.sin(ang)
        return jnp.concatenate([c, c], axis=-1), jnp.concatenate([-s, s], axis=-1)

    cr, sr = cs(t // GRID_W)
    cc, sc = cs(t % GRID_W)
    return (jnp.concatenate([cr, cc] * n_rep, axis=-1),
            jnp.concatenate([sr, sc] * n_rep, axis=-1))


def _na_bias(rpb):
    qc = np.arange(GRID_W)[:, None]
    kc = np.arange(GRID_W)[None, :]
    ws = np.clip(qc - NA_KW // 2, 0, GRID_W - NA_KW)
    valid = (kc >= ws) & (kc < ws + NA_KW)
    dcol = np.clip(kc - qc + NA_KW - 1, 0, 2 * NA_KW - 2)
    n_dcol = 2 * NA_KW - 1
    onehot = (dcol.reshape(-1)[None, :] == np.arange(n_dcol)[:, None]).astype(np.float32)
    t = jnp.einsum("lhdc,cq->lhdq", rpb.astype(F32), jnp.asarray(onehot),
                   precision=lax.Precision.HIGHEST)
    t = t.reshape(DEPTH, NA_HEADS, 2 * NA_KH - 1, GRID_W, GRID_W) * LOG2E
    t = jnp.where(jnp.asarray(valid)[None, None, None], t, NEG_BIG)
    masked = jnp.full((DEPTH, NA_HEADS, 1, GRID_W, GRID_W), NEG_BIG, F32)
    lo, hi = NA_KH // 2 - 1, NA_KH + NA_KH // 2 - 2
    both = jnp.concatenate([t[:, :, :-1], t[:, :, 1:]], axis=-1)
    second = jnp.concatenate([masked, t[:, :, lo:lo + 1]], axis=-1)
    first = jnp.concatenate([t[:, :, hi:hi + 1], masked], axis=-1)
    none = jnp.concatenate([masked, masked], axis=-1)
    return jnp.concatenate([both, second, first, none], axis=2)


def _na_pair_index(r, key_row):
    r0 = min(max(r - NA_KH // 2, 0), GRID_ROWS - NA_KH)
    in_a = r0 <= key_row < r0 + NA_KH
    in_b = r0 <= key_row + 1 < r0 + NA_KH
    d = key_row - r + NA_KH - 1
    if in_a and in_b:
        return d
    if in_b:
        assert d + 1 == NA_KH // 2 - 1
        return NA_PAIRS_BOTH
    if in_a:
        assert d == NA_KH + NA_KH // 2 - 2
        return NA_PAIRS_BOTH + 1
    return NA_PAIRS_BOTH + 2


def kernel(x_prompt, x_sample, c, cache_na_k, cache_na_v, cache_gqa_k, cache_gqa_v,
           cache_diff_k, cache_diff_v, c_ctx, w_ada, b_ada, norm_g, w_in, w_out, na_rpb,
           gqa_q_g, gqa_k_g, diff_lam, diff_g, w_up, w_down):
    np_rows = BATCH * SEQ
    ns_rows = DEC_BATCH * DEC_SEQ
    xp = x_prompt.reshape(np_rows, D_MODEL)
    xs = x_sample.reshape(ns_rows, D_MODEL)

    cv = jnp.concatenate(
        [c_ctx[None, :], c, jnp.zeros((MOD_ROWS - 1 - DEC_BATCH, D_MODEL), F32)], axis=0)
    mod = _modulation(cv, w_ada, b_ada).reshape(DEPTH, MOD_ROWS, N_MOD, 1, D_MODEL)

    w_stacks = [w_in, w_out, w_up, w_down]
    w_in_b, w_out_b, w_up_b, w_down_b = [w[0].astype(BF16) for w in w_stacks]

    caches = [cache_na_k, cache_na_v, cache_gqa_k, cache_gqa_v, cache_diff_k, cache_diff_v]

    cos_g, sin_g = _rope_tables(HEAD_DIM // 4, 1)
    cos_d, sin_d = _rope_tables(DIFF_QK_DIM // 4, 2)
    na_bias = _na_bias(na_rpb)

    new_kv = None
    for l in range(DEPTH):
        lambda_init = 0.8 - 0.6 * math.exp(-0.3 * l)
        mod_l = mod[l]
        g = norm_g[l].reshape(4, 1, D_MODEL)
        q_g = gqa_q_g[l].reshape(1, HEAD_DIM)
        k_g = gqa_k_g[l].reshape(1, HEAD_DIM)
        d_g = diff_g[l].reshape(1, HEAD_DIM)
        lam_vec = diff_lam[l]

        qkv_p = _inproj(xp, g[0], mod_l, w_in_b, None, 0)
        o_p, *new_kv = _ctx_attn(qkv_p, q_g, k_g, lam_vec, d_g, new_kv, l, lambda_init)
        xp = _outproj([o_p], w_out_b, xp, g[1], mod_l, None, 0)
        xp, _, flat = _mlp(xp, g[2], g[3], mod_l, w_up_b, w_down_b, None, 0,
                           flatten=caches if l == 0 else None)
        if l == 0:
            c_na_k, c_na_v, c_g_k, c_g_v, c_d_k, c_d_v = flat

        qkv_s = _inproj(xs, g[0], mod_l, w_in_b, DEC_SEQ, 1)
        o_na = _lat_na(qkv_s, c_na_k, c_na_v, na_bias, l)
        o_gqa = _lat_gqa(qkv_s, c_g_k, c_g_v, cos_g, sin_g, q_g, k_g, l)
        o_diff = _lat_diff(qkv_s, c_d_k, c_d_v, cos_d, sin_d, lam_vec, d_g, l, lambda_init)
        xs = _outproj([o_na, o_gqa, o_diff], w_out_b, xs, g[1], mod_l, DEC_SEQ, 1)
        cast = (w_stacks, l + 1) if l + 1 < DEPTH else None
        xs, nxt, _ = _mlp(xs, g[2], g[3], mod_l, w_up_b, w_down_b, DEC_SEQ, 1, cast)
        if nxt:
            w_in_b, w_out_b, w_up_b, w_down_b = nxt

    return (xp.reshape(BATCH, SEQ, D_MODEL), xs.reshape(DEC_BATCH, DEC_SEQ, D_MODEL), *new_kv)
```

```python
import functools
import math

import jax
import jax.numpy as jnp
import numpy as np
from jax import lax
from jax.experimental import pallas as pl
from jax.experimental.pallas import tpu as pltpu

D_MODEL = 2048
BATCH = 16
SEQ = 256
DEPTH = 4
DEC_BATCH = 8
DEC_SEQ = 1024
PAST_LEN = 256
GRID_W = 64
GRID_ROWS = DEC_SEQ // GRID_W
HEAD_DIM = 128
NA_HEADS = 4
GQA_Q_HEADS = 8
GQA_KV_HEADS = 2
GQA_GROUP = GQA_Q_HEADS // GQA_KV_HEADS
DIFF_HEADS = 4
DIFF_QK_DIM = HEAD_DIM // 2
NA_KH = 8
NA_KW = 16
D_FF = 4 * D_MODEL
ROPE_THETA = 10000.0
EPS = 1e-6
N_MOD = 6
NEG_BIG = -1e30

NA_W = NA_HEADS * HEAD_DIM
GQA_QW = GQA_Q_HEADS * HEAD_DIM
GQA_KVW = GQA_KV_HEADS * HEAD_DIM
DIFF_W = DIFF_HEADS * HEAD_DIM
D_IN = 3 * NA_W + GQA_QW + 2 * GQA_KVW + 3 * DIFF_W
D_MIX = NA_W + GQA_QW + DIFF_W
C_NA_Q, C_NA_K, C_NA_V = 0, NA_W, 2 * NA_W
C_G_Q = 3 * NA_W
C_G_K = C_G_Q + GQA_QW
C_G_V = C_G_K + GQA_KVW
C_D_Q = C_G_V + GQA_KVW
C_D_K = C_D_Q + DIFF_W
C_D_V = C_D_K + DIFF_W

KV_HEADS = (NA_HEADS, NA_HEADS, GQA_KV_HEADS, GQA_KV_HEADS, DIFF_HEADS, DIFF_HEADS)
N_KEYS = DEC_SEQ + PAST_LEN
LOG2E = 1.4426950408889634
QSCALE = HEAD_DIM ** -0.5 * LOG2E
DIFF_QSCALE = DIFF_QK_DIM ** -0.5 * LOG2E

NA_QROWS = 4
NA_QTOK = NA_QROWS * GRID_W
NA_GROUPS = GRID_ROWS // NA_QROWS
NA_SLAB_ROWS = 12
NA_SLAB = NA_SLAB_ROWS * GRID_W
NA_NKEY = NA_SLAB + PAST_LEN
NA_SLAB_START = (0, 0, 4, 4)
NA_PAIRS_BOTH = 2 * NA_KH - 2
MOD_ROWS = 16

F32 = jnp.float32
BF16 = jnp.bfloat16

VMEM_LIMIT = 52 * 1024 * 1024

TM_IN = 1024
TN_IN = 1536
TM_OUT = 512
TM_MLP = 512
TF_MLP = 1024
TN_ADA = 1024
ATT_ROWS = 256
NORM_ROWS = 128

def _params(sem):
    return pltpu.CompilerParams(dimension_semantics=sem, vmem_limit_bytes=VMEM_LIMIT)


def _rms(x, g):
    ms = jnp.mean(x * x, axis=-1, keepdims=True)
    return x * lax.rsqrt(ms + EPS) * g


def _nt(a, b):
    return lax.dot_general(a, b, (((1,), (1,)), ((), ())), preferred_element_type=F32)


def _mm(a, b):
    return jnp.dot(a, b, preferred_element_type=F32)


def _scores_exp(q, k, bias=None):
    s = _nt(q, k)
    if bias is not None:
        s = s + bias
    return jnp.exp2(s - jnp.max(s, axis=-1, keepdims=True)).astype(BF16)


def _weighted(e, v_ones):
    oa = _mm(e, v_ones)
    d = v_ones.shape[1] // 2
    return oa[:, :d] / oa[:, d:]


def _with_ones(v):
    return jnp.concatenate([v, jnp.ones_like(v)], axis=1)


def _rope(x, cos, sin_signed, half):
    n = x.shape[-1]
    lane = lax.broadcasted_iota(jnp.int32, x.shape, 1)
    first = (lane % (2 * half)) < half
    rot = jnp.where(first, pltpu.roll(x, n - half, 1), pltpu.roll(x, half, 1))
    return x * cos + rot * sin_signed


def _diff_lambda(lam_ref, lambda_init):
    lf = lam_ref[...]
    a = jnp.sum(lf[0:1] * lf[1:2], axis=-1, keepdims=True)
    b = jnp.sum(lf[2:3] * lf[3:4], axis=-1, keepdims=True)
    return jnp.exp(a) - jnp.exp(b) + lambda_init


def _mod_kernel(cv_ref, w_ref, b_ref, o_ref):
    cv = cv_ref[...]
    s = cv / (1.0 + jnp.exp(-cv))
    o_ref[...] = _mm(s.astype(BF16), w_ref[...].astype(BF16)) + b_ref[...]


def _modulation(cv, w_ada, b_ada):
    n = N_MOD * D_MODEL
    return pl.pallas_call(
        _mod_kernel,
        grid=(DEPTH, n // TN_ADA),
        in_specs=[
            pl.BlockSpec((MOD_ROWS, D_MODEL), lambda l, j: (0, 0)),
            pl.BlockSpec((None, D_MODEL, TN_ADA), lambda l, j: (l, 0, j)),
            pl.BlockSpec((None, 1, TN_ADA), lambda l, j: (l, 0, j)),
        ],
        out_specs=pl.BlockSpec((None, MOD_ROWS, TN_ADA), lambda l, j: (l, 0, j)),
        out_shape=jax.ShapeDtypeStruct((DEPTH, MOD_ROWS, n), F32),
        compiler_params=_params(("parallel", "parallel")),
        name="modulation",
    )(cv, w_ada, b_ada.reshape(DEPTH, 1, n))


def _mod_spec(chunk, row_fn):
    return pl.BlockSpec((None, None, 1, D_MODEL), lambda i, *_: (row_fn(i), chunk, 0, 0))


def _row_fn(tm, rows_per_batch, first_row):
    if rows_per_batch is None:
        return lambda i: first_row
    return lambda i: first_row + (i * tm) // rows_per_batch


def _row_chunks(n_rows, body):
    def step(c, carry):
        body(pl.ds(pl.multiple_of(c * NORM_ROWS, NORM_ROWS), NORM_ROWS))
        return carry

    lax.fori_loop(0, n_rows // NORM_ROWS, step, 0)


def _modulated_norm(h_ref, x_ref, g_ref, sc_ref, sh_ref, gm_ref):
    gm_ref[...] = g_ref[...] * (1.0 + sc_ref[...])

    def body(rows):
        x = x_ref[rows, :]
        r = lax.rsqrt(jnp.mean(x * x, axis=-1, keepdims=True) + EPS)
        h_ref[rows, :] = (x * r * gm_ref[...] + sh_ref[...]).astype(BF16)

    _row_chunks(x_ref.shape[0], body)


def _gated_norm_residual(out_ref, x_ref, y_ref, g_ref, gt_ref, gm_ref):
    gm_ref[...] = gt_ref[...] * g_ref[...]

    def body(rows):
        y = y_ref[rows, :]
        r = lax.rsqrt(jnp.mean(y * y, axis=-1, keepdims=True) + EPS)
        out_ref[rows, :] = x_ref[rows, :] + y * r * gm_ref[...]

    _row_chunks(x_ref.shape[0], body)


def _inproj_kernel(x_ref, g_ref, sc_ref, sh_ref, w_ref, o_ref, h_ref, gm_ref):
    @pl.when(pl.program_id(1) == 0)
    def _():
        _modulated_norm(h_ref, x_ref, g_ref, sc_ref, sh_ref, gm_ref)

    o_ref[...] = _mm(h_ref[...], w_ref[...])


def _inproj(x, g_pre, mod_l, w_in, rows_per_batch, first_row):
    m = x.shape[0]
    row = _row_fn(TM_IN, rows_per_batch, first_row)
    return pl.pallas_call(
        _inproj_kernel,
        grid=(m // TM_IN, D_IN // TN_IN),
        in_specs=[
            pl.BlockSpec((TM_IN, D_MODEL), lambda i, j: (i, 0)),
            pl.BlockSpec((1, D_MODEL), lambda i, j: (0, 0)),
            _mod_spec(1, row),
            _mod_spec(0, row),
            pl.BlockSpec((D_MODEL, TN_IN), lambda i, j: (0, j)),
        ],
        out_specs=pl.BlockSpec((TM_IN, TN_IN), lambda i, j: (i, j)),
        out_shape=jax.ShapeDtypeStruct((m, D_IN), F32),
        scratch_shapes=[pltpu.VMEM((TM_IN, D_MODEL), BF16), pltpu.VMEM((1, D_MODEL), F32)],
        compiler_params=_params(("parallel", "arbitrary")),
        name="inproj",
    )(x, g_pre, mod_l, mod_l, w_in)


def _ctx_attn_kernel(qkv_ref, qg_ref, kg_ref, lam_ref, dg_ref, *rest, lambda_init, first):
    if first:
        o_ref, *kv_refs = rest
        for ref in kv_refs:
            ref[1:] = jnp.zeros((DEPTH - 1,) + ref.shape[1:], F32)
        nak_ref, nav_ref, gk_ref, gv_ref, dk_ref, dv_ref = [ref.at[0] for ref in kv_refs]
    else:
        o_ref, nak_ref, nav_ref, gk_ref, gv_ref, dk_ref, dv_ref = rest[len(KV_HEADS):]
    hd = HEAD_DIM

    def cols(c0, h):
        return qkv_ref[:, c0 + h * hd:c0 + (h + 1) * hd]

    for ref, c0 in ((nak_ref, C_NA_K), (nav_ref, C_NA_V), (gv_ref, C_G_V),
                    (dk_ref, C_D_K), (dv_ref, C_D_V)):
        nh = ref.shape[1]
        ref[...] = qkv_ref[:, c0:c0 + nh * hd].reshape(SEQ, nh, hd)

    for h in range(NA_HEADS):
        q = (cols(C_NA_Q, h) * QSCALE).astype(BF16)
        k = cols(C_NA_K, h).astype(BF16)
        v1 = _with_ones(cols(C_NA_V, h).astype(BF16))
        o_ref[:, h * hd:(h + 1) * hd] = _weighted(_scores_exp(q, k), v1).astype(BF16)

    kfs = [_rms(cols(C_G_K, kv), kg_ref[...]) for kv in range(GQA_KV_HEADS)]
    gk_ref[...] = jnp.concatenate(kfs, axis=1).reshape(SEQ, GQA_KV_HEADS, hd)
    for kv in range(GQA_KV_HEADS):
        k = kfs[kv].astype(BF16)
        v1 = _with_ones(cols(C_G_V, kv).astype(BF16))
        for g in range(GQA_GROUP):
            hq = kv * GQA_GROUP + g
            q = (_rms(cols(C_G_Q, hq), qg_ref[...]) * QSCALE).astype(BF16)
            o = _weighted(_scores_exp(q, k), v1)
            o_ref[:, NA_W + hq * hd:NA_W + (hq + 1) * hd] = o.astype(BF16)

    lam = _diff_lambda(lam_ref, lambda_init)
    lane = lax.broadcasted_iota(jnp.int32, (SEQ, hd), 1)
    for h in range(DIFF_HEADS):
        qf = cols(C_D_Q, h) * DIFF_QSCALE
        k = cols(C_D_K, h).astype(BF16)
        v1 = _with_ones(cols(C_D_V, h).astype(BF16))
        q0 = jnp.where(lane < DIFF_QK_DIM, qf, 0.0).astype(BF16)
        q1 = jnp.where(lane >= DIFF_QK_DIM, qf, 0.0).astype(BF16)
        od = _weighted(_scores_exp(q0, k), v1) - lam * _weighted(_scores_exp(q1, k), v1)
        od = _rms(od, dg_ref[...]) * (1.0 - lambda_init)
        c0 = NA_W + GQA_QW + h * hd
        o_ref[:, c0:c0 + hd] = od.astype(BF16)


def _ctx_attn(qkv, q_g, k_g, lam_vec, diff_g, kv_bufs, layer, lambda_init):
    n = qkv.shape[0]
    vec = pl.BlockSpec((1, HEAD_DIM), lambda b: (0, 0))
    n_in = 5
    first = kv_bufs is None

    def rows(w):
        return pl.BlockSpec((SEQ, w), lambda b: (b, 0))

    def heads(nh):
        if first:
            return pl.BlockSpec((None, DEPTH, SEQ, nh, HEAD_DIM), lambda b: (b, 0, 0, 0, 0))
        return pl.BlockSpec((None, None, SEQ, nh, HEAD_DIM), lambda b: (b, layer, 0, 0, 0))

    bufs = [] if first else list(kv_bufs)
    return pl.pallas_call(
        functools.partial(_ctx_attn_kernel, lambda_init=lambda_init, first=first),
        grid=(n // SEQ,),
        in_specs=[rows(D_IN), vec, vec,
                  pl.BlockSpec((4, DIFF_QK_DIM), lambda b: (0, 0)), vec]
                 + [pl.BlockSpec(memory_space=pl.ANY)] * len(bufs),
        out_specs=[rows(D_MIX)] + [heads(nh) for nh in KV_HEADS],
        out_shape=[jax.ShapeDtypeStruct((n, D_MIX), BF16)]
                  + [jax.ShapeDtypeStruct((BATCH, DEPTH, SEQ, nh, HEAD_DIM), F32)
                     for nh in KV_HEADS],
        input_output_aliases={n_in + i: 1 + i for i in range(len(bufs))},
        compiler_params=_params(("parallel",)),
        name="ctx_attn",
    )(qkv, q_g, k_g, lam_vec, diff_g, *bufs)


def _lat_na_kernel(q_ref, k_ref, v_ref, kc_ref, vc_ref, bias_ref, o_ref, kbig_ref, vbig_ref):
    lat0, lat1 = PAST_LEN, PAST_LEN + DEC_SEQ
    kc = kc_ref[...].astype(BF16)
    vc = _with_ones(vc_ref[...].astype(BF16))
    kbig_ref[0:lat0, :] = kc
    kbig_ref[lat0:lat1, :] = k_ref[...].astype(BF16)
    kbig_ref[lat1:, :] = kc
    vbig_ref[0:lat0, :] = vc
    vbig_ref[lat0:lat1, :] = _with_ones(v_ref[...].astype(BF16))
    vbig_ref[lat1:, :] = vc
    for j in range(NA_GROUPS):
        s0 = NA_SLAB_START[j]
        w0 = 0 if s0 == 0 else PAST_LEN + s0 * GRID_W
        q = (q_ref[j * NA_QTOK:(j + 1) * NA_QTOK, :] * QSCALE).astype(BF16)
        ctx0 = jnp.zeros((GRID_W, PAST_LEN), F32)
        bias_rows = []
        for r in range(j * NA_QROWS, (j + 1) * NA_QROWS):
            slab = [bias_ref[_na_pair_index(r, kr)] for kr in range(s0, s0 + NA_SLAB_ROWS, 2)]
            bias_rows.append(jnp.concatenate([ctx0] + slab if s0 == 0 else slab + [ctx0], axis=1))
        bias = jnp.concatenate(bias_rows, axis=0)
        e = _scores_exp(q, kbig_ref[w0:w0 + NA_NKEY, :], bias)
        o = _weighted(e, vbig_ref[w0:w0 + NA_NKEY, :])
        o_ref[j * NA_QTOK:(j + 1) * NA_QTOK, :] = o.astype(BF16)


def _lat_na(qkv, cache_k, cache_v, bias, layer):
    hd = HEAD_DIM
    cq, ck, cv = C_NA_Q // hd, C_NA_K // hd, C_NA_V // hd
    cache = pl.BlockSpec((None, None, PAST_LEN, hd), lambda h, b: (b, layer, 0, h))
    nbig = DEC_SEQ + 2 * PAST_LEN
    return pl.pallas_call(
        _lat_na_kernel,
        grid=(NA_HEADS, DEC_BATCH),
        in_specs=[
            pl.BlockSpec((DEC_SEQ, hd), lambda h, b: (b, cq + h)),
            pl.BlockSpec((DEC_SEQ, hd), lambda h, b: (b, ck + h)),
            pl.BlockSpec((DEC_SEQ, hd), lambda h, b: (b, cv + h)),
            cache, cache,
            pl.BlockSpec((None, None, NA_PAIRS_BOTH + 3, GRID_W, 2 * GRID_W),
                         lambda h, b: (layer, h, 0, 0, 0)),
        ],
        out_specs=pl.BlockSpec((DEC_SEQ, hd), lambda h, b: (b, h)),
        out_shape=jax.ShapeDtypeStruct((DEC_BATCH * DEC_SEQ, NA_W), BF16),
        scratch_shapes=[pltpu.VMEM((nbig, hd), BF16), pltpu.VMEM((nbig, 2 * hd), BF16)],
        compiler_params=_params(("parallel", "parallel")),
        name="lat_na",
    )(qkv, qkv, qkv, cache_k, cache_v, bias)


def _fill_keys(kall_ref, vall_ref, k_lat, v_ref, kc_ref, vc_ref):
    kall_ref[0:DEC_SEQ, :] = k_lat.astype(BF16)
    kall_ref[DEC_SEQ:N_KEYS, :] = kc_ref[...].astype(BF16)
    vall_ref[0:DEC_SEQ, :] = _with_ones(v_ref[...].astype(BF16))
    vall_ref[DEC_SEQ:N_KEYS, :] = _with_ones(vc_ref[...].astype(BF16))


def _lat_gqa_kernel(q_ref, k_ref, v_ref, kc_ref, vc_ref, cos_ref, sin_ref, qg_ref, kg_ref,
                    o_ref, kall_ref, vall_ref):
    hd = HEAD_DIM
    half = HEAD_DIM // 4
    kf = _rope(_rms(k_ref[...], kg_ref[...]), cos_ref[...], sin_ref[...], half)
    _fill_keys(kall_ref, vall_ref, kf, v_ref, kc_ref, vc_ref)
    for s in range(DEC_SEQ // ATT_ROWS):
        rows = slice(s * ATT_ROWS, (s + 1) * ATT_ROWS)
        cos = cos_ref[rows, :]
        sin = sin_ref[rows, :]
        for g in range(GQA_GROUP):
            qf = _rms(q_ref[rows, g * hd:(g + 1) * hd], qg_ref[...])
            q = (_rope(qf, cos, sin, half) * QSCALE).astype(BF16)
            o = _weighted(_scores_exp(q, kall_ref[...]), vall_ref[...])
            o_ref[rows, g * hd:(g + 1) * hd] = o.astype(BF16)


def _lat_gqa(qkv, cache_k, cache_v, cos, sin, q_g, k_g, layer):
    hd = HEAD_DIM
    gw = GQA_GROUP * hd
    cq, ck, cv = C_G_Q // gw, C_G_K // hd, C_G_V // hd
    cache = pl.BlockSpec((None, None, PAST_LEN, hd), lambda b, kv: (b, layer, 0, kv))
    table = pl.BlockSpec((DEC_SEQ, hd), lambda b, kv: (0, 0))
    vec = pl.BlockSpec((1, hd), lambda b, kv: (0, 0))
    return pl.pallas_call(
        _lat_gqa_kernel,
        grid=(DEC_BATCH, GQA_KV_HEADS),
        in_specs=[
            pl.BlockSpec((DEC_SEQ, gw), lambda b, kv: (b, cq + kv)),
            pl.BlockSpec((DEC_SEQ, hd), lambda b, kv: (b, ck + kv)),
            pl.BlockSpec((DEC_SEQ, hd), lambda b, kv: (b, cv + kv)),
            cache, cache, table, table, vec, vec,
        ],
        out_specs=pl.BlockSpec((DEC_SEQ, gw), lambda b, kv: (b, kv)),
        out_shape=jax.ShapeDtypeStruct((DEC_BATCH * DEC_SEQ, GQA_QW), BF16),
        scratch_shapes=[pltpu.VMEM((N_KEYS, hd), BF16), pltpu.VMEM((N_KEYS, 2 * hd), BF16)],
        compiler_params=_params(("parallel", "parallel")),
        name="lat_gqa",
    )(qkv, qkv, qkv, cache_k, cache_v, cos, sin, q_g, k_g)


def _lat_diff_kernel(q_ref, k_ref, v_ref, kc_ref, vc_ref, cos_ref, sin_ref, lam_ref, dg_ref,
                     o_ref, kall_ref, vall_ref, *, lambda_init):
    half = DIFF_QK_DIM // 4
    kf = _rope(k_ref[...], cos_ref[...], sin_ref[...], half)
    _fill_keys(kall_ref, vall_ref, kf, v_ref, kc_ref, vc_ref)
    lam = _diff_lambda(lam_ref, lambda_init)
    lane = lax.broadcasted_iota(jnp.int32, (ATT_ROWS, HEAD_DIM), 1)
    for s in range(DEC_SEQ // ATT_ROWS):
        rows = slice(s * ATT_ROWS, (s + 1) * ATT_ROWS)
        qf = _rope(q_ref[rows, :], cos_ref[rows, :], sin_ref[rows, :], half) * DIFF_QSCALE
        q0 = jnp.where(lane < DIFF_QK_DIM, qf, 0.0).astype(BF16)
        q1 = jnp.where(lane >= DIFF_QK_DIM, qf, 0.0).astype(BF16)
        od = (_weighted(_scores_exp(q0, kall_ref[...]), vall_ref[...])
              - lam * _weighted(_scores_exp(q1, kall_ref[...]), vall_ref[...]))
        od = _rms(od, dg_ref[...]) * (1.0 - lambda_init)
        o_ref[rows, :] = od.astype(BF16)


def _lat_diff(qkv, cache_k, cache_v, cos, sin, lam_vec, diff_g, layer, lambda_init):
    hd = HEAD_DIM
    cq, ck, cv = C_D_Q // hd, C_D_K // hd, C_D_V // hd
    cache = pl.BlockSpec((None, None, PAST_LEN, hd), lambda b, h: (b, layer, 0, h))
    table = pl.BlockSpec((DEC_SEQ, hd), lambda b, h: (0, 0))
    return pl.pallas_call(
        functools.partial(_lat_diff_kernel, lambda_init=lambda_init),
        grid=(DEC_BATCH, DIFF_HEADS),
        in_specs=[
            pl.BlockSpec((DEC_SEQ, hd), lambda b, h: (b, cq + h)),
            pl.BlockSpec((DEC_SEQ, hd), lambda b, h: (b, ck + h)),
            pl.BlockSpec((DEC_SEQ, hd), lambda b, h: (b, cv + h)),
            cache, cache, table, table,
            pl.BlockSpec((4, DIFF_QK_DIM), lambda b, h: (0, 0)),
            pl.BlockSpec((1, hd), lambda b, h: (0, 0)),
        ],
        out_specs=pl.BlockSpec((DEC_SEQ, hd), lambda b, h: (b, h)),
        out_shape=jax.ShapeDtypeStruct((DEC_BATCH * DEC_SEQ, DIFF_W), BF16),
        scratch_shapes=[pltpu.VMEM((N_KEYS, hd), BF16), pltpu.VMEM((N_KEYS, 2 * hd), BF16)],
        compiler_params=_params(("parallel", "parallel")),
        name="lat_diff",
    )(qkv, qkv, qkv, cache_k, cache_v, cos, sin, lam_vec, diff_g)


def _outproj_kernel(*refs, n_o):
    o_refs = refs[:n_o]
    w_ref, x_ref, g_ref, gt_ref, out_ref = refs[n_o:]
    y = None
    off = 0
    for o_ref in o_refs:
        wd = o_ref.shape[1]
        part = _mm(o_ref[...], w_ref[off:off + wd, :])
        y = part if y is None else y + part
        off += wd
    out_ref[...] = x_ref[...] + gt_ref[...] * _rms(y, g_ref[...])


def _outproj(o_parts, w_out, x, g_post, mod_l, rows_per_batch, first_row):
    m = x.shape[0]
    row = _row_fn(TM_OUT, rows_per_batch, first_row)
    full = pl.BlockSpec((TM_OUT, D_MODEL), lambda i: (i, 0))
    return pl.pallas_call(
        functools.partial(_outproj_kernel, n_o=len(o_parts)),
        grid=(m // TM_OUT,),
        in_specs=[pl.BlockSpec((TM_OUT, o.shape[1]), lambda i: (i, 0)) for o in o_parts] + [
            pl.BlockSpec((D_MIX, D_MODEL), lambda i: (0, 0)),
            full,
            pl.BlockSpec((1, D_MODEL), lambda i: (0, 0)),
            _mod_spec(2, row),
        ],
        out_specs=full,
        out_shape=jax.ShapeDtypeStruct((m, D_MODEL), F32),
        compiler_params=_params(("parallel",)),
        name="outproj",
    )(*o_parts, w_out, x, g_post, mod_l)


def _mlp_kernel(x_ref, gpre_ref, sc_ref, sh_ref, wup_ref, wdn_ref, gpost_ref, gt_ref, *rest,
                n_cast, n_flat):
    n_side = n_cast + n_flat
    cast_in, flat_in = rest[:n_cast], rest[n_cast:n_side]
    out_ref = rest[n_side]
    cast_out = rest[n_side + 1:n_side + 1 + n_cast]
    flat_out = rest[n_side + 1 + n_cast:2 * n_side + 1]
    h_ref, acc_ref, gm_ref = rest[2 * n_side + 1:]
    k = pl.program_id(1)

    @pl.when(k == 0)
    def _():
        _modulated_norm(h_ref, x_ref, gpre_ref, sc_ref, sh_ref, gm_ref)
        acc_ref[...] = jnp.zeros_like(acc_ref)

    u = _mm(h_ref[...], wup_ref[...])
    a = jnp.square(jnp.maximum(u, 0.0)).astype(BF16)
    acc_ref[...] += _mm(a, wdn_ref[...])

    for src, dst in zip(cast_in, cast_out):
        dst[...] = src[...].astype(BF16)
    for src, dst in zip(flat_in, flat_out):
        dst[...] = src[...].reshape(dst.shape)

    @pl.when(k == pl.num_programs(1) - 1)
    def _():
        _gated_norm_residual(out_ref, x_ref, acc_ref, gpost_ref, gt_ref, gm_ref)


def _mlp(x, g_pre, g_post, mod_l, w_up, w_down, rows_per_batch, first_row, cast=None,
         flatten=None):
    m = x.shape[0]
    row = _row_fn(TM_MLP, rows_per_batch, first_row)
    full = pl.BlockSpec((TM_MLP, D_MODEL), lambda i, k: (i, 0))
    vec = pl.BlockSpec((1, D_MODEL), lambda i, k: (0, 0))
    n_k = D_FF // TF_MLP
    n_steps = (m // TM_MLP) * n_k
    cast_in_specs, cast_out_specs, cast_shapes, cast_args = [], [], [], []
    if cast is not None:
        stacks, layer = cast
        for w in stacks:
            rows, cols = w.shape[1], w.shape[2]
            rb = rows // n_steps
            cast_in_specs.append(
                pl.BlockSpec((None, rb, cols), lambda i, k: (layer, i * n_k + k, 0)))
            cast_out_specs.append(pl.BlockSpec((rb, cols), lambda i, k: (i * n_k + k, 0)))
            cast_shapes.append(jax.ShapeDtypeStruct((rows, cols), BF16))
            cast_args.append(w)
    flat_in_specs, flat_out_specs, flat_shapes = [], [], []
    flatten = flatten or []
    if flatten:
        tok = DEC_BATCH * DEPTH * PAST_LEN // n_steps
        per = PAST_LEN // tok

        def where(i, k):
            s = i * n_k + k
            return s // (DEPTH * per), (s // per) % DEPTH, s % per

        for a in flatten:
            nh = a.shape[3]
            flat_in_specs.append(pl.BlockSpec((None, None, tok, nh, HEAD_DIM),
                                              lambda i, k: (*where(i, k), 0, 0)))
            flat_out_specs.append(pl.BlockSpec((None, None, tok, nh * HEAD_DIM),
                                               lambda i, k: (*where(i, k), 0)))
            flat_shapes.append(
                jax.ShapeDtypeStruct((DEC_BATCH, DEPTH, PAST_LEN, nh * HEAD_DIM), a.dtype))
    n_cast = len(cast_args)
    res = pl.pallas_call(
        functools.partial(_mlp_kernel, n_cast=n_cast, n_flat=len(flatten)),
        grid=(m // TM_MLP, n_k),
        in_specs=[
            full, vec, _mod_spec(4, row), _mod_spec(3, row),
            pl.BlockSpec((D_MODEL, TF_MLP), lambda i, k: (0, k)),
            pl.BlockSpec((TF_MLP, D_MODEL), lambda i, k: (k, 0)),
            vec, _mod_spec(5, row),
        ] + cast_in_specs + flat_in_specs,
        out_specs=[full] + cast_out_specs + flat_out_specs,
        out_shape=[jax.ShapeDtypeStruct((m, D_MODEL), F32)] + cast_shapes + flat_shapes,
        scratch_shapes=[pltpu.VMEM((TM_MLP, D_MODEL), BF16), pltpu.VMEM((TM_MLP, D_MODEL), F32),
                        pltpu.VMEM((1, D_MODEL), F32)],
        compiler_params=_params(("parallel", "arbitrary")),
        name="mlp",
    )(x, g_pre, mod_l, mod_l, w_up, w_down, g_post, mod_l, *cast_args, *flatten)
    return res[0], res[1:1 + n_cast], res[1 + n_cast:]


def _rope_tables(half, n_rep):
    t = jnp.arange(DEC_SEQ)
    inv = ROPE_THETA ** (-jnp.arange(half, dtype=F32) / half)

    def cs(pos):
        ang = pos.astype(F32)[:, None] * inv[None, :]
        c, s = jnp.cos(ang), jnp.sin(ang)
        return jnp.concatenate([c, c], axis=-1), jnp.concatenate([-s, s], axis=-1)

    cr, sr = cs(t // GRID_W)
    cc, sc = cs(t % GRID_W)
    return (jnp.concatenate([cr, cc] * n_rep, axis=-1),
            jnp.concatenate([sr, sc] * n_rep, axis=-1))


def _na_bias(rpb):
    qc = np.arange(GRID_W)[:, None]
    kc = np.arange(GRID_W)[None, :]
    ws = np.clip(qc - NA_KW // 2, 0, GRID_W - NA_KW)
    valid = (kc >= ws) & (kc < ws + NA_KW)
    dcol = np.clip(kc - qc + NA_KW - 1, 0, 2 * NA_KW - 2)
    n_dcol = 2 * NA_KW - 1
    onehot = (dcol.reshape(-1)[None, :] == np.arange(n_dcol)[:, None]).astype(np.float32)
    t = jnp.einsum("lhdc,cq->lhdq", rpb.astype(F32), jnp.asarray(onehot),
                   precision=lax.Precision.HIGHEST)
    t = t.reshape(DEPTH, NA_HEADS, 2 * NA_KH - 1, GRID_W, GRID_W) * LOG2E
    t = jnp.where(jnp.asarray(valid)[None, None, None], t, NEG_BIG)
    masked = jnp.full((DEPTH, NA_HEADS, 1, GRID_W, GRID_W), NEG_BIG, F32)
    lo, hi = NA_KH // 2 - 1, NA_KH + NA_KH // 2 - 2
    both = jnp.concatenate([t[:, :, :-1], t[:, :, 1:]], axis=-1)
    second = jnp.concatenate([masked, t[:, :, lo:lo + 1]], axis=-1)
    first = jnp.concatenate([t[:, :, hi:hi + 1], masked], axis=-1)
    none = jnp.concatenate([masked, masked], axis=-1)
    return jnp.concatenate([both, second, first, none], axis=2)


def _na_pair_index(r, key_row):
    r0 = min(max(r - NA_KH // 2, 0), GRID_ROWS - NA_KH)
    in_a = r0 <= key_row < r0 + NA_KH
    in_b = r0 <= key_row + 1 < r0 + NA_KH
    d = key_row - r + NA_KH - 1
    if in_a and in_b:
        return d
    if in_b:
        assert d + 1 == NA_KH // 2 - 1
        return NA_PAIRS_BOTH
    if in_a:
        assert d == NA_KH + NA_KH // 2 - 2
        return NA_PAIRS_BOTH + 1
    return NA_PAIRS_BOTH + 2


def kernel(x_prompt, x_sample, c, cache_na_k, cache_na_v, cache_gqa_k, cache_gqa_v,
           cache_diff_k, cache_diff_v, c_ctx, w_ada, b_ada, norm_g, w_in, w_out, na_rpb,
           gqa_q_g, gqa_k_g, diff_lam, diff_g, w_up, w_down):
    np_rows = BATCH * SEQ
    ns_rows = DEC_BATCH * DEC_SEQ
    xp = x_prompt.reshape(np_rows, D_MODEL)
    xs = x_sample.reshape(ns_rows, D_MODEL)

    cv = jnp.concatenate(
        [c_ctx[None, :], c, jnp.zeros((MOD_ROWS - 1 - DEC_BATCH, D_MODEL), F32)], axis=0)
    mod = _modulation(cv, w_ada, b_ada).reshape(DEPTH, MOD_ROWS, N_MOD, 1, D_MODEL)

    w_stacks = [w_in, w_out, w_up, w_down]
    w_in_b, w_out_b, w_up_b, w_down_b = [w[0].astype(BF16) for w in w_stacks]

    caches = [cache_na_k, cache_na_v, cache_gqa_k, cache_gqa_v, cache_diff_k, cache_diff_v]

    cos_g, sin_g = _rope_tables(HEAD_DIM // 4, 1)
    cos_d, sin_d = _rope_tables(DIFF_QK_DIM // 4, 2)
    na_bias = _na_bias(na_rpb)

    new_kv = None
    for l in range(DEPTH):
        lambda_init = 0.8 - 0.6 * math.exp(-0.3 * l)
        mod_l = mod[l]
        g = norm_g[l].reshape(4, 1, D_MODEL)
        q_g = gqa_q_g[l].reshape(1, HEAD_DIM)
        k_g = gqa_k_g[l].reshape(1, HEAD_DIM)
        d_g = diff_g[l].reshape(1, HEAD_DIM)
        lam_vec = diff_lam[l]

        qkv_p = _inproj(xp, g[0], mod_l, w_in_b, None, 0)
        o_p, *new_kv = _ctx_attn(qkv_p, q_g, k_g, lam_vec, d_g, new_kv, l, lambda_init)
        xp = _outproj([o_p], w_out_b, xp, g[1], mod_l, None, 0)
        xp, _, flat = _mlp(xp, g[2], g[3], mod_l, w_up_b, w_down_b, None, 0,
                           flatten=caches if l == 0 else None)
        if l == 0:
            c_na_k, c_na_v, c_g_k, c_g_v, c_d_k, c_d_v = flat

        qkv_s = _inproj(xs, g[0], mod_l, w_in_b, DEC_SEQ, 1)
        o_na = _lat_na(qkv_s, c_na_k, c_na_v, na_bias, l)
        o_gqa = _lat_gqa(qkv_s, c_g_k, c_g_v, cos_g, sin_g, q_g, k_g, l)
        o_diff = _lat_diff(qkv_s, c_d_k, c_d_v, cos_d, sin_d, lam_vec, d_g, l, lambda_init)
        xs = _outproj([o_na, o_gqa, o_diff], w_out_b, xs, g[1], mod_l, DEC_SEQ, 1)
        cast = (w_stacks, l + 1) if l + 1 < DEPTH else None
        xs, nxt, _ = _mlp(xs, g[2], g[3], mod_l, w_up_b, w_down_b, DEC_SEQ, 1, cast)
        if nxt:
            w_in_b, w_out_b, w_up_b, w_down_b = nxt

    return (xp.reshape(BATCH, SEQ, D_MODEL), xs.reshape(DEC_BATCH, DEC_SEQ, D_MODEL), *new_kv)
```

```python
import functools
import math

import jax
import jax.numpy as jnp
import numpy as np
from jax import lax
from jax.experimental import pallas as pl
from jax.experimental.pallas import tpu as pltpu

D_MODEL = 2048
BATCH = 16
SEQ = 256
DEPTH = 4
DEC_BATCH = 8
DEC_SEQ = 1024
PAST_LEN = 256
GRID_W = 64
GRID_ROWS = DEC_SEQ // GRID_W
HEAD_DIM = 128
NA_HEADS = 4
GQA_Q_HEADS = 8
GQA_KV_HEADS = 2
GQA_GROUP = GQA_Q_HEADS // GQA_KV_HEADS
DIFF_HEADS = 4
DIFF_QK_DIM = HEAD_DIM // 2
NA_KH = 8
NA_KW = 16
D_FF = 4 * D_MODEL
ROPE_THETA = 10000.0
EPS = 1e-6
N_MOD = 6
NEG_BIG = -1e30

NA_W = NA_HEADS * HEAD_DIM
GQA_QW = GQA_Q_HEADS * HEAD_DIM
GQA_KVW = GQA_KV_HEADS * HEAD_DIM
DIFF_W = DIFF_HEADS * HEAD_DIM
D_IN = 3 * NA_W + GQA_QW + 2 * GQA_KVW + 3 * DIFF_W
D_MIX = NA_W + GQA_QW + DIFF_W
C_NA_Q, C_NA_K, C_NA_V = 0, NA_W, 2 * NA_W
C_G_Q = 3 * NA_W
C_G_K = C_G_Q + GQA_QW
C_G_V = C_G_K + GQA_KVW
C_D_Q = C_G_V + GQA_KVW
C_D_K = C_D_Q + DIFF_W
C_D_V = C_D_K + DIFF_W

KV_HEADS = (NA_HEADS, NA_HEADS, GQA_KV_HEADS, GQA_KV_HEADS, DIFF_HEADS, DIFF_HEADS)
N_KEYS = DEC_SEQ + PAST_LEN
LOG2E = 1.4426950408889634
QSCALE = HEAD_DIM ** -0.5 * LOG2E
DIFF_QSCALE = DIFF_QK_DIM ** -0.5 * LOG2E

NA_QROWS = 4
NA_QTOK = NA_QROWS * GRID_W
NA_GROUPS = GRID_ROWS // NA_QROWS
NA_SLAB_ROWS = 12
NA_SLAB = NA_SLAB_ROWS * GRID_W
NA_NKEY = NA_SLAB + PAST_LEN
NA_SLAB_START = (0, 0, 4, 4)
NA_PAIRS_BOTH = 2 * NA_KH - 2
MOD_ROWS = 16

F32 = jnp.float32
BF16 = jnp.bfloat16

VMEM_LIMIT = 52 * 1024 * 1024

TM_IN = 1024
TN_IN = 1536
TM_OUT = 512
TM_MLP = 512
TF_MLP = 1024
TN_ADA = 1024
ATT_ROWS = 256
KEY_CHUNK = 256
NORM_ROWS = 128

def _params(sem):
    return pltpu.CompilerParams(dimension_semantics=sem, vmem_limit_bytes=VMEM_LIMIT)


def _rms(x, g):
    ms = jnp.mean(x * x, axis=-1, keepdims=True)
    return x * lax.rsqrt(ms + EPS) * g


def _nt(a, b):
    return lax.dot_general(a, b, (((1,), (1,)), ((), ())), preferred_element_type=F32)


def _mm(a, b):
    return jnp.dot(a, b, preferred_element_type=F32)


def _scores_exp(q, k, bias=None):
    s = _nt(q, k)
    if bias is not None:
        s = s + bias
    return jnp.exp2(s - jnp.max(s, axis=-1, keepdims=True)).astype(BF16)


def _scores_exp_chunked(q, kall_ref):
    s = jnp.concatenate([_nt(q, kall_ref[c:c + KEY_CHUNK, :])
                         for c in range(0, kall_ref.shape[0], KEY_CHUNK)], axis=1)
    return jnp.exp2(s - jnp.max(s, axis=-1, keepdims=True)).astype(BF16)


def _weighted(e, v_ones):
    oa = _mm(e, v_ones)
    d = v_ones.shape[1] // 2
    return oa[:, :d] / oa[:, d:]


def _with_ones(v):
    return jnp.concatenate([v, jnp.ones_like(v)], axis=1)


def _rope(x, cos, sin_signed, half):
    n = x.shape[-1]
    lane = lax.broadcasted_iota(jnp.int32, x.shape, 1)
    first = (lane % (2 * half)) < half
    rot = jnp.where(first, pltpu.roll(x, n - half, 1), pltpu.roll(x, half, 1))
    return x * cos + rot * sin_signed


def _diff_lambda(lam_ref, lambda_init):
    lf = lam_ref[...]
    a = jnp.sum(lf[0:1] * lf[1:2], axis=-1, keepdims=True)
    b = jnp.sum(lf[2:3] * lf[3:4], axis=-1, keepdims=True)
    return jnp.exp(a) - jnp.exp(b) + lambda_init


def _mod_kernel(cv_ref, w_ref, b_ref, o_ref):
    cv = cv_ref[...]
    s = cv / (1.0 + jnp.exp(-cv))
    o_ref[...] = _mm(s.astype(BF16), w_ref[...].astype(BF16)) + b_ref[...]


def _modulation(cv, w_ada, b_ada):
    n = N_MOD * D_MODEL
    return pl.pallas_call(
        _mod_kernel,
        grid=(DEPTH, n // TN_ADA),
        in_specs=[
            pl.BlockSpec((MOD_ROWS, D_MODEL), lambda l, j: (0, 0)),
            pl.BlockSpec((None, D_MODEL, TN_ADA), lambda l, j: (l, 0, j)),
            pl.BlockSpec((None, 1, TN_ADA), lambda l, j: (l, 0, j)),
        ],
        out_specs=pl.BlockSpec((None, MOD_ROWS, TN_ADA), lambda l, j: (l, 0, j)),
        out_shape=jax.ShapeDtypeStruct((DEPTH, MOD_ROWS, n), F32),
        compiler_params=_params(("parallel", "parallel")),
        name="modulation",
    )(cv, w_ada, b_ada.reshape(DEPTH, 1, n))


def _mod_spec(chunk, row_fn):
    return pl.BlockSpec((None, None, 1, D_MODEL), lambda i, *_: (row_fn(i), chunk, 0, 0))


def _row_fn(tm, rows_per_batch, first_row):
    if rows_per_batch is None:
        return lambda i: first_row
    return lambda i: first_row + (i * tm) // rows_per_batch


def _row_chunks(n_rows, body):
    def step(c, carry):
        body(pl.ds(pl.multiple_of(c * NORM_ROWS, NORM_ROWS), NORM_ROWS))
        return carry

    lax.fori_loop(0, n_rows // NORM_ROWS, step, 0)


def _modulated_norm(h_ref, x_ref, g_ref, sc_ref, sh_ref, gm_ref):
    gm_ref[...] = g_ref[...] * (1.0 + sc_ref[...])

    def body(rows):
        x = x_ref[rows, :]
        r = lax.rsqrt(jnp.mean(x * x, axis=-1, keepdims=True) + EPS)
        h_ref[rows, :] = (x * r * gm_ref[...] + sh_ref[...]).astype(BF16)

    _row_chunks(x_ref.shape[0], body)


def _gated_norm_residual(out_ref, x_ref, y_ref, g_ref, gt_ref, gm_ref):
    gm_ref[...] = gt_ref[...] * g_ref[...]

    def body(rows):
        y = y_ref[rows, :]
        r = lax.rsqrt(jnp.mean(y * y, axis=-1, keepdims=True) + EPS)
        out_ref[rows, :] = x_ref[rows, :] + y * r * gm_ref[...]

    _row_chunks(x_ref.shape[0], body)


def _inproj_kernel(x_ref, g_ref, sc_ref, sh_ref, w_ref, o_ref, h_ref, gm_ref):
    @pl.when(pl.program_id(1) == 0)
    def _():
        _modulated_norm(h_ref, x_ref, g_ref, sc_ref, sh_ref, gm_ref)

    o_ref[...] = _mm(h_ref[...], w_ref[...])


def _inproj(x, g_pre, mod_l, w_in, rows_per_batch, first_row):
    m = x.shape[0]
    row = _row_fn(TM_IN, rows_per_batch, first_row)
    return pl.pallas_call(
        _inproj_kernel,
        grid=(m // TM_IN, D_IN // TN_IN),
        in_specs=[
            pl.BlockSpec((TM_IN, D_MODEL), lambda i, j: (i, 0)),
            pl.BlockSpec((1, D_MODEL), lambda i, j: (0, 0)),
            _mod_spec(1, row),
            _mod_spec(0, row),
            pl.BlockSpec((D_MODEL, TN_IN), lambda i, j: (0, j)),
        ],
        out_specs=pl.BlockSpec((TM_IN, TN_IN), lambda i, j: (i, j)),
        out_shape=jax.ShapeDtypeStruct((m, D_IN), F32),
        scratch_shapes=[pltpu.VMEM((TM_IN, D_MODEL), BF16), pltpu.VMEM((1, D_MODEL), F32)],
        compiler_params=_params(("parallel", "arbitrary")),
        name="inproj",
    )(x, g_pre, mod_l, mod_l, w_in)


def _ctx_attn_kernel(qkv_ref, qg_ref, kg_ref, lam_ref, dg_ref, *rest, lambda_init, first):
    if first:
        o_ref, *kv_refs = rest
        for ref in kv_refs:
            ref[1:] = jnp.zeros((DEPTH - 1,) + ref.shape[1:], F32)
        nak_ref, nav_ref, gk_ref, gv_ref, dk_ref, dv_ref = [ref.at[0] for ref in kv_refs]
    else:
        o_ref, nak_ref, nav_ref, gk_ref, gv_ref, dk_ref, dv_ref = rest[len(KV_HEADS):]
    hd = HEAD_DIM

    def cols(c0, h):
        return qkv_ref[:, c0 + h * hd:c0 + (h + 1) * hd]

    for ref, c0 in ((nak_ref, C_NA_K), (nav_ref, C_NA_V), (gv_ref, C_G_V),
                    (dk_ref, C_D_K), (dv_ref, C_D_V)):
        nh = ref.shape[1]
        ref[...] = qkv_ref[:, c0:c0 + nh * hd].reshape(SEQ, nh, hd)

    for h in range(NA_HEADS):
        q = (cols(C_NA_Q, h) * QSCALE).astype(BF16)
        k = cols(C_NA_K, h).astype(BF16)
        v1 = _with_ones(cols(C_NA_V, h).astype(BF16))
        o_ref[:, h * hd:(h + 1) * hd] = _weighted(_scores_exp(q, k), v1).astype(BF16)

    kfs = [_rms(cols(C_G_K, kv), kg_ref[...]) for kv in range(GQA_KV_HEADS)]
    gk_ref[...] = jnp.concatenate(kfs, axis=1).reshape(SEQ, GQA_KV_HEADS, hd)
    for kv in range(GQA_KV_HEADS):
        k = kfs[kv].astype(BF16)
        v1 = _with_ones(cols(C_G_V, kv).astype(BF16))
        for g in range(GQA_GROUP):
            hq = kv * GQA_GROUP + g
            q = (_rms(cols(C_G_Q, hq), qg_ref[...]) * QSCALE).astype(BF16)
            o = _weighted(_scores_exp(q, k), v1)
            o_ref[:, NA_W + hq * hd:NA_W + (hq + 1) * hd] = o.astype(BF16)

    lam = _diff_lambda(lam_ref, lambda_init)
    lane = lax.broadcasted_iota(jnp.int32, (SEQ, hd), 1)
    for h in range(DIFF_HEADS):
        qf = cols(C_D_Q, h) * DIFF_QSCALE
        k = cols(C_D_K, h).astype(BF16)
        v1 = _with_ones(cols(C_D_V, h).astype(BF16))
        q0 = jnp.where(lane < DIFF_QK_DIM, qf, 0.0).astype(BF16)
        q1 = jnp.where(lane >= DIFF_QK_DIM, qf, 0.0).astype(BF16)
        od = _weighted(_scores_exp(q0, k), v1) - lam * _weighted(_scores_exp(q1, k), v1)
        od = _rms(od, dg_ref[...]) * (1.0 - lambda_init)
        c0 = NA_W + GQA_QW + h * hd
        o_ref[:, c0:c0 + hd] = od.astype(BF16)


def _ctx_attn(qkv, q_g, k_g, lam_vec, diff_g, kv_bufs, layer, lambda_init):
    n = qkv.shape[0]
    vec = pl.BlockSpec((1, HEAD_DIM), lambda b: (0, 0))
    n_in = 5
    first = kv_bufs is None

    def rows(w):
        return pl.BlockSpec((SEQ, w), lambda b: (b, 0))

    def heads(nh):
        if first:
            return pl.BlockSpec((None, DEPTH, SEQ, nh, HEAD_DIM), lambda b: (b, 0, 0, 0, 0))
        return pl.BlockSpec((None, None, SEQ, nh, HEAD_DIM), lambda b: (b, layer, 0, 0, 0))

    bufs = [] if first else list(kv_bufs)
    return pl.pallas_call(
        functools.partial(_ctx_attn_kernel, lambda_init=lambda_init, first=first),
        grid=(n // SEQ,),
        in_specs=[rows(D_IN), vec, vec,
                  pl.BlockSpec((4, DIFF_QK_DIM), lambda b: (0, 0)), vec]
                 + [pl.BlockSpec(memory_space=pl.ANY)] * len(bufs),
        out_specs=[rows(D_MIX)] + [heads(nh) for nh in KV_HEADS],
        out_shape=[jax.ShapeDtypeStruct((n, D_MIX), BF16)]
                  + [jax.ShapeDtypeStruct((BATCH, DEPTH, SEQ, nh, HEAD_DIM), F32)
                     for nh in KV_HEADS],
        input_output_aliases={n_in + i: 1 + i for i in range(len(bufs))},
        compiler_params=_params(("parallel",)),
        name="ctx_attn",
    )(qkv, q_g, k_g, lam_vec, diff_g, *bufs)


def _lat_na_kernel(q_ref, k_ref, v_ref, kc_ref, vc_ref, bias_ref, o_ref, kbig_ref, vbig_ref):
    lat0, lat1 = PAST_LEN, PAST_LEN + DEC_SEQ
    kc = kc_ref[...].astype(BF16)
    vc = _with_ones(vc_ref[...].astype(BF16))
    kbig_ref[0:lat0, :] = kc
    kbig_ref[lat0:lat1, :] = k_ref[...].astype(BF16)
    kbig_ref[lat1:, :] = kc
    vbig_ref[0:lat0, :] = vc
    vbig_ref[lat0:lat1, :] = _with_ones(v_ref[...].astype(BF16))
    vbig_ref[lat1:, :] = vc
    for j in range(NA_GROUPS):
        s0 = NA_SLAB_START[j]
        w0 = 0 if s0 == 0 else PAST_LEN + s0 * GRID_W
        q = (q_ref[j * NA_QTOK:(j + 1) * NA_QTOK, :] * QSCALE).astype(BF16)
        ctx0 = jnp.zeros((GRID_W, PAST_LEN), F32)
        bias_rows = []
        for r in range(j * NA_QROWS, (j + 1) * NA_QROWS):
            slab = [bias_ref[_na_pair_index(r, kr)] for kr in range(s0, s0 + NA_SLAB_ROWS, 2)]
            bias_rows.append(jnp.concatenate([ctx0] + slab if s0 == 0 else slab + [ctx0], axis=1))
        bias = jnp.concatenate(bias_rows, axis=0)
        e = _scores_exp(q, kbig_ref[w0:w0 + NA_NKEY, :], bias)
        o = _weighted(e, vbig_ref[w0:w0 + NA_NKEY, :])
        o_ref[j * NA_QTOK:(j + 1) * NA_QTOK, :] = o.astype(BF16)


def _lat_na(qkv, cache_k, cache_v, bias, layer):
    hd = HEAD_DIM
    cq, ck, cv = C_NA_Q // hd, C_NA_K // hd, C_NA_V // hd
    cache = pl.BlockSpec((None, None, PAST_LEN, hd), lambda h, b: (b, layer, 0, h))
    nbig = DEC_SEQ + 2 * PAST_LEN
    return pl.pallas_call(
        _lat_na_kernel,
        grid=(NA_HEADS, DEC_BATCH),
        in_specs=[
            pl.BlockSpec((DEC_SEQ, hd), lambda h, b: (b, cq + h)),
            pl.BlockSpec((DEC_SEQ, hd), lambda h, b: (b, ck + h)),
            pl.BlockSpec((DEC_SEQ, hd), lambda h, b: (b, cv + h)),
            cache, cache,
            pl.BlockSpec((None, None, NA_PAIRS_BOTH + 3, GRID_W, 2 * GRID_W),
                         lambda h, b: (layer, h, 0, 0, 0)),
        ],
        out_specs=pl.BlockSpec((DEC_SEQ, hd), lambda h, b: (b, h)),
        out_shape=jax.ShapeDtypeStruct((DEC_BATCH * DEC_SEQ, NA_W), BF16),
        scratch_shapes=[pltpu.VMEM((nbig, hd), BF16), pltpu.VMEM((nbig, 2 * hd), BF16)],
        compiler_params=_params(("parallel", "parallel")),
        name="lat_na",
    )(qkv, qkv, qkv, cache_k, cache_v, bias)


def _fill_keys(kall_ref, vall_ref, k_fn, v_ref, kc_ref, vc_ref):
    kall_ref[0:PAST_LEN, :] = kc_ref[...].astype(BF16)
    vall_ref[0:PAST_LEN, :] = _with_ones(vc_ref[...].astype(BF16))
    vall_ref[PAST_LEN:N_KEYS, :] = _with_ones(v_ref[...].astype(BF16))
    for c in range(0, DEC_SEQ, KEY_CHUNK):
        kall_ref[PAST_LEN + c:PAST_LEN + c + KEY_CHUNK, :] = k_fn(slice(c, c + KEY_CHUNK))


def _lat_gqa_kernel(q_ref, k_ref, v_ref, kc_ref, vc_ref, cos_ref, sin_ref, qg_ref, kg_ref,
                    o_ref, kall_ref, vall_ref):
    hd = HEAD_DIM
    half = HEAD_DIM // 4
    def k_fn(rows):
        kn = _rms(k_ref[rows, :], kg_ref[...])
        return _rope(kn, cos_ref[rows, :], sin_ref[rows, :], half).astype(BF16)

    _fill_keys(kall_ref, vall_ref, k_fn, v_ref, kc_ref, vc_ref)
    for s in range(DEC_SEQ // ATT_ROWS):
        rows = slice(s * ATT_ROWS, (s + 1) * ATT_ROWS)
        cos = cos_ref[rows, :]
        sin = sin_ref[rows, :]
        for g in range(GQA_GROUP):
            qf = _rms(q_ref[rows, g * hd:(g + 1) * hd], qg_ref[...])
            q = (_rope(qf, cos, sin, half) * QSCALE).astype(BF16)
            o = _weighted(_scores_exp_chunked(q, kall_ref), vall_ref[...])
            o_ref[rows, g * hd:(g + 1) * hd] = o.astype(BF16)


def _lat_gqa(qkv, cache_k, cache_v, cos, sin, q_g, k_g, layer):
    hd = HEAD_DIM
    gw = GQA_GROUP * hd
    cq, ck, cv = C_G_Q // gw, C_G_K // hd, C_G_V // hd
    cache = pl.BlockSpec((None, None, PAST_LEN, hd), lambda b, kv: (b, layer, 0, kv))
    table = pl.BlockSpec((DEC_SEQ, hd), lambda b, kv: (0, 0))
    vec = pl.BlockSpec((1, hd), lambda b, kv: (0, 0))
    return pl.pallas_call(
        _lat_gqa_kernel,
        grid=(DEC_BATCH, GQA_KV_HEADS),
        in_specs=[
            pl.BlockSpec((DEC_SEQ, gw), lambda b, kv: (b, cq + kv)),
            pl.BlockSpec((DEC_SEQ, hd), lambda b, kv: (b, ck + kv)),
            pl.BlockSpec((DEC_SEQ, hd), lambda b, kv: (b, cv + kv)),
            cache, cache, table, table, vec, vec,
        ],
        out_specs=pl.BlockSpec((DEC_SEQ, gw), lambda b, kv: (b, kv)),
        out_shape=jax.ShapeDtypeStruct((DEC_BATCH * DEC_SEQ, GQA_QW), BF16),
        scratch_shapes=[pltpu.VMEM((N_KEYS, hd), BF16), pltpu.VMEM((N_KEYS, 2 * hd), BF16)],
        compiler_params=_params(("parallel", "parallel")),
        name="lat_gqa",
    )(qkv, qkv, qkv, cache_k, cache_v, cos, sin, q_g, k_g)


def _lat_diff_kernel(q_ref, k_ref, v_ref, kc_ref, vc_ref, cos_ref, sin_ref, lam_ref, dg_ref,
                     o_ref, kall_ref, vall_ref, *, lambda_init):
    half = DIFF_QK_DIM // 4
    def k_fn(rows):
        return _rope(k_ref[rows, :], cos_ref[rows, :], sin_ref[rows, :], half).astype(BF16)

    _fill_keys(kall_ref, vall_ref, k_fn, v_ref, kc_ref, vc_ref)
    lam = _diff_lambda(lam_ref, lambda_init)
    lane = lax.broadcasted_iota(jnp.int32, (ATT_ROWS, HEAD_DIM), 1)
    for s in range(DEC_SEQ // ATT_ROWS):
        rows = slice(s * ATT_ROWS, (s + 1) * ATT_ROWS)
        qf = _rope(q_ref[rows, :], cos_ref[rows, :], sin_ref[rows, :], half) * DIFF_QSCALE
        q0 = jnp.where(lane < DIFF_QK_DIM, qf, 0.0).astype(BF16)
        q1 = jnp.where(lane >= DIFF_QK_DIM, qf, 0.0).astype(BF16)
        od = (_weighted(_scores_exp_chunked(q0, kall_ref), vall_ref[...])
              - lam * _weighted(_scores_exp_chunked(q1, kall_ref), vall_ref[...]))
        od = _rms(od, dg_ref[...]) * (1.0 - lambda_init)
        o_ref[rows, :] = od.astype(BF16)


def _lat_diff(qkv, cache_k, cache_v, cos, sin, lam_vec, diff_g, layer, lambda_init):
    hd = HEAD_DIM
    cq, ck, cv = C_D_Q // hd, C_D_K // hd, C_D_V // hd
    cache = pl.BlockSpec((None, None, PAST_LEN, hd), lambda b, h: (b, layer, 0, h))
    table = pl.BlockSpec((DEC_SEQ, hd), lambda b, h: (0, 0))
    return pl.pallas_call(
        functools.partial(_lat_diff_kernel, lambda_init=lambda_init),
        grid=(DEC_BATCH, DIFF_HEADS),
        in_specs=[
            pl.BlockSpec((DEC_SEQ, hd), lambda b, h: (b, cq + h)),
            pl.BlockSpec((DEC_SEQ, hd), lambda b, h: (b, ck + h)),
            pl.BlockSpec((DEC_SEQ, hd), lambda b, h: (b, cv + h)),
            cache, cache, table, table,
            pl.BlockSpec((4, DIFF_QK_DIM), lambda b, h: (0, 0)),
            pl.BlockSpec((1, hd), lambda b, h: (0, 0)),
        ],
        out_specs=pl.BlockSpec((DEC_SEQ, hd), lambda b, h: (b, h)),
        out_shape=jax.ShapeDtypeStruct((DEC_BATCH * DEC_SEQ, DIFF_W), BF16),
        scratch_shapes=[pltpu.VMEM((N_KEYS, hd), BF16), pltpu.VMEM((N_KEYS, 2 * hd), BF16)],
        compiler_params=_params(("parallel", "parallel")),
        name="lat_diff",
    )(qkv, qkv, qkv, cache_k, cache_v, cos, sin, lam_vec, diff_g)


def _outproj_kernel(*refs, n_o):
    o_refs = refs[:n_o]
    w_ref, x_ref, g_ref, gt_ref, out_ref = refs[n_o:]
    y = None
    off = 0
    for o_ref in o_refs:
        wd = o_ref.shape[1]
        part = _mm(o_ref[...], w_ref[off:off + wd, :])
        y = part if y is None else y + part
        off += wd
    out_ref[...] = x_ref[...] + gt_ref[...] * _rms(y, g_ref[...])


def _outproj(o_parts, w_out, x, g_post, mod_l, rows_per_batch, first_row):
    m = x.shape[0]
    row = _row_fn(TM_OUT, rows_per_batch, first_row)
    full = pl.BlockSpec((TM_OUT, D_MODEL), lambda i: (i, 0))
    return pl.pallas_call(
        functools.partial(_outproj_kernel, n_o=len(o_parts)),
        grid=(m // TM_OUT,),
        in_specs=[pl.BlockSpec((TM_OUT, o.shape[1]), lambda i: (i, 0)) for o in o_parts] + [
            pl.BlockSpec((D_MIX, D_MODEL), lambda i: (0, 0)),
            full,
            pl.BlockSpec((1, D_MODEL), lambda i: (0, 0)),
            _mod_spec(2, row),
        ],
        out_specs=full,
        out_shape=jax.ShapeDtypeStruct((m, D_MODEL), F32),
        compiler_params=_params(("parallel",)),
        name="outproj",
    )(*o_parts, w_out, x, g_post, mod_l)


def _mlp_kernel(x_ref, gpre_ref, sc_ref, sh_ref, wup_ref, wdn_ref, gpost_ref, gt_ref, *rest,
                n_cast, n_flat):
    n_side = n_cast + n_flat
    cast_in, flat_in = rest[:n_cast], rest[n_cast:n_side]
    out_ref = rest[n_side]
    cast_out = rest[n_side + 1:n_side + 1 + n_cast]
    flat_out = rest[n_side + 1 + n_cast:2 * n_side + 1]
    h_ref, acc_ref, gm_ref = rest[2 * n_side + 1:]
    k = pl.program_id(1)

    @pl.when(k == 0)
    def _():
        _modulated_norm(h_ref, x_ref, gpre_ref, sc_ref, sh_ref, gm_ref)
        acc_ref[...] = jnp.zeros_like(acc_ref)

    u = _mm(h_ref[...], wup_ref[...])
    a = jnp.square(jnp.maximum(u, 0.0)).astype(BF16)
    acc_ref[...] += _mm(a, wdn_ref[...])

    for src, dst in zip(cast_in, cast_out):
        dst[...] = src[...].astype(BF16)
    for src, dst in zip(flat_in, flat_out):
        dst[...] = src[...].reshape(dst.shape)

    @pl.when(k == pl.num_programs(1) - 1)
    def _():
        _gated_norm_residual(out_ref, x_ref, acc_ref, gpost_ref, gt_ref, gm_ref)


def _mlp(x, g_pre, g_post, mod_l, w_up, w_down, rows_per_batch, first_row, cast=None,
         flatten=None):
    m = x.shape[0]
    row = _row_fn(TM_MLP, rows_per_batch, first_row)
    full = pl.BlockSpec((TM_MLP, D_MODEL), lambda i, k: (i, 0))
    vec = pl.BlockSpec((1, D_MODEL), lambda i, k: (0, 0))
    n_k = D_FF // TF_MLP
    n_steps = (m // TM_MLP) * n_k
    cast_in_specs, cast_out_specs, cast_shapes, cast_args = [], [], [], []
    if cast is not None:
        stacks, layer = cast
        for w in stacks:
            rows, cols = w.shape[1], w.shape[2]
            rb = rows // n_steps
            cast_in_specs.append(
                pl.BlockSpec((None, rb, cols), lambda i, k: (layer, i * n_k + k, 0)))
            cast_out_specs.append(pl.BlockSpec((rb, cols), lambda i, k: (i * n_k + k, 0)))
            cast_shapes.append(jax.ShapeDtypeStruct((rows, cols), BF16))
            cast_args.append(w)
    flat_in_specs, flat_out_specs, flat_shapes = [], [], []
    flatten = flatten or []
    if flatten:
        tok = DEC_BATCH * DEPTH * PAST_LEN // n_steps
        per = PAST_LEN // tok

        def where(i, k):
            s = i * n_k + k
            return s // (DEPTH * per), (s // per) % DEPTH, s % per

        for a in flatten:
            nh = a.shape[3]
            flat_in_specs.append(pl.BlockSpec((None, None, tok, nh, HEAD_DIM),
                                              lambda i, k: (*where(i, k), 0, 0)))
            flat_out_specs.append(pl.BlockSpec((None, None, tok, nh * HEAD_DIM),
                                               lambda i, k: (*where(i, k), 0)))
            flat_shapes.append(
                jax.ShapeDtypeStruct((DEC_BATCH, DEPTH, PAST_LEN, nh * HEAD_DIM), a.dtype))
    n_cast = len(cast_args)
    res = pl.pallas_call(
        functools.partial(_mlp_kernel, n_cast=n_cast, n_flat=len(flatten)),
        grid=(m // TM_MLP, n_k),
        in_specs=[
            full, vec, _mod_spec(4, row), _mod_spec(3, row),
            pl.BlockSpec((D_MODEL, TF_MLP), lambda i, k: (0, k)),
            pl.BlockSpec((TF_MLP, D_MODEL), lambda i, k: (k, 0)),
            vec, _mod_spec(5, row),
        ] + cast_in_specs + flat_in_specs,
        out_specs=[full] + cast_out_specs + flat_out_specs,
        out_shape=[jax.ShapeDtypeStruct((m, D_MODEL), F32)] + cast_shapes + flat_shapes,
        scratch_shapes=[pltpu.VMEM((TM_MLP, D_MODEL), BF16), pltpu.VMEM((TM_MLP, D_MODEL), F32),
                        pltpu.VMEM((1, D_MODEL), F32)],
        compiler_params=_params(("parallel", "arbitrary")),
        name="mlp",
    )(x, g_pre, mod_l, mod_l, w_up, w_down, g_post, mod_l, *cast_args, *flatten)
    return res[0], res[1:1 + n_cast], res[1 + n_cast:]


def _rope_tables(half, n_rep):
    t = jnp.arange(DEC_SEQ)
    inv = ROPE_THETA ** (-jnp.arange(half, dtype=F32) / half)

    def cs(pos):
        ang = pos.astype(F32)[:, None] * inv[None, :]
        c, s = jnp.cos(ang), jnp.sin(ang)
        return jnp.concatenate([c, c], axis=-1), jnp.concatenate([-s, s], axis=-1)

    cr, sr = cs(t // GRID_W)
    cc, sc = cs(t % GRID_W)
    return (jnp.concatenate([cr, cc] * n_rep, axis=-1),
            jnp.concatenate([sr, sc] * n_rep, axis=-1))


def _na_bias(rpb):
    qc = np.arange(GRID_W)[:, None]
    kc = np.arange(GRID_W)[None, :]
    ws = np.clip(qc - NA_KW // 2, 0, GRID_W - NA_KW)
    valid = (kc >= ws) & (kc < ws + NA_KW)
    dcol = np.clip(kc - qc + NA_KW - 1, 0, 2 * NA_KW - 2)
    n_dcol = 2 * NA_KW - 1
    onehot = (dcol.reshape(-1)[None, :] == np.arange(n_dcol)[:, None]).astype(np.float32)
    t = jnp.einsum("lhdc,cq->lhdq", rpb.astype(F32), jnp.asarray(onehot),
                   precision=lax.Precision.HIGHEST)
    t = t.reshape(DEPTH, NA_HEADS, 2 * NA_KH - 1, GRID_W, GRID_W) * LOG2E
    t = jnp.where(jnp.asarray(valid)[None, None, None], t, NEG_BIG)
    masked = jnp.full((DEPTH, NA_HEADS, 1, GRID_W, GRID_W), NEG_BIG, F32)
    lo, hi = NA_KH // 2 - 1, NA_KH + NA_KH // 2 - 2
    both = jnp.concatenate([t[:, :, :-1], t[:, :, 1:]], axis=-1)
    second = jnp.concatenate([masked, t[:, :, lo:lo + 1]], axis=-1)
    first = jnp.concatenate([t[:, :, hi:hi + 1], masked], axis=-1)
    none = jnp.concatenate([masked, masked], axis=-1)
    return jnp.concatenate([both, second, first, none], axis=2)


def _na_pair_index(r, key_row):
    r0 = min(max(r - NA_KH // 2, 0), GRID_ROWS - NA_KH)
    in_a = r0 <= key_row < r0 + NA_KH
    in_b = r0 <= key_row + 1 < r0 + NA_KH
    d = key_row - r + NA_KH - 1
    if in_a and in_b:
        return d
    if in_b:
        assert d + 1 == NA_KH // 2 - 1
        return NA_PAIRS_BOTH
    if in_a:
        assert d == NA_KH + NA_KH // 2 - 2
        return NA_PAIRS_BOTH + 1
    return NA_PAIRS_BOTH + 2


def kernel(x_prompt, x_sample, c, cache_na_k, cache_na_v, cache_gqa_k, cache_gqa_v,
           cache_diff_k, cache_diff_v, c_ctx, w_ada, b_ada, norm_g, w_in, w_out, na_rpb,
           gqa_q_g, gqa_k_g, diff_lam, diff_g, w_up, w_down):
    np_rows = BATCH * SEQ
    ns_rows = DEC_BATCH * DEC_SEQ
    xp = x_prompt.reshape(np_rows, D_MODEL)
    xs = x_sample.reshape(ns_rows, D_MODEL)

    cv = jnp.concatenate(
        [c_ctx[None, :], c, jnp.zeros((MOD_ROWS - 1 - DEC_BATCH, D_MODEL), F32)], axis=0)
    mod = _modulation(cv, w_ada, b_ada).reshape(DEPTH, MOD_ROWS, N_MOD, 1, D_MODEL)

    w_stacks = [w_in, w_out, w_up, w_down]
    w_in_b, w_out_b, w_up_b, w_down_b = [w[0].astype(BF16) for w in w_stacks]

    caches = [cache_na_k, cache_na_v, cache_gqa_k, cache_gqa_v, cache_diff_k, cache_diff_v]

    cos_g, sin_g = _rope_tables(HEAD_DIM // 4, 1)
    cos_d, sin_d = _rope_tables(DIFF_QK_DIM // 4, 2)
    na_bias = _na_bias(na_rpb)

    new_kv = None
    for l in range(DEPTH):
        lambda_init = 0.8 - 0.6 * math.exp(-0.3 * l)
        mod_l = mod[l]
        g = norm_g[l].reshape(4, 1, D_MODEL)
        q_g = gqa_q_g[l].reshape(1, HEAD_DIM)
        k_g = gqa_k_g[l].reshape(1, HEAD_DIM)
        d_g = diff_g[l].reshape(1, HEAD_DIM)
        lam_vec = diff_lam[l]

        qkv_p = _inproj(xp, g[0], mod_l, w_in_b, None, 0)
        o_p, *new_kv = _ctx_attn(qkv_p, q_g, k_g, lam_vec, d_g, new_kv, l, lambda_init)
        xp = _outproj([o_p], w_out_b, xp, g[1], mod_l, None, 0)
        xp, _, flat = _mlp(xp, g[2], g[3], mod_l, w_up_b, w_down_b, None, 0,
                           flatten=caches if l == 0 else None)
        if l == 0:
            c_na_k, c_na_v, c_g_k, c_g_v, c_d_k, c_d_v = flat

        qkv_s = _inproj(xs, g[0], mod_l, w_in_b, DEC_SEQ, 1)
        o_na = _lat_na(qkv_s, c_na_k, c_na_v, na_bias, l)
        o_gqa = _lat_gqa(qkv_s, c_g_k, c_g_v, cos_g, sin_g, q_g, k_g, l)
        o_diff = _lat_diff(qkv_s, c_d_k, c_d_v, cos_d, sin_d, lam_vec, d_g, l, lambda_init)
        xs = _outproj([o_na, o_gqa, o_diff], w_out_b, xs, g[1], mod_l, DEC_SEQ, 1)
        cast = (w_stacks, l + 1) if l + 1 < DEPTH else None
        xs, nxt, _ = _mlp(xs, g[2], g[3], mod_l, w_up_b, w_down_b, DEC_SEQ, 1, cast)
        if nxt:
            w_in_b, w_out_b, w_up_b, w_down_b = nxt

    return (xp.reshape(BATCH, SEQ, D_MODEL), xs.reshape(DEC_BATCH, DEC_SEQ, D_MODEL), *new_kv)
```

```python
import functools
import math

import jax
import jax.numpy as jnp
import numpy as np
from jax import lax
from jax.experimental import pallas as pl
from jax.experimental.pallas import tpu as pltpu

D_MODEL = 2048
BATCH = 16
SEQ = 256
DEPTH = 4
DEC_BATCH = 8
DEC_SEQ = 1024
PAST_LEN = 256
GRID_W = 64
GRID_ROWS = DEC_SEQ // GRID_W
HEAD_DIM = 128
NA_HEADS = 4
GQA_Q_HEADS = 8
GQA_KV_HEADS = 2
GQA_GROUP = GQA_Q_HEADS // GQA_KV_HEADS
DIFF_HEADS = 4
DIFF_QK_DIM = HEAD_DIM // 2
NA_KH = 8
NA_KW = 16
D_FF = 4 * D_MODEL
ROPE_THETA = 10000.0
EPS = 1e-6
N_MOD = 6
NEG_BIG = -1e30

NA_W = NA_HEADS * HEAD_DIM
GQA_QW = GQA_Q_HEADS * HEAD_DIM
GQA_KVW = GQA_KV_HEADS * HEAD_DIM
DIFF_W = DIFF_HEADS * HEAD_DIM
D_IN = 3 * NA_W + GQA_QW + 2 * GQA_KVW + 3 * DIFF_W
D_MIX = NA_W + GQA_QW + DIFF_W
C_NA_Q, C_NA_K, C_NA_V = 0, NA_W, 2 * NA_W
C_G_Q = 3 * NA_W
C_G_K = C_G_Q + GQA_QW
C_G_V = C_G_K + GQA_KVW
C_D_Q = C_G_V + GQA_KVW
C_D_K = C_D_Q + DIFF_W
C_D_V = C_D_K + DIFF_W

KV_HEADS = (NA_HEADS, NA_HEADS, GQA_KV_HEADS, GQA_KV_HEADS, DIFF_HEADS, DIFF_HEADS)
N_KEYS = DEC_SEQ + PAST_LEN
LOG2E = 1.4426950408889634
QSCALE = HEAD_DIM ** -0.5 * LOG2E
DIFF_QSCALE = DIFF_QK_DIM ** -0.5 * LOG2E

NA_QROWS = 4
NA_QTOK = NA_QROWS * GRID_W
NA_GROUPS = GRID_ROWS // NA_QROWS
NA_SLAB_ROWS = 12
NA_SLAB = NA_SLAB_ROWS * GRID_W
NA_NKEY = NA_SLAB + PAST_LEN
NA_SLAB_START = (0, 0, 4, 4)
NA_PAIRS_BOTH = 2 * NA_KH - 2
MOD_ROWS = 16

F32 = jnp.float32
BF16 = jnp.bfloat16

VMEM_LIMIT = 52 * 1024 * 1024

TM_IN = 1024
TN_IN = 1536
TM_OUT = 512
TM_MLP = 512
TF_MLP = 1024
TN_ADA = 1024
ATT_ROWS = 256
NORM_ROWS = 128

def _params(sem):
    return pltpu.CompilerParams(dimension_semantics=sem, vmem_limit_bytes=VMEM_LIMIT)


def _rms(x, g):
    ms = jnp.mean(x * x, axis=-1, keepdims=True)
    return x * lax.rsqrt(ms + EPS) * g


def _nt(a, b):
    return lax.dot_general(a, b, (((1,), (1,)), ((), ())), preferred_element_type=F32)


def _mm(a, b):
    return jnp.dot(a, b, preferred_element_type=F32)


def _scores_exp(q, k, bias=None):
    s = _nt(q, k)
    if bias is not None:
        s = s + bias
    return jnp.exp2(s - jnp.max(s, axis=-1, keepdims=True)).astype(BF16)


def _weighted(e, v_ones):
    oa = _mm(e, v_ones)
    d = v_ones.shape[1] // 2
    return oa[:, :d] / oa[:, d:]


def _with_ones(v):
    return jnp.concatenate([v, jnp.ones_like(v)], axis=1)


def _rope(x, cos, sin_signed, half):
    n = x.shape[-1]
    lane = lax.broadcasted_iota(jnp.int32, x.shape, 1)
    first = (lane % (2 * half)) < half
    rot = jnp.where(first, pltpu.roll(x, n - half, 1), pltpu.roll(x, half, 1))
    return x * cos + rot * sin_signed


def _diff_lambda(lam_ref, lambda_init):
    lf = lam_ref[...]
    a = jnp.sum(lf[0:1] * lf[1:2], axis=-1, keepdims=True)
    b = jnp.sum(lf[2:3] * lf[3:4], axis=-1, keepdims=True)
    return jnp.exp(a) - jnp.exp(b) + lambda_init


def _mod_kernel(cv_ref, w_ref, b_ref, o_ref):
    cv = cv_ref[...]
    s = cv / (1.0 + jnp.exp(-cv))
    o_ref[...] = _mm(s.astype(BF16), w_ref[...].astype(BF16)) + b_ref[...]


def _modulation(cv, w_ada, b_ada):
    n = N_MOD * D_MODEL
    return pl.pallas_call(
        _mod_kernel,
        grid=(DEPTH, n // TN_ADA),
        in_specs=[
            pl.BlockSpec((MOD_ROWS, D_MODEL), lambda l, j: (0, 0)),
            pl.BlockSpec((None, D_MODEL, TN_ADA), lambda l, j: (l, 0, j)),
            pl.BlockSpec((None, 1, TN_ADA), lambda l, j: (l, 0, j)),
        ],
        out_specs=pl.BlockSpec((None, MOD_ROWS, TN_ADA), lambda l, j: (l, 0, j)),
        out_shape=jax.ShapeDtypeStruct((DEPTH, MOD_ROWS, n), F32),
        compiler_params=_params(("parallel", "parallel")),
        name="modulation",
    )(cv, w_ada, b_ada.reshape(DEPTH, 1, n))


def _mod_spec(chunk, row_fn):
    return pl.BlockSpec((None, None, 1, D_MODEL), lambda i, *_: (row_fn(i), chunk, 0, 0))


def _row_fn(tm, rows_per_batch, first_row):
    if rows_per_batch is None:
        return lambda i: first_row
    return lambda i: first_row + (i * tm) // rows_per_batch


def _row_chunks(n_rows, body):
    def step(c, carry):
        body(pl.ds(pl.multiple_of(c * NORM_ROWS, NORM_ROWS), NORM_ROWS))
        return carry

    lax.fori_loop(0, n_rows // NORM_ROWS, step, 0)


def _modulated_norm(h_ref, x_ref, g_ref, sc_ref, sh_ref, gm_ref):
    gm_ref[...] = g_ref[...] * (1.0 + sc_ref[...])

    def body(rows):
        x = x_ref[rows, :]
        r = lax.rsqrt(jnp.mean(x * x, axis=-1, keepdims=True) + EPS)
        h_ref[rows, :] = (x * r * gm_ref[...] + sh_ref[...]).astype(BF16)

    _row_chunks(x_ref.shape[0], body)


def _gated_norm_residual(out_ref, x_ref, y_ref, g_ref, gt_ref, gm_ref):
    gm_ref[...] = gt_ref[...] * g_ref[...]

    def body(rows):
        y = y_ref[rows, :]
        r = lax.rsqrt(jnp.mean(y * y, axis=-1, keepdims=True) + EPS)
        out_ref[rows, :] = x_ref[rows, :] + y * r * gm_ref[...]

    _row_chunks(x_ref.shape[0], body)


def _inproj_kernel(x_ref, g_ref, sc_ref, sh_ref, w_ref, o_ref, h_ref, gm_ref):
    @pl.when(pl.program_id(1) == 0)
    def _():
        _modulated_norm(h_ref, x_ref, g_ref, sc_ref, sh_ref, gm_ref)

    o_ref[...] = _mm(h_ref[...], w_ref[...])


def _inproj(x, g_pre, mod_l, w_in, rows_per_batch, first_row):
    m = x.shape[0]
    row = _row_fn(TM_IN, rows_per_batch, first_row)
    return pl.pallas_call(
        _inproj_kernel,
        grid=(m // TM_IN, D_IN // TN_IN),
        in_specs=[
            pl.BlockSpec((TM_IN, D_MODEL), lambda i, j: (i, 0)),
            pl.BlockSpec((1, D_MODEL), lambda i, j: (0, 0)),
            _mod_spec(1, row),
            _mod_spec(0, row),
            pl.BlockSpec((D_MODEL, TN_IN), lambda i, j: (0, j)),
        ],
        out_specs=pl.BlockSpec((TM_IN, TN_IN), lambda i, j: (i, j)),
        out_shape=jax.ShapeDtypeStruct((m, D_IN), F32),
        scratch_shapes=[pltpu.VMEM((TM_IN, D_MODEL), BF16), pltpu.VMEM((1, D_MODEL), F32)],
        compiler_params=_params(("parallel", "arbitrary")),
        name="inproj",
    )(x, g_pre, mod_l, mod_l, w_in)


def _ctx_attn_kernel(qkv_ref, qg_ref, kg_ref, lam_ref, dg_ref, *rest, lambda_init, first):
    if first:
        o_ref, *kv_refs = rest
        for ref in kv_refs:
            ref[1:] = jnp.zeros((DEPTH - 1,) + ref.shape[1:], F32)
        nak_ref, nav_ref, gk_ref, gv_ref, dk_ref, dv_ref = [ref.at[0] for ref in kv_refs]
    else:
        o_ref, nak_ref, nav_ref, gk_ref, gv_ref, dk_ref, dv_ref = rest[len(KV_HEADS):]
    hd = HEAD_DIM

    def cols(c0, h):
        return qkv_ref[:, c0 + h * hd:c0 + (h + 1) * hd]

    for ref, c0 in ((nak_ref, C_NA_K), (nav_ref, C_NA_V), (gv_ref, C_G_V),
                    (dk_ref, C_D_K), (dv_ref, C_D_V)):
        nh = ref.shape[1]
        ref[...] = qkv_ref[:, c0:c0 + nh * hd].reshape(SEQ, nh, hd)

    for h in range(NA_HEADS):
        q = (cols(C_NA_Q, h) * QSCALE).astype(BF16)
        k = cols(C_NA_K, h).astype(BF16)
        v1 = _with_ones(cols(C_NA_V, h).astype(BF16))
        o_ref[:, h * hd:(h + 1) * hd] = _weighted(_scores_exp(q, k), v1).astype(BF16)

    kfs = [_rms(cols(C_G_K, kv), kg_ref[...]) for kv in range(GQA_KV_HEADS)]
    gk_ref[...] = jnp.concatenate(kfs, axis=1).reshape(SEQ, GQA_KV_HEADS, hd)
    for kv in range(GQA_KV_HEADS):
        k = kfs[kv].astype(BF16)
        v1 = _with_ones(cols(C_G_V, kv).astype(BF16))
        for g in range(GQA_GROUP):
            hq = kv * GQA_GROUP + g
            q = (_rms(cols(C_G_Q, hq), qg_ref[...]) * QSCALE).astype(BF16)
            o = _weighted(_scores_exp(q, k), v1)
            o_ref[:, NA_W + hq * hd:NA_W + (hq + 1) * hd] = o.astype(BF16)

    lam = _diff_lambda(lam_ref, lambda_init)
    lane = lax.broadcasted_iota(jnp.int32, (SEQ, hd), 1)
    for h in range(DIFF_HEADS):
        qf = cols(C_D_Q, h) * DIFF_QSCALE
        k = cols(C_D_K, h).astype(BF16)
        v1 = _with_ones(cols(C_D_V, h).astype(BF16))
        q0 = jnp.where(lane < DIFF_QK_DIM, qf, 0.0).astype(BF16)
        q1 = jnp.where(lane >= DIFF_QK_DIM, qf, 0.0).astype(BF16)
        od = _weighted(_scores_exp(q0, k), v1) - lam * _weighted(_scores_exp(q1, k), v1)
        od = _rms(od, dg_ref[...]) * (1.0 - lambda_init)
        c0 = NA_W + GQA_QW + h * hd
        o_ref[:, c0:c0 + hd] = od.astype(BF16)


def _ctx_attn(qkv, q_g, k_g, lam_vec, diff_g, kv_bufs, layer, lambda_init):
    n = qkv.shape[0]
    vec = pl.BlockSpec((1, HEAD_DIM), lambda b: (0, 0))
    n_in = 5
    first = kv_bufs is None

    def rows(w):
        return pl.BlockSpec((SEQ, w), lambda b: (b, 0))

    def heads(nh):
        if first:
            return pl.BlockSpec((None, DEPTH, SEQ, nh, HEAD_DIM), lambda b: (b, 0, 0, 0, 0))
        return pl.BlockSpec((None, None, SEQ, nh, HEAD_DIM), lambda b: (b, layer, 0, 0, 0))

    bufs = [] if first else list(kv_bufs)
    return pl.pallas_call(
        functools.partial(_ctx_attn_kernel, lambda_init=lambda_init, first=first),
        grid=(n // SEQ,),
        in_specs=[rows(D_IN), vec, vec,
                  pl.BlockSpec((4, DIFF_QK_DIM), lambda b: (0, 0)), vec]
                 + [pl.BlockSpec(memory_space=pl.ANY)] * len(bufs),
        out_specs=[rows(D_MIX)] + [heads(nh) for nh in KV_HEADS],
        out_shape=[jax.ShapeDtypeStruct((n, D_MIX), BF16)]
                  + [jax.ShapeDtypeStruct((BATCH, DEPTH, SEQ, nh, HEAD_DIM), F32)
                     for nh in KV_HEADS],
        input_output_aliases={n_in + i: 1 + i for i in range(len(bufs))},
        compiler_params=_params(("parallel",)),
        name="ctx_attn",
    )(qkv, q_g, k_g, lam_vec, diff_g, *bufs)


def _lat_na_kernel(q_ref, k_ref, v_ref, kc_ref, vc_ref, bias_ref, o_ref, kbig_ref, vbig_ref):
    lat0, lat1 = PAST_LEN, PAST_LEN + DEC_SEQ
    kc = kc_ref[...].astype(BF16)
    vc = _with_ones(vc_ref[...].astype(BF16))
    kbig_ref[0:lat0, :] = kc
    kbig_ref[lat0:lat1, :] = k_ref[...].astype(BF16)
    kbig_ref[lat1:, :] = kc
    vbig_ref[0:lat0, :] = vc
    vbig_ref[lat0:lat1, :] = _with_ones(v_ref[...].astype(BF16))
    vbig_ref[lat1:, :] = vc
    for j in range(NA_GROUPS):
        s0 = NA_SLAB_START[j]
        w0 = 0 if s0 == 0 else PAST_LEN + s0 * GRID_W
        q = (q_ref[j * NA_QTOK:(j + 1) * NA_QTOK, :] * QSCALE).astype(BF16)
        ctx0 = jnp.zeros((GRID_W, PAST_LEN), F32)
        bias_rows = []
        for r in range(j * NA_QROWS, (j + 1) * NA_QROWS):
            slab = [bias_ref[_na_pair_index(r, kr)] for kr in range(s0, s0 + NA_SLAB_ROWS, 2)]
            bias_rows.append(jnp.concatenate([ctx0] + slab if s0 == 0 else slab + [ctx0], axis=1))
        bias = jnp.concatenate(bias_rows, axis=0)
        e = _scores_exp(q, kbig_ref[w0:w0 + NA_NKEY, :], bias)
        o = _weighted(e, vbig_ref[w0:w0 + NA_NKEY, :])
        o_ref[j * NA_QTOK:(j + 1) * NA_QTOK, :] = o.astype(BF16)


def _lat_na(qkv, cache_k, cache_v, bias, layer):
    hd = HEAD_DIM
    cq, ck, cv = C_NA_Q // hd, C_NA_K // hd, C_NA_V // hd
    cache = pl.BlockSpec((None, None, PAST_LEN, hd), lambda h, b: (b, layer, 0, h))
    nbig = DEC_SEQ + 2 * PAST_LEN
    return pl.pallas_call(
        _lat_na_kernel,
        grid=(NA_HEADS, DEC_BATCH),
        in_specs=[
            pl.BlockSpec((DEC_SEQ, hd), lambda h, b: (b, cq + h)),
            pl.BlockSpec((DEC_SEQ, hd), lambda h, b: (b, ck + h)),
            pl.BlockSpec((DEC_SEQ, hd), lambda h, b: (b, cv + h)),
            cache, cache,
            pl.BlockSpec((None, None, NA_PAIRS_BOTH + 3, GRID_W, 2 * GRID_W),
                         lambda h, b: (layer, h, 0, 0, 0)),
        ],
        out_specs=pl.BlockSpec((DEC_SEQ, hd), lambda h, b: (b, h)),
        out_shape=jax.ShapeDtypeStruct((DEC_BATCH * DEC_SEQ, NA_W), BF16),
        scratch_shapes=[pltpu.VMEM((nbig, hd), BF16), pltpu.VMEM((nbig, 2 * hd), BF16)],
        compiler_params=_params(("parallel", "parallel")),
        name="lat_na",
    )(qkv, qkv, qkv, cache_k, cache_v, bias)


def _fill_keys(kall_ref, vall_ref, k_lat, v_ref, kc_ref, vc_ref):
    kall_ref[0:DEC_SEQ, :] = k_lat.astype(BF16)
    kall_ref[DEC_SEQ:N_KEYS, :] = kc_ref[...].astype(BF16)
    vall_ref[0:DEC_SEQ, :] = _with_ones(v_ref[...].astype(BF16))
    vall_ref[DEC_SEQ:N_KEYS, :] = _with_ones(vc_ref[...].astype(BF16))


def _lat_gqa_kernel(q_ref, k_ref, v_ref, kc_ref, vc_ref, cos_ref, sin_ref, qg_ref, kg_ref,
                    o_ref, kall_ref, vall_ref):
    hd = HEAD_DIM
    half = HEAD_DIM // 4
    kf = _rope(_rms(k_ref[...], kg_ref[...]), cos_ref[...], sin_ref[...], half)
    _fill_keys(kall_ref, vall_ref, kf, v_ref, kc_ref, vc_ref)
    for s in range(DEC_SEQ // ATT_ROWS):
        rows = slice(s * ATT_ROWS, (s + 1) * ATT_ROWS)
        cos = cos_ref[rows, :]
        sin = sin_ref[rows, :]
        for g in range(GQA_GROUP):
            qf = _rms(q_ref[rows, g * hd:(g + 1) * hd], qg_ref[...])
            q = (_rope(qf, cos, sin, half) * QSCALE).astype(BF16)
            o = _weighted(_scores_exp(q, kall_ref[...]), vall_ref[...])
            o_ref[rows, g * hd:(g + 1) * hd] = o.astype(BF16)


def _lat_gqa(qkv, cache_k, cache_v, cos, sin, q_g, k_g, layer):
    hd = HEAD_DIM
    gw = GQA_GROUP * hd
    cq, ck, cv = C_G_Q // gw, C_G_K // hd, C_G_V // hd
    cache = pl.BlockSpec((None, None, PAST_LEN, hd), lambda b, kv: (b, layer, 0, kv))
    table = pl.BlockSpec((DEC_SEQ, hd), lambda b, kv: (0, 0))
    vec = pl.BlockSpec((1, hd), lambda b, kv: (0, 0))
    return pl.pallas_call(
        _lat_gqa_kernel,
        grid=(DEC_BATCH, GQA_KV_HEADS),
        in_specs=[
            pl.BlockSpec((DEC_SEQ, gw), lambda b, kv: (b, cq + kv)),
            pl.BlockSpec((DEC_SEQ, hd), lambda b, kv: (b, ck + kv)),
            pl.BlockSpec((DEC_SEQ, hd), lambda b, kv: (b, cv + kv)),
            cache, cache, table, table, vec, vec,
        ],
        out_specs=pl.BlockSpec((DEC_SEQ, gw), lambda b, kv: (b, kv)),
        out_shape=jax.ShapeDtypeStruct((DEC_BATCH * DEC_SEQ, GQA_QW), BF16),
        scratch_shapes=[pltpu.VMEM((N_KEYS, hd), BF16), pltpu.VMEM((N_KEYS, 2 * hd), BF16)],
        compiler_params=_params(("parallel", "parallel")),
        name="lat_gqa",
    )(qkv, qkv, qkv, cache_k, cache_v, cos, sin, q_g, k_g)


def _lat_diff_kernel(q_ref, k_ref, v_ref, kc_ref, vc_ref, cos_ref, sin_ref, lam_ref, dg_ref,
                     o_ref, kall_ref, vall_ref, *, lambda_init):
    half = DIFF_QK_DIM // 4
    kf = _rope(k_ref[...], cos_ref[...], sin_ref[...], half)
    _fill_keys(kall_ref, vall_ref, kf, v_ref, kc_ref, vc_ref)
    lam = _diff_lambda(lam_ref, lambda_init)
    lane = lax.broadcasted_iota(jnp.int32, (ATT_ROWS, HEAD_DIM), 1)
    for s in range(DEC_SEQ // ATT_ROWS):
        rows = slice(s * ATT_ROWS, (s + 1) * ATT_ROWS)
        qf = _rope(q_ref[rows, :], cos_ref[rows, :], sin_ref[rows, :], half) * DIFF_QSCALE
        q0 = jnp.where(lane < DIFF_QK_DIM, qf, 0.0).astype(BF16)
        q1 = jnp.where(lane >= DIFF_QK_DIM, qf, 0.0).astype(BF16)
        od = (_weighted(_scores_exp(q0, kall_ref[...]), vall_ref[...])
              - lam * _weighted(_scores_exp(q1, kall_ref[...]), vall_ref[...]))
        od = _rms(od, dg_ref[...]) * (1.0 - lambda_init)
        o_ref[rows, :] = od.astype(BF16)


def _lat_diff(qkv, cache_k, cache_v, cos, sin, lam_vec, diff_g, layer, lambda_init):
    hd = HEAD_DIM
    cq, ck, cv = C_D_Q // hd, C_D_K // hd, C_D_V // hd
    cache = pl.BlockSpec((None, None, PAST_LEN, hd), lambda b, h: (b, layer, 0, h))
    table = pl.BlockSpec((DEC_SEQ, hd), lambda b, h: (0, 0))
    return pl.pallas_call(
        functools.partial(_lat_diff_kernel, lambda_init=lambda_init),
        grid=(DEC_BATCH, DIFF_HEADS),
        in_specs=[
            pl.BlockSpec((DEC_SEQ, hd), lambda b, h: (b, cq + h)),
            pl.BlockSpec((DEC_SEQ, hd), lambda b, h: (b, ck + h)),
            pl.BlockSpec((DEC_SEQ, hd), lambda b, h: (b, cv + h)),
            cache, cache, table, table,
            pl.BlockSpec((4, DIFF_QK_DIM), lambda b, h: (0, 0)),
            pl.BlockSpec((1, hd), lambda b, h: (0, 0)),
        ],
        out_specs=pl.BlockSpec((DEC_SEQ, hd), lambda b, h: (b, h)),
        out_shape=jax.ShapeDtypeStruct((DEC_BATCH * DEC_SEQ, DIFF_W), BF16),
        scratch_shapes=[pltpu.VMEM((N_KEYS, hd), BF16), pltpu.VMEM((N_KEYS, 2 * hd), BF16)],
        compiler_params=_params(("parallel", "parallel")),
        name="lat_diff",
    )(qkv, qkv, qkv, cache_k, cache_v, cos, sin, lam_vec, diff_g)


def _outproj_kernel(*refs, n_o):
    o_refs = refs[:n_o]
    w_ref, x_ref, g_ref, gt_ref, out_ref = refs[n_o:]
    y = None
    off = 0
    for o_ref in o_refs:
        wd = o_ref.shape[1]
        part = _mm(o_ref[...], w_ref[off:off + wd, :])
        y = part if y is None else y + part
        off += wd
    out_ref[...] = x_ref[...] + gt_ref[...] * _rms(y, g_ref[...])


def _outproj(o_parts, w_out, x, g_post, mod_l, rows_per_batch, first_row):
    m = x.shape[0]
    row = _row_fn(TM_OUT, rows_per_batch, first_row)
    full = pl.BlockSpec((TM_OUT, D_MODEL), lambda i: (i, 0))
    return pl.pallas_call(
        functools.partial(_outproj_kernel, n_o=len(o_parts)),
        grid=(m // TM_OUT,),
        in_specs=[pl.BlockSpec((TM_OUT, o.shape[1]), lambda i: (i, 0)) for o in o_parts] + [
            pl.BlockSpec((D_MIX, D_MODEL), lambda i: (0, 0)),
            full,
            pl.BlockSpec((1, D_MODEL), lambda i: (0, 0)),
            _mod_spec(2, row),
        ],
        out_specs=full,
        out_shape=jax.ShapeDtypeStruct((m, D_MODEL), F32),
        compiler_params=_params(("parallel",)),
        name="outproj",
    )(*o_parts, w_out, x, g_post, mod_l)


def _mlp_kernel(x_ref, gpre_ref, sc_ref, sh_ref, wup_ref, wdn_ref, gpost_ref, gt_ref, *rest,
                n_cast, n_flat):
    n_side = n_cast + n_flat
    cast_in, flat_in = rest[:n_cast], rest[n_cast:n_side]
    out_ref = rest[n_side]
    cast_out = rest[n_side + 1:n_side + 1 + n_cast]
    flat_out = rest[n_side + 1 + n_cast:2 * n_side + 1]
    h_ref, gm_ref = rest[2 * n_side + 1:]
    acc_ref = out_ref
    k = pl.program_id(1)

    @pl.when(k == 0)
    def _():
        _modulated_norm(h_ref, x_ref, gpre_ref, sc_ref, sh_ref, gm_ref)
        acc_ref[...] = jnp.zeros_like(acc_ref)

    u = _mm(h_ref[...], wup_ref[...])
    a = jnp.square(jnp.maximum(u, 0.0)).astype(BF16)
    acc_ref[...] += _mm(a, wdn_ref[...])

    for src, dst in zip(cast_in, cast_out):
        dst[...] = src[...].astype(BF16)
    for src, dst in zip(flat_in, flat_out):
        dst[...] = src[...].reshape(dst.shape)

    @pl.when(k == pl.num_programs(1) - 1)
    def _():
        _gated_norm_residual(out_ref, x_ref, acc_ref, gpost_ref, gt_ref, gm_ref)


def _mlp(x, g_pre, g_post, mod_l, w_up, w_down, rows_per_batch, first_row, cast=None,
         flatten=None):
    m = x.shape[0]
    row = _row_fn(TM_MLP, rows_per_batch, first_row)
    full = pl.BlockSpec((TM_MLP, D_MODEL), lambda i, k: (i, 0))
    vec = pl.BlockSpec((1, D_MODEL), lambda i, k: (0, 0))
    n_k = D_FF // TF_MLP
    n_steps = (m // TM_MLP) * n_k
    cast_in_specs, cast_out_specs, cast_shapes, cast_args = [], [], [], []
    if cast is not None:
        stacks, layer = cast
        for w in stacks:
            rows, cols = w.shape[1], w.shape[2]
            rb = rows // n_steps
            cast_in_specs.append(
                pl.BlockSpec((None, rb, cols), lambda i, k: (layer, i * n_k + k, 0)))
            cast_out_specs.append(pl.BlockSpec((rb, cols), lambda i, k: (i * n_k + k, 0)))
            cast_shapes.append(jax.ShapeDtypeStruct((rows, cols), BF16))
            cast_args.append(w)
    flat_in_specs, flat_out_specs, flat_shapes = [], [], []
    flatten = flatten or []
    if flatten:
        tok = DEC_BATCH * DEPTH * PAST_LEN // n_steps
        per = PAST_LEN // tok

        def where(i, k):
            s = i * n_k + k
            return s // (DEPTH * per), (s // per) % DEPTH, s % per

        for a in flatten:
            nh = a.shape[3]
            flat_in_specs.append(pl.BlockSpec((None, None, tok, nh, HEAD_DIM),
                                              lambda i, k: (*where(i, k), 0, 0)))
            flat_out_specs.append(pl.BlockSpec((None, None, tok, nh * HEAD_DIM),
                                               lambda i, k: (*where(i, k), 0)))
            flat_shapes.append(
                jax.ShapeDtypeStruct((DEC_BATCH, DEPTH, PAST_LEN, nh * HEAD_DIM), a.dtype))
    n_cast = len(cast_args)
    res = pl.pallas_call(
        functools.partial(_mlp_kernel, n_cast=n_cast, n_flat=len(flatten)),
        grid=(m // TM_MLP, n_k),
        in_specs=[
            full, vec, _mod_spec(4, row), _mod_spec(3, row),
            pl.BlockSpec((D_MODEL, TF_MLP), lambda i, k: (0, k)),
            pl.BlockSpec((TF_MLP, D_MODEL), lambda i, k: (k, 0)),
            vec, _mod_spec(5, row),
        ] + cast_in_specs + flat_in_specs,
        out_specs=[full] + cast_out_specs + flat_out_specs,
        out_shape=[jax.ShapeDtypeStruct((m, D_MODEL), F32)] + cast_shapes + flat_shapes,
        scratch_shapes=[pltpu.VMEM((TM_MLP, D_MODEL), BF16), pltpu.VMEM((1, D_MODEL), F32)],
        compiler_params=_params(("parallel", "arbitrary")),
        name="mlp",
    )(x, g_pre, mod_l, mod_l, w_up, w_down, g_post, mod_l, *cast_args, *flatten)
    return res[0], res[1:1 + n_cast], res[1 + n_cast:]


def _rope_tables(half, n_rep):
    t = jnp.arange(DEC_SEQ)
    inv = ROPE_THETA ** (-jnp.arange(half, dtype=F32) / half)

    def cs(pos):
        ang = pos.astype(F32)[:, None] * inv[None, :]
        c, s = jnp.cos(ang), jnp.sin(ang)
        return jnp.concatenate([c, c], axis=-1), jnp.concatenate([-s, s], axis=-1)

    cr, sr = cs(t // GRID_W)
    cc, sc = cs(t % GRID_W)
    return (jnp.concatenate([cr, cc] * n_rep, axis=-1),
            jnp.concatenate([sr, sc] * n_rep, axis=-1))


def _na_bias(rpb):
    qc = np.arange(GRID_W)[:, None]
    kc = np.arange(GRID_W)[None, :]
    ws = np.clip(qc - NA_KW // 2, 0, GRID_W - NA_KW)
    valid = (kc >= ws) & (kc < ws + NA_KW)
    dcol = np.clip(kc - qc + NA_KW - 1, 0, 2 * NA_KW - 2)
    n_dcol = 2 * NA_KW - 1
    onehot = (dcol.reshape(-1)[None, :] == np.arange(n_dcol)[:, None]).astype(np.float32)
    t = jnp.einsum("lhdc,cq->lhdq", rpb.astype(F32), jnp.asarray(onehot),
                   precision=lax.Precision.HIGHEST)
    t = t.reshape(DEPTH, NA_HEADS, 2 * NA_KH - 1, GRID_W, GRID_W) * LOG2E
    t = jnp.where(jnp.asarray(valid)[None, None, None], t, NEG_BIG)
    masked = jnp.full((DEPTH, NA_HEADS, 1, GRID_W, GRID_W), NEG_BIG, F32)
    lo, hi = NA_KH // 2 - 1, NA_KH + NA_KH // 2 - 2
    both = jnp.concatenate([t[:, :, :-1], t[:, :, 1:]], axis=-1)
    second = jnp.concatenate([masked, t[:, :, lo:lo + 1]], axis=-1)
    first = jnp.concatenate([t[:, :, hi:hi + 1], masked], axis=-1)
    none = jnp.concatenate([masked, masked], axis=-1)
    return jnp.concatenate([both, second, first, none], axis=2)


def _na_pair_index(r, key_row):
    r0 = min(max(r - NA_KH // 2, 0), GRID_ROWS - NA_KH)
    in_a = r0 <= key_row < r0 + NA_KH
    in_b = r0 <= key_row + 1 < r0 + NA_KH
    d = key_row - r + NA_KH - 1
    if in_a and in_b:
        return d
    if in_b:
        assert d + 1 == NA_KH // 2 - 1
        return NA_PAIRS_BOTH
    if in_a:
        assert d == NA_KH + NA_KH // 2 - 2
        return NA_PAIRS_BOTH + 1
    return NA_PAIRS_BOTH + 2


def kernel(x_prompt, x_sample, c, cache_na_k, cache_na_v, cache_gqa_k, cache_gqa_v,
           cache_diff_k, cache_diff_v, c_ctx, w_ada, b_ada, norm_g, w_in, w_out, na_rpb,
           gqa_q_g, gqa_k_g, diff_lam, diff_g, w_up, w_down):
    np_rows = BATCH * SEQ
    ns_rows = DEC_BATCH * DEC_SEQ
    xp = x_prompt.reshape(np_rows, D_MODEL)
    xs = x_sample.reshape(ns_rows, D_MODEL)

    cv = jnp.concatenate(
        [c_ctx[None, :], c, jnp.zeros((MOD_ROWS - 1 - DEC_BATCH, D_MODEL), F32)], axis=0)
    mod = _modulation(cv, w_ada, b_ada).reshape(DEPTH, MOD_ROWS, N_MOD, 1, D_MODEL)

    w_stacks = [w_in, w_out, w_up, w_down]
    w_in_b, w_out_b, w_up_b, w_down_b = [w[0].astype(BF16) for w in w_stacks]

    caches = [cache_na_k, cache_na_v, cache_gqa_k, cache_gqa_v, cache_diff_k, cache_diff_v]

    cos_g, sin_g = _rope_tables(HEAD_DIM // 4, 1)
    cos_d, sin_d = _rope_tables(DIFF_QK_DIM // 4, 2)
    na_bias = _na_bias(na_rpb)

    new_kv = None
    for l in range(DEPTH):
        lambda_init = 0.8 - 0.6 * math.exp(-0.3 * l)
        mod_l = mod[l]
        g = norm_g[l].reshape(4, 1, D_MODEL)
        q_g = gqa_q_g[l].reshape(1, HEAD_DIM)
        k_g = gqa_k_g[l].reshape(1, HEAD_DIM)
        d_g = diff_g[l].reshape(1, HEAD_DIM)
        lam_vec = diff_lam[l]

        qkv_p = _inproj(xp, g[0], mod_l, w_in_b, None, 0)
        o_p, *new_kv = _ctx_attn(qkv_p, q_g, k_g, lam_vec, d_g, new_kv, l, lambda_init)
        xp = _outproj([o_p], w_out_b, xp, g[1], mod_l, None, 0)
        xp, _, flat = _mlp(xp, g[2], g[3], mod_l, w_up_b, w_down_b, None, 0,
                           flatten=caches if l == 0 else None)
        if l == 0:
            c_na_k, c_na_v, c_g_k, c_g_v, c_d_k, c_d_v = flat

        qkv_s = _inproj(xs, g[0], mod_l, w_in_b, DEC_SEQ, 1)
        o_na = _lat_na(qkv_s, c_na_k, c_na_v, na_bias, l)
        o_gqa = _lat_gqa(qkv_s, c_g_k, c_g_v, cos_g, sin_g, q_g, k_g, l)
        o_diff = _lat_diff(qkv_s, c_d_k, c_d_v, cos_d, sin_d, lam_vec, d_g, l, lambda_init)
        xs = _outproj([o_na, o_gqa, o_diff], w_out_b, xs, g[1], mod_l, DEC_SEQ, 1)
        cast = (w_stacks, l + 1) if l + 1 < DEPTH else None
        xs, nxt, _ = _mlp(xs, g[2], g[3], mod_l, w_up_b, w_down_b, DEC_SEQ, 1, cast)
        if nxt:
            w_in_b, w_out_b, w_up_b, w_down_b = nxt

    return (xp.reshape(BATCH, SEQ, D_MODEL), xs.reshape(DEC_BATCH, DEC_SEQ, D_MODEL), *new_kv)
```
